```python
import jax, jax.numpy as jnp
from jax import lax
import numpy as np

D_MODEL = 1024
BATCH = 8
SEQ = 4096
DEPTH = 1

HEAD_DIM = D_MODEL // 16
RET_HEADS = 8
NSA_HEADS = 8
NSA_KV_GROUPS = 2
HEADS_PER_GROUP = NSA_HEADS // NSA_KV_GROUPS
RET_WIDTH = RET_HEADS * HEAD_DIM
NSA_WIDTH = NSA_HEADS * HEAD_DIM
MIX_WIDTH = RET_WIDTH + NSA_WIDTH
KV_WIDTH = NSA_KV_GROUPS * HEAD_DIM
RET_CHUNK = 128
RET_ROPE_BASE = 10000.0
NSA_ROPE_BASE = 500000.0
ROPE_DIM = HEAD_DIM // 4
CMP_BLOCK = 32
CMP_STRIDE = 16
CMP_HIDDEN = 256
SLC_BLOCK = 64
SLC_TOPK = 16
WINDOW = 512
NSA_QBLK = 64
NEG = -1e30
BIG = 1e9
EPS = 1e-6
SPLITS = (RET_WIDTH, RET_WIDTH, RET_WIDTH, RET_WIDTH, NSA_WIDTH, 6 * KV_WIDTH, 3 * NSA_HEADS, NSA_WIDTH)
PROJ_WIDTH = sum(SPLITS)

kernel_name = "hymba_retention_nsa_hybrid"


def rmsnorm(x, g):
    xf = x.astype(jnp.float32)
    y = xf * lax.rsqrt(jnp.mean(xf * xf, axis=-1, keepdims=True) + EPS) * g.astype(jnp.float32)
    return y.astype(x.dtype)


def rotary(x, pos, base, rot_dim):
    half = rot_dim // 2
    inv = jnp.power(base, -jnp.arange(half, dtype=jnp.float32) / half)
    ang = pos.astype(jnp.float32)[..., None] * inv
    cos = jnp.cos(ang)[:, :, None, :]
    sin = jnp.sin(ang)[:, :, None, :]
    x1 = x[..., :half].astype(jnp.float32)
    x2 = x[..., half:rot_dim].astype(jnp.float32)
    out = jnp.concatenate([(x1 * cos - x2 * sin).astype(x.dtype),
                           (x1 * sin + x2 * cos).astype(x.dtype),
                           x[..., rot_dim:]], axis=-1)
    return out


def masked_softmax(s, mask):
    s = jnp.where(mask, s.astype(jnp.float32), NEG)
    return jax.nn.softmax(s, axis=-1) * mask


def retention(q, k, v, pos):
    B, S, H, Dh = q.shape
    q = rotary(q, pos, RET_ROPE_BASE, Dh).astype(jnp.float32) * (Dh ** -0.5)
    k = rotary(k, pos, RET_ROPE_BASE, Dh).astype(jnp.float32)
    v = v.astype(jnp.float32)
    log_g = jnp.log1p(-jnp.power(2.0, -5.0 - jnp.arange(H, dtype=jnp.float32)))
    C = RET_CHUNK
    N = S // C
    qc = q.reshape(B, N, C, H, Dh)
    kc = k.reshape(B, N, C, H, Dh)
    vc = v.reshape(B, N, C, H, Dh)
    idx = jnp.arange(C, dtype=jnp.float32)
    diff = idx[:, None] - idx[None, :]
    decay = jnp.where(diff[None] >= 0, jnp.exp(jnp.maximum(diff, 0.0)[None] * log_g[:, None, None]), 0.0)
    scores = jnp.einsum('bnihd,bnjhd->bnhij', qc, kc) * decay
    inner = jnp.einsum('bnhij,bnjhe->bnihe', scores, vc)
    zeta = jnp.exp((C - 1 - idx)[None, :] * log_g[:, None])
    kv_chunk = jnp.einsum('bnjhd,hj,bnjhe->nbhde', kc, zeta, vc)
    chunk_decay = jnp.exp(C * log_g)[None, :, None, None]

    def step(R, kv):
        return R * chunk_decay + kv, R

    _, R_prev = lax.scan(step, jnp.zeros((B, H, Dh, Dh), jnp.float32), kv_chunk)
    xi = jnp.exp((idx + 1.0)[None, :] * log_g[:, None])
    cross = jnp.einsum('bnihd,nbhde,hi->bnihe', qc, R_prev, xi)
    o = (inner + cross).reshape(B, S, H, Dh)
    mu = jnp.mean(o, axis=-1, keepdims=True)
    var = jnp.mean(jnp.square(o - mu), axis=-1, keepdims=True)
    return (o - mu) * lax.rsqrt(var + 1e-5)


def compress(t, pe, w1, w2):
    B, S, G, Dh = t.shape
    Nc = (S - CMP_BLOCK) // CMP_STRIDE + 1
    cidx = np.arange(Nc)[:, None] * CMP_STRIDE + np.arange(CMP_BLOCK)[None, :]
    blocks = t[:, cidx] + pe[None, None, :, None, :].astype(t.dtype)
    flat = blocks.transpose(0, 1, 3, 2, 4).reshape(B, Nc, G, CMP_BLOCK * Dh)
    return jax.nn.silu(flat @ w1) @ w2


def nsa(q, kv_all, gate_logits, pos, pe_k, w1_k, w2_k, pe_v, w1_v, w2_v):
    B, S, Hq, Dh = q.shape
    G, hpg = NSA_KV_GROUPS, HEADS_PER_GROUP
    kv = kv_all.reshape(B, S, 6, G, Dh)
    k_c, v_c, k_s, v_s, k_w, v_w = [kv[:, :, i] for i in range(6)]
    q = rotary(q, pos, NSA_ROPE_BASE, ROPE_DIM) * (Dh ** -0.5)
    k_s = rotary(k_s, pos, NSA_ROPE_BASE, ROPE_DIM)
    k_w = rotary(k_w, pos, NSA_ROPE_BASE, ROPE_DIM)
    Nc = (S - CMP_BLOCK) // CMP_STRIDE + 1
    NS = S // SLC_BLOCK
    topk = min(SLC_TOPK, NS)
    cmp_start = np.arange(Nc) * CMP_STRIDE
    cmp_end = cmp_start + CMP_BLOCK - 1
    slc_start = np.arange(NS) * SLC_BLOCK
    overlap = jnp.asarray(((cmp_start[:, None] <= slc_start[None, :] + SLC_BLOCK - 1) &
                           (cmp_end[:, None] >= slc_start[None, :])).astype(np.float32))
    kc = rotary(compress(k_c, pe_k, w1_k, w2_k), pos[:, cmp_end], NSA_ROPE_BASE, ROPE_DIM)
    vc = compress(v_c, pe_v, w1_v, w2_v)
    kcT = kc.transpose(0, 2, 1, 3)
    vcT = vc.transpose(0, 2, 1, 3)
    qg = q.reshape(B, S, G, hpg, Dh).transpose(0, 2, 3, 1, 4)
    ks_blk = k_s.reshape(B, NS, SLC_BLOCK, G, Dh).transpose(0, 3, 1, 2, 4)
    vs_blk = v_s.reshape(B, NS, SLC_BLOCK, G, Dh).transpose(0, 3, 1, 2, 4)
    pad = ((0, 0), (0, 0), (WINDOW, 0), (0, 0))
    kw_pad = jnp.pad(k_w.transpose(0, 2, 1, 3), pad)
    vw_pad = jnp.pad(v_w.transpose(0, 2, 1, 3), pad)
    bidx = jnp.arange(B)[:, None, None, None]
    gidx = jnp.arange(G)[None, :, None, None]
    blk_ids = jnp.arange(NS)

    def block(bi):
        t0 = bi * NSA_QBLK
        qb = lax.dynamic_slice_in_dim(qg, t0, NSA_QBLK, axis=3)
        tq = t0 + jnp.arange(NSA_QBLK)
        s = jnp.einsum('bghqd,bgnd->bghqn', qb, kcT)
        p_cmp = masked_softmax(s, cmp_end[None, :] <= tq[:, None])
        o_cmp = jnp.einsum('bghqn,bgnd->bghqd', p_cmp, vcT.astype(jnp.float32))
        imp = jnp.einsum('bghqn,ns->bgqs', p_cmp, overlap)
        cur = tq // SLC_BLOCK
        valid = blk_ids[None, :] <= cur[:, None]
        forced = (blk_ids[None, :] == 0) | (blk_ids[None, :] == cur[:, None]) | (blk_ids[None, :] == cur[:, None] - 1)
        score = jnp.where(forced, BIG, jnp.where(valid, imp, NEG))
        top_val, top_idx = lax.top_k(score, topk)
        kg = ks_blk[bidx, gidx, top_idx]
        vg = vs_blk[bidx, gidx, top_idx]
        s = jnp.einsum('bghqd,bgqnkd->bghqnk', qb, kg).reshape(B, G, hpg, NSA_QBLK, topk * SLC_BLOCK)
        tok = top_idx[..., None] * SLC_BLOCK + jnp.arange(SLC_BLOCK)
        m = (tok <= tq[None, None, :, None, None]) & (top_val > 0.5 * NEG)[..., None]
        p_slc = masked_softmax(s, m.reshape(B, G, 1, NSA_QBLK, topk * SLC_BLOCK))
        o_slc = jnp.einsum('bghqm,bgqmd->bghqd', p_slc,
                           vg.reshape(B, G, NSA_QBLK, topk * SLC_BLOCK, Dh).astype(jnp.float32))
        kwb = lax.dynamic_slice_in_dim(kw_pad, t0, WINDOW + NSA_QBLK, axis=2)
        vwb = lax.dynamic_slice_in_dim(vw_pad, t0, WINDOW + NSA_QBLK, axis=2)
        kp = t0 - WINDOW + jnp.arange(WINDOW + NSA_QBLK)
        mw = (kp[None, :] >= 0) & (kp[None, :] <= tq[:, None]) & (tq[:, None] - kp[None, :] < WINDOW)
        s = jnp.einsum('bghqd,bgkd->bghqk', qb, kwb)
        p_win = masked_softmax(s, mw)
        o_win = jnp.einsum('bghqk,bgkd->bghqd', p_win, vwb.astype(jnp.float32))
        return o_cmp, o_slc, o_win

    o_cmp, o_slc, o_win = lax.map(block, jnp.arange(S // NSA_QBLK))
    to_bshd = lambda o: o.transpose(1, 0, 4, 2, 3, 5).reshape(B, S, Hq, Dh)
    g = jax.nn.sigmoid(gate_logits.astype(jnp.float32)).reshape(B, S, Hq, 3)
    o = g[..., 0:1] * to_bshd(o_cmp) + g[..., 1:2] * to_bshd(o_slc) + g[..., 2:3] * to_bshd(o_win)
    return o


def setup_inputs(seed: int = 0) -> dict:
    key = jax.random.key(seed)
    ks = jax.random.split(key, 20)
    f32 = jnp.float32
    nrm = lambda k, shape, scale: jax.random.normal(k, shape, f32) * scale
    x = jax.random.normal(ks[0], (BATCH, SEQ, D_MODEL), f32)
    c = jax.random.normal(ks[1], (BATCH, D_MODEL), f32)
    start = jax.random.randint(ks[2], (BATCH, 1), 0, 1024, dtype=jnp.int32)
    positions = start + jnp.arange(SEQ, dtype=jnp.int32)[None, :]
    return {
        "x": x,
        "c": c,
        "positions": positions,
        "w_ada": nrm(ks[3], (DEPTH, D_MODEL, 3 * D_MODEL), 0.5 * D_MODEL ** -0.5),
        "b_ada": nrm(ks[4], (DEPTH, 3 * D_MODEL), 0.02),
        "g_pre": 1.0 + nrm(ks[5], (DEPTH, D_MODEL), 0.05),
        "g_post": 1.0 + nrm(ks[6], (DEPTH, D_MODEL), 0.05),
        "w_in": nrm(ks[7], (DEPTH, D_MODEL, PROJ_WIDTH), D_MODEL ** -0.5),
        "w_out": nrm(ks[8], (DEPTH, MIX_WIDTH, D_MODEL), MIX_WIDTH ** -0.5),
        "cmp_pe_k": nrm(ks[9], (DEPTH, CMP_BLOCK, HEAD_DIM), 0.1),
        "cmp_w1_k": nrm(ks[10], (DEPTH, CMP_BLOCK * HEAD_DIM, CMP_HIDDEN), (CMP_BLOCK * HEAD_DIM) ** -0.5),
        "cmp_w2_k": nrm(ks[11], (DEPTH, CMP_HIDDEN, HEAD_DIM), CMP_HIDDEN ** -0.5),
        "cmp_pe_v": nrm(ks[12], (DEPTH, CMP_BLOCK, HEAD_DIM), 0.1),
        "cmp_w1_v": nrm(ks[13], (DEPTH, CMP_BLOCK * HEAD_DIM, CMP_HIDDEN), (CMP_BLOCK * HEAD_DIM) ** -0.5),
        "cmp_w2_v": nrm(ks[14], (DEPTH, CMP_HIDDEN, HEAD_DIM), CMP_HIDDEN ** -0.5),
    }


def reference(x, c, positions, w_ada, b_ada, g_pre, g_post, w_in, w_out,
              cmp_pe_k, cmp_w1_k, cmp_w2_k, cmp_pe_v, cmp_w1_v, cmp_w2_v):
    B, S, _ = x.shape
    offsets = np.cumsum(SPLITS)[:-1].tolist()
    for l in range(DEPTH):
        mod = jax.nn.silu(c) @ w_ada[l] + b_ada[l]
        shift, scale, gate = jnp.split(mod, 3, axis=-1)
        h = rmsnorm(x, g_pre[l]) * (1.0 + scale[:, None, :]) + shift[:, None, :]
        proj = h @ w_in[l]
        rq, rk, rv, rg, nq, nkv, ngl, ng = jnp.split(proj, offsets, axis=-1)
        hd = lambda t, H: t.reshape(B, S, H, HEAD_DIM)
        ret = retention(hd(rq, RET_HEADS), hd(rk, RET_HEADS), hd(rv, RET_HEADS), positions)
        ret = ret.reshape(B, S, RET_WIDTH) * jax.nn.silu(rg.astype(jnp.float32))
        att = nsa(hd(nq, NSA_HEADS), nkv, ngl, positions,
                  cmp_pe_k[l], cmp_w1_k[l], cmp_w2_k[l], cmp_pe_v[l], cmp_w1_v[l], cmp_w2_v[l])
        att = att.reshape(B, S, NSA_WIDTH) * jax.nn.silu(ng.astype(jnp.float32))
        mixed = jnp.concatenate([ret, att], axis=-1).astype(x.dtype)
        y = rmsnorm(mixed @ w_out[l], g_post[l])
        x = x + gate[:, None, :].astype(x.dtype) * y
    return x
```

```python
import functools

import numpy as np
import jax
import jax.numpy as jnp
from jax import lax
from jax.experimental import pallas as pl
from jax.experimental.pallas import tpu as pltpu

F32 = jnp.float32
BF16 = jnp.bfloat16

LANES = 128
HEAD_DIM = 64
RET_HEADS = 8
NSA_HEADS = 8
NSA_KV_GROUPS = 2
HEADS_PER_GROUP = NSA_HEADS // NSA_KV_GROUPS
RET_WIDTH = RET_HEADS * HEAD_DIM
NSA_WIDTH = NSA_HEADS * HEAD_DIM
KV_WIDTH = NSA_KV_GROUPS * HEAD_DIM
RET_CHUNK = 128
RET_ROPE_BASE = 10000.0
NSA_ROPE_BASE = 500000.0
ROPE_DIM = HEAD_DIM // 4
CMP_BLOCK = 32
CMP_STRIDE = 16
CMP_HIDDEN = 256
SLC_BLOCK = 64
SLC_TOPK = 16
WINDOW = 512
NEG = -1e30
BIG = 1e9
EPS = 1e-6
GN_EPS = 1e-5
VMEM_LIMIT = 48 * 1024 * 1024

NT_DIMS = (((1,), (1,)), ((), ()))
TN_DIMS = (((0,), (0,)), ((), ()))


def _dot(a, b):
    return jnp.dot(a, b, preferred_element_type=F32)


def _dot_nt(a, b):
    return lax.dot_general(a, b, NT_DIMS, preferred_element_type=F32)


def _dot_tn(a, b):
    return lax.dot_general(a, b, TN_DIMS, preferred_element_type=F32)


def _silu(z):
    return z * jax.nn.sigmoid(z)


def _params(*sem):
    return pltpu.CompilerParams(dimension_semantics=sem, vmem_limit_bytes=VMEM_LIMIT)


HEAD_SHIFT = 6
SLC_SHIFT = 6
assert 1 << HEAD_SHIFT == HEAD_DIM and 1 << SLC_SHIFT == SLC_BLOCK


def _lane_in_head(shape):
    return lax.broadcasted_iota(jnp.int32, shape, len(shape) - 1) & (HEAD_DIM - 1)


def _ada_kernel(c_ref, w_ref, b_ref, o_ref):
    a = _silu(c_ref[...])
    o_ref[...] = jnp.dot(a, w_ref[...], precision=lax.Precision.HIGHEST,
                         preferred_element_type=F32) + b_ref[...]


def _ada(c, w, b):
    bsz, d = c.shape
    n = w.shape[1]
    tn = 1024
    return pl.pallas_call(
        _ada_kernel,
        grid=(n // tn,),
        in_specs=[pl.BlockSpec((bsz, d), lambda j: (0, 0)),
                  pl.BlockSpec((d, tn), lambda j: (0, j)),
                  pl.BlockSpec((1, tn), lambda j: (0, j))],
        out_specs=pl.BlockSpec((bsz, tn), lambda j: (0, j)),
        out_shape=jax.ShapeDtypeStruct((bsz, n), F32),
        compiler_params=_params("arbitrary"),
        name="ada",
    )(c, w, b.reshape(1, n))


_O_RQ, _O_RK, _O_RV, _O_RG, _O_NQ = 0, 512, 1024, 1536, 2048
_O_KV = 2560
_O_NG = 3328
_O_GL = 3840
_PROJ_COLS = 3968


def _rot_ret(z, cos, sin_signed, m):
    partner = jnp.where(m < HEAD_DIM // 2, pltpu.roll(z, LANES - HEAD_DIM // 2, 1),
                        pltpu.roll(z, HEAD_DIM // 2, 1))
    return z * cos + partner * sin_signed


def _rot_nsa(z, cos, sin_signed, m):
    half = ROPE_DIM // 2
    partner = jnp.where(m < half, pltpu.roll(z, LANES - half, 1), pltpu.roll(z, half, 1))
    return z * cos + partner * sin_signed


def _nsa_tables(posf, inv_row, m):
    half = ROPE_DIM // 2
    ang = posf * inv_row
    cos = jnp.where(m < ROPE_DIM, jnp.cos(ang), 1.0)
    s = jnp.sin(ang)
    sin = jnp.where(m < half, -s, jnp.where(m < ROPE_DIM, s, 0.0))
    return cos, sin


def _dup_groups(z, m_lane):
    r = pltpu.roll(z, HEAD_DIM, 1)
    lo = m_lane < HEAD_DIM
    return jnp.where(lo, z, r), jnp.where(lo, r, z)


def _proj_kernel(x_ref, shift_ref, scale_ref, gpre_ref, pos_ref, invr_ref, invn_ref, w_ref,
                 rq_ref, rk_ref, rv_ref, rgs_ref, nq_ref, cmps_ref, ksd_ref, vsd_ref,
                 kwd_ref, vwd_ref, ngs_ref, gl_ref,
                 cr_s, sr_s, cn_s, sn_s):
    ts = x_ref.shape[1]
    x = x_ref[0]
    ms = jnp.mean(x * x, axis=-1, keepdims=True)
    y = x * lax.rsqrt(ms + EPS) * gpre_ref[...]
    h = y * (1.0 + scale_ref[0]) + shift_ref[0]
    hb = h.astype(BF16)

    posf = pos_ref[0].astype(F32)
    lane = lax.broadcasted_iota(jnp.int32, (ts, LANES), 1)
    m = lane & (HEAD_DIM - 1)
    ang = posf * invr_ref[...]
    cr_s[...] = jnp.cos(ang)
    s = jnp.sin(ang)
    sr_s[...] = jnp.where(m < HEAD_DIM // 2, -s, s)
    cn, sn = _nsa_tables(posf, invn_ref[...], m)
    cn_s[...] = cn
    sn_s[...] = sn

    def mm(off, width=LANES):
        return _dot(hb, w_ref[:, off:off + width])

    scale_q = HEAD_DIM ** -0.5
    for c in range(RET_WIDTH // LANES):
        sl = slice(c * LANES, (c + 1) * LANES)
        rq_ref[0, :, sl] = (_rot_ret(mm(_O_RQ + c * LANES), cr_s[...], sr_s[...], m) * scale_q).astype(BF16)
        rk_ref[0, :, sl] = _rot_ret(mm(_O_RK + c * LANES), cr_s[...], sr_s[...], m).astype(BF16)
        rv_ref[0, :, sl] = mm(_O_RV + c * LANES).astype(BF16)
        rgs_ref[0, :, sl] = _silu(mm(_O_RG + c * LANES))
        nq_ref[0, :, sl] = (_rot_nsa(mm(_O_NQ + c * LANES), cn_s[...], sn_s[...], m) * scale_q).astype(BF16)
        ngs_ref[0, :, sl] = _silu(mm(_O_NG + c * LANES))

    kc0, kc1 = _dup_groups(mm(_O_KV), lane)
    vc0, vc1 = _dup_groups(mm(_O_KV + LANES), lane)
    for i, z in enumerate((kc0, kc1, vc0, vc1)):
        cmps_ref[0, :, i * LANES:(i + 1) * LANES] = z
    ks0, ks1 = _dup_groups(_rot_nsa(mm(_O_KV + 2 * LANES), cn_s[...], sn_s[...], m), lane)
    ksd_ref[0, :, :LANES] = ks0.astype(BF16)
    ksd_ref[0, :, LANES:] = ks1.astype(BF16)
    vs0, vs1 = _dup_groups(mm(_O_KV + 3 * LANES), lane)
    vsd_ref[0, :, :LANES] = vs0.astype(BF16)
    vsd_ref[0, :, LANES:] = vs1.astype(BF16)
    kw0, kw1 = _dup_groups(_rot_nsa(mm(_O_KV + 4 * LANES), cn_s[...], sn_s[...], m), lane)
    kwd_ref[0, :, :LANES] = kw0.astype(BF16)
    kwd_ref[0, :, LANES:] = kw1.astype(BF16)
    vw0, vw1 = _dup_groups(mm(_O_KV + 5 * LANES), lane)
    vwd_ref[0, :, :LANES] = vw0.astype(BF16)
    vwd_ref[0, :, LANES:] = vw1.astype(BF16)
    gl_ref[0] = jax.nn.sigmoid(mm(_O_GL))


def _proj(x, shift, scale, g_pre, pos3, inv_ret_row, inv_nsa_row, w_all, ts):
    bsz, seq, d = x.shape
    row = lambda w: pl.BlockSpec((1, ts, w), lambda b, s: (b, s, 0))
    vec = pl.BlockSpec((1, 1, d), lambda b, s: (b, 0, 0))
    const2 = lambda shp: pl.BlockSpec(shp, lambda b, s: (0, 0))
    outs = [(RET_WIDTH, BF16), (RET_WIDTH, BF16), (RET_WIDTH, BF16), (RET_WIDTH, F32),
            (NSA_WIDTH, BF16), (4 * LANES, F32), (2 * LANES, BF16), (2 * LANES, BF16),
            (2 * LANES, BF16), (2 * LANES, BF16), (NSA_WIDTH, F32), (LANES, F32)]
    return pl.pallas_call(
        _proj_kernel,
        grid=(bsz, seq // ts),
        in_specs=[row(d), vec, vec, const2((1, d)), row(1), const2((1, LANES)), const2((1, LANES)),
                  const2((d, _PROJ_COLS))],
        out_specs=[row(w) for w, _ in outs],
        out_shape=[jax.ShapeDtypeStruct((bsz, seq, w), dt) for w, dt in outs],
        scratch_shapes=[pltpu.VMEM((ts, LANES), F32)] * 4,
        compiler_params=_params("parallel", "arbitrary"),
        name="proj",
    )(x, shift, scale, g_pre, pos3, inv_ret_row, inv_nsa_row, w_all)


def _ret_kernel(q_ref, k_ref, v_ref, g_ref, dec_ref, zeta_ref, xi_ref, cd_ref, o_ref, r_s):
    rows = q_ref.shape[1]
    cc = RET_CHUNK

    @pl.when(pl.program_id(2) == 0)
    def _():
        r_s[...] = jnp.zeros_like(r_s)

    lane = lax.broadcasted_iota(jnp.int32, (cc, LANES), 1)
    rowi = lax.broadcasted_iota(jnp.int32, (cc, LANES), 0)
    lo = lane < HEAD_DIM
    blockdiag = (rowi < HEAD_DIM) == lo
    inv_n = 1.0 / HEAD_DIM
    for c in range(rows // cc):
        sl = slice(c * cc, (c + 1) * cc)
        q = q_ref[0, sl, :]
        k = k_ref[0, sl, :]
        v = v_ref[0, sl, :]
        zero = jnp.zeros_like(q)
        s_a = _dot_nt(jnp.where(lo, q, zero), k) * dec_ref[0]
        s_b = _dot_nt(jnp.where(lo, zero, q), k) * dec_ref[1]
        inner = jnp.where(lo, _dot(s_a.astype(BF16), v), _dot(s_b.astype(BF16), v))
        r_prev = r_s[...]
        cross = _dot(q, r_prev.astype(BF16)) * xi_ref[0]
        o = inner + cross
        kz = (k.astype(F32) * zeta_ref[0]).astype(BF16)
        kv = _dot_tn(kz, v)
        r_s[...] = r_prev * cd_ref[0] + jnp.where(blockdiag, kv, 0.0)
        sum_a = jnp.sum(jnp.where(lo, o, 0.0), axis=-1, keepdims=True)
        sum_b = jnp.sum(jnp.where(lo, 0.0, o), axis=-1, keepdims=True)
        d = o - jnp.where(lo, sum_a, sum_b) * inv_n
        d2 = d * d
        var_a = jnp.sum(jnp.where(lo, d2, 0.0), axis=-1, keepdims=True)
        var_b = jnp.sum(jnp.where(lo, 0.0, d2), axis=-1, keepdims=True)
        var = jnp.where(lo, var_a, var_b) * inv_n
        o_ref[0, sl, :] = d * lax.rsqrt(var + GN_EPS) * g_ref[0, sl, :]


def _retention(rq, rk, rv, rgs, dec, zeta_l, xi_l, cd_l, rows):
    bsz, seq, _ = rq.shape
    npair = RET_HEADS // 2
    blk = pl.BlockSpec((1, rows, LANES), lambda b, p, s: (b, s, p))
    return pl.pallas_call(
        _ret_kernel,
        grid=(bsz, npair, seq // rows),
        in_specs=[blk, blk, blk, blk,
                  pl.BlockSpec((2, RET_CHUNK, RET_CHUNK), lambda b, p, s: (p, 0, 0)),
                  pl.BlockSpec((1, RET_CHUNK, LANES), lambda b, p, s: (p, 0, 0)),
                  pl.BlockSpec((1, RET_CHUNK, LANES), lambda b, p, s: (p, 0, 0)),
                  pl.BlockSpec((1, 1, LANES), lambda b, p, s: (p, 0, 0))],
        out_specs=blk,
        out_shape=jax.ShapeDtypeStruct((bsz, seq, RET_WIDTH), F32),
        scratch_shapes=[pltpu.VMEM((LANES, LANES), F32)],
        compiler_params=_params("parallel", "parallel", "arbitrary"),
        name="retention",
    )(rq, rk, rv, rgs, dec, zeta_l, xi_l, cd_l)


def _cmp_kernel(src_ref, pe_ref, w1_ref, w2_ref, pos_ref, invn_ref, o_ref):
    ncp = o_ref.shape[3]
    half = CMP_BLOCK // 2
    p_lo = jnp.zeros((ncp, CMP_HIDDEN), F32)
    p_hi = jnp.zeros((ncp, CMP_HIDDEN), F32)
    for l in range(half):
        t = src_ref[0, pl.ds(l, ncp, stride=CMP_STRIDE), :]
        p_lo = p_lo + _dot((t + pe_ref[0, l:l + 1, :]).astype(BF16), w1_ref[0, l])
        p_hi = p_hi + _dot((t + pe_ref[0, half + l:half + l + 1, :]).astype(BF16), w1_ref[0, half + l])
    hidden = p_lo + pltpu.roll(p_hi, ncp - 1, 0)
    comp = _dot(_silu(hidden).astype(BF16), w2_ref[0])
    m = _lane_in_head(comp.shape)
    cos, sin = _nsa_tables(pos_ref[0].astype(F32), invn_ref[...], m)
    rotated = _rot_nsa(comp, cos, sin, m)
    is_key = pl.program_id(1) == 0
    o_ref[0, 0, 0] = jnp.where(is_key, rotated, comp).astype(BF16)


def _compress(cmps, pe2, w1p, w2d, pos_cmp, inv_nsa_row):
    bsz, seq, _ = cmps.shape
    ncp = seq // CMP_STRIDE
    g = NSA_KV_GROUPS
    return pl.pallas_call(
        _cmp_kernel,
        grid=(bsz, 2, g),
        in_specs=[pl.BlockSpec((1, seq, LANES), lambda b, j, gi: (b, 0, j * g + gi)),
                  pl.BlockSpec((1, CMP_BLOCK, LANES), lambda b, j, gi: (j, 0, 0)),
                  pl.BlockSpec((1, CMP_BLOCK, LANES, CMP_HIDDEN), lambda b, j, gi: (j, 0, 0, 0)),
                  pl.BlockSpec((1, CMP_HIDDEN, LANES), lambda b, j, gi: (j, 0, 0)),
                  pl.BlockSpec((1, ncp, 1), lambda b, j, gi: (b, 0, 0)),
                  pl.BlockSpec((1, LANES), lambda b, j, gi: (0, 0))],
        out_specs=pl.BlockSpec((1, 1, 1, ncp, LANES), lambda b, j, gi: (b, j, gi, 0, 0)),
        out_shape=jax.ShapeDtypeStruct((bsz, 2, g, ncp, LANES), BF16),
        compiler_params=_params("parallel", "arbitrary", "arbitrary"),
        name="compress",
    )(cmps, pe2, w1p, w2d, pos_cmp, inv_nsa_row)


def _cmpsel_kernel(q_ref, kc_ref, vc_ref, ovt_ref, o_ref, selt_ref):
    tq = q_ref.shape[1]
    ncp = kc_ref.shape[3]
    nblk = ovt_ref.shape[0]
    t0 = pl.program_id(2) * tq
    kc = kc_ref[0, 0, 0]
    vc = vc_ref[0, 0, 0]

    tok = t0 + lax.broadcasted_iota(jnp.int32, (tq, ncp), 0)
    cend = lax.broadcasted_iota(jnp.int32, (tq, ncp), 1) * CMP_STRIDE + (CMP_BLOCK - 1)
    mask = cend <= tok
    lo = lax.broadcasted_iota(jnp.int32, (tq, LANES), 1) < HEAD_DIM
    psum = jnp.zeros((tq, ncp), F32)
    for c in range(HEADS_PER_GROUP // 2):
        q = q_ref[0, :, c * LANES:(c + 1) * LANES]
        zero = jnp.zeros_like(q)
        outs = []
        for qm in (jnp.where(lo, q, zero), jnp.where(lo, zero, q)):
            s = jnp.where(mask, _dot_nt(qm, kc), NEG)
            mx = jnp.max(s, axis=-1, keepdims=True)
            e = jnp.where(mask, jnp.exp(s - mx), 0.0)
            den = jnp.sum(e, axis=-1, keepdims=True)
            p = e * jnp.where(den > 0.0, 1.0 / den, 0.0)
            psum = psum + p
            outs.append(_dot(p.astype(BF16), vc))
        o_ref[0, :, c * LANES:(c + 1) * LANES] = jnp.where(lo, outs[0], outs[1])

    p_hi = psum.astype(BF16)
    p_lo = (psum - p_hi.astype(F32)).astype(BF16)
    imp = _dot_nt(ovt_ref[...], p_hi) + _dot_nt(ovt_ref[...], p_lo)
    blk = lax.broadcasted_iota(jnp.int32, (nblk, tq), 0)
    cur = (t0 + lax.broadcasted_iota(jnp.int32, (nblk, tq), 1)) >> SLC_SHIFT
    forced = (blk == 0) | (blk == cur) | (blk == cur - 1)
    score = jnp.where(forced, BIG, jnp.where(blk <= cur, imp, NEG))
    rank = jnp.zeros((nblk, tq), F32)
    for i in range(nblk):
        ri = score[i:i + 1, :]
        beats = (ri > score) | ((ri == score) & (blk > i))
        rank = rank + jnp.where(beats, 1.0, 0.0)
    sel = jnp.where((rank < float(min(SLC_TOPK, nblk))) & (score > 0.5 * NEG), 1.0, 0.0)
    selt_ref[0, 0] = sel.T


def _cmpsel(nq, cmpkv, ovt, tq):
    bsz, seq, _ = nq.shape
    g = NSA_KV_GROUPS
    ncp = cmpkv.shape[3]
    nblk = ovt.shape[0]
    gw = HEADS_PER_GROUP * HEAD_DIM
    return pl.pallas_call(
        _cmpsel_kernel,
        grid=(bsz, g, seq // tq),
        in_specs=[pl.BlockSpec((1, tq, gw), lambda b, gi, i: (b, i, gi)),
                  pl.BlockSpec((1, 1, 1, ncp, LANES), lambda b, gi, i: (b, 0, gi, 0, 0)),
                  pl.BlockSpec((1, 1, 1, ncp, LANES), lambda b, gi, i: (b, 1, gi, 0, 0)),
                  pl.BlockSpec((nblk, ncp), lambda b, gi, i: (0, 0))],
        out_specs=[pl.BlockSpec((1, tq, gw), lambda b, gi, i: (b, i, gi)),
                   pl.BlockSpec((1, 1, tq, nblk), lambda b, gi, i: (b, gi, i, 0))],
        out_shape=[jax.ShapeDtypeStruct((bsz, seq, NSA_WIDTH), F32),
                   jax.ShapeDtypeStruct((bsz, g, seq, nblk), F32)],
        compiler_params=_params("parallel", "parallel", "arbitrary"),
        name="cmpsel",
    )(nq, cmpkv, cmpkv, ovt)


def _flash_kernel(*refs, window, tq, tk, nk):
    if window:
        q_ref, k_ref, v_ref, o_ref, qs_s, m_s, l_s, acc_s = refs
        selt_ref = None
    else:
        q_ref, k_ref, v_ref, selt_ref, o_ref, qs_s, m_s, l_s, acc_s = refs
    qi = pl.program_id(2)
    kj = pl.program_id(3)
    nh = HEADS_PER_GROUP

    @pl.when(kj == 0)
    def _():
        lo = lax.broadcasted_iota(jnp.int32, (tq, LANES), 1) < HEAD_DIM
        for c in range(nh // 2):
            q = q_ref[0, :, c * LANES:(c + 1) * LANES]
            zero = jnp.zeros_like(q)
            qs_s[(2 * c) * tq:(2 * c + 1) * tq, :] = jnp.where(lo, q, zero)
            qs_s[(2 * c + 1) * tq:(2 * c + 2) * tq, :] = jnp.where(lo, zero, q)
        m_s[...] = jnp.full_like(m_s, NEG)
        l_s[...] = jnp.zeros_like(l_s)
        acc_s[...] = jnp.zeros_like(acc_s)

    if window:
        tile = qi - (nk - 1) + kj
        active = tile >= 0
    else:
        tile = kj
        active = kj * tk <= qi * tq + (tq - 1)

    @pl.when(active)
    def _():
        tok = qi * tq + lax.broadcasted_iota(jnp.int32, (tq, tk), 0)
        key = tile * tk + lax.broadcasted_iota(jnp.int32, (tq, tk), 1)
        if window:
            mask = (key <= tok) & (tok - key < WINDOW)
        else:
            nblk = selt_ref.shape[3]
            expand = (lax.broadcasted_iota(jnp.int32, (nblk, tk), 0) ==
                      ((tile * tk + lax.broadcasted_iota(jnp.int32, (nblk, tk), 1)) >> SLC_SHIFT))
            chosen = _dot(selt_ref[0, 0].astype(BF16), jnp.where(expand, 1.0, 0.0).astype(BF16))
            mask = (key <= tok) & (chosen > 0.5)
        k = k_ref[0]
        v = v_ref[0]
        for h in range(nh):
            rows = slice(h * tq, (h + 1) * tq)
            s = jnp.where(mask, _dot_nt(qs_s[rows, :], k), NEG)
            m_prev = m_s[rows, :]
            m_new = jnp.maximum(m_prev, jnp.max(s, axis=-1, keepdims=True))
            p = jnp.where(mask, jnp.exp(s - m_new), 0.0)
            alpha = jnp.exp(m_prev - m_new)
            l_s[rows, :] = alpha * l_s[rows, :] + jnp.sum(p, axis=-1, keepdims=True)
            acc_s[rows, :] = alpha * acc_s[rows, :] + _dot(p.astype(BF16), v)
            m_s[rows, :] = m_new

    @pl.when(kj == nk - 1)
    def _():
        l = l_s[...]
        o = acc_s[...] * jnp.where(l > 0.0, 1.0 / l, 0.0)
        lo = lax.broadcasted_iota(jnp.int32, (tq, LANES), 1) < HEAD_DIM
        for c in range(nh // 2):
            o_ref[0, :, c * LANES:(c + 1) * LANES] = jnp.where(
                lo, o[(2 * c) * tq:(2 * c + 1) * tq], o[(2 * c + 1) * tq:(2 * c + 2) * tq])


def _flash(nq, kd, vd, selt, *, window, tq, tk):
    bsz, seq, _ = nq.shape
    g = NSA_KV_GROUPS
    gw = HEADS_PER_GROUP * HEAD_DIM
    if window:
        assert tq == tk and WINDOW % tk == 0
        nk = WINDOW // tk + 1
        kmap = lambda b, gi, i, j: (b, jnp.maximum(i - (nk - 1) + j, 0), gi)
    else:
        nk = seq // tk
        kmap = lambda b, gi, i, j: (b, jnp.minimum(j, (i * tq + tq - 1) // tk), gi)
    qspec = pl.BlockSpec((1, tq, gw), lambda b, gi, i, j: (b, i, gi))
    in_specs = [qspec, pl.BlockSpec((1, tk, LANES), kmap), pl.BlockSpec((1, tk, LANES), kmap)]
    args = [nq, kd, vd]
    if not window:
        nblk = selt.shape[3]
        in_specs.append(pl.BlockSpec((1, 1, tq, nblk), lambda b, gi, i, j: (b, gi, i, 0)))
        args.append(selt)
    rows = HEADS_PER_GROUP * tq
    return pl.pallas_call(
        functools.partial(_flash_kernel, window=window, tq=tq, tk=tk, nk=nk),
        grid=(bsz, g, seq // tq, nk),
        in_specs=in_specs,
        out_specs=qspec,
        out_shape=jax.ShapeDtypeStruct((bsz, seq, NSA_WIDTH), F32),
        scratch_shapes=[pltpu.VMEM((rows, LANES), BF16), pltpu.VMEM((rows, 1), F32),
                        pltpu.VMEM((rows, 1), F32), pltpu.VMEM((rows, LANES), F32)],
        compiler_params=_params("parallel", "parallel", "parallel", "arbitrary"),
        name="win_attn" if window else "slc_attn",
    )(*args)


def _out_kernel(x_ref, gate_ref, gpost_ref, ret_ref, oc_ref, os_ref, ow_ref, gl_ref, ngs_ref,
                wo_ref, o_ref):
    ts = x_ref.shape[1]
    gates = gl_ref[0]
    g_hi = gates.astype(BF16)
    g_lo = (gates - g_hi.astype(F32)).astype(BF16)
    src = lax.broadcasted_iota(jnp.int32, (LANES, NSA_WIDTH), 0)
    head = lax.broadcasted_iota(jnp.int32, (LANES, NSA_WIDTH), 1) >> HEAD_SHIFT
    att = jnp.zeros((ts, NSA_WIDTH), F32)
    for i, br in enumerate((oc_ref, os_ref, ow_ref)):
        expand = jnp.where(src == 3 * head + i, 1.0, 0.0).astype(BF16)
        att = att + (_dot(g_hi, expand) + _dot(g_lo, expand)) * br[0]
    att = att * ngs_ref[0]
    y = _dot(ret_ref[0].astype(BF16), wo_ref[:RET_WIDTH, :]) + _dot(att.astype(BF16), wo_ref[RET_WIDTH:, :])
    ms = jnp.mean(y * y, axis=-1, keepdims=True)
    y = y * lax.rsqrt(ms + EPS) * gpost_ref[...]
    o_ref[0] = x_ref[0] + gate_ref[0] * y


def _out(x, gate, g_post, ret, o_cmp, o_slc, o_win, gl, ngs, w_out, ts):
    bsz, seq, d = x.shape
    row = lambda w: pl.BlockSpec((1, ts, w), lambda b, s: (b, s, 0))
    return pl.pallas_call(
        _out_kernel,
        grid=(bsz, seq // ts),
        in_specs=[row(d), pl.BlockSpec((1, 1, d), lambda b, s: (b, 0, 0)),
                  pl.BlockSpec((1, d), lambda b, s: (0, 0)),
                  row(RET_WIDTH), row(NSA_WIDTH), row(NSA_WIDTH), row(NSA_WIDTH), row(LANES),
                  row(NSA_WIDTH), pl.BlockSpec(w_out.shape, lambda b, s: (0, 0))],
        out_specs=row(d),
        out_shape=jax.ShapeDtypeStruct((bsz, seq, d), x.dtype),
        compiler_params=_params("parallel", "arbitrary"),
        name="out",
    )(x, gate, g_post, ret, o_cmp, o_slc, o_win, gl, ngs, w_out)


def _retention_tables():
    h = RET_HEADS
    c = RET_CHUNK
    log_g = jnp.log1p(-jnp.power(2.0, -5.0 - jnp.arange(h, dtype=F32)))
    idx = jnp.arange(c, dtype=F32)
    diff = idx[:, None] - idx[None, :]
    dec = jnp.where(diff[None] >= 0, jnp.exp(jnp.maximum(diff, 0.0)[None] * log_g[:, None, None]), 0.0)
    zeta = jnp.exp((c - 1 - idx)[None, :] * log_g[:, None])
    xi = jnp.exp((idx + 1.0)[None, :] * log_g[:, None])
    cd = jnp.exp(c * log_g)
    to_lanes = lambda t: jnp.repeat(t.reshape(h // 2, 2, c).transpose(0, 2, 1), HEAD_DIM, axis=2)
    cd_l = jnp.repeat(cd.reshape(h // 2, 1, 2), HEAD_DIM, axis=2)
    return dec, to_lanes(zeta), to_lanes(xi), cd_l


def _overlap_t(seq):
    ncp = seq // CMP_STRIDE
    nc = (seq - CMP_BLOCK) // CMP_STRIDE + 1
    ns = seq // SLC_BLOCK
    cs = np.arange(ncp) * CMP_STRIDE
    ce = cs + CMP_BLOCK - 1
    ss = np.arange(ns) * SLC_BLOCK
    ov = (cs[None, :] <= ss[:, None] + SLC_BLOCK - 1) & (ce[None, :] >= ss[:, None]) & (np.arange(ncp) < nc)[None, :]
    return jnp.asarray(ov.astype(np.float32), dtype=BF16)


def _layer(x, c, positions, w_ada, b_ada, g_pre, g_post, w_in, w_out,
           pe_k, w1_k, w2_k, pe_v, w1_v, w2_v):
    bsz, seq, d = x.shape
    mod = _ada(c, w_ada, b_ada)
    shift, scale, gate = [t.reshape(bsz, 1, d) for t in jnp.split(mod, 3, axis=-1)]

    offs = np.cumsum((RET_WIDTH,) * 4 + (NSA_WIDTH, 6 * KV_WIDTH, 3 * NSA_HEADS, NSA_WIDTH))
    w_gl = w_in[:, offs[5]:offs[6]]
    w_all = jnp.concatenate(
        [w_in[:, :offs[5]], w_in[:, offs[6]:], w_gl,
         jnp.zeros((d, LANES - w_gl.shape[1]), w_in.dtype)], axis=1).astype(BF16)

    lanes = jnp.arange(LANES)
    half_r = HEAD_DIM // 2
    inv_r = jnp.power(RET_ROPE_BASE, -jnp.arange(half_r, dtype=F32) / half_r)
    half_n = ROPE_DIM // 2
    inv_n = jnp.power(NSA_ROPE_BASE, -jnp.arange(half_n, dtype=F32) / half_n)
    inv_ret_row = inv_r[lanes % half_r].reshape(1, LANES)
    inv_nsa_row = inv_n[lanes % half_n].reshape(1, LANES)
    pos3 = positions.reshape(bsz, seq, 1)

    (rq, rk, rv, rgs, nq, cmps, ksd, vsd, kwd, vwd, ngs, gl) = _proj(
        x, shift, scale, g_pre.reshape(1, d), pos3, inv_ret_row, inv_nsa_row, w_all, ts=256)

    dec, zeta_l, xi_l, cd_l = _retention_tables()
    ret = _retention(rq, rk, rv, rgs, dec, zeta_l, xi_l, cd_l, rows=min(512, seq))

    ncp = seq // CMP_STRIDE
    nc = (seq - CMP_BLOCK) // CMP_STRIDE + 1
    pad_l = lambda t: jnp.concatenate([t, jnp.zeros_like(t)], axis=-1)
    pe2 = jnp.stack([pad_l(pe_k), pad_l(pe_v)])
    w1r = jnp.stack([w1_k, w1_v]).reshape(2, CMP_BLOCK, HEAD_DIM, CMP_HIDDEN)
    w1p = jnp.concatenate([w1r, jnp.zeros_like(w1r)], axis=2).astype(BF16)
    w2s = jnp.stack([w2_k, w2_v])
    w2d = jnp.concatenate([w2s, w2s], axis=-1).astype(BF16)
    cmp_end = np.arange(nc) * CMP_STRIDE + CMP_BLOCK - 1
    pos_cmp = jnp.pad(positions[:, cmp_end], ((0, 0), (0, ncp - nc))).reshape(bsz, ncp, 1)
    cmpkv = _compress(cmps, pe2, w1p, w2d, pos_cmp, inv_nsa_row)

    o_cmp, selt = _cmpsel(nq, cmpkv, _overlap_t(seq), tq=min(256, seq))
    o_slc = _flash(nq, ksd, vsd, selt, window=False, tq=128, tk=256)
    o_win = _flash(nq, kwd, vwd, None, window=True, tq=128, tk=128)

    return _out(x, gate, g_post.reshape(1, d), ret, o_cmp, o_slc, o_win, gl, ngs,
                w_out.astype(BF16), ts=256)


def kernel(x, c, positions, w_ada, b_ada, g_pre, g_post, w_in, w_out, cmp_pe_k, cmp_w1_k, cmp_w2_k, cmp_pe_v, cmp_w1_v, cmp_w2_v):
    for l in range(w_in.shape[0]):
        x = _layer(x, c, positions, w_ada[l], b_ada[l], g_pre[l], g_post[l], w_in[l], w_out[l],
                   cmp_pe_k[l], cmp_w1_k[l], cmp_w2_k[l], cmp_pe_v[l], cmp_w1_v[l], cmp_w2_v[l])
    return x
```

```python
import functools

import numpy as np
import jax
import jax.numpy as jnp
from jax import lax
from jax.experimental import pallas as pl
from jax.experimental.pallas import tpu as pltpu

F32 = jnp.float32
BF16 = jnp.bfloat16

LANES = 128
HEAD_DIM = 64
RET_HEADS = 8
NSA_HEADS = 8
NSA_KV_GROUPS = 2
HEADS_PER_GROUP = NSA_HEADS // NSA_KV_GROUPS
RET_WIDTH = RET_HEADS * HEAD_DIM
NSA_WIDTH = NSA_HEADS * HEAD_DIM
KV_WIDTH = NSA_KV_GROUPS * HEAD_DIM
RET_CHUNK = 128
RET_ROPE_BASE = 10000.0
NSA_ROPE_BASE = 500000.0
ROPE_DIM = HEAD_DIM // 4
CMP_BLOCK = 32
CMP_STRIDE = 16
CMP_HIDDEN = 256
SLC_BLOCK = 64
SLC_TOPK = 16
WINDOW = 512
NEG = -1e30
BIG = 1e9
EPS = 1e-6
GN_EPS = 1e-5
VMEM_LIMIT = 48 * 1024 * 1024

NT_DIMS = (((1,), (1,)), ((), ()))
TN_DIMS = (((0,), (0,)), ((), ()))


def _dot(a, b):
    return jnp.dot(a, b, preferred_element_type=F32)


def _dot_nt(a, b):
    return lax.dot_general(a, b, NT_DIMS, preferred_element_type=F32)


def _dot_tn(a, b):
    return lax.dot_general(a, b, TN_DIMS, preferred_element_type=F32)


def _silu(z):
    return z * jax.nn.sigmoid(z)


def _params(*sem):
    return pltpu.CompilerParams(dimension_semantics=sem, vmem_limit_bytes=VMEM_LIMIT)


HEAD_SHIFT = 6
SLC_SHIFT = 6
assert 1 << HEAD_SHIFT == HEAD_DIM and 1 << SLC_SHIFT == SLC_BLOCK


def _lane_in_head(shape):
    return lax.broadcasted_iota(jnp.int32, shape, len(shape) - 1) & (HEAD_DIM - 1)


def _ada_kernel(c_ref, w_ref, b_ref, o_ref):
    a = _silu(c_ref[...])
    o_ref[...] = jnp.dot(a, w_ref[...], precision=lax.Precision.HIGHEST,
                         preferred_element_type=F32) + b_ref[...]


def _ada(c, w, b):
    bsz, d = c.shape
    n = w.shape[1]
    tn = 1024
    return pl.pallas_call(
        _ada_kernel,
        grid=(n // tn,),
        in_specs=[pl.BlockSpec((bsz, d), lambda j: (0, 0)),
                  pl.BlockSpec((d, tn), lambda j: (0, j)),
                  pl.BlockSpec((1, tn), lambda j: (0, j))],
        out_specs=pl.BlockSpec((bsz, tn), lambda j: (0, j)),
        out_shape=jax.ShapeDtypeStruct((bsz, n), F32),
        compiler_params=_params("arbitrary"),
        name="ada",
    )(c, w, b.reshape(1, n))


_O_RQ, _O_RK, _O_RV, _O_RG, _O_NQ = 0, 512, 1024, 1536, 2048
_O_KV = 2560
_O_NG = 3328
_O_GL = 3840
_PROJ_COLS = 3968


def _rot_ret(z, cos, sin_signed, m):
    partner = jnp.where(m < HEAD_DIM // 2, pltpu.roll(z, LANES - HEAD_DIM // 2, 1),
                        pltpu.roll(z, HEAD_DIM // 2, 1))
    return z * cos + partner * sin_signed


def _rot_nsa(z, cos, sin_signed, m):
    half = ROPE_DIM // 2
    partner = jnp.where(m < half, pltpu.roll(z, LANES - half, 1), pltpu.roll(z, half, 1))
    return z * cos + partner * sin_signed


def _nsa_tables(posf, inv_row, m):
    half = ROPE_DIM // 2
    ang = posf * inv_row
    cos = jnp.where(m < ROPE_DIM, jnp.cos(ang), 1.0)
    s = jnp.sin(ang)
    sin = jnp.where(m < half, -s, jnp.where(m < ROPE_DIM, s, 0.0))
    return cos, sin


def _dup_groups(z, m_lane):
    r = pltpu.roll(z, HEAD_DIM, 1)
    lo = m_lane < HEAD_DIM
    return jnp.where(lo, z, r), jnp.where(lo, r, z)


def _proj_kernel(x_ref, shift_ref, scale_ref, gpre_ref, pos_ref, invr_ref, invn_ref, w_ref,
                 rq_ref, rk_ref, rv_ref, rgs_ref, nq_ref, cmps_ref, ksd_ref, vsd_ref,
                 kwd_ref, vwd_ref, ngs_ref, gl_ref,
                 cr_s, sr_s, cn_s, sn_s):
    ts = x_ref.shape[1]
    x = x_ref[0]
    ms = jnp.mean(x * x, axis=-1, keepdims=True)
    y = x * lax.rsqrt(ms + EPS) * gpre_ref[...]
    h = y * (1.0 + scale_ref[0]) + shift_ref[0]
    hb = h.astype(BF16)

    posf = pos_ref[0].astype(F32)
    lane = lax.broadcasted_iota(jnp.int32, (ts, LANES), 1)
    m = lane & (HEAD_DIM - 1)
    ang = posf * invr_ref[...]
    cr_s[...] = jnp.cos(ang)
    s = jnp.sin(ang)
    sr_s[...] = jnp.where(m < HEAD_DIM // 2, -s, s)
    cn, sn = _nsa_tables(posf, invn_ref[...], m)
    cn_s[...] = cn
    sn_s[...] = sn

    def mm(off, width=LANES):
        return _dot(hb, w_ref[:, off:off + width])

    scale_q = HEAD_DIM ** -0.5
    for c in range(RET_WIDTH // LANES):
        sl = slice(c * LANES, (c + 1) * LANES)
        rq_ref[0, :, sl] = (_rot_ret(mm(_O_RQ + c * LANES), cr_s[...], sr_s[...], m) * scale_q).astype(BF16)
        rk_ref[0, :, sl] = _rot_ret(mm(_O_RK + c * LANES), cr_s[...], sr_s[...], m).astype(BF16)
        rv_ref[0, :, sl] = mm(_O_RV + c * LANES).astype(BF16)
        rgs_ref[0, :, sl] = _silu(mm(_O_RG + c * LANES))
        nq_ref[0, :, sl] = (_rot_nsa(mm(_O_NQ + c * LANES), cn_s[...], sn_s[...], m) * scale_q).astype(BF16)
        ngs_ref[0, :, sl] = _silu(mm(_O_NG + c * LANES))

    kc0, kc1 = _dup_groups(mm(_O_KV), lane)
    vc0, vc1 = _dup_groups(mm(_O_KV + LANES), lane)
    for i, z in enumerate((kc0, kc1, vc0, vc1)):
        cmps_ref[0, :, i * LANES:(i + 1) * LANES] = z
    ks0, ks1 = _dup_groups(_rot_nsa(mm(_O_KV + 2 * LANES), cn_s[...], sn_s[...], m), lane)
    ksd_ref[0, :, :LANES] = ks0.astype(BF16)
    ksd_ref[0, :, LANES:] = ks1.astype(BF16)
    vs0, vs1 = _dup_groups(mm(_O_KV + 3 * LANES), lane)
    vsd_ref[0, :, :LANES] = vs0.astype(BF16)
    vsd_ref[0, :, LANES:] = vs1.astype(BF16)
    kw0, kw1 = _dup_groups(_rot_nsa(mm(_O_KV + 4 * LANES), cn_s[...], sn_s[...], m), lane)
    kwd_ref[0, :, :LANES] = kw0.astype(BF16)
    kwd_ref[0, :, LANES:] = kw1.astype(BF16)
    vw0, vw1 = _dup_groups(mm(_O_KV + 5 * LANES), lane)
    vwd_ref[0, :, :LANES] = vw0.astype(BF16)
    vwd_ref[0, :, LANES:] = vw1.astype(BF16)
    gl_ref[0] = jax.nn.sigmoid(mm(_O_GL))


def _proj(x, shift, scale, g_pre, pos3, inv_ret_row, inv_nsa_row, w_all, ts):
    bsz, seq, d = x.shape
    row = lambda w: pl.BlockSpec((1, ts, w), lambda b, s: (b, s, 0))
    vec = pl.BlockSpec((1, 1, d), lambda b, s: (b, 0, 0))
    const2 = lambda shp: pl.BlockSpec(shp, lambda b, s: (0, 0))
    outs = [(RET_WIDTH, BF16), (RET_WIDTH, BF16), (RET_WIDTH, BF16), (RET_WIDTH, F32),
            (NSA_WIDTH, BF16), (4 * LANES, F32), (2 * LANES, BF16), (2 * LANES, BF16),
            (2 * LANES, BF16), (2 * LANES, BF16), (NSA_WIDTH, F32), (LANES, F32)]
    return pl.pallas_call(
        _proj_kernel,
        grid=(bsz, seq // ts),
        in_specs=[row(d), vec, vec, const2((1, d)), row(1), const2((1, LANES)), const2((1, LANES)),
                  const2((d, _PROJ_COLS))],
        out_specs=[row(w) for w, _ in outs],
        out_shape=[jax.ShapeDtypeStruct((bsz, seq, w), dt) for w, dt in outs],
        scratch_shapes=[pltpu.VMEM((ts, LANES), F32)] * 4,
        compiler_params=_params("parallel", "arbitrary"),
        name="proj",
    )(x, shift, scale, g_pre, pos3, inv_ret_row, inv_nsa_row, w_all)


def _ret_kernel(q_ref, k_ref, v_ref, g_ref, dec_ref, zeta_ref, xi_ref, cd_ref, o_ref, r_s):
    rows = q_ref.shape[1]
    cc = RET_CHUNK

    @pl.when(pl.program_id(2) == 0)
    def _():
        r_s[...] = jnp.zeros_like(r_s)

    lane = lax.broadcasted_iota(jnp.int32, (cc, LANES), 1)
    rowi = lax.broadcasted_iota(jnp.int32, (cc, LANES), 0)
    lo = lane < HEAD_DIM
    blockdiag = (rowi < HEAD_DIM) == lo
    inv_n = 1.0 / HEAD_DIM
    for c in range(rows // cc):
        sl = slice(c * cc, (c + 1) * cc)
        q = q_ref[0, sl, :]
        k = k_ref[0, sl, :]
        v = v_ref[0, sl, :]
        zero = jnp.zeros_like(q)
        s_a = _dot_nt(jnp.where(lo, q, zero), k) * dec_ref[0]
        s_b = _dot_nt(jnp.where(lo, zero, q), k) * dec_ref[1]
        inner = jnp.where(lo, _dot(s_a.astype(BF16), v), _dot(s_b.astype(BF16), v))
        r_prev = r_s[...]
        cross = _dot(q, r_prev.astype(BF16)) * xi_ref[0]
        o = inner + cross
        kz = (k.astype(F32) * zeta_ref[0]).astype(BF16)
        kv = _dot_tn(kz, v)
        r_s[...] = r_prev * cd_ref[0] + jnp.where(blockdiag, kv, 0.0)
        sum_a = jnp.sum(jnp.where(lo, o, 0.0), axis=-1, keepdims=True)
        sum_b = jnp.sum(jnp.where(lo, 0.0, o), axis=-1, keepdims=True)
        d = o - jnp.where(lo, sum_a, sum_b) * inv_n
        d2 = d * d
        var_a = jnp.sum(jnp.where(lo, d2, 0.0), axis=-1, keepdims=True)
        var_b = jnp.sum(jnp.where(lo, 0.0, d2), axis=-1, keepdims=True)
        var = jnp.where(lo, var_a, var_b) * inv_n
        o_ref[0, sl, :] = d * lax.rsqrt(var + GN_EPS) * g_ref[0, sl, :]


def _retention(rq, rk, rv, rgs, dec, zeta_l, xi_l, cd_l, rows):
    bsz, seq, _ = rq.shape
    npair = RET_HEADS // 2
    blk = pl.BlockSpec((1, rows, LANES), lambda b, p, s: (b, s, p))
    return pl.pallas_call(
        _ret_kernel,
        grid=(bsz, npair, seq // rows),
        in_specs=[blk, blk, blk, blk,
                  pl.BlockSpec((2, RET_CHUNK, RET_CHUNK), lambda b, p, s: (p, 0, 0)),
                  pl.BlockSpec((1, RET_CHUNK, LANES), lambda b, p, s: (p, 0, 0)),
                  pl.BlockSpec((1, RET_CHUNK, LANES), lambda b, p, s: (p, 0, 0)),
                  pl.BlockSpec((1, 1, LANES), lambda b, p, s: (p, 0, 0))],
        out_specs=blk,
        out_shape=jax.ShapeDtypeStruct((bsz, seq, RET_WIDTH), F32),
        scratch_shapes=[pltpu.VMEM((LANES, LANES), F32)],
        compiler_params=_params("parallel", "parallel", "arbitrary"),
        name="retention",
    )(rq, rk, rv, rgs, dec, zeta_l, xi_l, cd_l)


def _cmp_kernel(src_ref, pe_ref, w1_ref, w2_ref, pos_ref, invn_ref, o_ref):
    ncp = o_ref.shape[3]
    half = CMP_BLOCK // 2
    p_lo = jnp.zeros((ncp, CMP_HIDDEN), F32)
    p_hi = jnp.zeros((ncp, CMP_HIDDEN), F32)
    for l in range(half):
        t = src_ref[0, pl.ds(l, ncp, stride=CMP_STRIDE), :]
        p_lo = p_lo + _dot((t + pe_ref[0, l:l + 1, :]).astype(BF16), w1_ref[0, l])
        p_hi = p_hi + _dot((t + pe_ref[0, half + l:half + l + 1, :]).astype(BF16), w1_ref[0, half + l])
    hidden = p_lo + pltpu.roll(p_hi, ncp - 1, 0)
    comp = _dot(_silu(hidden).astype(BF16), w2_ref[0])
    m = _lane_in_head(comp.shape)
    cos, sin = _nsa_tables(pos_ref[0].astype(F32), invn_ref[...], m)
    rotated = _rot_nsa(comp, cos, sin, m)
    is_key = pl.program_id(1) == 0
    o_ref[0, 0, 0] = jnp.where(is_key, rotated, comp).astype(BF16)


def _compress(cmps, pe2, w1p, w2d, pos_cmp, inv_nsa_row):
    bsz, seq, _ = cmps.shape
    ncp = seq // CMP_STRIDE
    g = NSA_KV_GROUPS
    return pl.pallas_call(
        _cmp_kernel,
        grid=(bsz, 2, g),
        in_specs=[pl.BlockSpec((1, seq, LANES), lambda b, j, gi: (b, 0, j * g + gi)),
                  pl.BlockSpec((1, CMP_BLOCK, LANES), lambda b, j, gi: (j, 0, 0)),
                  pl.BlockSpec((1, CMP_BLOCK, LANES, CMP_HIDDEN), lambda b, j, gi: (j, 0, 0, 0)),
                  pl.BlockSpec((1, CMP_HIDDEN, LANES), lambda b, j, gi: (j, 0, 0)),
                  pl.BlockSpec((1, ncp, 1), lambda b, j, gi: (b, 0, 0)),
                  pl.BlockSpec((1, LANES), lambda b, j, gi: (0, 0))],
        out_specs=pl.BlockSpec((1, 1, 1, ncp, LANES), lambda b, j, gi: (b, j, gi, 0, 0)),
        out_shape=jax.ShapeDtypeStruct((bsz, 2, g, ncp, LANES), BF16),
        compiler_params=_params("parallel", "arbitrary", "arbitrary"),
        name="compress",
    )(cmps, pe2, w1p, w2d, pos_cmp, inv_nsa_row)


def _cmpsel_kernel(q_ref, kc_ref, vc_ref, ovt_ref, o_ref, selt_ref):
    tq = q_ref.shape[1]
    ncp = kc_ref.shape[3]
    nblk = ovt_ref.shape[0]
    t0 = pl.program_id(2) * tq
    kc = kc_ref[0, 0, 0]
    vc = vc_ref[0, 0, 0]

    tok = t0 + lax.broadcasted_iota(jnp.int32, (tq, ncp), 0)
    cend = lax.broadcasted_iota(jnp.int32, (tq, ncp), 1) * CMP_STRIDE + (CMP_BLOCK - 1)
    mask = cend <= tok
    lo = lax.broadcasted_iota(jnp.int32, (tq, LANES), 1) < HEAD_DIM
    psum = jnp.zeros((tq, ncp), F32)
    for c in range(HEADS_PER_GROUP // 2):
        q = q_ref[0, :, c * LANES:(c + 1) * LANES]
        zero = jnp.zeros_like(q)
        outs = []
        for qm in (jnp.where(lo, q, zero), jnp.where(lo, zero, q)):
            s = jnp.where(mask, _dot_nt(qm, kc), NEG)
            mx = jnp.max(s, axis=-1, keepdims=True)
            e = jnp.where(mask, jnp.exp(s - mx), 0.0)
            den = jnp.sum(e, axis=-1, keepdims=True)
            p = e * jnp.where(den > 0.0, 1.0 / den, 0.0)
            psum = psum + p
            outs.append(_dot(p.astype(BF16), vc))
        o_ref[0, :, c * LANES:(c + 1) * LANES] = jnp.where(lo, outs[0], outs[1])

    p_hi = psum.astype(BF16)
    p_lo = (psum - p_hi.astype(F32)).astype(BF16)
    imp = _dot_nt(ovt_ref[...], p_hi) + _dot_nt(ovt_ref[...], p_lo)
    blk = lax.broadcasted_iota(jnp.int32, (nblk, tq), 0)
    cur = (t0 + lax.broadcasted_iota(jnp.int32, (nblk, tq), 1)) >> SLC_SHIFT
    forced = (blk == 0) | (blk == cur) | (blk == cur - 1)
    score = jnp.where(forced, BIG, jnp.where(blk <= cur, imp, NEG))
    rank = jnp.zeros((nblk, tq), F32)
    for i in range(nblk):
        ri = score[i:i + 1, :]
        beats = (ri > score) | ((ri == score) & (blk > i))
        rank = rank + jnp.where(beats, 1.0, 0.0)
    sel = jnp.where((rank < float(min(SLC_TOPK, nblk))) & (score > 0.5 * NEG), 1.0, 0.0)
    selt_ref[0, 0] = sel.T


def _cmpsel(nq, cmpkv, ovt, tq):
    bsz, seq, _ = nq.shape
    g = NSA_KV_GROUPS
    ncp = cmpkv.shape[3]
    nblk = ovt.shape[0]
    gw = HEADS_PER_GROUP * HEAD_DIM
    return pl.pallas_call(
        _cmpsel_kernel,
        grid=(bsz, g, seq // tq),
        in_specs=[pl.BlockSpec((1, tq, gw), lambda b, gi, i: (b, i, gi)),
                  pl.BlockSpec((1, 1, 1, ncp, LANES), lambda b, gi, i: (b, 0, gi, 0, 0)),
                  pl.BlockSpec((1, 1, 1, ncp, LANES), lambda b, gi, i: (b, 1, gi, 0, 0)),
                  pl.BlockSpec((nblk, ncp), lambda b, gi, i: (0, 0))],
        out_specs=[pl.BlockSpec((1, tq, gw), lambda b, gi, i: (b, i, gi)),
                   pl.BlockSpec((1, 1, tq, nblk), lambda b, gi, i: (b, gi, i, 0))],
        out_shape=[jax.ShapeDtypeStruct((bsz, seq, NSA_WIDTH), F32),
                   jax.ShapeDtypeStruct((bsz, g, seq, nblk), F32)],
        compiler_params=_params("parallel", "parallel", "arbitrary"),
        name="cmpsel",
    )(nq, cmpkv, cmpkv, ovt)


def _head_queries(q_ref, tq):
    lo = lax.broadcasted_iota(jnp.int32, (tq, LANES), 1) < HEAD_DIM
    qs = []
    for c in range(HEADS_PER_GROUP // 2):
        q = q_ref[0, :, c * LANES:(c + 1) * LANES]
        zero = jnp.zeros_like(q)
        qs += [jnp.where(lo, q, zero), jnp.where(lo, zero, q)]
    return qs


def _store_heads(o_ref, outs, tq):
    lo = lax.broadcasted_iota(jnp.int32, (tq, LANES), 1) < HEAD_DIM
    for c in range(HEADS_PER_GROUP // 2):
        o_ref[0, :, c * LANES:(c + 1) * LANES] = jnp.where(lo, outs[2 * c], outs[2 * c + 1])


def _win_kernel(q_ref, k_ref, v_ref, o_ref, *, tq):
    span = WINDOW + tq
    t0 = pl.program_id(2) * tq
    start = pl.multiple_of(jnp.maximum(t0 - WINDOW, 0), tq)
    k = k_ref[0, pl.ds(start, span), :]
    v = v_ref[0, pl.ds(start, span), :]
    tok = t0 + lax.broadcasted_iota(jnp.int32, (tq, span), 0)
    key = start + lax.broadcasted_iota(jnp.int32, (tq, span), 1)
    bias = jnp.where((key <= tok) & (tok - key < WINDOW), 0.0, NEG)
    outs = []
    for q in _head_queries(q_ref, tq):
        s = _dot_nt(q, k) + bias
        p = jnp.exp(s - jnp.max(s, axis=-1, keepdims=True))
        l = jnp.sum(p, axis=-1, keepdims=True)
        outs.append(_dot(p.astype(BF16), v) * (1.0 / l))
    _store_heads(o_ref, outs, tq)


def _slc_kernel(q_ref, k_ref, v_ref, selt_ref, o_ref, m_s, l_s, acc_s, *, tq, tk):
    nh = HEADS_PER_GROUP
    t0 = pl.program_id(2) * tq
    nblk = selt_ref.shape[3]
    qs = _head_queries(q_ref, tq)
    selt = selt_ref[0, 0].astype(BF16)
    m_s[...] = jnp.full_like(m_s, NEG)
    l_s[...] = jnp.zeros_like(l_s)
    acc_s[...] = jnp.zeros_like(acc_s)
    tok = t0 + lax.broadcasted_iota(jnp.int32, (tq, tk), 0)

    def body(j, carry):
        k0 = pl.multiple_of(j * tk, tk)
        k = k_ref[0, pl.ds(k0, tk), :]
        v = v_ref[0, pl.ds(k0, tk), :]
        key = k0 + lax.broadcasted_iota(jnp.int32, (tq, tk), 1)
        expand = (lax.broadcasted_iota(jnp.int32, (nblk, tk), 0) ==
                  ((k0 + lax.broadcasted_iota(jnp.int32, (nblk, tk), 1)) >> SLC_SHIFT))
        chosen = _dot(selt, jnp.where(expand, 1.0, 0.0).astype(BF16))
        bias = jnp.where((key <= tok) & (chosen > 0.5), 0.0, NEG)
        for h in range(nh):
            s = _dot_nt(qs[h], k) + bias
            m_prev = m_s[h]
            m_new = jnp.maximum(m_prev, jnp.max(s, axis=-1, keepdims=True))
            p = jnp.exp(s - m_new)
            alpha = jnp.exp(m_prev - m_new)
            l_s[h] = alpha * l_s[h] + jnp.sum(p, axis=-1, keepdims=True)
            acc_s[h] = alpha * acc_s[h] + _dot(p.astype(BF16), v)
            m_s[h] = m_new
        return carry

    lax.fori_loop(0, (t0 + tq - 1) // tk + 1, body, 0)
    _store_heads(o_ref, [acc_s[h] * (1.0 / l_s[h]) for h in range(nh)], tq)


def _flash(nq, kd, vd, selt, *, window, tq, tk=None):
    bsz, seq, _ = nq.shape
    g = NSA_KV_GROUPS
    gw = HEADS_PER_GROUP * HEAD_DIM
    nh = HEADS_PER_GROUP
    qspec = pl.BlockSpec((1, tq, gw), lambda b, gi, i: (b, i, gi))
    kvspec = pl.BlockSpec((1, seq, LANES), lambda b, gi, i: (b, 0, gi))
    in_specs = [qspec, kvspec, kvspec]
    args = [nq, kd, vd]
    if window:
        assert WINDOW % tq == 0 and seq >= WINDOW + tq
        body = functools.partial(_win_kernel, tq=tq)
        scratch = []
    else:
        assert seq % tk == 0
        nblk = selt.shape[3]
        in_specs.append(pl.BlockSpec((1, 1, tq, nblk), lambda b, gi, i: (b, gi, i, 0)))
        args.append(selt)
        body = functools.partial(_slc_kernel, tq=tq, tk=tk)
        scratch = [pltpu.VMEM((nh, tq, 1), F32), pltpu.VMEM((nh, tq, 1), F32),
                   pltpu.VMEM((nh, tq, LANES), F32)]
    return pl.pallas_call(
        body,
        grid=(bsz, g, seq // tq),
        in_specs=in_specs,
        out_specs=qspec,
        out_shape=jax.ShapeDtypeStruct((bsz, seq, NSA_WIDTH), F32),
        scratch_shapes=scratch,
        compiler_params=_params("parallel", "parallel", "arbitrary"),
        name="win_attn" if window else "slc_attn",
    )(*args)


def _out_kernel(x_ref, gate_ref, gpost_ref, ret_ref, oc_ref, os_ref, ow_ref, gl_ref, ngs_ref,
                wo_ref, o_ref):
    ts = x_ref.shape[1]
    gates = gl_ref[0]
    g_hi = gates.astype(BF16)
    g_lo = (gates - g_hi.astype(F32)).astype(BF16)
    src = lax.broadcasted_iota(jnp.int32, (LANES, NSA_WIDTH), 0)
    head = lax.broadcasted_iota(jnp.int32, (LANES, NSA_WIDTH), 1) >> HEAD_SHIFT
    att = jnp.zeros((ts, NSA_WIDTH), F32)
    for i, br in enumerate((oc_ref, os_ref, ow_ref)):
        expand = jnp.where(src == 3 * head + i, 1.0, 0.0).astype(BF16)
        att = att + (_dot(g_hi, expand) + _dot(g_lo, expand)) * br[0]
    att = att * ngs_ref[0]
    y = _dot(ret_ref[0].astype(BF16), wo_ref[:RET_WIDTH, :]) + _dot(att.astype(BF16), wo_ref[RET_WIDTH:, :])
    ms = jnp.mean(y * y, axis=-1, keepdims=True)
    y = y * lax.rsqrt(ms + EPS) * gpost_ref[...]
    o_ref[0] = x_ref[0] + gate_ref[0] * y


def _out(x, gate, g_post, ret, o_cmp, o_slc, o_win, gl, ngs, w_out, ts):
    bsz, seq, d = x.shape
    row = lambda w: pl.BlockSpec((1, ts, w), lambda b, s: (b, s, 0))
    return pl.pallas_call(
        _out_kernel,
        grid=(bsz, seq // ts),
        in_specs=[row(d), pl.BlockSpec((1, 1, d), lambda b, s: (b, 0, 0)),
                  pl.BlockSpec((1, d), lambda b, s: (0, 0)),
                  row(RET_WIDTH), row(NSA_WIDTH), row(NSA_WIDTH), row(NSA_WIDTH), row(LANES),
                  row(NSA_WIDTH), pl.BlockSpec(w_out.shape, lambda b, s: (0, 0))],
        out_specs=row(d),
        out_shape=jax.ShapeDtypeStruct((bsz, seq, d), x.dtype),
        compiler_params=_params("parallel", "arbitrary"),
        name="out",
    )(x, gate, g_post, ret, o_cmp, o_slc, o_win, gl, ngs, w_out)


def _retention_tables():
    h = RET_HEADS
    c = RET_CHUNK
    log_g = jnp.log1p(-jnp.power(2.0, -5.0 - jnp.arange(h, dtype=F32)))
    idx = jnp.arange(c, dtype=F32)
    diff = idx[:, None] - idx[None, :]
    dec = jnp.where(diff[None] >= 0, jnp.exp(jnp.maximum(diff, 0.0)[None] * log_g[:, None, None]), 0.0)
    zeta = jnp.exp((c - 1 - idx)[None, :] * log_g[:, None])
    xi = jnp.exp((idx + 1.0)[None, :] * log_g[:, None])
    cd = jnp.exp(c * log_g)
    to_lanes = lambda t: jnp.repeat(t.reshape(h // 2, 2, c).transpose(0, 2, 1), HEAD_DIM, axis=2)
    cd_l = jnp.repeat(cd.reshape(h // 2, 1, 2), HEAD_DIM, axis=2)
    return dec, to_lanes(zeta), to_lanes(xi), cd_l


def _overlap_t(seq):
    ncp = seq // CMP_STRIDE
    nc = (seq - CMP_BLOCK) // CMP_STRIDE + 1
    ns = seq // SLC_BLOCK
    cs = np.arange(ncp) * CMP_STRIDE
    ce = cs + CMP_BLOCK - 1
    ss = np.arange(ns) * SLC_BLOCK
    ov = (cs[None, :] <= ss[:, None] + SLC_BLOCK - 1) & (ce[None, :] >= ss[:, None]) & (np.arange(ncp) < nc)[None, :]
    return jnp.asarray(ov.astype(np.float32), dtype=BF16)


def _layer(x, c, positions, w_ada, b_ada, g_pre, g_post, w_in, w_out,
           pe_k, w1_k, w2_k, pe_v, w1_v, w2_v):
    bsz, seq, d = x.shape
    mod = _ada(c, w_ada, b_ada)
    shift, scale, gate = [t.reshape(bsz, 1, d) for t in jnp.split(mod, 3, axis=-1)]

    offs = np.cumsum((RET_WIDTH,) * 4 + (NSA_WIDTH, 6 * KV_WIDTH, 3 * NSA_HEADS, NSA_WIDTH))
    w_gl = w_in[:, offs[5]:offs[6]]
    w_all = jnp.concatenate(
        [w_in[:, :offs[5]], w_in[:, offs[6]:], w_gl,
         jnp.zeros((d, LANES - w_gl.shape[1]), w_in.dtype)], axis=1).astype(BF16)

    lanes = jnp.arange(LANES)
    half_r = HEAD_DIM // 2
    inv_r = jnp.power(RET_ROPE_BASE, -jnp.arange(half_r, dtype=F32) / half_r)
    half_n = ROPE_DIM // 2
    inv_n = jnp.power(NSA_ROPE_BASE, -jnp.arange(half_n, dtype=F32) / half_n)
    inv_ret_row = inv_r[lanes % half_r].reshape(1, LANES)
    inv_nsa_row = inv_n[lanes % half_n].reshape(1, LANES)
    pos3 = positions.reshape(bsz, seq, 1)

    (rq, rk, rv, rgs, nq, cmps, ksd, vsd, kwd, vwd, ngs, gl) = _proj(
        x, shift, scale, g_pre.reshape(1, d), pos3, inv_ret_row, inv_nsa_row, w_all, ts=256)

    dec, zeta_l, xi_l, cd_l = _retention_tables()
    ret = _retention(rq, rk, rv, rgs, dec, zeta_l, xi_l, cd_l, rows=min(512, seq))

    ncp = seq // CMP_STRIDE
    nc = (seq - CMP_BLOCK) // CMP_STRIDE + 1
    pad_l = lambda t: jnp.concatenate([t, jnp.zeros_like(t)], axis=-1)
    pe2 = jnp.stack([pad_l(pe_k), pad_l(pe_v)])
    w1r = jnp.stack([w1_k, w1_v]).reshape(2, CMP_BLOCK, HEAD_DIM, CMP_HIDDEN)
    w1p = jnp.concatenate([w1r, jnp.zeros_like(w1r)], axis=2).astype(BF16)
    w2s = jnp.stack([w2_k, w2_v])
    w2d = jnp.concatenate([w2s, w2s], axis=-1).astype(BF16)
    cmp_end = np.arange(nc) * CMP_STRIDE + CMP_BLOCK - 1
    pos_cmp = jnp.pad(positions[:, cmp_end], ((0, 0), (0, ncp - nc))).reshape(bsz, ncp, 1)
    cmpkv = _compress(cmps, pe2, w1p, w2d, pos_cmp, inv_nsa_row)

    o_cmp, selt = _cmpsel(nq, cmpkv, _overlap_t(seq), tq=min(256, seq))
    o_slc = _flash(nq, ksd, vsd, selt, window=False, tq=128, tk=512)
    o_win = _flash(nq, kwd, vwd, None, window=True, tq=128)

    return _out(x, gate, g_post.reshape(1, d), ret, o_cmp, o_slc, o_win, gl, ngs,
                w_out.astype(BF16), ts=256)


def kernel(x, c, positions, w_ada, b_ada, g_pre, g_post, w_in, w_out, cmp_pe_k, cmp_w1_k, cmp_w2_k, cmp_pe_v, cmp_w1_v, cmp_w2_v):
    for l in range(w_in.shape[0]):
        x = _layer(x, c, positions, w_ada[l], b_ada[l], g_pre[l], g_post[l], w_in[l], w_out[l],
                   cmp_pe_k[l], cmp_w1_k[l], cmp_w2_k[l], cmp_pe_v[l], cmp_w1_v[l], cmp_w2_v[l])
    return x
```

```python
import functools

import numpy as np
import jax
import jax.numpy as jnp
from jax import lax
from jax.experimental import pallas as pl
from jax.experimental.pallas import tpu as pltpu

F32 = jnp.float32
BF16 = jnp.bfloat16

LANES = 128
HEAD_DIM = 64
RET_HEADS = 8
NSA_HEADS = 8
NSA_KV_GROUPS = 2
HEADS_PER_GROUP = NSA_HEADS // NSA_KV_GROUPS
RET_WIDTH = RET_HEADS * HEAD_DIM
NSA_WIDTH = NSA_HEADS * HEAD_DIM
KV_WIDTH = NSA_KV_GROUPS * HEAD_DIM
RET_CHUNK = 128
RET_ROPE_BASE = 10000.0
NSA_ROPE_BASE = 500000.0
ROPE_DIM = HEAD_DIM // 4
CMP_BLOCK = 32
CMP_STRIDE = 16
CMP_HIDDEN = 256
SLC_BLOCK = 64
SLC_TOPK = 16
WINDOW = 512
NEG = -1e30
BIG = 1e9
EPS = 1e-6
GN_EPS = 1e-5
VMEM_LIMIT = 48 * 1024 * 1024

NT_DIMS = (((1,), (1,)), ((), ()))
TN_DIMS = (((0,), (0,)), ((), ()))


def _dot(a, b):
    return jnp.dot(a, b, preferred_element_type=F32)


def _dot_nt(a, b):
    return lax.dot_general(a, b, NT_DIMS, preferred_element_type=F32)


def _dot_tn(a, b):
    return lax.dot_general(a, b, TN_DIMS, preferred_element_type=F32)


def _silu(z):
    return z * jax.nn.sigmoid(z)


def _params(*sem):
    return pltpu.CompilerParams(dimension_semantics=sem, vmem_limit_bytes=VMEM_LIMIT)


HEAD_SHIFT = 6
SLC_SHIFT = 6
assert 1 << HEAD_SHIFT == HEAD_DIM and 1 << SLC_SHIFT == SLC_BLOCK


def _lane_in_head(shape):
    return lax.broadcasted_iota(jnp.int32, shape, len(shape) - 1) & (HEAD_DIM - 1)


def _ada_kernel(c_ref, w_ref, b_ref, o_ref):
    a = _silu(c_ref[...])
    o_ref[...] = jnp.dot(a, w_ref[...], precision=lax.Precision.HIGHEST,
                         preferred_element_type=F32) + b_ref[...]


def _ada(c, w, b):
    bsz, d = c.shape
    n = w.shape[1]
    tn = 1024
    return pl.pallas_call(
        _ada_kernel,
        grid=(n // tn,),
        in_specs=[pl.BlockSpec((bsz, d), lambda j: (0, 0)),
                  pl.BlockSpec((d, tn), lambda j: (0, j)),
                  pl.BlockSpec((1, tn), lambda j: (0, j))],
        out_specs=pl.BlockSpec((bsz, tn), lambda j: (0, j)),
        out_shape=jax.ShapeDtypeStruct((bsz, n), F32),
        compiler_params=_params("arbitrary"),
        name="ada",
    )(c, w, b.reshape(1, n))


_O_RQ, _O_RK, _O_RV, _O_RG, _O_NQ = 0, 512, 1024, 1536, 2048
_O_KV = 2560
_O_NG = 3328
_O_GL = 3840
_PROJ_COLS = 3968


def _rot_ret(z, cos, sin_signed, m):
    partner = jnp.where(m < HEAD_DIM // 2, pltpu.roll(z, LANES - HEAD_DIM // 2, 1),
                        pltpu.roll(z, HEAD_DIM // 2, 1))
    return z * cos + partner * sin_signed


def _rot_nsa(z, cos, sin_signed, m):
    half = ROPE_DIM // 2
    partner = jnp.where(m < half, pltpu.roll(z, LANES - half, 1), pltpu.roll(z, half, 1))
    return z * cos + partner * sin_signed


def _nsa_tables(posf, inv_row, m):
    half = ROPE_DIM // 2
    ang = posf * inv_row
    cos = jnp.where(m < ROPE_DIM, jnp.cos(ang), 1.0)
    s = jnp.sin(ang)
    sin = jnp.where(m < half, -s, jnp.where(m < ROPE_DIM, s, 0.0))
    return cos, sin


def _dup_groups(z, m_lane):
    r = pltpu.roll(z, HEAD_DIM, 1)
    lo = m_lane < HEAD_DIM
    return jnp.where(lo, z, r), jnp.where(lo, r, z)


def _proj_kernel(x_ref, shift_ref, scale_ref, gpre_ref, pos_ref, invr_ref, invn_ref, w_ref,
                 rq_ref, rk_ref, rv_ref, rgs_ref, nq_ref, cmps_ref, ksd_ref, vsd_ref,
                 kwd_ref, vwd_ref, ngs_ref, gl_ref,
                 cr_s, sr_s, cn_s, sn_s):
    ts = x_ref.shape[1]
    x = x_ref[0]
    ms = jnp.mean(x * x, axis=-1, keepdims=True)
    y = x * lax.rsqrt(ms + EPS) * gpre_ref[...]
    h = y * (1.0 + scale_ref[0]) + shift_ref[0]
    hb = h.astype(BF16)

    posf = pos_ref[0].astype(F32)
    lane = lax.broadcasted_iota(jnp.int32, (ts, LANES), 1)
    m = lane & (HEAD_DIM - 1)
    ang = posf * invr_ref[...]
    cr_s[...] = jnp.cos(ang)
    s = jnp.sin(ang)
    sr_s[...] = jnp.where(m < HEAD_DIM // 2, -s, s)
    cn, sn = _nsa_tables(posf, invn_ref[...], m)
    cn_s[...] = cn
    sn_s[...] = sn

    def mm(off, width=LANES):
        return _dot(hb, w_ref[:, off:off + width])

    scale_q = HEAD_DIM ** -0.5
    for c in range(RET_WIDTH // LANES):
        sl = slice(c * LANES, (c + 1) * LANES)
        rq_ref[0, :, sl] = (_rot_ret(mm(_O_RQ + c * LANES), cr_s[...], sr_s[...], m) * scale_q).astype(BF16)
        rk_ref[0, :, sl] = _rot_ret(mm(_O_RK + c * LANES), cr_s[...], sr_s[...], m).astype(BF16)
        rv_ref[0, :, sl] = mm(_O_RV + c * LANES).astype(BF16)
        rgs_ref[0, :, sl] = _silu(mm(_O_RG + c * LANES))
        nq_ref[0, :, sl] = (_rot_nsa(mm(_O_NQ + c * LANES), cn_s[...], sn_s[...], m) * scale_q).astype(BF16)
        ngs_ref[0, :, sl] = _silu(mm(_O_NG + c * LANES))

    kc0, kc1 = _dup_groups(mm(_O_KV), lane)
    vc0, vc1 = _dup_groups(mm(_O_KV + LANES), lane)
    for i, z in enumerate((kc0, kc1, vc0, vc1)):
        cmps_ref[0, :, i * LANES:(i + 1) * LANES] = z
    ks0, ks1 = _dup_groups(_rot_nsa(mm(_O_KV + 2 * LANES), cn_s[...], sn_s[...], m), lane)
    ksd_ref[0, :, :LANES] = ks0.astype(BF16)
    ksd_ref[0, :, LANES:] = ks1.astype(BF16)
    vs0, vs1 = _dup_groups(mm(_O_KV + 3 * LANES), lane)
    vsd_ref[0, :, :LANES] = vs0.astype(BF16)
    vsd_ref[0, :, LANES:] = vs1.astype(BF16)
    kw0, kw1 = _dup_groups(_rot_nsa(mm(_O_KV + 4 * LANES), cn_s[...], sn_s[...], m), lane)
    kwd_ref[0, :, :LANES] = kw0.astype(BF16)
    kwd_ref[0, :, LANES:] = kw1.astype(BF16)
    vw0, vw1 = _dup_groups(mm(_O_KV + 5 * LANES), lane)
    vwd_ref[0, :, :LANES] = vw0.astype(BF16)
    vwd_ref[0, :, LANES:] = vw1.astype(BF16)
    gl_ref[0] = jax.nn.sigmoid(mm(_O_GL))


def _proj(x, shift, scale, g_pre, pos3, inv_ret_row, inv_nsa_row, w_all, ts):
    bsz, seq, d = x.shape
    row = lambda w: pl.BlockSpec((1, ts, w), lambda b, s: (b, s, 0))
    vec = pl.BlockSpec((1, 1, d), lambda b, s: (b, 0, 0))
    const2 = lambda shp: pl.BlockSpec(shp, lambda b, s: (0, 0))
    outs = [(RET_WIDTH, BF16), (RET_WIDTH, BF16), (RET_WIDTH, BF16), (RET_WIDTH, F32),
            (NSA_WIDTH, BF16), (4 * LANES, F32), (2 * LANES, BF16), (2 * LANES, BF16),
            (2 * LANES, BF16), (2 * LANES, BF16), (NSA_WIDTH, F32), (LANES, F32)]
    return pl.pallas_call(
        _proj_kernel,
        grid=(bsz, seq // ts),
        in_specs=[row(d), vec, vec, const2((1, d)), row(1), const2((1, LANES)), const2((1, LANES)),
                  const2((d, _PROJ_COLS))],
        out_specs=[row(w) for w, _ in outs],
        out_shape=[jax.ShapeDtypeStruct((bsz, seq, w), dt) for w, dt in outs],
        scratch_shapes=[pltpu.VMEM((ts, LANES), F32)] * 4,
        compiler_params=_params("parallel", "arbitrary"),
        name="proj",
    )(x, shift, scale, g_pre, pos3, inv_ret_row, inv_nsa_row, w_all)


def _ret_kernel(q_ref, k_ref, v_ref, g_ref, dec_ref, zeta_ref, xi_ref, cd_ref, o_ref, r_s):
    rows = q_ref.shape[1]
    cc = RET_CHUNK

    @pl.when(pl.program_id(2) == 0)
    def _():
        r_s[...] = jnp.zeros_like(r_s)

    lane = lax.broadcasted_iota(jnp.int32, (cc, LANES), 1)
    rowi = lax.broadcasted_iota(jnp.int32, (cc, LANES), 0)
    lo = lane < HEAD_DIM
    blockdiag = (rowi < HEAD_DIM) == lo
    inv_n = 1.0 / HEAD_DIM
    for c in range(rows // cc):
        sl = slice(c * cc, (c + 1) * cc)
        q = q_ref[0, sl, :]
        k = k_ref[0, sl, :]
        v = v_ref[0, sl, :]
        zero = jnp.zeros_like(q)
        s_a = _dot_nt(jnp.where(lo, q, zero), k) * dec_ref[0]
        s_b = _dot_nt(jnp.where(lo, zero, q), k) * dec_ref[1]
        inner = jnp.where(lo, _dot(s_a.astype(BF16), v), _dot(s_b.astype(BF16), v))
        r_prev = r_s[...]
        cross = _dot(q, r_prev.astype(BF16)) * xi_ref[0]
        o = inner + cross
        kz = (k.astype(F32) * zeta_ref[0]).astype(BF16)
        kv = _dot_tn(kz, v)
        r_s[...] = r_prev * cd_ref[0] + jnp.where(blockdiag, kv, 0.0)
        sum_a = jnp.sum(jnp.where(lo, o, 0.0), axis=-1, keepdims=True)
        sum_b = jnp.sum(jnp.where(lo, 0.0, o), axis=-1, keepdims=True)
        d = o - jnp.where(lo, sum_a, sum_b) * inv_n
        d2 = d * d
        var_a = jnp.sum(jnp.where(lo, d2, 0.0), axis=-1, keepdims=True)
        var_b = jnp.sum(jnp.where(lo, 0.0, d2), axis=-1, keepdims=True)
        var = jnp.where(lo, var_a, var_b) * inv_n
        o_ref[0, sl, :] = d * lax.rsqrt(var + GN_EPS) * g_ref[0, sl, :]


def _retention(rq, rk, rv, rgs, dec, zeta_l, xi_l, cd_l, rows):
    bsz, seq, _ = rq.shape
    npair = RET_HEADS // 2
    blk = pl.BlockSpec((1, rows, LANES), lambda b, p, s: (b, s, p))
    return pl.pallas_call(
        _ret_kernel,
        grid=(bsz, npair, seq // rows),
        in_specs=[blk, blk, blk, blk,
                  pl.BlockSpec((2, RET_CHUNK, RET_CHUNK), lambda b, p, s: (p, 0, 0)),
                  pl.BlockSpec((1, RET_CHUNK, LANES), lambda b, p, s: (p, 0, 0)),
                  pl.BlockSpec((1, RET_CHUNK, LANES), lambda b, p, s: (p, 0, 0)),
                  pl.BlockSpec((1, 1, LANES), lambda b, p, s: (p, 0, 0))],
        out_specs=blk,
        out_shape=jax.ShapeDtypeStruct((bsz, seq, RET_WIDTH), F32),
        scratch_shapes=[pltpu.VMEM((LANES, LANES), F32)],
        compiler_params=_params("parallel", "parallel", "arbitrary"),
        name="retention",
    )(rq, rk, rv, rgs, dec, zeta_l, xi_l, cd_l)


def _cmp_kernel(src_ref, pe_ref, w1_ref, w2_ref, pos_ref, invn_ref, o_ref):
    ncp = o_ref.shape[3]
    half = CMP_BLOCK // 2
    p_lo = jnp.zeros((ncp, CMP_HIDDEN), F32)
    p_hi = jnp.zeros((ncp, CMP_HIDDEN), F32)
    for l in range(half):
        t = src_ref[0, pl.ds(l, ncp, stride=CMP_STRIDE), :]
        p_lo = p_lo + _dot((t + pe_ref[0, l:l + 1, :]).astype(BF16), w1_ref[0, l])
        p_hi = p_hi + _dot((t + pe_ref[0, half + l:half + l + 1, :]).astype(BF16), w1_ref[0, half + l])
    hidden = p_lo + pltpu.roll(p_hi, ncp - 1, 0)
    comp = _dot(_silu(hidden).astype(BF16), w2_ref[0])
    m = _lane_in_head(comp.shape)
    cos, sin = _nsa_tables(pos_ref[0].astype(F32), invn_ref[...], m)
    rotated = _rot_nsa(comp, cos, sin, m)
    is_key = pl.program_id(1) == 0
    o_ref[0, 0, 0] = jnp.where(is_key, rotated, comp).astype(BF16)


def _compress(cmps, pe2, w1p, w2d, pos_cmp, inv_nsa_row):
    bsz, seq, _ = cmps.shape
    ncp = seq // CMP_STRIDE
    g = NSA_KV_GROUPS
    return pl.pallas_call(
        _cmp_kernel,
        grid=(bsz, 2, g),
        in_specs=[pl.BlockSpec((1, seq, LANES), lambda b, j, gi: (b, 0, j * g + gi)),
                  pl.BlockSpec((1, CMP_BLOCK, LANES), lambda b, j, gi: (j, 0, 0)),
                  pl.BlockSpec((1, CMP_BLOCK, LANES, CMP_HIDDEN), lambda b, j, gi: (j, 0, 0, 0)),
                  pl.BlockSpec((1, CMP_HIDDEN, LANES), lambda b, j, gi: (j, 0, 0)),
                  pl.BlockSpec((1, ncp, 1), lambda b, j, gi: (b, 0, 0)),
                  pl.BlockSpec((1, LANES), lambda b, j, gi: (0, 0))],
        out_specs=pl.BlockSpec((1, 1, 1, ncp, LANES), lambda b, j, gi: (b, j, gi, 0, 0)),
        out_shape=jax.ShapeDtypeStruct((bsz, 2, g, ncp, LANES), BF16),
        compiler_params=_params("parallel", "arbitrary", "arbitrary"),
        name="compress",
    )(cmps, pe2, w1p, w2d, pos_cmp, inv_nsa_row)


def _cmpsel_kernel(q_ref, kc_ref, vc_ref, ovt_ref, o_ref, selt_ref):
    tq = q_ref.shape[1]
    ncp = kc_ref.shape[3]
    nblk = ovt_ref.shape[0]
    t0 = pl.program_id(2) * tq
    kc = kc_ref[0, 0, 0]
    vc = vc_ref[0, 0, 0]

    tok = t0 + lax.broadcasted_iota(jnp.int32, (tq, ncp), 0)
    cend = lax.broadcasted_iota(jnp.int32, (tq, ncp), 1) * CMP_STRIDE + (CMP_BLOCK - 1)
    mask = cend <= tok
    lo = lax.broadcasted_iota(jnp.int32, (tq, LANES), 1) < HEAD_DIM
    psum = jnp.zeros((tq, ncp), F32)
    for c in range(HEADS_PER_GROUP // 2):
        q = q_ref[0, :, c * LANES:(c + 1) * LANES]
        zero = jnp.zeros_like(q)
        outs = []
        for qm in (jnp.where(lo, q, zero), jnp.where(lo, zero, q)):
            s = jnp.where(mask, _dot_nt(qm, kc), NEG)
            mx = jnp.max(s, axis=-1, keepdims=True)
            e = jnp.where(mask, jnp.exp(s - mx), 0.0)
            den = jnp.sum(e, axis=-1, keepdims=True)
            p = e * jnp.where(den > 0.0, 1.0 / den, 0.0)
            psum = psum + p
            outs.append(_dot(p.astype(BF16), vc))
        o_ref[0, :, c * LANES:(c + 1) * LANES] = jnp.where(lo, outs[0], outs[1])

    p_hi = psum.astype(BF16)
    p_lo = (psum - p_hi.astype(F32)).astype(BF16)
    imp = _dot_nt(ovt_ref[...], p_hi) + _dot_nt(ovt_ref[...], p_lo)
    blk = lax.broadcasted_iota(jnp.int32, (nblk, tq), 0)
    cur = (t0 + lax.broadcasted_iota(jnp.int32, (nblk, tq), 1)) >> SLC_SHIFT
    forced = (blk == 0) | (blk == cur) | (blk == cur - 1)
    score = jnp.where(forced, BIG, jnp.where(blk <= cur, imp, NEG))
    rank = jnp.zeros((nblk, tq), F32)
    for i in range(nblk):
        ri = score[i:i + 1, :]
        beats = (ri > score) | ((ri == score) & (blk > i))
        rank = rank + jnp.where(beats, 1.0, 0.0)
    sel = jnp.where((rank < float(min(SLC_TOPK, nblk))) & (score > 0.5 * NEG), 1.0, 0.0)
    selt_ref[0, 0] = sel.T


def _cmpsel(nq, cmpkv, ovt, tq):
    bsz, seq, _ = nq.shape
    g = NSA_KV_GROUPS
    ncp = cmpkv.shape[3]
    nblk = ovt.shape[0]
    gw = HEADS_PER_GROUP * HEAD_DIM
    return pl.pallas_call(
        _cmpsel_kernel,
        grid=(bsz, g, seq // tq),
        in_specs=[pl.BlockSpec((1, tq, gw), lambda b, gi, i: (b, i, gi)),
                  pl.BlockSpec((1, 1, 1, ncp, LANES), lambda b, gi, i: (b, 0, gi, 0, 0)),
                  pl.BlockSpec((1, 1, 1, ncp, LANES), lambda b, gi, i: (b, 1, gi, 0, 0)),
                  pl.BlockSpec((nblk, ncp), lambda b, gi, i: (0, 0))],
        out_specs=[pl.BlockSpec((1, tq, gw), lambda b, gi, i: (b, i, gi)),
                   pl.BlockSpec((1, 1, tq, nblk), lambda b, gi, i: (b, gi, i, 0))],
        out_shape=[jax.ShapeDtypeStruct((bsz, seq, NSA_WIDTH), F32),
                   jax.ShapeDtypeStruct((bsz, g, seq, nblk), F32)],
        compiler_params=_params("parallel", "parallel", "arbitrary"),
        name="cmpsel",
    )(nq, cmpkv, cmpkv, ovt)


def _stack_head_queries(q_ref, qs_s, tq):
    lo = lax.broadcasted_iota(jnp.int32, (tq, LANES), 1) < HEAD_DIM
    for c in range(HEADS_PER_GROUP // 2):
        q = q_ref[0, :, c * LANES:(c + 1) * LANES]
        zero = jnp.zeros_like(q)
        qs_s[(2 * c) * tq:(2 * c + 1) * tq, :] = jnp.where(lo, q, zero)
        qs_s[(2 * c + 1) * tq:(2 * c + 2) * tq, :] = jnp.where(lo, zero, q)


def _store_heads(o_ref, outs, tq):
    lo = lax.broadcasted_iota(jnp.int32, (tq, LANES), 1) < HEAD_DIM
    for c in range(HEADS_PER_GROUP // 2):
        o_ref[0, :, c * LANES:(c + 1) * LANES] = jnp.where(lo, outs[2 * c], outs[2 * c + 1])


def _lane_tiles(z):
    return [z[:, c * LANES:(c + 1) * LANES] for c in range(z.shape[1] // LANES)]


def _row_stat(tiles, combine, reduce):
    acc = functools.reduce(combine, tiles)
    return jnp.broadcast_to(reduce(acc, axis=-1, keepdims=True), acc.shape)


def _win_kernel(q_ref, k_ref, v_ref, o_ref, qs_s, *, tq):
    nh = HEADS_PER_GROUP
    span = WINDOW + tq
    t0 = pl.program_id(2) * tq
    start = pl.multiple_of(jnp.maximum(t0 - WINDOW, 0), tq)
    k = k_ref[0, pl.ds(start, span), :]
    v = v_ref[0, pl.ds(start, span), :]
    tok = t0 + lax.broadcasted_iota(jnp.int32, (tq, span), 0)
    key = start + lax.broadcasted_iota(jnp.int32, (tq, span), 1)
    bias = jnp.where((key <= tok) & (tok - key < WINDOW), 0.0, NEG)
    _stack_head_queries(q_ref, qs_s, tq)
    s_all = _dot_nt(qs_s[...], k)
    probs, inv_l = [], []
    for h in range(nh):
        tiles = _lane_tiles(s_all[h * tq:(h + 1) * tq, :] + bias)
        m = _row_stat(tiles, jnp.maximum, jnp.max)
        ps = [jnp.exp(t - m) for t in tiles]
        inv_l.append(1.0 / _row_stat(ps, jnp.add, jnp.sum))
        probs.append(jnp.concatenate(ps, axis=1).astype(BF16))
    pv = _dot(jnp.concatenate(probs, axis=0), v)
    _store_heads(o_ref, [pv[h * tq:(h + 1) * tq, :] * inv_l[h] for h in range(nh)], tq)


def _slc_kernel(q_ref, k_ref, v_ref, selt_ref, o_ref, qs_s, m_s, l_s, acc_s, *, tq, tk):
    nh = HEADS_PER_GROUP
    t0 = pl.program_id(2) * tq
    nblk = selt_ref.shape[3]
    _stack_head_queries(q_ref, qs_s, tq)
    selt = selt_ref[0, 0].astype(BF16)
    m_s[...] = jnp.full_like(m_s, NEG)
    l_s[...] = jnp.zeros_like(l_s)
    acc_s[...] = jnp.zeros_like(acc_s)
    tok = t0 + lax.broadcasted_iota(jnp.int32, (tq, tk), 0)

    def body(j, carry):
        k0 = pl.multiple_of(j * tk, tk)
        k = k_ref[0, pl.ds(k0, tk), :]
        v = v_ref[0, pl.ds(k0, tk), :]
        key = k0 + lax.broadcasted_iota(jnp.int32, (tq, tk), 1)
        expand = (lax.broadcasted_iota(jnp.int32, (nblk, tk), 0) ==
                  ((k0 + lax.broadcasted_iota(jnp.int32, (nblk, tk), 1)) >> SLC_SHIFT))
        chosen = _dot(selt, jnp.where(expand, 1.0, 0.0).astype(BF16))
        bias = jnp.where((key <= tok) & (chosen > 0.5), 0.0, NEG)
        s_all = _dot_nt(qs_s[...], k)
        probs, alphas = [], []
        for h in range(nh):
            tiles = _lane_tiles(s_all[h * tq:(h + 1) * tq, :] + bias)
            m_prev = m_s[h]
            m_new = jnp.maximum(m_prev, _row_stat(tiles, jnp.maximum, jnp.max))
            alpha = jnp.exp(m_prev - m_new)
            ps = [jnp.exp(t - m_new) for t in tiles]
            l_s[h] = alpha * l_s[h] + _row_stat(ps, jnp.add, jnp.sum)
            m_s[h] = m_new
            alphas.append(alpha)
            probs.append(jnp.concatenate(ps, axis=1).astype(BF16))
        pv = _dot(jnp.concatenate(probs, axis=0), v)
        for h in range(nh):
            acc_s[h] = alphas[h] * acc_s[h] + pv[h * tq:(h + 1) * tq, :]
        return carry

    lax.fori_loop(0, (t0 + tq - 1) // tk + 1, body, 0)
    _store_heads(o_ref, [acc_s[h] * (1.0 / l_s[h]) for h in range(nh)], tq)


def _flash(nq, kd, vd, selt, *, window, tq, tk=None):
    bsz, seq, _ = nq.shape
    g = NSA_KV_GROUPS
    gw = HEADS_PER_GROUP * HEAD_DIM
    nh = HEADS_PER_GROUP
    qspec = pl.BlockSpec((1, tq, gw), lambda b, gi, i: (b, i, gi))
    kvspec = pl.BlockSpec((1, seq, LANES), lambda b, gi, i: (b, 0, gi))
    in_specs = [qspec, kvspec, kvspec]
    args = [nq, kd, vd]
    if window:
        assert WINDOW % tq == 0 and seq >= WINDOW + tq
        body = functools.partial(_win_kernel, tq=tq)
        scratch = [pltpu.VMEM((nh * tq, LANES), BF16)]
    else:
        assert seq % tk == 0
        nblk = selt.shape[3]
        in_specs.append(pl.BlockSpec((1, 1, tq, nblk), lambda b, gi, i: (b, gi, i, 0)))
        args.append(selt)
        body = functools.partial(_slc_kernel, tq=tq, tk=tk)
        scratch = [pltpu.VMEM((nh * tq, LANES), BF16), pltpu.VMEM((nh, tq, LANES), F32),
                   pltpu.VMEM((nh, tq, LANES), F32), pltpu.VMEM((nh, tq, LANES), F32)]
    return pl.pallas_call(
        body,
        grid=(bsz, g, seq // tq),
        in_specs=in_specs,
        out_specs=qspec,
        out_shape=jax.ShapeDtypeStruct((bsz, seq, NSA_WIDTH), F32),
        scratch_shapes=scratch,
        compiler_params=_params("parallel", "parallel", "arbitrary"),
        name="win_attn" if window else "slc_attn",
    )(*args)


def _out_kernel(x_ref, gate_ref, gpost_ref, ret_ref, oc_ref, os_ref, ow_ref, gl_ref, ngs_ref,
                wo_ref, o_ref):
    ts = x_ref.shape[1]
    gates = gl_ref[0]
    g_hi = gates.astype(BF16)
    g_lo = (gates - g_hi.astype(F32)).astype(BF16)
    src = lax.broadcasted_iota(jnp.int32, (LANES, NSA_WIDTH), 0)
    head = lax.broadcasted_iota(jnp.int32, (LANES, NSA_WIDTH), 1) >> HEAD_SHIFT
    att = jnp.zeros((ts, NSA_WIDTH), F32)
    for i, br in enumerate((oc_ref, os_ref, ow_ref)):
        expand = jnp.where(src == 3 * head + i, 1.0, 0.0).astype(BF16)
        att = att + (_dot(g_hi, expand) + _dot(g_lo, expand)) * br[0]
    att = att * ngs_ref[0]
    y = _dot(ret_ref[0].astype(BF16), wo_ref[:RET_WIDTH, :]) + _dot(att.astype(BF16), wo_ref[RET_WIDTH:, :])
    ms = jnp.mean(y * y, axis=-1, keepdims=True)
    y = y * lax.rsqrt(ms + EPS) * gpost_ref[...]
    o_ref[0] = x_ref[0] + gate_ref[0] * y


def _out(x, gate, g_post, ret, o_cmp, o_slc, o_win, gl, ngs, w_out, ts):
    bsz, seq, d = x.shape
    row = lambda w: pl.BlockSpec((1, ts, w), lambda b, s: (b, s, 0))
    return pl.pallas_call(
        _out_kernel,
        grid=(bsz, seq // ts),
        in_specs=[row(d), pl.BlockSpec((1, 1, d), lambda b, s: (b, 0, 0)),
                  pl.BlockSpec((1, d), lambda b, s: (0, 0)),
                  row(RET_WIDTH), row(NSA_WIDTH), row(NSA_WIDTH), row(NSA_WIDTH), row(LANES),
                  row(NSA_WIDTH), pl.BlockSpec(w_out.shape, lambda b, s: (0, 0))],
        out_specs=row(d),
        out_shape=jax.ShapeDtypeStruct((bsz, seq, d), x.dtype),
        compiler_params=_params("parallel", "arbitrary"),
        name="out",
    )(x, gate, g_post, ret, o_cmp, o_slc, o_win, gl, ngs, w_out)


def _retention_tables():
    h = RET_HEADS
    c = RET_CHUNK
    log_g = jnp.log1p(-jnp.power(2.0, -5.0 - jnp.arange(h, dtype=F32)))
    idx = jnp.arange(c, dtype=F32)
    diff = idx[:, None] - idx[None, :]
    dec = jnp.where(diff[None] >= 0, jnp.exp(jnp.maximum(diff, 0.0)[None] * log_g[:, None, None]), 0.0)
    zeta = jnp.exp((c - 1 - idx)[None, :] * log_g[:, None])
    xi = jnp.exp((idx + 1.0)[None, :] * log_g[:, None])
    cd = jnp.exp(c * log_g)
    to_lanes = lambda t: jnp.repeat(t.reshape(h // 2, 2, c).transpose(0, 2, 1), HEAD_DIM, axis=2)
    cd_l = jnp.repeat(cd.reshape(h // 2, 1, 2), HEAD_DIM, axis=2)
    return dec, to_lanes(zeta), to_lanes(xi), cd_l


def _overlap_t(seq):
    ncp = seq // CMP_STRIDE
    nc = (seq - CMP_BLOCK) // CMP_STRIDE + 1
    ns = seq // SLC_BLOCK
    cs = np.arange(ncp) * CMP_STRIDE
    ce = cs + CMP_BLOCK - 1
    ss = np.arange(ns) * SLC_BLOCK
    ov = (cs[None, :] <= ss[:, None] + SLC_BLOCK - 1) & (ce[None, :] >= ss[:, None]) & (np.arange(ncp) < nc)[None, :]
    return jnp.asarray(ov.astype(np.float32), dtype=BF16)


def _layer(x, c, positions, w_ada, b_ada, g_pre, g_post, w_in, w_out,
           pe_k, w1_k, w2_k, pe_v, w1_v, w2_v):
    bsz, seq, d = x.shape
    mod = _ada(c, w_ada, b_ada)
    shift, scale, gate = [t.reshape(bsz, 1, d) for t in jnp.split(mod, 3, axis=-1)]

    offs = np.cumsum((RET_WIDTH,) * 4 + (NSA_WIDTH, 6 * KV_WIDTH, 3 * NSA_HEADS, NSA_WIDTH))
    w_gl = w_in[:, offs[5]:offs[6]]
    w_all = jnp.concatenate(
        [w_in[:, :offs[5]], w_in[:, offs[6]:], w_gl,
         jnp.zeros((d, LANES - w_gl.shape[1]), w_in.dtype)], axis=1).astype(BF16)

    lanes = jnp.arange(LANES)
    half_r = HEAD_DIM // 2
    inv_r = jnp.power(RET_ROPE_BASE, -jnp.arange(half_r, dtype=F32) / half_r)
    half_n = ROPE_DIM // 2
    inv_n = jnp.power(NSA_ROPE_BASE, -jnp.arange(half_n, dtype=F32) / half_n)
    inv_ret_row = inv_r[lanes % half_r].reshape(1, LANES)
    inv_nsa_row = inv_n[lanes % half_n].reshape(1, LANES)
    pos3 = positions.reshape(bsz, seq, 1)

    (rq, rk, rv, rgs, nq, cmps, ksd, vsd, kwd, vwd, ngs, gl) = _proj(
        x, shift, scale, g_pre.reshape(1, d), pos3, inv_ret_row, inv_nsa_row, w_all, ts=256)

    dec, zeta_l, xi_l, cd_l = _retention_tables()
    ret = _retention(rq, rk, rv, rgs, dec, zeta_l, xi_l, cd_l, rows=min(512, seq))

    ncp = seq // CMP_STRIDE
    nc = (seq - CMP_BLOCK) // CMP_STRIDE + 1
    pad_l = lambda t: jnp.concatenate([t, jnp.zeros_like(t)], axis=-1)
    pe2 = jnp.stack([pad_l(pe_k), pad_l(pe_v)])
    w1r = jnp.stack([w1_k, w1_v]).reshape(2, CMP_BLOCK, HEAD_DIM, CMP_HIDDEN)
    w1p = jnp.concatenate([w1r, jnp.zeros_like(w1r)], axis=2).astype(BF16)
    w2s = jnp.stack([w2_k, w2_v])
    w2d = jnp.concatenate([w2s, w2s], axis=-1).astype(BF16)
    cmp_end = np.arange(nc) * CMP_STRIDE + CMP_BLOCK - 1
    pos_cmp = jnp.pad(positions[:, cmp_end], ((0, 0), (0, ncp - nc))).reshape(bsz, ncp, 1)
    cmpkv = _compress(cmps, pe2, w1p, w2d, pos_cmp, inv_nsa_row)

    o_cmp, selt = _cmpsel(nq, cmpkv, _overlap_t(seq), tq=min(256, seq))
    o_slc = _flash(nq, ksd, vsd, selt, window=False, tq=128, tk=512)
    o_win = _flash(nq, kwd, vwd, None, window=True, tq=128)

    return _out(x, gate, g_post.reshape(1, d), ret, o_cmp, o_slc, o_win, gl, ngs,
                w_out.astype(BF16), ts=256)


def kernel(x, c, positions, w_ada, b_ada, g_pre, g_post, w_in, w_out, cmp_pe_k, cmp_w1_k, cmp_w2_k, cmp_pe_v, cmp_w1_v, cmp_w2_v):
    for l in range(w_in.shape[0]):
        x = _layer(x, c, positions, w_ada[l], b_ada[l], g_pre[l], g_post[l], w_in[l], w_out[l],
                   cmp_pe_k[l], cmp_w1_k[l], cmp_w2_k[l], cmp_pe_v[l], cmp_w1_v[l], cmp_w2_v[l])
    return x
```

```python
import functools

import numpy as np
import jax
import jax.numpy as jnp
from jax import lax
from jax.experimental import pallas as pl
from jax.experimental.pallas import tpu as pltpu

F32 = jnp.float32
BF16 = jnp.bfloat16

LANES = 128
HEAD_DIM = 64
RET_HEADS = 8
NSA_HEADS = 8
NSA_KV_GROUPS = 2
HEADS_PER_GROUP = NSA_HEADS // NSA_KV_GROUPS
RET_WIDTH = RET_HEADS * HEAD_DIM
NSA_WIDTH = NSA_HEADS * HEAD_DIM
KV_WIDTH = NSA_KV_GROUPS * HEAD_DIM
RET_CHUNK = 128
RET_ROPE_BASE = 10000.0
NSA_ROPE_BASE = 500000.0
ROPE_DIM = HEAD_DIM // 4
CMP_BLOCK = 32
CMP_STRIDE = 16
CMP_HIDDEN = 256
SLC_BLOCK = 64
SLC_TOPK = 16
WINDOW = 512
NEG = -1e30
BIG = 1e9
EPS = 1e-6
GN_EPS = 1e-5
VMEM_LIMIT = 48 * 1024 * 1024

NT_DIMS = (((1,), (1,)), ((), ()))
TN_DIMS = (((0,), (0,)), ((), ()))


def _dot(a, b):
    return jnp.dot(a, b, preferred_element_type=F32)


def _dot_nt(a, b):
    return lax.dot_general(a, b, NT_DIMS, preferred_element_type=F32)


def _dot_tn(a, b):
    return lax.dot_general(a, b, TN_DIMS, preferred_element_type=F32)


def _silu(z):
    return z * jax.nn.sigmoid(z)


def _params(*sem):
    return pltpu.CompilerParams(dimension_semantics=sem, vmem_limit_bytes=VMEM_LIMIT)


HEAD_SHIFT = 6
SLC_SHIFT = 6
assert 1 << HEAD_SHIFT == HEAD_DIM and 1 << SLC_SHIFT == SLC_BLOCK


def _lane_in_head(shape):
    return lax.broadcasted_iota(jnp.int32, shape, len(shape) - 1) & (HEAD_DIM - 1)


def _ada_kernel(c_ref, w_ref, b_ref, o_ref):
    a = _silu(c_ref[...])
    o_ref[...] = jnp.dot(a, w_ref[...], precision=lax.Precision.HIGHEST,
                         preferred_element_type=F32) + b_ref[...]


def _ada(c, w, b):
    bsz, d = c.shape
    n = w.shape[1]
    tn = 1024
    return pl.pallas_call(
        _ada_kernel,
        grid=(n // tn,),
        in_specs=[pl.BlockSpec((bsz, d), lambda j: (0, 0)),
                  pl.BlockSpec((d, tn), lambda j: (0, j)),
                  pl.BlockSpec((1, tn), lambda j: (0, j))],
        out_specs=pl.BlockSpec((bsz, tn), lambda j: (0, j)),
        out_shape=jax.ShapeDtypeStruct((bsz, n), F32),
        compiler_params=_params("arbitrary"),
        name="ada",
    )(c, w, b.reshape(1, n))


_O_RQ, _O_RK, _O_RV, _O_RG, _O_NQ = 0, 512, 1024, 1536, 2048
_O_KV = 2560
_O_NG = 3328
_O_GL = 3840
_PROJ_COLS = 3968


def _rot_ret(z, cos, sin_signed, m):
    partner = jnp.where(m < HEAD_DIM // 2, pltpu.roll(z, LANES - HEAD_DIM // 2, 1),
                        pltpu.roll(z, HEAD_DIM // 2, 1))
    return z * cos + partner * sin_signed


def _rot_nsa(z, cos, sin_signed, m):
    half = ROPE_DIM // 2
    partner = jnp.where(m < half, pltpu.roll(z, LANES - half, 1), pltpu.roll(z, half, 1))
    return z * cos + partner * sin_signed


def _nsa_tables(posf, inv_row, m):
    half = ROPE_DIM // 2
    ang = posf * inv_row
    cos = jnp.where(m < ROPE_DIM, jnp.cos(ang), 1.0)
    s = jnp.sin(ang)
    sin = jnp.where(m < half, -s, jnp.where(m < ROPE_DIM, s, 0.0))
    return cos, sin


def _dup_groups(z, m_lane):
    r = pltpu.roll(z, HEAD_DIM, 1)
    lo = m_lane < HEAD_DIM
    return jnp.where(lo, z, r), jnp.where(lo, r, z)


def _proj_kernel(x_ref, shift_ref, scale_ref, gpre_ref, pos_ref, invr_ref, invn_ref, w_ref,
                 rq_ref, rk_ref, rv_ref, rgs_ref, nq_ref, cmps_ref, ksd_ref, vsd_ref,
                 kwd_ref, vwd_ref, ngs_ref, gl_ref,
                 cr_s, sr_s, cn_s, sn_s, z_s):
    ts = x_ref.shape[1]
    x = x_ref[0]
    ms = jnp.mean(x * x, axis=-1, keepdims=True)
    y = x * lax.rsqrt(ms + EPS) * gpre_ref[...]
    h = y * (1.0 + scale_ref[0]) + shift_ref[0]
    hb = h.astype(BF16)

    posf = pos_ref[0].astype(F32)
    lane = lax.broadcasted_iota(jnp.int32, (ts, LANES), 1)
    m = lane & (HEAD_DIM - 1)
    ang = posf * invr_ref[...]
    cr_s[...] = jnp.cos(ang)
    s = jnp.sin(ang)
    sr_s[...] = jnp.where(m < HEAD_DIM // 2, -s, s)
    cn, sn = _nsa_tables(posf, invn_ref[...], m)
    cn_s[...] = cn
    sn_s[...] = sn

    def mm(off, width):
        z_s[:, :width] = _dot(hb, w_ref[:, off:off + width])
        return lambda c: z_s[:, c * LANES:(c + 1) * LANES]

    scale_q = HEAD_DIM ** -0.5
    tiles = lambda width: [(c, slice(c * LANES, (c + 1) * LANES)) for c in range(width // LANES)]
    z = mm(_O_RQ, RET_WIDTH)
    for c, sl in tiles(RET_WIDTH):
        rq_ref[0, :, sl] = (_rot_ret(z(c), cr_s[...], sr_s[...], m) * scale_q).astype(BF16)
    z = mm(_O_RK, RET_WIDTH)
    for c, sl in tiles(RET_WIDTH):
        rk_ref[0, :, sl] = _rot_ret(z(c), cr_s[...], sr_s[...], m).astype(BF16)
    z = mm(_O_RV, RET_WIDTH)
    for c, sl in tiles(RET_WIDTH):
        rv_ref[0, :, sl] = z(c).astype(BF16)
    z = mm(_O_RG, RET_WIDTH)
    for c, sl in tiles(RET_WIDTH):
        rgs_ref[0, :, sl] = _silu(z(c))
    z = mm(_O_NQ, NSA_WIDTH)
    for c, sl in tiles(NSA_WIDTH):
        nq_ref[0, :, sl] = (_rot_nsa(z(c), cn_s[...], sn_s[...], m) * scale_q).astype(BF16)

    z = mm(_O_KV, 6 * KV_WIDTH)
    kc0, kc1 = _dup_groups(z(0), lane)
    vc0, vc1 = _dup_groups(z(1), lane)
    for i, t in enumerate((kc0, kc1, vc0, vc1)):
        cmps_ref[0, :, i * LANES:(i + 1) * LANES] = t
    for c, ref, rotate in ((2, ksd_ref, True), (3, vsd_ref, False), (4, kwd_ref, True), (5, vwd_ref, False)):
        t = _rot_nsa(z(c), cn_s[...], sn_s[...], m) if rotate else z(c)
        d0, d1 = _dup_groups(t, lane)
        ref[0, :, :LANES] = d0.astype(BF16)
        ref[0, :, LANES:] = d1.astype(BF16)

    z = mm(_O_NG, NSA_WIDTH + LANES)
    for c, sl in tiles(NSA_WIDTH):
        ngs_ref[0, :, sl] = _silu(z(c))
    gl_ref[0] = jax.nn.sigmoid(z(NSA_WIDTH // LANES))


def _proj(x, shift, scale, g_pre, pos3, inv_ret_row, inv_nsa_row, w_all, ts):
    bsz, seq, d = x.shape
    row = lambda w: pl.BlockSpec((1, ts, w), lambda b, s: (b, s, 0))
    vec = pl.BlockSpec((1, 1, d), lambda b, s: (b, 0, 0))
    const2 = lambda shp: pl.BlockSpec(shp, lambda b, s: (0, 0))
    outs = [(RET_WIDTH, BF16), (RET_WIDTH, BF16), (RET_WIDTH, BF16), (RET_WIDTH, F32),
            (NSA_WIDTH, BF16), (4 * LANES, F32), (2 * LANES, BF16), (2 * LANES, BF16),
            (2 * LANES, BF16), (2 * LANES, BF16), (NSA_WIDTH, F32), (LANES, F32)]
    return pl.pallas_call(
        _proj_kernel,
        grid=(bsz, seq // ts),
        in_specs=[row(d), vec, vec, const2((1, d)), row(1), const2((1, LANES)), const2((1, LANES)),
                  const2((d, _PROJ_COLS))],
        out_specs=[row(w) for w, _ in outs],
        out_shape=[jax.ShapeDtypeStruct((bsz, seq, w), dt) for w, dt in outs],
        scratch_shapes=[pltpu.VMEM((ts, LANES), F32)] * 4 + [pltpu.VMEM((ts, 6 * KV_WIDTH), F32)],
        compiler_params=_params("parallel", "arbitrary"),
        name="proj",
    )(x, shift, scale, g_pre, pos3, inv_ret_row, inv_nsa_row, w_all)


def _ret_kernel(q_ref, k_ref, v_ref, g_ref, dec_ref, zeta_ref, xi_ref, cd_ref, o_ref, r_s):
    rows = q_ref.shape[1]
    cc = RET_CHUNK

    @pl.when(pl.program_id(2) == 0)
    def _():
        r_s[...] = jnp.zeros_like(r_s)

    lane = lax.broadcasted_iota(jnp.int32, (cc, LANES), 1)
    rowi = lax.broadcasted_iota(jnp.int32, (cc, LANES), 0)
    lo = lane < HEAD_DIM
    blockdiag = (rowi < HEAD_DIM) == lo
    inv_n = 1.0 / HEAD_DIM
    for c in range(rows // cc):
        sl = slice(c * cc, (c + 1) * cc)
        q = q_ref[0, sl, :]
        k = k_ref[0, sl, :]
        v = v_ref[0, sl, :]
        zero = jnp.zeros_like(q)
        s_a = _dot_nt(jnp.where(lo, q, zero), k) * dec_ref[0]
        s_b = _dot_nt(jnp.where(lo, zero, q), k) * dec_ref[1]
        inner = jnp.where(lo, _dot(s_a.astype(BF16), v), _dot(s_b.astype(BF16), v))
        r_prev = r_s[...]
        cross = _dot(q, r_prev.astype(BF16)) * xi_ref[0]
        o = inner + cross
        kz = (k.astype(F32) * zeta_ref[0]).astype(BF16)
        kv = _dot_tn(kz, v)
        r_s[...] = r_prev * cd_ref[0] + jnp.where(blockdiag, kv, 0.0)
        sum_a = jnp.sum(jnp.where(lo, o, 0.0), axis=-1, keepdims=True)
        sum_b = jnp.sum(jnp.where(lo, 0.0, o), axis=-1, keepdims=True)
        d = o - jnp.where(lo, sum_a, sum_b) * inv_n
        d2 = d * d
        var_a = jnp.sum(jnp.where(lo, d2, 0.0), axis=-1, keepdims=True)
        var_b = jnp.sum(jnp.where(lo, 0.0, d2), axis=-1, keepdims=True)
        var = jnp.where(lo, var_a, var_b) * inv_n
        o_ref[0, sl, :] = d * lax.rsqrt(var + GN_EPS) * g_ref[0, sl, :]


def _retention(rq, rk, rv, rgs, dec, zeta_l, xi_l, cd_l, rows):
    bsz, seq, _ = rq.shape
    npair = RET_HEADS // 2
    blk = pl.BlockSpec((1, rows, LANES), lambda b, p, s: (b, s, p))
    return pl.pallas_call(
        _ret_kernel,
        grid=(bsz, npair, seq // rows),
        in_specs=[blk, blk, blk, blk,
                  pl.BlockSpec((2, RET_CHUNK, RET_CHUNK), lambda b, p, s: (p, 0, 0)),
                  pl.BlockSpec((1, RET_CHUNK, LANES), lambda b, p, s: (p, 0, 0)),
                  pl.BlockSpec((1, RET_CHUNK, LANES), lambda b, p, s: (p, 0, 0)),
                  pl.BlockSpec((1, 1, LANES), lambda b, p, s: (p, 0, 0))],
        out_specs=blk,
        out_shape=jax.ShapeDtypeStruct((bsz, seq, RET_WIDTH), F32),
        scratch_shapes=[pltpu.VMEM((LANES, LANES), F32)],
        compiler_params=_params("parallel", "parallel", "arbitrary"),
        name="retention",
    )(rq, rk, rv, rgs, dec, zeta_l, xi_l, cd_l)


def _cmp_kernel(src_ref, pe_ref, w1_ref, w2_ref, pos_ref, invn_ref, o_ref):
    ncp = o_ref.shape[3]
    half = CMP_BLOCK // 2
    p_lo = jnp.zeros((ncp, CMP_HIDDEN), F32)
    p_hi = jnp.zeros((ncp, CMP_HIDDEN), F32)
    for l in range(half):
        t = src_ref[0, pl.ds(l, ncp, stride=CMP_STRIDE), :]
        p_lo = p_lo + _dot((t + pe_ref[0, l:l + 1, :]).astype(BF16), w1_ref[0, l])
        p_hi = p_hi + _dot((t + pe_ref[0, half + l:half + l + 1, :]).astype(BF16), w1_ref[0, half + l])
    hidden = p_lo + pltpu.roll(p_hi, ncp - 1, 0)
    comp = _dot(_silu(hidden).astype(BF16), w2_ref[0])
    m = _lane_in_head(comp.shape)
    cos, sin = _nsa_tables(pos_ref[0].astype(F32), invn_ref[...], m)
    rotated = _rot_nsa(comp, cos, sin, m)
    is_key = pl.program_id(1) == 0
    o_ref[0, 0, 0] = jnp.where(is_key, rotated, comp).astype(BF16)


def _compress(cmps, pe2, w1p, w2d, pos_cmp, inv_nsa_row):
    bsz, seq, _ = cmps.shape
    ncp = seq // CMP_STRIDE
    g = NSA_KV_GROUPS
    return pl.pallas_call(
        _cmp_kernel,
        grid=(bsz, 2, g),
        in_specs=[pl.BlockSpec((1, seq, LANES), lambda b, j, gi: (b, 0, j * g + gi)),
                  pl.BlockSpec((1, CMP_BLOCK, LANES), lambda b, j, gi: (j, 0, 0)),
                  pl.BlockSpec((1, CMP_BLOCK, LANES, CMP_HIDDEN), lambda b, j, gi: (j, 0, 0, 0)),
                  pl.BlockSpec((1, CMP_HIDDEN, LANES), lambda b, j, gi: (j, 0, 0)),
                  pl.BlockSpec((1, ncp, 1), lambda b, j, gi: (b, 0, 0)),
                  pl.BlockSpec((1, LANES), lambda b, j, gi: (0, 0))],
        out_specs=pl.BlockSpec((1, 1, 1, ncp, LANES), lambda b, j, gi: (b, j, gi, 0, 0)),
        out_shape=jax.ShapeDtypeStruct((bsz, 2, g, ncp, LANES), BF16),
        compiler_params=_params("parallel", "arbitrary", "arbitrary"),
        name="compress",
    )(cmps, pe2, w1p, w2d, pos_cmp, inv_nsa_row)


def _cmpsel_kernel(q_ref, kc_ref, vc_ref, ovt_ref, o_ref, selt_ref, qs_s, score_s, rank_s):
    nh = HEADS_PER_GROUP
    sub = 8
    tq = q_ref.shape[1]
    ncp = kc_ref.shape[3]
    nblk = ovt_ref.shape[0]
    t0 = pl.program_id(2) * tq
    kc = kc_ref[0, 0, 0]
    vc = vc_ref[0, 0, 0]

    tok = t0 + lax.broadcasted_iota(jnp.int32, (tq, LANES), 0)
    lane = lax.broadcasted_iota(jnp.int32, (tq, LANES), 1)
    masks = [(lane + c * LANES) * CMP_STRIDE + (CMP_BLOCK - 1) <= tok for c in range(ncp // LANES)]
    _stack_head_queries(q_ref, qs_s, tq)
    s_all = _dot_nt(qs_s[...], kc)
    psum, probs = None, []
    for h in range(nh):
        tiles = [jnp.where(mk, t, NEG) for mk, t in zip(masks, _lane_tiles(s_all[h * tq:(h + 1) * tq, :]))]
        mx = _row_stat(tiles, jnp.maximum, jnp.max)
        es = [jnp.where(mk, jnp.exp(t - mx), 0.0) for mk, t in zip(masks, tiles)]
        den = _row_stat(es, jnp.add, jnp.sum)
        inv = jnp.where(den > 0.0, 1.0 / den, 0.0)
        ps = [e * inv for e in es]
        psum = ps if psum is None else [a + b for a, b in zip(psum, ps)]
        probs.append(jnp.concatenate(ps, axis=1).astype(BF16))
    pv = _dot(jnp.concatenate(probs, axis=0), vc)
    _store_heads(o_ref, [pv[h * tq:(h + 1) * tq, :] for h in range(nh)], tq)

    psum = jnp.concatenate(psum, axis=1)
    p_hi = psum.astype(BF16)
    p_lo = (psum - p_hi.astype(F32)).astype(BF16)
    imp = _dot_nt(ovt_ref[...], p_hi) + _dot_nt(ovt_ref[...], p_lo)
    blk = lax.broadcasted_iota(jnp.int32, (nblk, tq), 0)
    cur = (t0 + lax.broadcasted_iota(jnp.int32, (nblk, tq), 1)) >> SLC_SHIFT
    forced = (blk == 0) | (blk == cur) | (blk == cur - 1)
    score_s[...] = jnp.where(forced, BIG, jnp.where(blk <= cur, imp, NEG))

    rank_s[...] = jnp.zeros_like(rank_s)
    last_blk = (t0 + tq - 1) >> SLC_SHIFT
    row_in_group = lax.broadcasted_iota(jnp.int32, (sub, tq), 0)
    for ig in range(nblk // sub):
        @pl.when(ig * sub <= last_blk)
        def _():
            groups = [slice(r * sub, (r + 1) * sub) for r in range(nblk // sub)]
            scs = [score_s[g, :] for g in groups]
            cnts = [rank_s[g, :] for g in groups]
            for i in range(ig * sub, (ig + 1) * sub):
                ri = jnp.broadcast_to(score_s[i:i + 1, :], (sub, tq))
                for r, sc in enumerate(scs):
                    if r * sub > i:
                        beats = ri >= sc
                    elif (r + 1) * sub - 1 <= i:
                        beats = ri > sc
                    else:
                        beats = (ri > sc) | ((ri == sc) & (row_in_group > i - r * sub))
                    cnts[r] = cnts[r] + jnp.where(beats, 1.0, 0.0)
            for g, cnt in zip(groups, cnts):
                rank_s[g, :] = cnt
    sel = jnp.where((rank_s[...] < float(min(SLC_TOPK, nblk))) & (score_s[...] > 0.5 * NEG), 1.0, 0.0)
    selt_ref[0, 0] = sel.T


def _cmpsel(nq, cmpkv, ovt, tq):
    bsz, seq, _ = nq.shape
    g = NSA_KV_GROUPS
    ncp = cmpkv.shape[3]
    nblk = ovt.shape[0]
    gw = HEADS_PER_GROUP * HEAD_DIM
    return pl.pallas_call(
        _cmpsel_kernel,
        grid=(bsz, g, seq // tq),
        in_specs=[pl.BlockSpec((1, tq, gw), lambda b, gi, i: (b, i, gi)),
                  pl.BlockSpec((1, 1, 1, ncp, LANES), lambda b, gi, i: (b, 0, gi, 0, 0)),
                  pl.BlockSpec((1, 1, 1, ncp, LANES), lambda b, gi, i: (b, 1, gi, 0, 0)),
                  pl.BlockSpec((nblk, ncp), lambda b, gi, i: (0, 0))],
        out_specs=[pl.BlockSpec((1, tq, gw), lambda b, gi, i: (b, i, gi)),
                   pl.BlockSpec((1, 1, tq, nblk), lambda b, gi, i: (b, gi, i, 0))],
        out_shape=[jax.ShapeDtypeStruct((bsz, seq, NSA_WIDTH), F32),
                   jax.ShapeDtypeStruct((bsz, g, seq, nblk), F32)],
        scratch_shapes=[pltpu.VMEM((HEADS_PER_GROUP * tq, LANES), BF16),
                        pltpu.VMEM((nblk, tq), F32), pltpu.VMEM((nblk, tq), F32)],
        compiler_params=_params("parallel", "parallel", "arbitrary"),
        name="cmpsel",
    )(nq, cmpkv, cmpkv, ovt)


def _stack_head_queries(q_ref, qs_s, rows, src0=0, dst0=0):
    lo = lax.broadcasted_iota(jnp.int32, (rows, LANES), 1) < HEAD_DIM
    for c in range(HEADS_PER_GROUP // 2):
        q = q_ref[0, src0:src0 + rows, c * LANES:(c + 1) * LANES]
        zero = jnp.zeros_like(q)
        qs_s[dst0 + (2 * c) * rows:dst0 + (2 * c + 1) * rows, :] = jnp.where(lo, q, zero)
        qs_s[dst0 + (2 * c + 1) * rows:dst0 + (2 * c + 2) * rows, :] = jnp.where(lo, zero, q)


def _store_heads(o_ref, outs, tq):
    lo = lax.broadcasted_iota(jnp.int32, (tq, LANES), 1) < HEAD_DIM
    for c in range(HEADS_PER_GROUP // 2):
        o_ref[0, :, c * LANES:(c + 1) * LANES] = jnp.where(lo, outs[2 * c], outs[2 * c + 1])


def _lane_tiles(z):
    return [z[:, c * LANES:(c + 1) * LANES] for c in range(z.shape[1] // LANES)]


def _row_stat(tiles, combine, reduce):
    acc = functools.reduce(combine, tiles)
    return jnp.broadcast_to(reduce(acc, axis=-1, keepdims=True), acc.shape)


def _win_kernel(q_ref, k_ref, v_ref, o_ref, qs_s, s_s, *, tq):
    nh = HEADS_PER_GROUP
    span = WINDOW + tq
    t0 = pl.program_id(2) * tq
    start = pl.multiple_of(jnp.maximum(t0 - WINDOW, 0), LANES)
    k = k_ref[0, pl.ds(start, span), :]
    v = v_ref[0, pl.ds(start, span), :]
    th = tq // 2
    for r in range(2):
        _stack_head_queries(q_ref, qs_s, th, src0=r * th, dst0=r * nh * th)
        s_s[r] = _dot_nt(qs_s[r * nh * th:(r + 1) * nh * th, :], k)
    outs = [[None, None] for _ in range(nh)]
    for r in range(2):
        tok = t0 + r * th + lax.broadcasted_iota(jnp.int32, (th, span), 0)
        key = start + lax.broadcasted_iota(jnp.int32, (th, span), 1)
        bias = jnp.where((key <= tok) & (tok - key < WINDOW), 0.0, NEG)
        probs, inv_l = [], []
        for h in range(nh):
            tiles = _lane_tiles(s_s[r, h * th:(h + 1) * th, :] + bias)
            m = _row_stat(tiles, jnp.maximum, jnp.max)
            ps = [jnp.exp(t - m) for t in tiles]
            inv_l.append(1.0 / _row_stat(ps, jnp.add, jnp.sum))
            probs.append(jnp.concatenate(ps, axis=1).astype(BF16))
        pv = _dot(jnp.concatenate(probs, axis=0), v)
        for h in range(nh):
            outs[h][r] = pv[h * th:(h + 1) * th, :] * inv_l[h]
    _store_heads(o_ref, [jnp.concatenate(o, axis=0) for o in outs], tq)


def _slc_kernel(q_ref, k_ref, v_ref, selt_ref, o_ref, qs_s, s0_s, s1_s, p0_s, p1_s, a0_s, a1_s,
                m_s, l_s, acc_s, *, tq, tk):
    nh = HEADS_PER_GROUP
    t0 = pl.program_id(2) * tq
    nblk = selt_ref.shape[3]
    n_tiles = (t0 + tq - 1) // tk + 1
    s_b, p_b, a_b = (s0_s, s1_s), (p0_s, p1_s), (a0_s, a1_s)
    _stack_head_queries(q_ref, qs_s, tq)
    selt = selt_ref[0, 0].astype(BF16)
    m_s[...] = jnp.full_like(m_s, NEG)
    l_s[...] = jnp.zeros_like(l_s)
    acc_s[...] = jnp.zeros_like(acc_s)

    def rows_of(j):
        return pl.ds(pl.multiple_of(j * tk, tk), tk)

    def scores(j, slot):
        s_b[slot][...] = _dot_nt(qs_s[...], k_ref[0, rows_of(j), :])

    def values(j, slot):
        pv = _dot(p_b[slot][...], v_ref[0, rows_of(j), :])
        for h in range(nh):
            acc_s[h] = a_b[slot][h] * acc_s[h] + pv[h * tq:(h + 1) * tq, :]

    def mask_bias(j, diagonal):
        k0 = j * tk
        expand = (lax.broadcasted_iota(jnp.int32, (nblk, tk), 0) ==
                  ((k0 + lax.broadcasted_iota(jnp.int32, (nblk, tk), 1)) >> SLC_SHIFT))
        keep = _dot(selt, jnp.where(expand, 1.0, 0.0).astype(BF16)) > 0.5
        if diagonal:
            tok = t0 + lax.broadcasted_iota(jnp.int32, (tq, tk), 0)
            keep = keep & (k0 + lax.broadcasted_iota(jnp.int32, (tq, tk), 1) <= tok)
        return jnp.where(keep, 0.0, NEG)

    def softmax(slot, bias):
        for h in range(nh):
            rows = slice(h * tq, (h + 1) * tq)
            tiles = _lane_tiles(s_b[slot][rows, :] + bias)
            m_prev = m_s[h]
            m_new = jnp.maximum(m_prev, _row_stat(tiles, jnp.maximum, jnp.max))
            alpha = jnp.exp(m_prev - m_new)
            ps = [jnp.exp(t - m_new) for t in tiles]
            l_s[h] = alpha * l_s[h] + _row_stat(ps, jnp.add, jnp.sum)
            m_s[h] = m_new
            a_b[slot][h] = alpha
            p_b[slot][rows, :] = jnp.concatenate(ps, axis=1).astype(BF16)

    def stage(j, slot, diagonal=False):
        bias = mask_bias(j, diagonal)
        values(j - 1, 1 - slot)
        if not diagonal:
            scores(j + 1, 1 - slot)
        softmax(slot, bias)

    scores(0, 0)

    @pl.when(n_tiles == 1)
    def _():
        softmax(0, mask_bias(0, True))
        values(0, 0)

    @pl.when(n_tiles >= 2)
    def _():
        bias = mask_bias(0, False)
        scores(1, 1)
        softmax(0, bias)

        def pair(i, carry):
            stage(2 * i + 1, 1)
            stage(2 * i + 2, 0)
            return carry

        lax.fori_loop(0, (n_tiles - 2) // 2, pair, 0)
        last = n_tiles - 1

        @pl.when(last % 2 == 0)
        def _():
            stage(last - 1, 1)
            stage(last, 0, diagonal=True)
            values(last, 0)

        @pl.when(last % 2 == 1)
        def _():
            stage(last, 1, diagonal=True)
            values(last, 1)

    _store_heads(o_ref, [acc_s[h] * (1.0 / l_s[h]) for h in range(nh)], tq)


def _flash(nq, kd, vd, selt, *, window, tq, tk=None):
    bsz, seq, _ = nq.shape
    g = NSA_KV_GROUPS
    gw = HEADS_PER_GROUP * HEAD_DIM
    nh = HEADS_PER_GROUP
    qspec = pl.BlockSpec((1, tq, gw), lambda b, gi, i: (b, i, gi))
    kvspec = pl.BlockSpec((1, seq, LANES), lambda b, gi, i: (b, 0, gi))
    in_specs = [qspec, kvspec, kvspec]
    args = [nq, kd, vd]
    if window:
        assert WINDOW % tq == 0 and seq >= WINDOW + tq
        body = functools.partial(_win_kernel, tq=tq)
        scratch = [pltpu.VMEM((nh * tq, LANES), BF16),
                   pltpu.VMEM((2, nh * tq // 2, WINDOW + tq), F32)]
    else:
        assert seq % tk == 0 and tk % tq == 0
        nblk = selt.shape[3]
        in_specs.append(pl.BlockSpec((1, 1, tq, nblk), lambda b, gi, i: (b, gi, i, 0)))
        args.append(selt)
        body = functools.partial(_slc_kernel, tq=tq, tk=tk)
        stat = pltpu.VMEM((nh, tq, LANES), F32)
        scratch = ([pltpu.VMEM((nh * tq, LANES), BF16)] + [pltpu.VMEM((nh * tq, tk), F32)] * 2 +
                   [pltpu.VMEM((nh * tq, tk), BF16)] * 2 + [stat] * 5)
    return pl.pallas_call(
        body,
        grid=(bsz, g, seq // tq),
        in_specs=in_specs,
        out_specs=qspec,
        out_shape=jax.ShapeDtypeStruct((bsz, seq, NSA_WIDTH), F32),
        scratch_shapes=scratch,
        compiler_params=_params("parallel", "parallel", "arbitrary"),
        name="win_attn" if window else "slc_attn",
    )(*args)


def _out_kernel(x_ref, gate_ref, gpost_ref, ret_ref, oc_ref, os_ref, ow_ref, gl_ref, ngs_ref,
                wo_ref, o_ref):
    ts = x_ref.shape[1]
    gates = gl_ref[0]
    g_hi = gates.astype(BF16)
    g_lo = (gates - g_hi.astype(F32)).astype(BF16)
    src = lax.broadcasted_iota(jnp.int32, (LANES, NSA_WIDTH), 0)
    head = lax.broadcasted_iota(jnp.int32, (LANES, NSA_WIDTH), 1) >> HEAD_SHIFT
    att = jnp.zeros((ts, NSA_WIDTH), F32)
    for i, br in enumerate((oc_ref, os_ref, ow_ref)):
        expand = jnp.where(src == 3 * head + i, 1.0, 0.0).astype(BF16)
        att = att + (_dot(g_hi, expand) + _dot(g_lo, expand)) * br[0]
    att = att * ngs_ref[0]
    y = _dot(ret_ref[0].astype(BF16), wo_ref[:RET_WIDTH, :]) + _dot(att.astype(BF16), wo_ref[RET_WIDTH:, :])
    ms = jnp.mean(y * y, axis=-1, keepdims=True)
    y = y * lax.rsqrt(ms + EPS) * gpost_ref[...]
    o_ref[0] = x_ref[0] + gate_ref[0] * y


def _out(x, gate, g_post, ret, o_cmp, o_slc, o_win, gl, ngs, w_out, ts):
    bsz, seq, d = x.shape
    row = lambda w: pl.BlockSpec((1, ts, w), lambda b, s: (b, s, 0))
    return pl.pallas_call(
        _out_kernel,
        grid=(bsz, seq // ts),
        in_specs=[row(d), pl.BlockSpec((1, 1, d), lambda b, s: (b, 0, 0)),
                  pl.BlockSpec((1, d), lambda b, s: (0, 0)),
                  row(RET_WIDTH), row(NSA_WIDTH), row(NSA_WIDTH), row(NSA_WIDTH), row(LANES),
                  row(NSA_WIDTH), pl.BlockSpec(w_out.shape, lambda b, s: (0, 0))],
        out_specs=row(d),
        out_shape=jax.ShapeDtypeStruct((bsz, seq, d), x.dtype),
        compiler_params=_params("parallel", "arbitrary"),
        name="out",
    )(x, gate, g_post, ret, o_cmp, o_slc, o_win, gl, ngs, w_out)


def _retention_tables():
    h = RET_HEADS
    c = RET_CHUNK
    log_g = jnp.log1p(-jnp.power(2.0, -5.0 - jnp.arange(h, dtype=F32)))
    idx = jnp.arange(c, dtype=F32)
    diff = idx[:, None] - idx[None, :]
    dec = jnp.where(diff[None] >= 0, jnp.exp(jnp.maximum(diff, 0.0)[None] * log_g[:, None, None]), 0.0)
    zeta = jnp.exp((c - 1 - idx)[None, :] * log_g[:, None])
    xi = jnp.exp((idx + 1.0)[None, :] * log_g[:, None])
    cd = jnp.exp(c * log_g)
    to_lanes = lambda t: jnp.repeat(t.reshape(h // 2, 2, c).transpose(0, 2, 1), HEAD_DIM, axis=2)
    cd_l = jnp.repeat(cd.reshape(h // 2, 1, 2), HEAD_DIM, axis=2)
    return dec, to_lanes(zeta), to_lanes(xi), cd_l


def _overlap_t(seq):
    ncp = seq // CMP_STRIDE
    nc = (seq - CMP_BLOCK) // CMP_STRIDE + 1
    ns = seq // SLC_BLOCK
    cs = np.arange(ncp) * CMP_STRIDE
    ce = cs + CMP_BLOCK - 1
    ss = np.arange(ns) * SLC_BLOCK
    ov = (cs[None, :] <= ss[:, None] + SLC_BLOCK - 1) & (ce[None, :] >= ss[:, None]) & (np.arange(ncp) < nc)[None, :]
    return jnp.asarray(ov.astype(np.float32), dtype=BF16)


def _layer(x, c, positions, w_ada, b_ada, g_pre, g_post, w_in, w_out,
           pe_k, w1_k, w2_k, pe_v, w1_v, w2_v):
    bsz, seq, d = x.shape
    mod = _ada(c, w_ada, b_ada)
    shift, scale, gate = [t.reshape(bsz, 1, d) for t in jnp.split(mod, 3, axis=-1)]

    offs = np.cumsum((RET_WIDTH,) * 4 + (NSA_WIDTH, 6 * KV_WIDTH, 3 * NSA_HEADS, NSA_WIDTH))
    w_gl = w_in[:, offs[5]:offs[6]]
    w_all = jnp.concatenate(
        [w_in[:, :offs[5]], w_in[:, offs[6]:], w_gl,
         jnp.zeros((d, LANES - w_gl.shape[1]), w_in.dtype)], axis=1).astype(BF16)

    lanes = jnp.arange(LANES)
    half_r = HEAD_DIM // 2
    inv_r = jnp.power(RET_ROPE_BASE, -jnp.arange(half_r, dtype=F32) / half_r)
    half_n = ROPE_DIM // 2
    inv_n = jnp.power(NSA_ROPE_BASE, -jnp.arange(half_n, dtype=F32) / half_n)
    inv_ret_row = inv_r[lanes % half_r].reshape(1, LANES)
    inv_nsa_row = inv_n[lanes % half_n].reshape(1, LANES)
    pos3 = positions.reshape(bsz, seq, 1)

    (rq, rk, rv, rgs, nq, cmps, ksd, vsd, kwd, vwd, ngs, gl) = _proj(
        x, shift, scale, g_pre.reshape(1, d), pos3, inv_ret_row, inv_nsa_row, w_all, ts=256)

    dec, zeta_l, xi_l, cd_l = _retention_tables()
    ret = _retention(rq, rk, rv, rgs, dec, zeta_l, xi_l, cd_l, rows=min(512, seq))

    ncp = seq // CMP_STRIDE
    nc = (seq - CMP_BLOCK) // CMP_STRIDE + 1
    pad_l = lambda t: jnp.concatenate([t, jnp.zeros_like(t)], axis=-1)
    pe2 = jnp.stack([pad_l(pe_k), pad_l(pe_v)])
    w1r = jnp.stack([w1_k, w1_v]).reshape(2, CMP_BLOCK, HEAD_DIM, CMP_HIDDEN)
    w1p = jnp.concatenate([w1r, jnp.zeros_like(w1r)], axis=2).astype(BF16)
    w2s = jnp.stack([w2_k, w2_v])
    w2d = jnp.concatenate([w2s, w2s], axis=-1).astype(BF16)
    cmp_end = np.arange(nc) * CMP_STRIDE + CMP_BLOCK - 1
    pos_cmp = jnp.pad(positions[:, cmp_end], ((0, 0), (0, ncp - nc))).reshape(bsz, ncp, 1)
    cmpkv = _compress(cmps, pe2, w1p, w2d, pos_cmp, inv_nsa_row)

    o_cmp, selt = _cmpsel(nq, cmpkv, _overlap_t(seq), tq=min(256, seq))
    o_slc = _flash(nq, ksd, vsd, selt, window=False, tq=128, tk=512)
    o_win = _flash(nq, kwd, vwd, None, window=True, tq=256)

    return _out(x, gate, g_post.reshape(1, d), ret, o_cmp, o_slc, o_win, gl, ngs,
                w_out.astype(BF16), ts=256)


def kernel(x, c, positions, w_ada, b_ada, g_pre, g_post, w_in, w_out, cmp_pe_k, cmp_w1_k, cmp_w2_k, cmp_pe_v, cmp_w1_v, cmp_w2_v):
    for l in range(w_in.shape[0]):
        x = _layer(x, c, positions, w_ada[l], b_ada[l], g_pre[l], g_post[l], w_in[l], w_out[l],
                   cmp_pe_k[l], cmp_w1_k[l], cmp_w2_k[l], cmp_pe_v[l], cmp_w1_v[l], cmp_w2_v[l])
    return x
```

```python
import functools

import numpy as np
import jax
import jax.numpy as jnp
from jax import lax
from jax.experimental import pallas as pl
from jax.experimental.pallas import tpu as pltpu

F32 = jnp.float32
BF16 = jnp.bfloat16

LANES = 128
HEAD_DIM = 64
RET_HEADS = 8
NSA_HEADS = 8
NSA_KV_GROUPS = 2
HEADS_PER_GROUP = NSA_HEADS // NSA_KV_GROUPS
RET_WIDTH = RET_HEADS * HEAD_DIM
NSA_WIDTH = NSA_HEADS * HEAD_DIM
KV_WIDTH = NSA_KV_GROUPS * HEAD_DIM
RET_CHUNK = 128
RET_ROPE_BASE = 10000.0
NSA_ROPE_BASE = 500000.0
ROPE_DIM = HEAD_DIM // 4
CMP_BLOCK = 32
CMP_STRIDE = 16
CMP_HIDDEN = 256
SLC_BLOCK = 64
SLC_TOPK = 16
WINDOW = 512
NEG = -1e30
LOG2E = 1.4426950408889634
BIG = 1e9
EPS = 1e-6
GN_EPS = 1e-5
VMEM_LIMIT = 48 * 1024 * 1024

NT_DIMS = (((1,), (1,)), ((), ()))
TN_DIMS = (((0,), (0,)), ((), ()))


def _dot(a, b):
    return jnp.dot(a, b, preferred_element_type=F32)


def _dot_nt(a, b):
    return lax.dot_general(a, b, NT_DIMS, preferred_element_type=F32)


def _dot_tn(a, b):
    return lax.dot_general(a, b, TN_DIMS, preferred_element_type=F32)


def _silu(z):
    return z * jax.nn.sigmoid(z)


def _params(*sem):
    return pltpu.CompilerParams(dimension_semantics=sem, vmem_limit_bytes=VMEM_LIMIT)


HEAD_SHIFT = 6
SLC_SHIFT = 6
assert 1 << HEAD_SHIFT == HEAD_DIM and 1 << SLC_SHIFT == SLC_BLOCK


def _lane_in_head(shape):
    return lax.broadcasted_iota(jnp.int32, shape, len(shape) - 1) & (HEAD_DIM - 1)


def _ada_kernel(c_ref, w_ref, b_ref, o_ref):
    a = _silu(c_ref[...])
    o_ref[...] = jnp.dot(a, w_ref[...], precision=lax.Precision.HIGHEST,
                         preferred_element_type=F32) + b_ref[...]


def _ada(c, w, b):
    bsz, d = c.shape
    n = w.shape[1]
    tn = 1024
    return pl.pallas_call(
        _ada_kernel,
        grid=(n // tn,),
        in_specs=[pl.BlockSpec((bsz, d), lambda j: (0, 0)),
                  pl.BlockSpec((d, tn), lambda j: (0, j)),
                  pl.BlockSpec((1, tn), lambda j: (0, j))],
        out_specs=pl.BlockSpec((bsz, tn), lambda j: (0, j)),
        out_shape=jax.ShapeDtypeStruct((bsz, n), F32),
        compiler_params=_params("arbitrary"),
        name="ada",
    )(c, w, b.reshape(1, n))


_O_RQ, _O_RK, _O_RV, _O_RG, _O_NQ = 0, 512, 1024, 1536, 2048
_O_KV = 2560
_O_NG = 3328
_O_GL = 3840
_PROJ_COLS = 3968


def _rot_ret(z, cos, sin_signed, m):
    partner = jnp.where(m < HEAD_DIM // 2, pltpu.roll(z, LANES - HEAD_DIM // 2, 1),
                        pltpu.roll(z, HEAD_DIM // 2, 1))
    return z * cos + partner * sin_signed


def _rot_nsa(z, cos, sin_signed, m):
    half = ROPE_DIM // 2
    partner = jnp.where(m < half, pltpu.roll(z, LANES - half, 1), pltpu.roll(z, half, 1))
    return z * cos + partner * sin_signed


def _nsa_tables(posf, inv_row, m):
    half = ROPE_DIM // 2
    ang = posf * inv_row
    cos = jnp.where(m < ROPE_DIM, jnp.cos(ang), 1.0)
    s = jnp.sin(ang)
    sin = jnp.where(m < half, -s, jnp.where(m < ROPE_DIM, s, 0.0))
    return cos, sin


def _dup_groups(z, m_lane):
    r = pltpu.roll(z, HEAD_DIM, 1)
    lo = m_lane < HEAD_DIM
    return jnp.where(lo, z, r), jnp.where(lo, r, z)


def _proj_kernel(x_ref, shift_ref, scale_ref, gpre_ref, pos_ref, invr_ref, invn_ref, w_ref,
                 rq_ref, rk_ref, rv_ref, rgs_ref, nq_ref, cmps_ref, ksd_ref, vsd_ref,
                 kwd_ref, vwd_ref, ngs_ref, gl_ref,
                 cr_s, sr_s, cn_s, sn_s, z_s):
    ts = x_ref.shape[1]
    x = x_ref[0]
    ms = jnp.mean(x * x, axis=-1, keepdims=True)
    y = x * lax.rsqrt(ms + EPS) * gpre_ref[...]
    h = y * (1.0 + scale_ref[0]) + shift_ref[0]
    hb = h.astype(BF16)

    posf = pos_ref[0].astype(F32)
    lane = lax.broadcasted_iota(jnp.int32, (ts, LANES), 1)
    m = lane & (HEAD_DIM - 1)
    ang = posf * invr_ref[...]
    cr_s[...] = jnp.cos(ang)
    s = jnp.sin(ang)
    sr_s[...] = jnp.where(m < HEAD_DIM // 2, -s, s)
    cn, sn = _nsa_tables(posf, invn_ref[...], m)
    cn_s[...] = cn
    sn_s[...] = sn

    def mm(off, width):
        z_s[:, :width] = _dot(hb, w_ref[:, off:off + width])
        return lambda c: z_s[:, c * LANES:(c + 1) * LANES]

    scale_q = HEAD_DIM ** -0.5
    tiles = lambda width: [(c, slice(c * LANES, (c + 1) * LANES)) for c in range(width // LANES)]
    z = mm(_O_RQ, RET_WIDTH)
    for c, sl in tiles(RET_WIDTH):
        rq_ref[0, :, sl] = (_rot_ret(z(c), cr_s[...], sr_s[...], m) * scale_q).astype(BF16)
    z = mm(_O_RK, RET_WIDTH)
    for c, sl in tiles(RET_WIDTH):
        rk_ref[0, :, sl] = _rot_ret(z(c), cr_s[...], sr_s[...], m).astype(BF16)
    z = mm(_O_RV, RET_WIDTH)
    for c, sl in tiles(RET_WIDTH):
        rv_ref[0, :, sl] = z(c).astype(BF16)
    z = mm(_O_RG, RET_WIDTH)
    for c, sl in tiles(RET_WIDTH):
        rgs_ref[0, :, sl] = _silu(z(c))
    z = mm(_O_NQ, NSA_WIDTH)
    for c, sl in tiles(NSA_WIDTH):
        nq_ref[0, :, sl] = (_rot_nsa(z(c), cn_s[...], sn_s[...], m) * (scale_q * LOG2E)).astype(BF16)

    z = mm(_O_KV, 6 * KV_WIDTH)
    kc0, kc1 = _dup_groups(z(0), lane)
    vc0, vc1 = _dup_groups(z(1), lane)
    for i, t in enumerate((kc0, kc1, vc0, vc1)):
        cmps_ref[0, :, i * LANES:(i + 1) * LANES] = t
    lo = lane < HEAD_DIM
    tok = pl.program_id(1) * ts + lax.broadcasted_iota(jnp.int32, (ts, LANES), 0)
    onehot = jnp.where(lane - HEAD_DIM == (tok >> SLC_SHIFT), 1.0, 0.0)
    ones = jnp.ones((ts, LANES), F32)
    for c, ref, rotate, upper in ((2, ksd_ref, True, onehot), (3, vsd_ref, False, ones),
                                  (4, kwd_ref, True, None), (5, vwd_ref, False, ones)):
        t = _rot_nsa(z(c), cn_s[...], sn_s[...], m) if rotate else z(c)
        if upper is None:
            d0, d1 = _dup_groups(t, lane)
        else:
            d0, d1 = jnp.where(lo, t, upper), jnp.where(lo, pltpu.roll(t, HEAD_DIM, 1), upper)
        ref[0, :, :LANES] = d0.astype(BF16)
        ref[0, :, LANES:] = d1.astype(BF16)

    z = mm(_O_NG, NSA_WIDTH + LANES)
    for c, sl in tiles(NSA_WIDTH):
        ngs_ref[0, :, sl] = _silu(z(c))
    gl_ref[0] = jax.nn.sigmoid(z(NSA_WIDTH // LANES))


def _proj(x, shift, scale, g_pre, pos3, inv_ret_row, inv_nsa_row, w_all, ts):
    bsz, seq, d = x.shape
    row = lambda w: pl.BlockSpec((1, ts, w), lambda b, s: (b, s, 0))
    vec = pl.BlockSpec((1, 1, d), lambda b, s: (b, 0, 0))
    const2 = lambda shp: pl.BlockSpec(shp, lambda b, s: (0, 0))
    outs = [(RET_WIDTH, BF16), (RET_WIDTH, BF16), (RET_WIDTH, BF16), (RET_WIDTH, F32),
            (NSA_WIDTH, BF16), (4 * LANES, F32), (2 * LANES, BF16), (2 * LANES, BF16),
            (2 * LANES, BF16), (2 * LANES, BF16), (NSA_WIDTH, F32), (LANES, F32)]
    return pl.pallas_call(
        _proj_kernel,
        grid=(bsz, seq // ts),
        in_specs=[row(d), vec, vec, const2((1, d)), row(1), const2((1, LANES)), const2((1, LANES)),
                  const2((d, _PROJ_COLS))],
        out_specs=[row(w) for w, _ in outs],
        out_shape=[jax.ShapeDtypeStruct((bsz, seq, w), dt) for w, dt in outs],
        scratch_shapes=[pltpu.VMEM((ts, LANES), F32)] * 4 + [pltpu.VMEM((ts, 6 * KV_WIDTH), F32)],
        compiler_params=_params("parallel", "arbitrary"),
        name="proj",
    )(x, shift, scale, g_pre, pos3, inv_ret_row, inv_nsa_row, w_all)


def _ret_kernel(q_ref, k_ref, v_ref, g_ref, dec_ref, zeta_ref, xi_ref, cd_ref, o_ref, r_s):
    rows = q_ref.shape[1]
    cc = RET_CHUNK

    @pl.when(pl.program_id(2) == 0)
    def _():
        r_s[...] = jnp.zeros_like(r_s)

    lane = lax.broadcasted_iota(jnp.int32, (cc, LANES), 1)
    rowi = lax.broadcasted_iota(jnp.int32, (cc, LANES), 0)
    lo = lane < HEAD_DIM
    blockdiag = (rowi < HEAD_DIM) == lo
    inv_n = 1.0 / HEAD_DIM
    chunks = [slice(c * cc, (c + 1) * cc) for c in range(rows // cc)]
    s_ab, kvs = [], []
    for sl in chunks:
        q, k, v = q_ref[0, sl, :], k_ref[0, sl, :], v_ref[0, sl, :]
        zero = jnp.zeros_like(q)
        s_ab.append((_dot_nt(jnp.where(lo, q, zero), k), _dot_nt(jnp.where(lo, zero, q), k)))
        kz = (k.astype(F32) * zeta_ref[0]).astype(BF16)
        kvs.append(jnp.where(blockdiag, _dot_tn(kz, v), 0.0))
    inners = []
    for sl, (s_a, s_b) in zip(chunks, s_ab):
        v = v_ref[0, sl, :]
        inners.append(jnp.where(lo, _dot((s_a * dec_ref[0]).astype(BF16), v),
                                _dot((s_b * dec_ref[1]).astype(BF16), v)))
    r = r_s[...]
    outs = []
    for sl, inner, kv in zip(chunks, inners, kvs):
        outs.append(inner + _dot(q_ref[0, sl, :], r.astype(BF16)) * xi_ref[0])
        r = r * cd_ref[0] + kv
    r_s[...] = r
    for sl, o in zip(chunks, outs):
        sum_a = jnp.sum(jnp.where(lo, o, 0.0), axis=-1, keepdims=True)
        sum_b = jnp.sum(jnp.where(lo, 0.0, o), axis=-1, keepdims=True)
        d = o - jnp.where(lo, sum_a, sum_b) * inv_n
        d2 = d * d
        var_a = jnp.sum(jnp.where(lo, d2, 0.0), axis=-1, keepdims=True)
        var_b = jnp.sum(jnp.where(lo, 0.0, d2), axis=-1, keepdims=True)
        var = jnp.where(lo, var_a, var_b) * inv_n
        o_ref[0, sl, :] = d * lax.rsqrt(var + GN_EPS) * g_ref[0, sl, :]


def _retention(rq, rk, rv, rgs, dec, zeta_l, xi_l, cd_l, rows):
    bsz, seq, _ = rq.shape
    npair = RET_HEADS // 2
    blk = pl.BlockSpec((1, rows, LANES), lambda b, p, s: (b, s, p))
    return pl.pallas_call(
        _ret_kernel,
        grid=(bsz, npair, seq // rows),
        in_specs=[blk, blk, blk, blk,
                  pl.BlockSpec((2, RET_CHUNK, RET_CHUNK), lambda b, p, s: (p, 0, 0)),
                  pl.BlockSpec((1, RET_CHUNK, LANES), lambda b, p, s: (p, 0, 0)),
                  pl.BlockSpec((1, RET_CHUNK, LANES), lambda b, p, s: (p, 0, 0)),
                  pl.BlockSpec((1, 1, LANES), lambda b, p, s: (p, 0, 0))],
        out_specs=blk,
        out_shape=jax.ShapeDtypeStruct((bsz, seq, RET_WIDTH), F32),
        scratch_shapes=[pltpu.VMEM((LANES, LANES), F32)],
        compiler_params=_params("parallel", "parallel", "arbitrary"),
        name="retention",
    )(rq, rk, rv, rgs, dec, zeta_l, xi_l, cd_l)


def _cmp_kernel(src_ref, pe_ref, w1_ref, w2_ref, pos_ref, invn_ref, o_ref):
    ncp = o_ref.shape[3]
    half = CMP_BLOCK // 2
    p_lo = jnp.zeros((ncp, CMP_HIDDEN), F32)
    p_hi = jnp.zeros((ncp, CMP_HIDDEN), F32)
    for l in range(half):
        t = src_ref[0, pl.ds(l, ncp, stride=CMP_STRIDE), :]
        p_lo = p_lo + _dot((t + pe_ref[0, l:l + 1, :]).astype(BF16), w1_ref[0, l])
        p_hi = p_hi + _dot((t + pe_ref[0, half + l:half + l + 1, :]).astype(BF16), w1_ref[0, half + l])
    hidden = p_lo + pltpu.roll(p_hi, ncp - 1, 0)
    comp = _dot(_silu(hidden).astype(BF16), w2_ref[0])
    m = _lane_in_head(comp.shape)
    cos, sin = _nsa_tables(pos_ref[0].astype(F32), invn_ref[...], m)
    rotated = _rot_nsa(comp, cos, sin, m)
    is_key = pl.program_id(1) == 0
    o_ref[0, 0, 0] = jnp.where(is_key, rotated, comp).astype(BF16)


def _compress(cmps, pe2, w1p, w2d, pos_cmp, inv_nsa_row):
    bsz, seq, _ = cmps.shape
    ncp = seq // CMP_STRIDE
    g = NSA_KV_GROUPS
    return pl.pallas_call(
        _cmp_kernel,
        grid=(bsz, 2, g),
        in_specs=[pl.BlockSpec((1, seq, LANES), lambda b, j, gi: (b, 0, j * g + gi)),
                  pl.BlockSpec((1, CMP_BLOCK, LANES), lambda b, j, gi: (j, 0, 0)),
                  pl.BlockSpec((1, CMP_BLOCK, LANES, CMP_HIDDEN), lambda b, j, gi: (j, 0, 0, 0)),
                  pl.BlockSpec((1, CMP_HIDDEN, LANES), lambda b, j, gi: (j, 0, 0)),
                  pl.BlockSpec((1, ncp, 1), lambda b, j, gi: (b, 0, 0)),
                  pl.BlockSpec((1, LANES), lambda b, j, gi: (0, 0))],
        out_specs=pl.BlockSpec((1, 1, 1, ncp, LANES), lambda b, j, gi: (b, j, gi, 0, 0)),
        out_shape=jax.ShapeDtypeStruct((bsz, 2, g, ncp, LANES), BF16),
        compiler_params=_params("parallel", "arbitrary", "arbitrary"),
        name="compress",
    )(cmps, pe2, w1p, w2d, pos_cmp, inv_nsa_row)


def _cmpsel_kernel(q_ref, kc_ref, vc_ref, ovt_ref, o_ref, selt_ref, qs_s, score_s, rank_s):
    nh = HEADS_PER_GROUP
    sub = 8
    tq = q_ref.shape[1]
    ncp = kc_ref.shape[3]
    nblk = ovt_ref.shape[0]
    t0 = pl.program_id(2) * tq
    kc = kc_ref[0, 0, 0]
    vc = vc_ref[0, 0, 0]

    tok = t0 + lax.broadcasted_iota(jnp.int32, (tq, LANES), 0)
    lane = lax.broadcasted_iota(jnp.int32, (tq, LANES), 1)
    masks = [(lane + c * LANES) * CMP_STRIDE + (CMP_BLOCK - 1) <= tok for c in range(ncp // LANES)]
    _stack_head_queries(q_ref, qs_s, tq)
    s_all = _dot_nt(qs_s[...], kc)
    psum, probs = None, []
    for h in range(nh):
        tiles = [jnp.where(mk, t, NEG) for mk, t in zip(masks, _lane_tiles(s_all[h * tq:(h + 1) * tq, :]))]
        mx = _row_stat(tiles, jnp.maximum, jnp.max)
        es = [jnp.where(mk, jnp.exp2(t - mx), 0.0) for mk, t in zip(masks, tiles)]
        den = _row_stat(es, jnp.add, jnp.sum)
        inv = jnp.where(den > 0.0, 1.0 / den, 0.0)
        ps = [e * inv for e in es]
        psum = ps if psum is None else [a + b for a, b in zip(psum, ps)]
        probs.append(jnp.concatenate(ps, axis=1).astype(BF16))
    pv = _dot(jnp.concatenate(probs, axis=0), vc)
    _store_heads(o_ref, [pv[h * tq:(h + 1) * tq, :] for h in range(nh)], tq)

    psum = jnp.concatenate(psum, axis=1)
    p_hi = psum.astype(BF16)
    p_lo = (psum - p_hi.astype(F32)).astype(BF16)
    imp = _dot_nt(ovt_ref[...], p_hi) + _dot_nt(ovt_ref[...], p_lo)
    blk = lax.broadcasted_iota(jnp.int32, (nblk, tq), 0)
    cur = (t0 + lax.broadcasted_iota(jnp.int32, (nblk, tq), 1)) >> SLC_SHIFT
    forced = (blk == 0) | (blk == cur) | (blk == cur - 1)
    score_s[...] = jnp.where(forced, BIG, jnp.where(blk <= cur, imp, NEG))

    rank_s[...] = jnp.zeros_like(rank_s)
    last_blk = (t0 + tq - 1) >> SLC_SHIFT
    row_in_group = lax.broadcasted_iota(jnp.int32, (sub, tq), 0)
    for ig in range(nblk // sub):
        @pl.when(ig * sub <= last_blk)
        def _():
            groups = [slice(r * sub, (r + 1) * sub) for r in range(nblk // sub)]
            scs = [score_s[g, :] for g in groups]
            cnts = [rank_s[g, :] for g in groups]
            for i in range(ig * sub, (ig + 1) * sub):
                ri = jnp.broadcast_to(score_s[i:i + 1, :], (sub, tq))
                for r, sc in enumerate(scs):
                    if r * sub > i:
                        beats = ri >= sc
                    elif (r + 1) * sub - 1 <= i:
                        beats = ri > sc
                    else:
                        beats = (ri > sc) | ((ri == sc) & (row_in_group > i - r * sub))
                    cnts[r] = cnts[r] + jnp.where(beats, 1.0, 0.0)
            for g, cnt in zip(groups, cnts):
                rank_s[g, :] = cnt
    sel = jnp.where((rank_s[...] < float(min(SLC_TOPK, nblk))) & (score_s[...] > 0.5 * NEG), 1.0, 0.0)
    pads = [jnp.zeros((HEAD_DIM, tq), F32), sel]
    if nblk < HEAD_DIM:
        pads.append(jnp.zeros((HEAD_DIM - nblk, tq), F32))
    selt_ref[0, 0] = jnp.concatenate(pads, axis=0).T


def _cmpsel(nq, cmpkv, ovt, tq):
    bsz, seq, _ = nq.shape
    g = NSA_KV_GROUPS
    ncp = cmpkv.shape[3]
    nblk = ovt.shape[0]
    gw = HEADS_PER_GROUP * HEAD_DIM
    return pl.pallas_call(
        _cmpsel_kernel,
        grid=(bsz, g, seq // tq),
        in_specs=[pl.BlockSpec((1, tq, gw), lambda b, gi, i: (b, i, gi)),
                  pl.BlockSpec((1, 1, 1, ncp, LANES), lambda b, gi, i: (b, 0, gi, 0, 0)),
                  pl.BlockSpec((1, 1, 1, ncp, LANES), lambda b, gi, i: (b, 1, gi, 0, 0)),
                  pl.BlockSpec((nblk, ncp), lambda b, gi, i: (0, 0))],
        out_specs=[pl.BlockSpec((1, tq, gw), lambda b, gi, i: (b, i, gi)),
                   pl.BlockSpec((1, 1, tq, LANES), lambda b, gi, i: (b, gi, i, 0))],
        out_shape=[jax.ShapeDtypeStruct((bsz, seq, NSA_WIDTH), F32),
                   jax.ShapeDtypeStruct((bsz, g, seq, LANES), F32)],
        scratch_shapes=[pltpu.VMEM((HEADS_PER_GROUP * tq, LANES), BF16),
                        pltpu.VMEM((nblk, tq), F32), pltpu.VMEM((nblk, tq), F32)],
        compiler_params=_params("parallel", "parallel", "arbitrary"),
        name="cmpsel",
    )(nq, cmpkv, cmpkv, ovt)


def _stack_head_queries(q_ref, qs_s, rows, src0=0, dst0=0):
    lo = lax.broadcasted_iota(jnp.int32, (rows, LANES), 1) < HEAD_DIM
    for c in range(HEADS_PER_GROUP // 2):
        q = q_ref[0, src0:src0 + rows, c * LANES:(c + 1) * LANES]
        zero = jnp.zeros_like(q)
        qs_s[dst0 + (2 * c) * rows:dst0 + (2 * c + 1) * rows, :] = jnp.where(lo, q, zero)
        qs_s[dst0 + (2 * c + 1) * rows:dst0 + (2 * c + 2) * rows, :] = jnp.where(lo, zero, q)


def _store_heads(o_ref, outs, tq):
    lo = lax.broadcasted_iota(jnp.int32, (tq, LANES), 1) < HEAD_DIM
    for c in range(HEADS_PER_GROUP // 2):
        o_ref[0, :, c * LANES:(c + 1) * LANES] = jnp.where(lo, outs[2 * c], outs[2 * c + 1])


def _store_heads_normalized(o_ref, accs, tq):
    lo = lax.broadcasted_iota(jnp.int32, (tq, LANES), 1) < HEAD_DIM
    outs = [a * (1.0 / jnp.where(lo, pltpu.roll(a, HEAD_DIM, 1), 1.0)) for a in accs]
    for c in range(HEADS_PER_GROUP // 2):
        o_ref[0, :, c * LANES:(c + 1) * LANES] = jnp.where(
            lo, outs[2 * c], pltpu.roll(outs[2 * c + 1], HEAD_DIM, 1))


def _lane_tiles(z):
    return [z[:, c * LANES:(c + 1) * LANES] for c in range(z.shape[1] // LANES)]


def _row_stat(tiles, combine, reduce):
    acc = functools.reduce(combine, tiles)
    return jnp.broadcast_to(reduce(acc, axis=-1, keepdims=True), acc.shape)


def _win_kernel(q_ref, k_ref, v_ref, o_ref, qs_s, s_s, *, tq):
    nh = HEADS_PER_GROUP
    span = WINDOW + tq
    t0 = pl.program_id(2) * tq
    start = pl.multiple_of(jnp.maximum(t0 - WINDOW, 0), LANES)
    k = k_ref[0, pl.ds(start, span), :]
    v = v_ref[0, pl.ds(start, span), :]
    th = tq // 2
    for r in range(2):
        _stack_head_queries(q_ref, qs_s, th, src0=r * th, dst0=r * nh * th)
        s_s[r] = _dot_nt(qs_s[r * nh * th:(r + 1) * nh * th, :], k)
    outs = [[None, None] for _ in range(nh)]
    for r in range(2):
        tok = t0 + r * th + lax.broadcasted_iota(jnp.int32, (th, span), 0)
        key = start + lax.broadcasted_iota(jnp.int32, (th, span), 1)
        bias = jnp.where((key <= tok) & (tok - key < WINDOW), 0.0, NEG)
        probs = []
        for h in range(nh):
            tiles = _lane_tiles(s_s[r, h * th:(h + 1) * th, :] + bias)
            m = _row_stat(tiles, jnp.maximum, jnp.max)
            probs.append(jnp.concatenate([jnp.exp2(t - m) for t in tiles], axis=1).astype(BF16))
        pv = _dot(jnp.concatenate(probs, axis=0), v)
        for h in range(nh):
            outs[h][r] = pv[h * th:(h + 1) * th, :]
    _store_heads_normalized(o_ref, [jnp.concatenate(o, axis=0) for o in outs], tq)


def _slc_kernel(q_ref, k_ref, v_ref, selt_ref, o_ref, qs_s, s0_s, s1_s, p0_s, p1_s, a0_s, a1_s,
                m_s, acc_s, *, tq, tk):
    nh = HEADS_PER_GROUP
    t0 = pl.program_id(2) * tq
    n_tiles = (t0 + tq - 1) // tk + 1
    s_b, p_b, a_b = (s0_s, s1_s), (p0_s, p1_s), (a0_s, a1_s)

    lo = lax.broadcasted_iota(jnp.int32, (tq, LANES), 1) < HEAD_DIM
    unselected = jnp.where(selt_ref[0, 0] > 0.5, 0.0, NEG)
    for c in range(nh // 2):
        q = q_ref[0, :, c * LANES:(c + 1) * LANES].astype(F32)
        qs_s[(2 * c) * tq:(2 * c + 1) * tq, :] = jnp.where(lo, q, unselected).astype(BF16)
        qs_s[(2 * c + 1) * tq:(2 * c + 2) * tq, :] = jnp.where(
            lo, pltpu.roll(q, HEAD_DIM, 1), unselected).astype(BF16)
    m_s[...] = jnp.full_like(m_s, NEG)
    acc_s[...] = jnp.zeros_like(acc_s)

    def rows_of(j):
        return pl.ds(pl.multiple_of(j * tk, tk), tk)

    def scores(j, slot):
        s_b[slot][...] = _dot_nt(qs_s[...], k_ref[0, rows_of(j), :])

    def values(j, slot):
        pv = _dot(p_b[slot][...], v_ref[0, rows_of(j), :])
        for h in range(nh):
            acc_s[h] = a_b[slot][h] * acc_s[h] + pv[h * tq:(h + 1) * tq, :]

    def softmax(j, slot, diagonal):
        if diagonal:
            tok = t0 + lax.broadcasted_iota(jnp.int32, (tq, tk), 0)
            key = j * tk + lax.broadcasted_iota(jnp.int32, (tq, tk), 1)
            bias = jnp.where(key <= tok, 0.0, NEG)
        for h in range(nh):
            rows = slice(h * tq, (h + 1) * tq)
            s = s_b[slot][rows, :]
            tiles = _lane_tiles(s + bias if diagonal else s)
            m_prev = m_s[h]
            m_new = jnp.maximum(m_prev, _row_stat(tiles, jnp.maximum, jnp.max))
            m_s[h] = m_new
            a_b[slot][h] = jnp.exp2(m_prev - m_new)
            p_b[slot][rows, :] = jnp.concatenate([jnp.exp2(t - m_new) for t in tiles], axis=1).astype(BF16)

    def stage(j, slot, diagonal=False):
        values(j - 1, 1 - slot)
        if not diagonal:
            scores(j + 1, 1 - slot)
        softmax(j, slot, diagonal)

    scores(0, 0)

    @pl.when(n_tiles == 1)
    def _():
        softmax(0, 0, True)
        values(0, 0)

    @pl.when(n_tiles >= 2)
    def _():
        scores(1, 1)
        softmax(0, 0, False)

        def pair(i, carry):
            stage(2 * i + 1, 1)
            stage(2 * i + 2, 0)
            return carry

        lax.fori_loop(0, (n_tiles - 2) // 2, pair, 0)
        last = n_tiles - 1

        @pl.when(last % 2 == 0)
        def _():
            stage(last - 1, 1)
            stage(last, 0, diagonal=True)
            values(last, 0)

        @pl.when(last % 2 == 1)
        def _():
            stage(last, 1, diagonal=True)
            values(last, 1)

    _store_heads_normalized(o_ref, [acc_s[h] for h in range(nh)], tq)


def _flash(nq, kd, vd, selt, *, window, tq, tk=None):
    bsz, seq, _ = nq.shape
    g = NSA_KV_GROUPS
    gw = HEADS_PER_GROUP * HEAD_DIM
    nh = HEADS_PER_GROUP
    qspec = pl.BlockSpec((1, tq, gw), lambda b, gi, i: (b, i, gi))
    kvspec = pl.BlockSpec((1, seq, LANES), lambda b, gi, i: (b, 0, gi))
    in_specs = [qspec, kvspec, kvspec]
    args = [nq, kd, vd]
    if window:
        assert WINDOW % tq == 0 and seq >= WINDOW + tq
        body = functools.partial(_win_kernel, tq=tq)
        scratch = [pltpu.VMEM((nh * tq, LANES), BF16),
                   pltpu.VMEM((2, nh * tq // 2, WINDOW + tq), F32)]
    else:
        assert seq % tk == 0 and tk % tq == 0
        assert seq // SLC_BLOCK <= HEAD_DIM
        in_specs.append(pl.BlockSpec((1, 1, tq, LANES), lambda b, gi, i: (b, gi, i, 0)))
        args.append(selt)
        body = functools.partial(_slc_kernel, tq=tq, tk=tk)
        stat = pltpu.VMEM((nh, tq, LANES), F32)
        scratch = ([pltpu.VMEM((nh * tq, LANES), BF16)] + [pltpu.VMEM((nh * tq, tk), F32)] * 2 +
                   [pltpu.VMEM((nh * tq, tk), BF16)] * 2 + [stat] * 4)
    return pl.pallas_call(
        body,
        grid=(bsz, g, seq // tq),
        in_specs=in_specs,
        out_specs=qspec,
        out_shape=jax.ShapeDtypeStruct((bsz, seq, NSA_WIDTH), F32),
        scratch_shapes=scratch,
        compiler_params=_params("parallel", "parallel", "arbitrary"),
        name="win_attn" if window else "slc_attn",
    )(*args)


def _out_kernel(x_ref, gate_ref, gpost_ref, ret_ref, oc_ref, os_ref, ow_ref, gl_ref, ngs_ref,
                wo_ref, o_ref):
    ts = x_ref.shape[1]
    gates = gl_ref[0]
    g_hi = gates.astype(BF16)
    g_lo = (gates - g_hi.astype(F32)).astype(BF16)
    src = lax.broadcasted_iota(jnp.int32, (LANES, NSA_WIDTH), 0)
    head = lax.broadcasted_iota(jnp.int32, (LANES, NSA_WIDTH), 1) >> HEAD_SHIFT
    att = jnp.zeros((ts, NSA_WIDTH), F32)
    for i, br in enumerate((oc_ref, os_ref, ow_ref)):
        expand = jnp.where(src == 3 * head + i, 1.0, 0.0).astype(BF16)
        att = att + (_dot(g_hi, expand) + _dot(g_lo, expand)) * br[0]
    att = att * ngs_ref[0]
    y = _dot(ret_ref[0].astype(BF16), wo_ref[:RET_WIDTH, :]) + _dot(att.astype(BF16), wo_ref[RET_WIDTH:, :])
    ms = jnp.mean(y * y, axis=-1, keepdims=True)
    y = y * lax.rsqrt(ms + EPS) * gpost_ref[...]
    o_ref[0] = x_ref[0] + gate_ref[0] * y


def _out(x, gate, g_post, ret, o_cmp, o_slc, o_win, gl, ngs, w_out, ts):
    bsz, seq, d = x.shape
    row = lambda w: pl.BlockSpec((1, ts, w), lambda b, s: (b, s, 0))
    return pl.pallas_call(
        _out_kernel,
        grid=(bsz, seq // ts),
        in_specs=[row(d), pl.BlockSpec((1, 1, d), lambda b, s: (b, 0, 0)),
                  pl.BlockSpec((1, d), lambda b, s: (0, 0)),
                  row(RET_WIDTH), row(NSA_WIDTH), row(NSA_WIDTH), row(NSA_WIDTH), row(LANES),
                  row(NSA_WIDTH), pl.BlockSpec(w_out.shape, lambda b, s: (0, 0))],
        out_specs=row(d),
        out_shape=jax.ShapeDtypeStruct((bsz, seq, d), x.dtype),
        compiler_params=_params("parallel", "arbitrary"),
        name="out",
    )(x, gate, g_post, ret, o_cmp, o_slc, o_win, gl, ngs, w_out)


def _retention_tables():
    h = RET_HEADS
    c = RET_CHUNK
    log_g = jnp.log1p(-jnp.power(2.0, -5.0 - jnp.arange(h, dtype=F32)))
    idx = jnp.arange(c, dtype=F32)
    diff = idx[:, None] - idx[None, :]
    dec = jnp.where(diff[None] >= 0, jnp.exp(jnp.maximum(diff, 0.0)[None] * log_g[:, None, None]), 0.0)
    zeta = jnp.exp((c - 1 - idx)[None, :] * log_g[:, None])
    xi = jnp.exp((idx + 1.0)[None, :] * log_g[:, None])
    cd = jnp.exp(c * log_g)
    to_lanes = lambda t: jnp.repeat(t.reshape(h // 2, 2, c).transpose(0, 2, 1), HEAD_DIM, axis=2)
    cd_l = jnp.repeat(cd.reshape(h // 2, 1, 2), HEAD_DIM, axis=2)
    return dec, to_lanes(zeta), to_lanes(xi), cd_l


def _overlap_t(seq):
    ncp = seq // CMP_STRIDE
    nc = (seq - CMP_BLOCK) // CMP_STRIDE + 1
    ns = seq // SLC_BLOCK
    cs = np.arange(ncp) * CMP_STRIDE
    ce = cs + CMP_BLOCK - 1
    ss = np.arange(ns) * SLC_BLOCK
    ov = (cs[None, :] <= ss[:, None] + SLC_BLOCK - 1) & (ce[None, :] >= ss[:, None]) & (np.arange(ncp) < nc)[None, :]
    return jnp.asarray(ov.astype(np.float32), dtype=BF16)


def _layer(x, c, positions, w_ada, b_ada, g_pre, g_post, w_in, w_out,
           pe_k, w1_k, w2_k, pe_v, w1_v, w2_v):
    bsz, seq, d = x.shape
    mod = _ada(c, w_ada, b_ada)
    shift, scale, gate = [t.reshape(bsz, 1, d) for t in jnp.split(mod, 3, axis=-1)]

    offs = np.cumsum((RET_WIDTH,) * 4 + (NSA_WIDTH, 6 * KV_WIDTH, 3 * NSA_HEADS, NSA_WIDTH))
    w_gl = w_in[:, offs[5]:offs[6]]
    w_all = jnp.concatenate(
        [w_in[:, :offs[5]], w_in[:, offs[6]:], w_gl,
         jnp.zeros((d, LANES - w_gl.shape[1]), w_in.dtype)], axis=1).astype(BF16)

    lanes = jnp.arange(LANES)
    half_r = HEAD_DIM // 2
    inv_r = jnp.power(RET_ROPE_BASE, -jnp.arange(half_r, dtype=F32) / half_r)
    half_n = ROPE_DIM // 2
    inv_n = jnp.power(NSA_ROPE_BASE, -jnp.arange(half_n, dtype=F32) / half_n)
    inv_ret_row = inv_r[lanes % half_r].reshape(1, LANES)
    inv_nsa_row = inv_n[lanes % half_n].reshape(1, LANES)
    pos3 = positions.reshape(bsz, seq, 1)

    (rq, rk, rv, rgs, nq, cmps, ksd, vsd, kwd, vwd, ngs, gl) = _proj(
        x, shift, scale, g_pre.reshape(1, d), pos3, inv_ret_row, inv_nsa_row, w_all, ts=256)

    dec, zeta_l, xi_l, cd_l = _retention_tables()
    ret = _retention(rq, rk, rv, rgs, dec, zeta_l, xi_l, cd_l, rows=min(512, seq))

    ncp = seq // CMP_STRIDE
    nc = (seq - CMP_BLOCK) // CMP_STRIDE + 1
    pad_l = lambda t: jnp.concatenate([t, jnp.zeros_like(t)], axis=-1)
    pe2 = jnp.stack([pad_l(pe_k), pad_l(pe_v)])
    w1r = jnp.stack([w1_k, w1_v]).reshape(2, CMP_BLOCK, HEAD_DIM, CMP_HIDDEN)
    w1p = jnp.concatenate([w1r, jnp.zeros_like(w1r)], axis=2).astype(BF16)
    w2s = jnp.stack([w2_k, w2_v])
    w2d = jnp.concatenate([w2s, w2s], axis=-1).astype(BF16)
    cmp_end = np.arange(nc) * CMP_STRIDE + CMP_BLOCK - 1
    pos_cmp = jnp.pad(positions[:, cmp_end], ((0, 0), (0, ncp - nc))).reshape(bsz, ncp, 1)
    cmpkv = _compress(cmps, pe2, w1p, w2d, pos_cmp, inv_nsa_row)

    o_cmp, selt = _cmpsel(nq, cmpkv, _overlap_t(seq), tq=min(256, seq))
    o_slc = _flash(nq, ksd, vsd, selt, window=False, tq=128, tk=512)
    o_win = _flash(nq, kwd, vwd, None, window=True, tq=256)

    return _out(x, gate, g_post.reshape(1, d), ret, o_cmp, o_slc, o_win, gl, ngs,
                w_out.astype(BF16), ts=256)


def kernel(x, c, positions, w_ada, b_ada, g_pre, g_post, w_in, w_out, cmp_pe_k, cmp_w1_k, cmp_w2_k, cmp_pe_v, cmp_w1_v, cmp_w2_v):
    for l in range(w_in.shape[0]):
        x = _layer(x, c, positions, w_ada[l], b_ada[l], g_pre[l], g_post[l], w_in[l], w_out[l],
                   cmp_pe_k[l], cmp_w1_k[l], cmp_w2_k[l], cmp_pe_v[l], cmp_w1_v[l], cmp_w2_v[l])
    return x
```

```python
import functools

import numpy as np
import jax
import jax.numpy as jnp
from jax import lax
from jax.experimental import pallas as pl
from jax.experimental.pallas import tpu as pltpu

F32 = jnp.float32
BF16 = jnp.bfloat16

LANES = 128
HEAD_DIM = 64
RET_HEADS = 8
NSA_HEADS = 8
NSA_KV_GROUPS = 2
HEADS_PER_GROUP = NSA_HEADS // NSA_KV_GROUPS
RET_WIDTH = RET_HEADS * HEAD_DIM
NSA_WIDTH = NSA_HEADS * HEAD_DIM
KV_WIDTH = NSA_KV_GROUPS * HEAD_DIM
RET_CHUNK = 128
RET_ROPE_BASE = 10000.0
NSA_ROPE_BASE = 500000.0
ROPE_DIM = HEAD_DIM // 4
CMP_BLOCK = 32
CMP_STRIDE = 16
CMP_HIDDEN = 256
SLC_BLOCK = 64
SLC_TOPK = 16
WINDOW = 512
NEG = -1e30
LOG2E = 1.4426950408889634
BIG = 1e9
EPS = 1e-6
GN_EPS = 1e-5
VMEM_LIMIT = 48 * 1024 * 1024

NT_DIMS = (((1,), (1,)), ((), ()))
TN_DIMS = (((0,), (0,)), ((), ()))


def _dot(a, b):
    return jnp.dot(a, b, preferred_element_type=F32)


def _dot_nt(a, b):
    return lax.dot_general(a, b, NT_DIMS, preferred_element_type=F32)


def _dot_tn(a, b):
    return lax.dot_general(a, b, TN_DIMS, preferred_element_type=F32)


def _silu(z):
    return z * jax.nn.sigmoid(z)


def _params(*sem):
    return pltpu.CompilerParams(dimension_semantics=sem, vmem_limit_bytes=VMEM_LIMIT)


HEAD_SHIFT = 6
SLC_SHIFT = 6
assert 1 << HEAD_SHIFT == HEAD_DIM and 1 << SLC_SHIFT == SLC_BLOCK


def _lane_in_head(shape):
    return lax.broadcasted_iota(jnp.int32, shape, len(shape) - 1) & (HEAD_DIM - 1)


def _ada_kernel(c_ref, w_ref, b_ref, o_ref):
    a = _silu(c_ref[...])
    o_ref[...] = jnp.dot(a, w_ref[...], precision=lax.Precision.HIGHEST,
                         preferred_element_type=F32) + b_ref[...]


def _ada(c, w, b):
    bsz, d = c.shape
    n = w.shape[1]
    tn = 1024
    return pl.pallas_call(
        _ada_kernel,
        grid=(n // tn,),
        in_specs=[pl.BlockSpec((bsz, d), lambda j: (0, 0)),
                  pl.BlockSpec((d, tn), lambda j: (0, j)),
                  pl.BlockSpec((1, tn), lambda j: (0, j))],
        out_specs=pl.BlockSpec((bsz, tn), lambda j: (0, j)),
        out_shape=jax.ShapeDtypeStruct((bsz, n), F32),
        compiler_params=_params("arbitrary"),
        name="ada",
    )(c, w, b.reshape(1, n))


_O_RQ, _O_RK, _O_RV, _O_RG, _O_NQ = 0, 512, 1024, 1536, 2048
_O_KV = 2560
_O_NG = 3328
_O_GL = 3840
_PROJ_COLS = 3968


def _rot_ret(z, cos, sin_signed, m):
    partner = jnp.where(m < HEAD_DIM // 2, pltpu.roll(z, LANES - HEAD_DIM // 2, 1),
                        pltpu.roll(z, HEAD_DIM // 2, 1))
    return z * cos + partner * sin_signed


def _rot_nsa(z, cos, sin_signed, m):
    half = ROPE_DIM // 2
    partner = jnp.where(m < half, pltpu.roll(z, LANES - half, 1), pltpu.roll(z, half, 1))
    return z * cos + partner * sin_signed


def _nsa_tables(posf, inv_row, m):
    half = ROPE_DIM // 2
    ang = posf * inv_row
    cos = jnp.where(m < ROPE_DIM, jnp.cos(ang), 1.0)
    s = jnp.sin(ang)
    sin = jnp.where(m < half, -s, jnp.where(m < ROPE_DIM, s, 0.0))
    return cos, sin


def _dup_groups(z, m_lane):
    r = pltpu.roll(z, HEAD_DIM, 1)
    lo = m_lane < HEAD_DIM
    return jnp.where(lo, z, r), jnp.where(lo, r, z)


def _proj_kernel(x_ref, shift_ref, scale_ref, gpre_ref, pos_ref, invr_ref, invn_ref, w_ref,
                 rq_ref, rk_ref, rv_ref, rgs_ref, nq_ref, cmps_ref, ksd_ref, vsd_ref,
                 kwd_ref, vwd_ref, ngs_ref, gl_ref,
                 cr_s, sr_s, cn_s, sn_s, z_s):
    ts = x_ref.shape[1]
    x = x_ref[0]
    ms = jnp.mean(x * x, axis=-1, keepdims=True)
    y = x * lax.rsqrt(ms + EPS) * gpre_ref[...]
    h = y * (1.0 + scale_ref[0]) + shift_ref[0]
    hb = h.astype(BF16)

    posf = pos_ref[0].astype(F32)
    lane = lax.broadcasted_iota(jnp.int32, (ts, LANES), 1)
    m = lane & (HEAD_DIM - 1)
    ang = posf * invr_ref[...]
    cr_s[...] = jnp.cos(ang)
    s = jnp.sin(ang)
    sr_s[...] = jnp.where(m < HEAD_DIM // 2, -s, s)
    cn, sn = _nsa_tables(posf, invn_ref[...], m)
    cn_s[...] = cn
    sn_s[...] = sn

    def mm(off, width):
        z_s[:, :width] = _dot(hb, w_ref[:, off:off + width])
        return lambda c: z_s[:, c * LANES:(c + 1) * LANES]

    scale_q = HEAD_DIM ** -0.5
    tiles = lambda width: [(c, slice(c * LANES, (c + 1) * LANES)) for c in range(width // LANES)]
    z = mm(_O_RQ, RET_WIDTH)
    for c, sl in tiles(RET_WIDTH):
        rq_ref[0, :, sl] = (_rot_ret(z(c), cr_s[...], sr_s[...], m) * scale_q).astype(BF16)
    z = mm(_O_RK, RET_WIDTH)
    for c, sl in tiles(RET_WIDTH):
        rk_ref[0, :, sl] = _rot_ret(z(c), cr_s[...], sr_s[...], m).astype(BF16)
    z = mm(_O_RV, RET_WIDTH)
    for c, sl in tiles(RET_WIDTH):
        rv_ref[0, :, sl] = z(c).astype(BF16)
    z = mm(_O_RG, RET_WIDTH)
    for c, sl in tiles(RET_WIDTH):
        rgs_ref[0, :, sl] = _silu(z(c))
    z = mm(_O_NQ, NSA_WIDTH)
    for c, sl in tiles(NSA_WIDTH):
        nq_ref[0, :, sl] = (_rot_nsa(z(c), cn_s[...], sn_s[...], m) * (scale_q * LOG2E)).astype(BF16)

    z = mm(_O_KV, 6 * KV_WIDTH)
    kc0, kc1 = _dup_groups(z(0), lane)
    vc0, vc1 = _dup_groups(z(1), lane)
    for i, t in enumerate((kc0, kc1, vc0, vc1)):
        cmps_ref[0, :, i * LANES:(i + 1) * LANES] = t
    lo = lane < HEAD_DIM
    tok = pl.program_id(1) * ts + lax.broadcasted_iota(jnp.int32, (ts, LANES), 0)
    onehot = jnp.where(lane - HEAD_DIM == (tok >> SLC_SHIFT), 1.0, 0.0)
    ones = jnp.ones((ts, LANES), F32)
    for c, ref, rotate, upper in ((2, ksd_ref, True, onehot), (3, vsd_ref, False, ones),
                                  (4, kwd_ref, True, None), (5, vwd_ref, False, ones)):
        t = _rot_nsa(z(c), cn_s[...], sn_s[...], m) if rotate else z(c)
        if upper is None:
            d0, d1 = _dup_groups(t, lane)
        else:
            d0, d1 = jnp.where(lo, t, upper), jnp.where(lo, pltpu.roll(t, HEAD_DIM, 1), upper)
        ref[0, :, :LANES] = d0.astype(BF16)
        ref[0, :, LANES:] = d1.astype(BF16)

    z = mm(_O_NG, NSA_WIDTH + LANES)
    for c, sl in tiles(NSA_WIDTH):
        ngs_ref[0, :, sl] = _silu(z(c))
    gl_ref[0] = jax.nn.sigmoid(z(NSA_WIDTH // LANES))


def _proj(x, shift, scale, g_pre, pos3, inv_ret_row, inv_nsa_row, w_all, ts):
    bsz, seq, d = x.shape
    row = lambda w: pl.BlockSpec((1, ts, w), lambda b, s: (b, s, 0))
    vec = pl.BlockSpec((1, 1, d), lambda b, s: (b, 0, 0))
    const2 = lambda shp: pl.BlockSpec(shp, lambda b, s: (0, 0))
    outs = [(RET_WIDTH, BF16), (RET_WIDTH, BF16), (RET_WIDTH, BF16), (RET_WIDTH, F32),
            (NSA_WIDTH, BF16), (4 * LANES, F32), (2 * LANES, BF16), (2 * LANES, BF16),
            (2 * LANES, BF16), (2 * LANES, BF16), (NSA_WIDTH, F32), (LANES, F32)]
    return pl.pallas_call(
        _proj_kernel,
        grid=(bsz, seq // ts),
        in_specs=[row(d), vec, vec, const2((1, d)), row(1), const2((1, LANES)), const2((1, LANES)),
                  const2((d, _PROJ_COLS))],
        out_specs=[row(w) for w, _ in outs],
        out_shape=[jax.ShapeDtypeStruct((bsz, seq, w), dt) for w, dt in outs],
        scratch_shapes=[pltpu.VMEM((ts, LANES), F32)] * 4 + [pltpu.VMEM((ts, 6 * KV_WIDTH), F32)],
        compiler_params=_params("parallel", "arbitrary"),
        name="proj",
    )(x, shift, scale, g_pre, pos3, inv_ret_row, inv_nsa_row, w_all)


def _ret_kernel(q_ref, k_ref, v_ref, g_ref, dec_ref, zeta_ref, xi_ref, cd_ref, o_ref, r_s):
    rows = q_ref.shape[1]
    cc = RET_CHUNK

    @pl.when(pl.program_id(2) == 0)
    def _():
        r_s[...] = jnp.zeros_like(r_s)

    lane = lax.broadcasted_iota(jnp.int32, (cc, LANES), 1)
    rowi = lax.broadcasted_iota(jnp.int32, (cc, LANES), 0)
    lo = lane < HEAD_DIM
    blockdiag = (rowi < HEAD_DIM) == lo
    inv_n = 1.0 / HEAD_DIM
    chunks = [slice(c * cc, (c + 1) * cc) for c in range(rows // cc)]
    s_ab, kvs = [], []
    for sl in chunks:
        q, k, v = q_ref[0, sl, :], k_ref[0, sl, :], v_ref[0, sl, :]
        zero = jnp.zeros_like(q)
        s_ab.append((_dot_nt(jnp.where(lo, q, zero), k), _dot_nt(jnp.where(lo, zero, q), k)))
        kz = (k.astype(F32) * zeta_ref[0]).astype(BF16)
        kvs.append(jnp.where(blockdiag, _dot_tn(kz, v), 0.0))
    inners = []
    for sl, (s_a, s_b) in zip(chunks, s_ab):
        v = v_ref[0, sl, :]
        inners.append(jnp.where(lo, _dot((s_a * dec_ref[0]).astype(BF16), v),
                                _dot((s_b * dec_ref[1]).astype(BF16), v)))
    r = r_s[...]
    outs = []
    for sl, inner, kv in zip(chunks, inners, kvs):
        outs.append(inner + _dot(q_ref[0, sl, :], r.astype(BF16)) * xi_ref[0])
        r = r * cd_ref[0] + kv
    r_s[...] = r
    for sl, o in zip(chunks, outs):
        sum_a = jnp.sum(jnp.where(lo, o, 0.0), axis=-1, keepdims=True)
        sum_b = jnp.sum(jnp.where(lo, 0.0, o), axis=-1, keepdims=True)
        d = o - jnp.where(lo, sum_a, sum_b) * inv_n
        d2 = d * d
        var_a = jnp.sum(jnp.where(lo, d2, 0.0), axis=-1, keepdims=True)
        var_b = jnp.sum(jnp.where(lo, 0.0, d2), axis=-1, keepdims=True)
        var = jnp.where(lo, var_a, var_b) * inv_n
        o_ref[0, sl, :] = d * lax.rsqrt(var + GN_EPS) * g_ref[0, sl, :]


def _retention(rq, rk, rv, rgs, dec, zeta_l, xi_l, cd_l, rows):
    bsz, seq, _ = rq.shape
    npair = RET_HEADS // 2
    blk = pl.BlockSpec((1, rows, LANES), lambda b, p, s: (b, s, p))
    return pl.pallas_call(
        _ret_kernel,
        grid=(bsz, npair, seq // rows),
        in_specs=[blk, blk, blk, blk,
                  pl.BlockSpec((2, RET_CHUNK, RET_CHUNK), lambda b, p, s: (p, 0, 0)),
                  pl.BlockSpec((1, RET_CHUNK, LANES), lambda b, p, s: (p, 0, 0)),
                  pl.BlockSpec((1, RET_CHUNK, LANES), lambda b, p, s: (p, 0, 0)),
                  pl.BlockSpec((1, 1, LANES), lambda b, p, s: (p, 0, 0))],
        out_specs=blk,
        out_shape=jax.ShapeDtypeStruct((bsz, seq, RET_WIDTH), F32),
        scratch_shapes=[pltpu.VMEM((LANES, LANES), F32)],
        compiler_params=_params("parallel", "parallel", "arbitrary"),
        name="retention",
    )(rq, rk, rv, rgs, dec, zeta_l, xi_l, cd_l)


def _cmp_kernel(src_ref, pe_ref, w1_ref, w2_ref, pos_ref, invn_ref, o_ref):
    ncp = o_ref.shape[3]
    half = CMP_BLOCK // 2
    p_lo = jnp.zeros((ncp, CMP_HIDDEN), F32)
    p_hi = jnp.zeros((ncp, CMP_HIDDEN), F32)
    for l in range(half):
        t = src_ref[0, pl.ds(l, ncp, stride=CMP_STRIDE), :]
        p_lo = p_lo + _dot((t + pe_ref[0, l:l + 1, :]).astype(BF16), w1_ref[0, l])
        p_hi = p_hi + _dot((t + pe_ref[0, half + l:half + l + 1, :]).astype(BF16), w1_ref[0, half + l])
    hidden = p_lo + pltpu.roll(p_hi, ncp - 1, 0)
    comp = _dot(_silu(hidden).astype(BF16), w2_ref[0])
    m = _lane_in_head(comp.shape)
    cos, sin = _nsa_tables(pos_ref[0].astype(F32), invn_ref[...], m)
    rotated = _rot_nsa(comp, cos, sin, m)
    is_key = pl.program_id(1) == 0
    o_ref[0, 0, 0] = jnp.where(is_key, rotated, comp).astype(BF16)


def _compress(cmps, pe2, w1p, w2d, pos_cmp, inv_nsa_row):
    bsz, seq, _ = cmps.shape
    ncp = seq // CMP_STRIDE
    g = NSA_KV_GROUPS
    return pl.pallas_call(
        _cmp_kernel,
        grid=(bsz, 2, g),
        in_specs=[pl.BlockSpec((1, seq, LANES), lambda b, j, gi: (b, 0, j * g + gi)),
                  pl.BlockSpec((1, CMP_BLOCK, LANES), lambda b, j, gi: (j, 0, 0)),
                  pl.BlockSpec((1, CMP_BLOCK, LANES, CMP_HIDDEN), lambda b, j, gi: (j, 0, 0, 0)),
                  pl.BlockSpec((1, CMP_HIDDEN, LANES), lambda b, j, gi: (j, 0, 0)),
                  pl.BlockSpec((1, ncp, 1), lambda b, j, gi: (b, 0, 0)),
                  pl.BlockSpec((1, LANES), lambda b, j, gi: (0, 0))],
        out_specs=pl.BlockSpec((1, 1, 1, ncp, LANES), lambda b, j, gi: (b, j, gi, 0, 0)),
        out_shape=jax.ShapeDtypeStruct((bsz, 2, g, ncp, LANES), BF16),
        compiler_params=_params("parallel", "arbitrary", "arbitrary"),
        name="compress",
    )(cmps, pe2, w1p, w2d, pos_cmp, inv_nsa_row)


def _cmpsel_kernel(q_ref, kc_ref, vc_ref, ovt_ref, o_ref, selt_ref, qs_s, score_s, rank_s):
    nh = HEADS_PER_GROUP
    sub = 8
    tq = q_ref.shape[1]
    ncp = kc_ref.shape[3]
    nblk = ovt_ref.shape[0]
    t0 = pl.program_id(2) * tq
    kc = kc_ref[0, 0, 0]
    vc = vc_ref[0, 0, 0]

    tok = t0 + lax.broadcasted_iota(jnp.int32, (tq, LANES), 0)
    lane = lax.broadcasted_iota(jnp.int32, (tq, LANES), 1)
    masks = [(lane + c * LANES) * CMP_STRIDE + (CMP_BLOCK - 1) <= tok for c in range(ncp // LANES)]
    _stack_head_queries(q_ref, qs_s, tq)
    s_all = _dot_nt(qs_s[...], kc)
    psum, probs = None, []
    for h in range(nh):
        tiles = [jnp.where(mk, t, NEG) for mk, t in zip(masks, _lane_tiles(s_all[h * tq:(h + 1) * tq, :]))]
        mx = _row_stat(tiles, jnp.maximum, jnp.max)
        es = [jnp.where(mk, jnp.exp2(t - mx), 0.0) for mk, t in zip(masks, tiles)]
        den = _row_stat(es, jnp.add, jnp.sum)
        inv = jnp.where(den > 0.0, 1.0 / den, 0.0)
        ps = [e * inv for e in es]
        psum = ps if psum is None else [a + b for a, b in zip(psum, ps)]
        probs.append(jnp.concatenate(ps, axis=1).astype(BF16))
    pv = _dot(jnp.concatenate(probs, axis=0), vc)
    _store_heads(o_ref, [pv[h * tq:(h + 1) * tq, :] for h in range(nh)], tq)

    psum = jnp.concatenate(psum, axis=1)
    p_hi = psum.astype(BF16)
    p_lo = (psum - p_hi.astype(F32)).astype(BF16)
    imp = _dot_nt(ovt_ref[...], p_hi) + _dot_nt(ovt_ref[...], p_lo)
    blk = lax.broadcasted_iota(jnp.int32, (nblk, tq), 0)
    cur = (t0 + lax.broadcasted_iota(jnp.int32, (nblk, tq), 1)) >> SLC_SHIFT
    forced = (blk == 0) | (blk == cur) | (blk == cur - 1)
    score_s[...] = jnp.where(forced, BIG, jnp.where(blk <= cur, imp, NEG))

    rank_s[...] = jnp.zeros_like(rank_s)
    last_blk = (t0 + tq - 1) >> SLC_SHIFT
    row_in_group = lax.broadcasted_iota(jnp.int32, (sub, tq), 0)
    for ig in range(nblk // sub):
        @pl.when(ig * sub <= last_blk)
        def _():
            groups = [slice(r * sub, (r + 1) * sub) for r in range(nblk // sub)]
            scs = [score_s[g, :] for g in groups]
            cnts = [rank_s[g, :] for g in groups]
            for i in range(ig * sub, (ig + 1) * sub):
                ri = jnp.broadcast_to(score_s[i:i + 1, :], (sub, tq))
                for r, sc in enumerate(scs):
                    if r * sub > i:
                        beats = ri >= sc
                    elif (r + 1) * sub - 1 <= i:
                        beats = ri > sc
                    else:
                        beats = (ri > sc) | ((ri == sc) & (row_in_group > i - r * sub))
                    cnts[r] = cnts[r] + jnp.where(beats, 1.0, 0.0)
            for g, cnt in zip(groups, cnts):
                rank_s[g, :] = cnt
    sel = jnp.where((rank_s[...] < float(min(SLC_TOPK, nblk))) & (score_s[...] > 0.5 * NEG), 1.0, 0.0)
    pads = [jnp.zeros((HEAD_DIM, tq), F32), sel]
    if nblk < HEAD_DIM:
        pads.append(jnp.zeros((HEAD_DIM - nblk, tq), F32))
    selt_ref[0, 0] = jnp.concatenate(pads, axis=0).T


def _cmpsel(nq, cmpkv, ovt, tq):
    bsz, seq, _ = nq.shape
    g = NSA_KV_GROUPS
    ncp = cmpkv.shape[3]
    nblk = ovt.shape[0]
    gw = HEADS_PER_GROUP * HEAD_DIM
    return pl.pallas_call(
        _cmpsel_kernel,
        grid=(bsz, g, seq // tq),
        in_specs=[pl.BlockSpec((1, tq, gw), lambda b, gi, i: (b, i, gi)),
                  pl.BlockSpec((1, 1, 1, ncp, LANES), lambda b, gi, i: (b, 0, gi, 0, 0)),
                  pl.BlockSpec((1, 1, 1, ncp, LANES), lambda b, gi, i: (b, 1, gi, 0, 0)),
                  pl.BlockSpec((nblk, ncp), lambda b, gi, i: (0, 0))],
        out_specs=[pl.BlockSpec((1, tq, gw), lambda b, gi, i: (b, i, gi)),
                   pl.BlockSpec((1, 1, tq, LANES), lambda b, gi, i: (b, gi, i, 0))],
        out_shape=[jax.ShapeDtypeStruct((bsz, seq, NSA_WIDTH), F32),
                   jax.ShapeDtypeStruct((bsz, g, seq, LANES), F32)],
        scratch_shapes=[pltpu.VMEM((HEADS_PER_GROUP * tq, LANES), BF16),
                        pltpu.VMEM((nblk, tq), F32), pltpu.VMEM((nblk, tq), F32)],
        compiler_params=_params("parallel", "parallel", "arbitrary"),
        name="cmpsel",
    )(nq, cmpkv, cmpkv, ovt)


def _stack_head_queries(q_ref, qs_s, rows, src0=0, dst0=0):
    lo = lax.broadcasted_iota(jnp.int32, (rows, LANES), 1) < HEAD_DIM
    for c in range(HEADS_PER_GROUP // 2):
        q = q_ref[0, src0:src0 + rows, c * LANES:(c + 1) * LANES]
        zero = jnp.zeros_like(q)
        qs_s[dst0 + (2 * c) * rows:dst0 + (2 * c + 1) * rows, :] = jnp.where(lo, q, zero)
        qs_s[dst0 + (2 * c + 1) * rows:dst0 + (2 * c + 2) * rows, :] = jnp.where(lo, zero, q)


def _store_heads(o_ref, outs, tq):
    lo = lax.broadcasted_iota(jnp.int32, (tq, LANES), 1) < HEAD_DIM
    for c in range(HEADS_PER_GROUP // 2):
        o_ref[0, :, c * LANES:(c + 1) * LANES] = jnp.where(lo, outs[2 * c], outs[2 * c + 1])


def _store_heads_normalized(o_ref, accs, tq, rows=slice(None)):
    lo = lax.broadcasted_iota(jnp.int32, (tq, LANES), 1) < HEAD_DIM
    outs = [a * (1.0 / jnp.where(lo, pltpu.roll(a, HEAD_DIM, 1), 1.0)) for a in accs]
    for c in range(HEADS_PER_GROUP // 2):
        o_ref[0, rows, c * LANES:(c + 1) * LANES] = jnp.where(
            lo, outs[2 * c], pltpu.roll(outs[2 * c + 1], HEAD_DIM, 1))


def _lane_tiles(z):
    return [z[:, c * LANES:(c + 1) * LANES] for c in range(z.shape[1] // LANES)]


def _row_stat(tiles, combine, reduce):
    acc = functools.reduce(combine, tiles)
    return jnp.broadcast_to(reduce(acc, axis=-1, keepdims=True), acc.shape)


def _win_kernel(q_ref, k_ref, v_ref, o_ref, qs_s, s_s, *, tq):
    nh = HEADS_PER_GROUP
    span = WINDOW + tq
    t0 = pl.program_id(2) * tq
    start = pl.multiple_of(jnp.maximum(t0 - WINDOW, 0), LANES)
    k = k_ref[0, pl.ds(start, span), :]
    v = v_ref[0, pl.ds(start, span), :]
    th = tq // 2
    for r in range(2):
        _stack_head_queries(q_ref, qs_s, th, src0=r * th, dst0=r * nh * th)
        s_s[r] = _dot_nt(qs_s[r * nh * th:(r + 1) * nh * th, :], k)
    outs = [[None, None] for _ in range(nh)]
    for r in range(2):
        tok = t0 + r * th + lax.broadcasted_iota(jnp.int32, (th, span), 0)
        key = start + lax.broadcasted_iota(jnp.int32, (th, span), 1)
        bias = jnp.where((key <= tok) & (tok - key < WINDOW), 0.0, NEG)
        probs = []
        for h in range(nh):
            tiles = _lane_tiles(s_s[r, h * th:(h + 1) * th, :] + bias)
            m = _row_stat(tiles, jnp.maximum, jnp.max)
            probs.append(jnp.concatenate([jnp.exp2(t - m) for t in tiles], axis=1).astype(BF16))
        pv = _dot(jnp.concatenate(probs, axis=0), v)
        for h in range(nh):
            outs[h][r] = pv[h * th:(h + 1) * th, :]
    _store_heads_normalized(o_ref, [jnp.concatenate(o, axis=0) for o in outs], tq)


SLC_UNROLL = 4
SLC_STATE_BUFFERS = 8
assert SLC_UNROLL % 2 == 0 and SLC_UNROLL + 1 <= SLC_STATE_BUFFERS


def _slc_kernel(q_ref, k_ref, v_ref, selt_ref, o_ref, qs_s, s0_s, s1_s, p0_s, p1_s, a0_s, a1_s,
                m_s, acc_s, *, tq, tk):
    nh = HEADS_PER_GROUP
    seq = q_ref.shape[1]
    n_qt = seq // tq
    s_b, p_b, a_b = (s0_s, s1_s), (p0_s, p1_s), (a0_s, a1_s)
    lo = lax.broadcasted_iota(jnp.int32, (tq, LANES), 1) < HEAD_DIM

    def n_tiles(qi):
        return (qi * tq + tq - 1) // tk + 1

    n_items = sum((qi * tq + tq - 1) // tk + 1 for qi in range(n_qt))

    def advance(qi, j):
        wrap = j + 1 >= n_tiles(qi)
        return jnp.where(wrap, jnp.minimum(qi + 1, n_qt - 1), qi), jnp.where(wrap, 0, j + 1)

    def q_rows(qi):
        return pl.ds(pl.multiple_of(qi * tq, tq), tq)

    def k_rows(j):
        return pl.ds(pl.multiple_of(j * tk, tk), tk)

    def stack_queries(qi):
        par = qi % SLC_STATE_BUFFERS
        unselected = jnp.where(selt_ref[0, 0, q_rows(qi), :] > 0.5, 0.0, NEG)
        for c in range(nh // 2):
            q = q_ref[0, q_rows(qi), c * LANES:(c + 1) * LANES].astype(F32)
            qs_s[par, (2 * c) * tq:(2 * c + 1) * tq, :] = jnp.where(lo, q, unselected).astype(BF16)
            qs_s[par, (2 * c + 1) * tq:(2 * c + 2) * tq, :] = jnp.where(
                lo, pltpu.roll(q, HEAD_DIM, 1), unselected).astype(BF16)

    def scores(qi, j, slot):
        s_b[slot][...] = _dot_nt(qs_s[qi % SLC_STATE_BUFFERS], k_ref[0, k_rows(j), :])

    def values(qi, j, slot):
        pv = _dot(p_b[slot][...], v_ref[0, k_rows(j), :])
        par = qi % SLC_STATE_BUFFERS
        for h in range(nh):
            acc_s[par, h] = a_b[slot][h] * acc_s[par, h] + pv[h * tq:(h + 1) * tq, :]

    def softmax(qi, j, slot):
        tok = qi * tq + lax.broadcasted_iota(jnp.int32, (tq, tk), 0)
        key = j * tk + lax.broadcasted_iota(jnp.int32, (tq, tk), 1)
        bias = jnp.where(key <= tok, 0.0, NEG)
        par = qi % SLC_STATE_BUFFERS
        for h in range(nh):
            rows = slice(h * tq, (h + 1) * tq)
            tiles = _lane_tiles(s_b[slot][rows, :] + bias)
            m_prev = jnp.where(j == 0, NEG, m_s[par, h])
            m_new = jnp.maximum(m_prev, _row_stat(tiles, jnp.maximum, jnp.max))
            m_s[par, h] = m_new
            a_b[slot][h] = jnp.exp2(m_prev - m_new)
            p_b[slot][rows, :] = jnp.concatenate([jnp.exp2(t - m_new) for t in tiles], axis=1).astype(BF16)

    def finalize(qi):
        _store_heads_normalized(o_ref, [acc_s[qi % SLC_STATE_BUFFERS, h] for h in range(nh)], tq, q_rows(qi))

    def is_last(qi, j):
        return (j + 1 >= n_tiles(qi)).astype(jnp.int32)

    def finalize_if(flag, qi):
        @pl.when(flag == 1)
        def _():
            finalize(qi)

    def block(carry, n_stages):
        fin, (qp, jp, p_last), (qc, jc), (qn, jn) = carry
        for flag, qi in fin:
            finalize_if(flag, qi)
        items = [(qp, jp, p_last), (qc, jc, is_last(qc, jc)), (qn, jn, is_last(qn, jn))]
        for _ in range(n_stages - 1):
            qx, jx = advance(items[-1][0], items[-1][1])
            items.append((qx, jx, is_last(qx, jx)))
        for qx, jx, _ in items[2:]:
            @pl.when(jx == 0)
            def _():
                stack_queries(qx)
        for t in range(n_stages):
            slot = t % 2
            values(items[t][0], items[t][1], 1 - slot)
            scores(items[t + 2][0], items[t + 2][1], 1 - slot)
            softmax(items[t + 1][0], items[t + 1][1], slot)
        fin = tuple((items[t][2], items[t][0]) for t in range(n_stages))
        nxt = advance(items[-1][0], items[-1][1])
        return fin, items[n_stages], items[n_stages + 1][:2], nxt

    m_s[...] = jnp.full_like(m_s, NEG)
    acc_s[...] = jnp.zeros_like(acc_s)
    p1_s[...] = jnp.zeros_like(p1_s)
    a1_s[...] = jnp.ones_like(a1_s)
    zero = jnp.int32(0)
    stack_queries(zero)
    scores(zero, zero, 0)
    unroll = SLC_UNROLL
    carry = (((zero, zero),) * unroll, (zero, zero, zero), (zero, zero), advance(zero, zero))
    carry = lax.fori_loop(0, n_items // unroll, lambda i, c: block(c, unroll), carry)
    if n_items % unroll:
        carry = block(carry, n_items % unroll)
    fin, (qp, jp, _), _, _ = carry
    for flag, qi in fin:
        finalize_if(flag, qi)
    values(qp, jp, (n_items - 1) % 2)
    finalize(qp)


def _slc_attention(nq, kd, vd, selt, *, tq, tk):
    bsz, seq, _ = nq.shape
    g = NSA_KV_GROUPS
    gw = HEADS_PER_GROUP * HEAD_DIM
    nh = HEADS_PER_GROUP
    assert seq % tk == 0 and tk % tq == 0
    assert seq // SLC_BLOCK <= HEAD_DIM
    whole = lambda w: pl.BlockSpec((1, seq, w), lambda b, gi: (b, 0, gi))
    stat = pltpu.VMEM((nh, tq, LANES), F32)
    state = pltpu.VMEM((SLC_STATE_BUFFERS, nh, tq, LANES), F32)
    return pl.pallas_call(
        functools.partial(_slc_kernel, tq=tq, tk=tk),
        grid=(bsz, g),
        in_specs=[whole(gw), whole(LANES), whole(LANES),
                  pl.BlockSpec((1, 1, seq, LANES), lambda b, gi: (b, gi, 0, 0))],
        out_specs=whole(gw),
        out_shape=jax.ShapeDtypeStruct((bsz, seq, NSA_WIDTH), F32),
        scratch_shapes=([pltpu.VMEM((SLC_STATE_BUFFERS, nh * tq, LANES), BF16)] + [pltpu.VMEM((nh * tq, tk), F32)] * 2 +
                        [pltpu.VMEM((nh * tq, tk), BF16)] * 2 + [stat] * 2 + [state] * 2),
        compiler_params=_params("parallel", "arbitrary"),
        name="slc_attn",
    )(nq, kd, vd, selt)


def _win_attention(nq, kd, vd, *, tq):
    bsz, seq, _ = nq.shape
    g = NSA_KV_GROUPS
    gw = HEADS_PER_GROUP * HEAD_DIM
    nh = HEADS_PER_GROUP
    assert WINDOW % tq == 0 and seq >= WINDOW + tq
    qspec = pl.BlockSpec((1, tq, gw), lambda b, gi, i: (b, i, gi))
    kvspec = pl.BlockSpec((1, seq, LANES), lambda b, gi, i: (b, 0, gi))
    return pl.pallas_call(
        functools.partial(_win_kernel, tq=tq),
        grid=(bsz, g, seq // tq),
        in_specs=[qspec, kvspec, kvspec],
        out_specs=qspec,
        out_shape=jax.ShapeDtypeStruct((bsz, seq, NSA_WIDTH), F32),
        scratch_shapes=[pltpu.VMEM((nh * tq, LANES), BF16),
                        pltpu.VMEM((2, nh * tq // 2, WINDOW + tq), F32)],
        compiler_params=_params("parallel", "parallel", "arbitrary"),
        name="win_attn",
    )(nq, kd, vd)


def _out_kernel(x_ref, gate_ref, gpost_ref, ret_ref, oc_ref, os_ref, ow_ref, gl_ref, ngs_ref,
                wo_ref, o_ref):
    ts = x_ref.shape[1]
    gates = gl_ref[0]
    g_hi = gates.astype(BF16)
    g_lo = (gates - g_hi.astype(F32)).astype(BF16)
    src = lax.broadcasted_iota(jnp.int32, (LANES, NSA_WIDTH), 0)
    head = lax.broadcasted_iota(jnp.int32, (LANES, NSA_WIDTH), 1) >> HEAD_SHIFT
    att = jnp.zeros((ts, NSA_WIDTH), F32)
    for i, br in enumerate((oc_ref, os_ref, ow_ref)):
        expand = jnp.where(src == 3 * head + i, 1.0, 0.0).astype(BF16)
        att = att + (_dot(g_hi, expand) + _dot(g_lo, expand)) * br[0]
    att = att * ngs_ref[0]
    y = _dot(ret_ref[0].astype(BF16), wo_ref[:RET_WIDTH, :]) + _dot(att.astype(BF16), wo_ref[RET_WIDTH:, :])
    ms = jnp.mean(y * y, axis=-1, keepdims=True)
    y = y * lax.rsqrt(ms + EPS) * gpost_ref[...]
    o_ref[0] = x_ref[0] + gate_ref[0] * y


def _out(x, gate, g_post, ret, o_cmp, o_slc, o_win, gl, ngs, w_out, ts):
    bsz, seq, d = x.shape
    row = lambda w: pl.BlockSpec((1, ts, w), lambda b, s: (b, s, 0))
    return pl.pallas_call(
        _out_kernel,
        grid=(bsz, seq // ts),
        in_specs=[row(d), pl.BlockSpec((1, 1, d), lambda b, s: (b, 0, 0)),
                  pl.BlockSpec((1, d), lambda b, s: (0, 0)),
                  row(RET_WIDTH), row(NSA_WIDTH), row(NSA_WIDTH), row(NSA_WIDTH), row(LANES),
                  row(NSA_WIDTH), pl.BlockSpec(w_out.shape, lambda b, s: (0, 0))],
        out_specs=row(d),
        out_shape=jax.ShapeDtypeStruct((bsz, seq, d), x.dtype),
        compiler_params=_params("parallel", "arbitrary"),
        name="out",
    )(x, gate, g_post, ret, o_cmp, o_slc, o_win, gl, ngs, w_out)


def _retention_tables():
    h = RET_HEADS
    c = RET_CHUNK
    log_g = jnp.log1p(-jnp.power(2.0, -5.0 - jnp.arange(h, dtype=F32)))
    idx = jnp.arange(c, dtype=F32)
    diff = idx[:, None] - idx[None, :]
    dec = jnp.where(diff[None] >= 0, jnp.exp(jnp.maximum(diff, 0.0)[None] * log_g[:, None, None]), 0.0)
    zeta = jnp.exp((c - 1 - idx)[None, :] * log_g[:, None])
    xi = jnp.exp((idx + 1.0)[None, :] * log_g[:, None])
    cd = jnp.exp(c * log_g)
    to_lanes = lambda t: jnp.repeat(t.reshape(h // 2, 2, c).transpose(0, 2, 1), HEAD_DIM, axis=2)
    cd_l = jnp.repeat(cd.reshape(h // 2, 1, 2), HEAD_DIM, axis=2)
    return dec, to_lanes(zeta), to_lanes(xi), cd_l


def _overlap_t(seq):
    ncp = seq // CMP_STRIDE
    nc = (seq - CMP_BLOCK) // CMP_STRIDE + 1
    ns = seq // SLC_BLOCK
    cs = np.arange(ncp) * CMP_STRIDE
    ce = cs + CMP_BLOCK - 1
    ss = np.arange(ns) * SLC_BLOCK
    ov = (cs[None, :] <= ss[:, None] + SLC_BLOCK - 1) & (ce[None, :] >= ss[:, None]) & (np.arange(ncp) < nc)[None, :]
    return jnp.asarray(ov.astype(np.float32), dtype=BF16)


def _layer(x, c, positions, w_ada, b_ada, g_pre, g_post, w_in, w_out,
           pe_k, w1_k, w2_k, pe_v, w1_v, w2_v):
    bsz, seq, d = x.shape
    mod = _ada(c, w_ada, b_ada)
    shift, scale, gate = [t.reshape(bsz, 1, d) for t in jnp.split(mod, 3, axis=-1)]

    offs = np.cumsum((RET_WIDTH,) * 4 + (NSA_WIDTH, 6 * KV_WIDTH, 3 * NSA_HEADS, NSA_WIDTH))
    w_gl = w_in[:, offs[5]:offs[6]]
    w_all = jnp.concatenate(
        [w_in[:, :offs[5]], w_in[:, offs[6]:], w_gl,
         jnp.zeros((d, LANES - w_gl.shape[1]), w_in.dtype)], axis=1).astype(BF16)

    lanes = jnp.arange(LANES)
    half_r = HEAD_DIM // 2
    inv_r = jnp.power(RET_ROPE_BASE, -jnp.arange(half_r, dtype=F32) / half_r)
    half_n = ROPE_DIM // 2
    inv_n = jnp.power(NSA_ROPE_BASE, -jnp.arange(half_n, dtype=F32) / half_n)
    inv_ret_row = inv_r[lanes % half_r].reshape(1, LANES)
    inv_nsa_row = inv_n[lanes % half_n].reshape(1, LANES)
    pos3 = positions.reshape(bsz, seq, 1)

    (rq, rk, rv, rgs, nq, cmps, ksd, vsd, kwd, vwd, ngs, gl) = _proj(
        x, shift, scale, g_pre.reshape(1, d), pos3, inv_ret_row, inv_nsa_row, w_all, ts=256)

    dec, zeta_l, xi_l, cd_l = _retention_tables()
    ret = _retention(rq, rk, rv, rgs, dec, zeta_l, xi_l, cd_l, rows=min(512, seq))

    ncp = seq // CMP_STRIDE
    nc = (seq - CMP_BLOCK) // CMP_STRIDE + 1
    pad_l = lambda t: jnp.concatenate([t, jnp.zeros_like(t)], axis=-1)
    pe2 = jnp.stack([pad_l(pe_k), pad_l(pe_v)])
    w1r = jnp.stack([w1_k, w1_v]).reshape(2, CMP_BLOCK, HEAD_DIM, CMP_HIDDEN)
    w1p = jnp.concatenate([w1r, jnp.zeros_like(w1r)], axis=2).astype(BF16)
    w2s = jnp.stack([w2_k, w2_v])
    w2d = jnp.concatenate([w2s, w2s], axis=-1).astype(BF16)
    cmp_end = np.arange(nc) * CMP_STRIDE + CMP_BLOCK - 1
    pos_cmp = jnp.pad(positions[:, cmp_end], ((0, 0), (0, ncp - nc))).reshape(bsz, ncp, 1)
    cmpkv = _compress(cmps, pe2, w1p, w2d, pos_cmp, inv_nsa_row)

    o_cmp, selt = _cmpsel(nq, cmpkv, _overlap_t(seq), tq=min(256, seq))
    o_slc = _slc_attention(nq, ksd, vsd, selt, tq=128, tk=512)
    o_win = _win_attention(nq, kwd, vwd, tq=256)

    return _out(x, gate, g_post.reshape(1, d), ret, o_cmp, o_slc, o_win, gl, ngs,
                w_out.astype(BF16), ts=256)


def kernel(x, c, positions, w_ada, b_ada, g_pre, g_post, w_in, w_out, cmp_pe_k, cmp_w1_k, cmp_w2_k, cmp_pe_v, cmp_w1_v, cmp_w2_v):
    for l in range(w_in.shape[0]):
        x = _layer(x, c, positions, w_ada[l], b_ada[l], g_pre[l], g_post[l], w_in[l], w_out[l],
                   cmp_pe_k[l], cmp_w1_k[l], cmp_w2_k[l], cmp_pe_v[l], cmp_w1_v[l], cmp_w2_v[l])
    return x
```

```python
import functools

import numpy as np
import jax
import jax.numpy as jnp
from jax import lax
from jax.experimental import pallas as pl
from jax.experimental.pallas import tpu as pltpu

F32 = jnp.float32
BF16 = jnp.bfloat16

LANES = 128
HEAD_DIM = 64
RET_HEADS = 8
NSA_HEADS = 8
NSA_KV_GROUPS = 2
HEADS_PER_GROUP = NSA_HEADS // NSA_KV_GROUPS
RET_WIDTH = RET_HEADS * HEAD_DIM
NSA_WIDTH = NSA_HEADS * HEAD_DIM
KV_WIDTH = NSA_KV_GROUPS * HEAD_DIM
RET_CHUNK = 128
RET_ROPE_BASE = 10000.0
NSA_ROPE_BASE = 500000.0
ROPE_DIM = HEAD_DIM // 4
CMP_BLOCK = 32
CMP_STRIDE = 16
CMP_HIDDEN = 256
SLC_BLOCK = 64
SLC_TOPK = 16
WINDOW = 512
NEG = -1e30
LOG2E = 1.4426950408889634
BIG = 1e9
EPS = 1e-6
GN_EPS = 1e-5
VMEM_LIMIT = 48 * 1024 * 1024

NT_DIMS = (((1,), (1,)), ((), ()))
TN_DIMS = (((0,), (0,)), ((), ()))


def _dot(a, b):
    return jnp.dot(a, b, preferred_element_type=F32)


def _dot_nt(a, b):
    return lax.dot_general(a, b, NT_DIMS, preferred_element_type=F32)


def _dot_tn(a, b):
    return lax.dot_general(a, b, TN_DIMS, preferred_element_type=F32)


def _silu(z):
    return z * jax.nn.sigmoid(z)


def _params(*sem):
    return pltpu.CompilerParams(dimension_semantics=sem, vmem_limit_bytes=VMEM_LIMIT)


HEAD_SHIFT = 6
SLC_SHIFT = 6
assert 1 << HEAD_SHIFT == HEAD_DIM and 1 << SLC_SHIFT == SLC_BLOCK


def _lane_in_head(shape):
    return lax.broadcasted_iota(jnp.int32, shape, len(shape) - 1) & (HEAD_DIM - 1)


def _ada_kernel(c_ref, w_ref, b_ref, o_ref):
    a = _silu(c_ref[...])
    o_ref[...] = jnp.dot(a, w_ref[...], precision=lax.Precision.HIGHEST,
                         preferred_element_type=F32) + b_ref[...]


def _ada(c, w, b):
    bsz, d = c.shape
    n = w.shape[1]
    tn = 1024
    return pl.pallas_call(
        _ada_kernel,
        grid=(n // tn,),
        in_specs=[pl.BlockSpec((bsz, d), lambda j: (0, 0)),
                  pl.BlockSpec((d, tn), lambda j: (0, j)),
                  pl.BlockSpec((1, tn), lambda j: (0, j))],
        out_specs=pl.BlockSpec((bsz, tn), lambda j: (0, j)),
        out_shape=jax.ShapeDtypeStruct((bsz, n), F32),
        compiler_params=_params("arbitrary"),
        name="ada",
    )(c, w, b.reshape(1, n))


_O_RQ, _O_RK, _O_RV, _O_RG, _O_NQ = 0, 512, 1024, 1536, 2048
_O_KV = 2560
_O_NG = 3328
_O_GL = 3840
_PROJ_COLS = 3968


def _rot_ret(z, cos, sin_signed, m):
    partner = jnp.where(m < HEAD_DIM // 2, pltpu.roll(z, LANES - HEAD_DIM // 2, 1),
                        pltpu.roll(z, HEAD_DIM // 2, 1))
    return z * cos + partner * sin_signed


def _rot_nsa(z, cos, sin_signed, m):
    half = ROPE_DIM // 2
    partner = jnp.where(m < half, pltpu.roll(z, LANES - half, 1), pltpu.roll(z, half, 1))
    return z * cos + partner * sin_signed


def _nsa_tables(posf, inv_row, m):
    half = ROPE_DIM // 2
    ang = posf * inv_row
    cos = jnp.where(m < ROPE_DIM, jnp.cos(ang), 1.0)
    s = jnp.sin(ang)
    sin = jnp.where(m < half, -s, jnp.where(m < ROPE_DIM, s, 0.0))
    return cos, sin


def _dup_groups(z, m_lane):
    r = pltpu.roll(z, HEAD_DIM, 1)
    lo = m_lane < HEAD_DIM
    return jnp.where(lo, z, r), jnp.where(lo, r, z)


def _proj_kernel(x_ref, shift_ref, scale_ref, gpre_ref, pos_ref, invr_ref, invn_ref, w_ref,
                 rq_ref, rk_ref, rv_ref, rgs_ref, nq_ref, cmps_ref, ksd_ref, vsd_ref,
                 kwd_ref, vwd_ref, ngs_ref, gl_ref,
                 cr_s, sr_s, cn_s, sn_s, z_s):
    ts = x_ref.shape[1]
    x = x_ref[0]
    ms = jnp.mean(x * x, axis=-1, keepdims=True)
    y = x * lax.rsqrt(ms + EPS) * gpre_ref[...]
    h = y * (1.0 + scale_ref[0]) + shift_ref[0]
    hb = h.astype(BF16)

    posf = pos_ref[0].astype(F32)
    lane = lax.broadcasted_iota(jnp.int32, (ts, LANES), 1)
    m = lane & (HEAD_DIM - 1)
    ang = posf * invr_ref[...]
    cr_s[...] = jnp.cos(ang)
    s = jnp.sin(ang)
    sr_s[...] = jnp.where(m < HEAD_DIM // 2, -s, s)
    cn, sn = _nsa_tables(posf, invn_ref[...], m)
    cn_s[...] = cn
    sn_s[...] = sn

    def mm(off, width):
        z_s[:, :width] = _dot(hb, w_ref[:, off:off + width])
        return lambda c: z_s[:, c * LANES:(c + 1) * LANES]

    scale_q = HEAD_DIM ** -0.5
    tiles = lambda width: [(c, slice(c * LANES, (c + 1) * LANES)) for c in range(width // LANES)]
    z = mm(_O_RQ, RET_WIDTH)
    for c, sl in tiles(RET_WIDTH):
        rq_ref[0, :, sl] = (_rot_ret(z(c), cr_s[...], sr_s[...], m) * scale_q).astype(BF16)
    z = mm(_O_RK, RET_WIDTH)
    for c, sl in tiles(RET_WIDTH):
        rk_ref[0, :, sl] = _rot_ret(z(c), cr_s[...], sr_s[...], m).astype(BF16)
    z = mm(_O_RV, RET_WIDTH)
    for c, sl in tiles(RET_WIDTH):
        rv_ref[0, :, sl] = z(c).astype(BF16)
    z = mm(_O_RG, RET_WIDTH)
    for c, sl in tiles(RET_WIDTH):
        rgs_ref[0, :, sl] = _silu(z(c))
    z = mm(_O_NQ, NSA_WIDTH)
    for c, sl in tiles(NSA_WIDTH):
        nq_ref[0, :, sl] = (_rot_nsa(z(c), cn_s[...], sn_s[...], m) * (scale_q * LOG2E)).astype(BF16)

    z = mm(_O_KV, 6 * KV_WIDTH)
    kc0, kc1 = _dup_groups(z(0), lane)
    vc0, vc1 = _dup_groups(z(1), lane)
    for i, t in enumerate((kc0, kc1, vc0, vc1)):
        cmps_ref[0, :, i * LANES:(i + 1) * LANES] = t
    lo = lane < HEAD_DIM
    tok = pl.program_id(1) * ts + lax.broadcasted_iota(jnp.int32, (ts, LANES), 0)
    onehot = jnp.where(lane - HEAD_DIM == (tok >> SLC_SHIFT), 1.0, 0.0)
    ones = jnp.ones((ts, LANES), F32)
    for c, ref, rotate, upper in ((2, ksd_ref, True, onehot), (3, vsd_ref, False, ones),
                                  (4, kwd_ref, True, None), (5, vwd_ref, False, ones)):
        t = _rot_nsa(z(c), cn_s[...], sn_s[...], m) if rotate else z(c)
        if upper is None:
            d0, d1 = _dup_groups(t, lane)
        else:
            d0, d1 = jnp.where(lo, t, upper), jnp.where(lo, pltpu.roll(t, HEAD_DIM, 1), upper)
        ref[0, :, :LANES] = d0.astype(BF16)
        ref[0, :, LANES:] = d1.astype(BF16)

    z = mm(_O_NG, NSA_WIDTH + LANES)
    for c, sl in tiles(NSA_WIDTH):
        ngs_ref[0, :, sl] = _silu(z(c))
    gl_ref[0] = jax.nn.sigmoid(z(NSA_WIDTH // LANES))


def _proj(x, shift, scale, g_pre, pos3, inv_ret_row, inv_nsa_row, w_all, ts):
    bsz, seq, d = x.shape
    row = lambda w: pl.BlockSpec((1, ts, w), lambda b, s: (b, s, 0))
    vec = pl.BlockSpec((1, 1, d), lambda b, s: (b, 0, 0))
    const2 = lambda shp: pl.BlockSpec(shp, lambda b, s: (0, 0))
    outs = [(RET_WIDTH, BF16), (RET_WIDTH, BF16), (RET_WIDTH, BF16), (RET_WIDTH, F32),
            (NSA_WIDTH, BF16), (4 * LANES, F32), (2 * LANES, BF16), (2 * LANES, BF16),
            (2 * LANES, BF16), (2 * LANES, BF16), (NSA_WIDTH, F32), (LANES, F32)]
    return pl.pallas_call(
        _proj_kernel,
        grid=(bsz, seq // ts),
        in_specs=[row(d), vec, vec, const2((1, d)), row(1), const2((1, LANES)), const2((1, LANES)),
                  const2((d, _PROJ_COLS))],
        out_specs=[row(w) for w, _ in outs],
        out_shape=[jax.ShapeDtypeStruct((bsz, seq, w), dt) for w, dt in outs],
        scratch_shapes=[pltpu.VMEM((ts, LANES), F32)] * 4 + [pltpu.VMEM((ts, 6 * KV_WIDTH), F32)],
        compiler_params=_params("parallel", "arbitrary"),
        name="proj",
    )(x, shift, scale, g_pre, pos3, inv_ret_row, inv_nsa_row, w_all)


def _ret_kernel(q_ref, k_ref, v_ref, g_ref, dec_ref, zeta_ref, xi_ref, cd_ref, o_ref, r_s):
    rows = q_ref.shape[1]
    cc = RET_CHUNK

    @pl.when(pl.program_id(2) == 0)
    def _():
        r_s[...] = jnp.zeros_like(r_s)

    lane = lax.broadcasted_iota(jnp.int32, (cc, LANES), 1)
    rowi = lax.broadcasted_iota(jnp.int32, (cc, LANES), 0)
    lo = lane < HEAD_DIM
    blockdiag = (rowi < HEAD_DIM) == lo
    inv_n = 1.0 / HEAD_DIM
    chunks = [slice(c * cc, (c + 1) * cc) for c in range(rows // cc)]
    s_ab, kvs = [], []
    for sl in chunks:
        q, k, v = q_ref[0, sl, :], k_ref[0, sl, :], v_ref[0, sl, :]
        zero = jnp.zeros_like(q)
        s_ab.append((_dot_nt(jnp.where(lo, q, zero), k), _dot_nt(jnp.where(lo, zero, q), k)))
        kz = (k.astype(F32) * zeta_ref[0]).astype(BF16)
        kvs.append(jnp.where(blockdiag, _dot_tn(kz, v), 0.0))
    inners = []
    for sl, (s_a, s_b) in zip(chunks, s_ab):
        v = v_ref[0, sl, :]
        inners.append(jnp.where(lo, _dot((s_a * dec_ref[0]).astype(BF16), v),
                                _dot((s_b * dec_ref[1]).astype(BF16), v)))
    r = r_s[...]
    outs = []
    for sl, inner, kv in zip(chunks, inners, kvs):
        outs.append(inner + _dot(q_ref[0, sl, :], r.astype(BF16)) * xi_ref[0])
        r = r * cd_ref[0] + kv
    r_s[...] = r
    head_ones = jnp.where(blockdiag, 1.0, 0.0).astype(BF16)

    def head_sums(z):
        hi = z.astype(BF16)
        lo_part = (z - hi.astype(F32)).astype(BF16)
        return _dot(hi, head_ones) + _dot(lo_part, head_ones)

    devs = [o - s * inv_n for o, s in zip(outs, [head_sums(o) for o in outs])]
    variances = [head_sums(d * d) * inv_n for d in devs]
    for sl, d, var in zip(chunks, devs, variances):
        o_ref[0, sl, :] = d * lax.rsqrt(var + GN_EPS) * g_ref[0, sl, :]


def _retention(rq, rk, rv, rgs, dec, zeta_l, xi_l, cd_l, rows):
    bsz, seq, _ = rq.shape
    npair = RET_HEADS // 2
    blk = pl.BlockSpec((1, rows, LANES), lambda b, p, s: (b, s, p))
    return pl.pallas_call(
        _ret_kernel,
        grid=(bsz, npair, seq // rows),
        in_specs=[blk, blk, blk, blk,
                  pl.BlockSpec((2, RET_CHUNK, RET_CHUNK), lambda b, p, s: (p, 0, 0)),
                  pl.BlockSpec((1, RET_CHUNK, LANES), lambda b, p, s: (p, 0, 0)),
                  pl.BlockSpec((1, RET_CHUNK, LANES), lambda b, p, s: (p, 0, 0)),
                  pl.BlockSpec((1, 1, LANES), lambda b, p, s: (p, 0, 0))],
        out_specs=blk,
        out_shape=jax.ShapeDtypeStruct((bsz, seq, RET_WIDTH), F32),
        scratch_shapes=[pltpu.VMEM((LANES, LANES), F32)],
        compiler_params=_params("parallel", "parallel", "arbitrary"),
        name="retention",
    )(rq, rk, rv, rgs, dec, zeta_l, xi_l, cd_l)


def _cmp_kernel(src_ref, pe_ref, w1_ref, w2_ref, pos_ref, invn_ref, o_ref):
    ncp = o_ref.shape[3]
    half = CMP_BLOCK // 2
    p_lo = jnp.zeros((ncp, CMP_HIDDEN), F32)
    p_hi = jnp.zeros((ncp, CMP_HIDDEN), F32)
    for l in range(half):
        t = src_ref[0, pl.ds(l, ncp, stride=CMP_STRIDE), :]
        p_lo = p_lo + _dot((t + pe_ref[0, l:l + 1, :]).astype(BF16), w1_ref[0, l])
        p_hi = p_hi + _dot((t + pe_ref[0, half + l:half + l + 1, :]).astype(BF16), w1_ref[0, half + l])
    hidden = p_lo + pltpu.roll(p_hi, ncp - 1, 0)
    comp = _dot(_silu(hidden).astype(BF16), w2_ref[0])
    m = _lane_in_head(comp.shape)
    cos, sin = _nsa_tables(pos_ref[0].astype(F32), invn_ref[...], m)
    rotated = _rot_nsa(comp, cos, sin, m)
    is_key = pl.program_id(1) == 0
    o_ref[0, 0, 0] = jnp.where(is_key, rotated, comp).astype(BF16)


def _compress(cmps, pe2, w1p, w2d, pos_cmp, inv_nsa_row):
    bsz, seq, _ = cmps.shape
    ncp = seq // CMP_STRIDE
    g = NSA_KV_GROUPS
    return pl.pallas_call(
        _cmp_kernel,
        grid=(bsz, 2, g),
        in_specs=[pl.BlockSpec((1, seq, LANES), lambda b, j, gi: (b, 0, j * g + gi)),
                  pl.BlockSpec((1, CMP_BLOCK, LANES), lambda b, j, gi: (j, 0, 0)),
                  pl.BlockSpec((1, CMP_BLOCK, LANES, CMP_HIDDEN), lambda b, j, gi: (j, 0, 0, 0)),
                  pl.BlockSpec((1, CMP_HIDDEN, LANES), lambda b, j, gi: (j, 0, 0)),
                  pl.BlockSpec((1, ncp, 1), lambda b, j, gi: (b, 0, 0)),
                  pl.BlockSpec((1, LANES), lambda b, j, gi: (0, 0))],
        out_specs=pl.BlockSpec((1, 1, 1, ncp, LANES), lambda b, j, gi: (b, j, gi, 0, 0)),
        out_shape=jax.ShapeDtypeStruct((bsz, 2, g, ncp, LANES), BF16),
        compiler_params=_params("parallel", "arbitrary", "arbitrary"),
        name="compress",
    )(cmps, pe2, w1p, w2d, pos_cmp, inv_nsa_row)


def _cmpsel_kernel(q_ref, kc_ref, vc_ref, ovt_ref, o_ref, selt_ref, qs_s, score_s, rank_s):
    nh = HEADS_PER_GROUP
    sub = 8
    tq = q_ref.shape[1]
    ncp = kc_ref.shape[3]
    nblk = ovt_ref.shape[0]
    t0 = pl.program_id(2) * tq
    kc = kc_ref[0, 0, 0]
    vc = vc_ref[0, 0, 0]

    tok = t0 + lax.broadcasted_iota(jnp.int32, (tq, LANES), 0)
    lane = lax.broadcasted_iota(jnp.int32, (tq, LANES), 1)
    masks = [(lane + c * LANES) * CMP_STRIDE + (CMP_BLOCK - 1) <= tok for c in range(ncp // LANES)]
    _stack_head_queries(q_ref, qs_s, tq)
    s_all = _dot_nt(qs_s[...], kc)
    psum, probs = None, []
    for h in range(nh):
        tiles = [jnp.where(mk, t, NEG) for mk, t in zip(masks, _lane_tiles(s_all[h * tq:(h + 1) * tq, :]))]
        mx = _row_stat(tiles, jnp.maximum, jnp.max)
        es = [jnp.where(mk, jnp.exp2(t - mx), 0.0) for mk, t in zip(masks, tiles)]
        den = _row_stat(es, jnp.add, jnp.sum)
        inv = jnp.where(den > 0.0, 1.0 / den, 0.0)
        ps = [e * inv for e in es]
        psum = ps if psum is None else [a + b for a, b in zip(psum, ps)]
        probs.append(jnp.concatenate(ps, axis=1).astype(BF16))
    pv = _dot(jnp.concatenate(probs, axis=0), vc)
    _store_heads(o_ref, [pv[h * tq:(h + 1) * tq, :] for h in range(nh)], tq)

    psum = jnp.concatenate(psum, axis=1)
    p_hi = psum.astype(BF16)
    p_lo = (psum - p_hi.astype(F32)).astype(BF16)
    imp = _dot_nt(ovt_ref[...], p_hi) + _dot_nt(ovt_ref[...], p_lo)
    blk = lax.broadcasted_iota(jnp.int32, (nblk, tq), 0)
    cur = (t0 + lax.broadcasted_iota(jnp.int32, (nblk, tq), 1)) >> SLC_SHIFT
    forced = (blk == 0) | (blk == cur) | (blk == cur - 1)
    score_s[...] = jnp.where(forced, BIG, jnp.where(blk <= cur, imp, NEG))

    rank_s[...] = jnp.zeros_like(rank_s)
    last_blk = (t0 + tq - 1) >> SLC_SHIFT
    row_in_group = lax.broadcasted_iota(jnp.int32, (sub, tq), 0)
    for ig in range(nblk // sub):
        @pl.when(ig * sub <= last_blk)
        def _():
            groups = [slice(r * sub, (r + 1) * sub) for r in range(nblk // sub)]
            scs = [score_s[g, :] for g in groups]
            cnts = [rank_s[g, :] for g in groups]
            for i in range(ig * sub, (ig + 1) * sub):
                ri = jnp.broadcast_to(score_s[i:i + 1, :], (sub, tq))
                for r, sc in enumerate(scs):
                    if r * sub > i:
                        beats = ri >= sc
                    elif (r + 1) * sub - 1 <= i:
                        beats = ri > sc
                    else:
                        beats = (ri > sc) | ((ri == sc) & (row_in_group > i - r * sub))
                    cnts[r] = cnts[r] + jnp.where(beats, 1.0, 0.0)
            for g, cnt in zip(groups, cnts):
                rank_s[g, :] = cnt
    sel = jnp.where((rank_s[...] < float(min(SLC_TOPK, nblk))) & (score_s[...] > 0.5 * NEG), 1.0, 0.0)
    pads = [jnp.zeros((HEAD_DIM, tq), F32), sel]
    if nblk < HEAD_DIM:
        pads.append(jnp.zeros((HEAD_DIM - nblk, tq), F32))
    selt_ref[0, 0] = jnp.concatenate(pads, axis=0).T


def _cmpsel(nq, cmpkv, ovt, tq):
    bsz, seq, _ = nq.shape
    g = NSA_KV_GROUPS
    ncp = cmpkv.shape[3]
    nblk = ovt.shape[0]
    gw = HEADS_PER_GROUP * HEAD_DIM
    return pl.pallas_call(
        _cmpsel_kernel,
        grid=(bsz, g, seq // tq),
        in_specs=[pl.BlockSpec((1, tq, gw), lambda b, gi, i: (b, i, gi)),
                  pl.BlockSpec((1, 1, 1, ncp, LANES), lambda b, gi, i: (b, 0, gi, 0, 0)),
                  pl.BlockSpec((1, 1, 1, ncp, LANES), lambda b, gi, i: (b, 1, gi, 0, 0)),
                  pl.BlockSpec((nblk, ncp), lambda b, gi, i: (0, 0))],
        out_specs=[pl.BlockSpec((1, tq, gw), lambda b, gi, i: (b, i, gi)),
                   pl.BlockSpec((1, 1, tq, LANES), lambda b, gi, i: (b, gi, i, 0))],
        out_shape=[jax.ShapeDtypeStruct((bsz, seq, NSA_WIDTH), F32),
                   jax.ShapeDtypeStruct((bsz, g, seq, LANES), F32)],
        scratch_shapes=[pltpu.VMEM((HEADS_PER_GROUP * tq, LANES), BF16),
                        pltpu.VMEM((nblk, tq), F32), pltpu.VMEM((nblk, tq), F32)],
        compiler_params=_params("parallel", "parallel", "arbitrary"),
        name="cmpsel",
    )(nq, cmpkv, cmpkv, ovt)


def _stack_head_queries(q_ref, qs_s, rows, src=None):
    src = slice(0, rows) if src is None else src
    lo = lax.broadcasted_iota(jnp.int32, (rows, LANES), 1) < HEAD_DIM
    for c in range(HEADS_PER_GROUP // 2):
        q = q_ref[0, src, c * LANES:(c + 1) * LANES]
        zero = jnp.zeros_like(q)
        qs_s[(2 * c) * rows:(2 * c + 1) * rows, :] = jnp.where(lo, q, zero)
        qs_s[(2 * c + 1) * rows:(2 * c + 2) * rows, :] = jnp.where(lo, zero, q)


def _store_heads(o_ref, outs, tq):
    lo = lax.broadcasted_iota(jnp.int32, (tq, LANES), 1) < HEAD_DIM
    for c in range(HEADS_PER_GROUP // 2):
        o_ref[0, :, c * LANES:(c + 1) * LANES] = jnp.where(lo, outs[2 * c], outs[2 * c + 1])


def _store_heads_normalized(o_ref, accs, tq, rows=slice(None)):
    lo = lax.broadcasted_iota(jnp.int32, (tq, LANES), 1) < HEAD_DIM
    outs = [a * (1.0 / jnp.where(lo, pltpu.roll(a, HEAD_DIM, 1), 1.0)) for a in accs]
    for c in range(HEADS_PER_GROUP // 2):
        o_ref[0, rows, c * LANES:(c + 1) * LANES] = jnp.where(
            lo, outs[2 * c], pltpu.roll(outs[2 * c + 1], HEAD_DIM, 1))


def _lane_tiles(z):
    return [z[:, c * LANES:(c + 1) * LANES] for c in range(z.shape[1] // LANES)]


def _row_stat(tiles, combine, reduce):
    acc = functools.reduce(combine, tiles)
    return jnp.broadcast_to(reduce(acc, axis=-1, keepdims=True), acc.shape)


WIN_UNROLL = 4


def _win_kernel(q_ref, k_ref, v_ref, o_ref, qs0_s, qs1_s, s0_s, s1_s, p0_s, p1_s, *, tq):
    nh = HEADS_PER_GROUP
    span = WINDOW + tq
    n_qt = q_ref.shape[1] // tq
    qs_b, s_b, p_b = (qs0_s, qs1_s), (s0_s, s1_s), (p0_s, p1_s)

    def q_rows(i):
        return pl.ds(pl.multiple_of(i * tq, tq), tq)

    def k_rows(i):
        return pl.ds(pl.multiple_of(jnp.maximum(i * tq - WINDOW, 0), tq), span)

    def stack_queries(i, slot):
        _stack_head_queries(q_ref, qs_b[slot], tq, src=q_rows(jnp.minimum(i, n_qt - 1)))

    def scores(i, slot):
        i = jnp.minimum(i, n_qt - 1)
        s_b[slot][...] = _dot_nt(qs_b[slot][...], k_ref[0, k_rows(i), :])

    def values(i, slot):
        i = jnp.maximum(i, 0)
        pv = _dot(p_b[slot][...], v_ref[0, k_rows(i), :])
        _store_heads_normalized(o_ref, [pv[h * tq:(h + 1) * tq, :] for h in range(nh)], tq, q_rows(i))

    def softmax(i, slot):
        tok = i * tq + lax.broadcasted_iota(jnp.int32, (tq, span), 0)
        key = jnp.maximum(i * tq - WINDOW, 0) + lax.broadcasted_iota(jnp.int32, (tq, span), 1)
        bias = jnp.where((key <= tok) & (tok - key < WINDOW), 0.0, NEG)
        for h in range(nh):
            rows = slice(h * tq, (h + 1) * tq)
            tiles = _lane_tiles(s_b[slot][rows, :] + bias)
            m = _row_stat(tiles, jnp.maximum, jnp.max)
            p_b[slot][rows, :] = jnp.concatenate([jnp.exp2(t - m) for t in tiles], axis=1).astype(BF16)

    def stages(first, count):
        for t in range(count):
            i, slot = first + t, t % 2
            values(i - 1, 1 - slot)
            scores(i + 1, 1 - slot)
            stack_queries(i + 2, slot)
            softmax(i, slot)

    p1_s[...] = jnp.ones_like(p1_s)
    stack_queries(0, 0)
    stack_queries(1, 1)
    scores(0, 0)

    def body(b, carry):
        stages(b * WIN_UNROLL, WIN_UNROLL)
        return carry

    lax.fori_loop(0, n_qt // WIN_UNROLL, body, 0)
    stages(n_qt - n_qt % WIN_UNROLL, n_qt % WIN_UNROLL)
    values(n_qt - 1, (n_qt - 1) % 2)


SLC_UNROLL = 4
SLC_STATE_BUFFERS = 8
assert SLC_UNROLL % 2 == 0 and SLC_UNROLL + 1 <= SLC_STATE_BUFFERS


def _slc_kernel(q_ref, k_ref, v_ref, selt_ref, o_ref, qs_s, s0_s, s1_s, p0_s, p1_s, a0_s, a1_s,
                m_s, acc_s, *, tq, tk):
    nh = HEADS_PER_GROUP
    seq = q_ref.shape[1]
    n_qt = seq // tq
    s_b, p_b, a_b = (s0_s, s1_s), (p0_s, p1_s), (a0_s, a1_s)
    lo = lax.broadcasted_iota(jnp.int32, (tq, LANES), 1) < HEAD_DIM

    def n_tiles(qi):
        return (qi * tq + tq - 1) // tk + 1

    n_items = sum((qi * tq + tq - 1) // tk + 1 for qi in range(n_qt))

    def advance(qi, j):
        wrap = j + 1 >= n_tiles(qi)
        return jnp.where(wrap, jnp.minimum(qi + 1, n_qt - 1), qi), jnp.where(wrap, 0, j + 1)

    def q_rows(qi):
        return pl.ds(pl.multiple_of(qi * tq, tq), tq)

    def k_rows(j):
        return pl.ds(pl.multiple_of(j * tk, tk), tk)

    def stack_queries(qi):
        par = qi % SLC_STATE_BUFFERS
        unselected = jnp.where(selt_ref[0, 0, q_rows(qi), :] > 0.5, 0.0, NEG)
        for c in range(nh // 2):
            q = q_ref[0, q_rows(qi), c * LANES:(c + 1) * LANES].astype(F32)
            qs_s[par, (2 * c) * tq:(2 * c + 1) * tq, :] = jnp.where(lo, q, unselected).astype(BF16)
            qs_s[par, (2 * c + 1) * tq:(2 * c + 2) * tq, :] = jnp.where(
                lo, pltpu.roll(q, HEAD_DIM, 1), unselected).astype(BF16)

    def scores(qi, j, slot):
        s_b[slot][...] = _dot_nt(qs_s[qi % SLC_STATE_BUFFERS], k_ref[0, k_rows(j), :])

    def values(qi, j, slot):
        pv = _dot(p_b[slot][...], v_ref[0, k_rows(j), :])
        par = qi % SLC_STATE_BUFFERS
        for h in range(nh):
            acc_s[par, h] = a_b[slot][h] * acc_s[par, h] + pv[h * tq:(h + 1) * tq, :]

    def softmax(qi, j, slot):
        tok = qi * tq + lax.broadcasted_iota(jnp.int32, (tq, tk), 0)
        key = j * tk + lax.broadcasted_iota(jnp.int32, (tq, tk), 1)
        bias = jnp.where(key <= tok, 0.0, NEG)
        par = qi % SLC_STATE_BUFFERS
        for h in range(nh):
            rows = slice(h * tq, (h + 1) * tq)
            tiles = _lane_tiles(s_b[slot][rows, :] + bias)
            m_prev = jnp.where(j == 0, NEG, m_s[par, h])
            m_new = jnp.maximum(m_prev, _row_stat(tiles, jnp.maximum, jnp.max))
            m_s[par, h] = m_new
            a_b[slot][h] = jnp.exp2(m_prev - m_new)
            p_b[slot][rows, :] = jnp.concatenate([jnp.exp2(t - m_new) for t in tiles], axis=1).astype(BF16)

    def finalize(qi):
        _store_heads_normalized(o_ref, [acc_s[qi % SLC_STATE_BUFFERS, h] for h in range(nh)], tq, q_rows(qi))

    def is_last(qi, j):
        return (j + 1 >= n_tiles(qi)).astype(jnp.int32)

    def finalize_if(flag, qi):
        @pl.when(flag == 1)
        def _():
            finalize(qi)

    def block(carry, n_stages):
        fin, (qp, jp, p_last), (qc, jc), (qn, jn) = carry
        for flag, qi in fin:
            finalize_if(flag, qi)
        items = [(qp, jp, p_last), (qc, jc, is_last(qc, jc)), (qn, jn, is_last(qn, jn))]
        for _ in range(n_stages - 1):
            qx, jx = advance(items[-1][0], items[-1][1])
            items.append((qx, jx, is_last(qx, jx)))
        for qx, jx, _ in items[2:]:
            @pl.when(jx == 0)
            def _():
                stack_queries(qx)
        for t in range(n_stages):
            slot = t % 2
            values(items[t][0], items[t][1], 1 - slot)
            scores(items[t + 2][0], items[t + 2][1], 1 - slot)
            softmax(items[t + 1][0], items[t + 1][1], slot)
        fin = tuple((items[t][2], items[t][0]) for t in range(n_stages))
        nxt = advance(items[-1][0], items[-1][1])
        return fin, items[n_stages], items[n_stages + 1][:2], nxt

    m_s[...] = jnp.full_like(m_s, NEG)
    acc_s[...] = jnp.zeros_like(acc_s)
    p1_s[...] = jnp.zeros_like(p1_s)
    a1_s[...] = jnp.ones_like(a1_s)
    zero = jnp.int32(0)
    stack_queries(zero)
    scores(zero, zero, 0)
    unroll = SLC_UNROLL
    carry = (((zero, zero),) * unroll, (zero, zero, zero), (zero, zero), advance(zero, zero))
    carry = lax.fori_loop(0, n_items // unroll, lambda i, c: block(c, unroll), carry)
    if n_items % unroll:
        carry = block(carry, n_items % unroll)
    fin, (qp, jp, _), _, _ = carry
    for flag, qi in fin:
        finalize_if(flag, qi)
    values(qp, jp, (n_items - 1) % 2)
    finalize(qp)


def _slc_attention(nq, kd, vd, selt, *, tq, tk):
    bsz, seq, _ = nq.shape
    g = NSA_KV_GROUPS
    gw = HEADS_PER_GROUP * HEAD_DIM
    nh = HEADS_PER_GROUP
    assert seq % tk == 0 and tk % tq == 0
    assert seq // SLC_BLOCK <= HEAD_DIM
    whole = lambda w: pl.BlockSpec((1, seq, w), lambda b, gi: (b, 0, gi))
    stat = pltpu.VMEM((nh, tq, LANES), F32)
    state = pltpu.VMEM((SLC_STATE_BUFFERS, nh, tq, LANES), F32)
    return pl.pallas_call(
        functools.partial(_slc_kernel, tq=tq, tk=tk),
        grid=(bsz, g),
        in_specs=[whole(gw), whole(LANES), whole(LANES),
                  pl.BlockSpec((1, 1, seq, LANES), lambda b, gi: (b, gi, 0, 0))],
        out_specs=whole(gw),
        out_shape=jax.ShapeDtypeStruct((bsz, seq, NSA_WIDTH), F32),
        scratch_shapes=([pltpu.VMEM((SLC_STATE_BUFFERS, nh * tq, LANES), BF16)] + [pltpu.VMEM((nh * tq, tk), F32)] * 2 +
                        [pltpu.VMEM((nh * tq, tk), BF16)] * 2 + [stat] * 2 + [state] * 2),
        compiler_params=_params("parallel", "arbitrary"),
        name="slc_attn",
    )(nq, kd, vd, selt)


def _win_attention(nq, kd, vd, *, tq):
    bsz, seq, _ = nq.shape
    g = NSA_KV_GROUPS
    gw = HEADS_PER_GROUP * HEAD_DIM
    nh = HEADS_PER_GROUP
    assert WINDOW % tq == 0 and seq >= WINDOW + tq and seq // tq >= 2 and WIN_UNROLL % 2 == 0
    whole = lambda w: pl.BlockSpec((1, seq, w), lambda b, gi: (b, 0, gi))
    span = WINDOW + tq
    return pl.pallas_call(
        functools.partial(_win_kernel, tq=tq),
        grid=(bsz, g),
        in_specs=[whole(gw), whole(LANES), whole(LANES)],
        out_specs=whole(gw),
        out_shape=jax.ShapeDtypeStruct((bsz, seq, NSA_WIDTH), F32),
        scratch_shapes=([pltpu.VMEM((nh * tq, LANES), BF16)] * 2 + [pltpu.VMEM((nh * tq, span), F32)] * 2 +
                        [pltpu.VMEM((nh * tq, span), BF16)] * 2),
        compiler_params=_params("parallel", "arbitrary"),
        name="win_attn",
    )(nq, kd, vd)


def _out_kernel(x_ref, gate_ref, gpost_ref, ret_ref, oc_ref, os_ref, ow_ref, gl_ref, ngs_ref,
                wo_ref, o_ref):
    ts = x_ref.shape[1]
    gates = gl_ref[0]
    lo_off = 32
    assert 3 * NSA_HEADS <= lo_off
    g_hi = gates.astype(BF16).astype(F32)
    g_lo = (gates - g_hi).astype(BF16).astype(F32)
    lane = lax.broadcasted_iota(jnp.int32, (ts, LANES), 1)
    g_split = jnp.where(lane < lo_off, g_hi, pltpu.roll(g_lo, lo_off, 1)).astype(BF16)
    src = lax.broadcasted_iota(jnp.int32, (LANES, NSA_WIDTH), 0)
    head = lax.broadcasted_iota(jnp.int32, (LANES, NSA_WIDTH), 1) >> HEAD_SHIFT
    att = jnp.zeros((ts, NSA_WIDTH), F32)
    for i, br in enumerate((oc_ref, os_ref, ow_ref)):
        expand = jnp.where((src == 3 * head + i) | (src == lo_off + 3 * head + i), 1.0, 0.0).astype(BF16)
        att = att + _dot(g_split, expand) * br[0]
    att = att * ngs_ref[0]
    y = _dot(ret_ref[0].astype(BF16), wo_ref[:RET_WIDTH, :]) + _dot(att.astype(BF16), wo_ref[RET_WIDTH:, :])
    ms = jnp.mean(y * y, axis=-1, keepdims=True)
    y = y * lax.rsqrt(ms + EPS) * gpost_ref[...]
    o_ref[0] = x_ref[0] + gate_ref[0] * y


def _out(x, gate, g_post, ret, o_cmp, o_slc, o_win, gl, ngs, w_out, ts):
    bsz, seq, d = x.shape
    row = lambda w: pl.BlockSpec((1, ts, w), lambda b, s: (b, s, 0))
    return pl.pallas_call(
        _out_kernel,
        grid=(bsz, seq // ts),
        in_specs=[row(d), pl.BlockSpec((1, 1, d), lambda b, s: (b, 0, 0)),
                  pl.BlockSpec((1, d), lambda b, s: (0, 0)),
                  row(RET_WIDTH), row(NSA_WIDTH), row(NSA_WIDTH), row(NSA_WIDTH), row(LANES),
                  row(NSA_WIDTH), pl.BlockSpec(w_out.shape, lambda b, s: (0, 0))],
        out_specs=row(d),
        out_shape=jax.ShapeDtypeStruct((bsz, seq, d), x.dtype),
        compiler_params=_params("parallel", "arbitrary"),
        name="out",
    )(x, gate, g_post, ret, o_cmp, o_slc, o_win, gl, ngs, w_out)


def _retention_tables():
    h = RET_HEADS
    c = RET_CHUNK
    log_g = jnp.log1p(-jnp.power(2.0, -5.0 - jnp.arange(h, dtype=F32)))
    idx = jnp.arange(c, dtype=F32)
    diff = idx[:, None] - idx[None, :]
    dec = jnp.where(diff[None] >= 0, jnp.exp(jnp.maximum(diff, 0.0)[None] * log_g[:, None, None]), 0.0)
    zeta = jnp.exp((c - 1 - idx)[None, :] * log_g[:, None])
    xi = jnp.exp((idx + 1.0)[None, :] * log_g[:, None])
    cd = jnp.exp(c * log_g)
    to_lanes = lambda t: jnp.repeat(t.reshape(h // 2, 2, c).transpose(0, 2, 1), HEAD_DIM, axis=2)
    cd_l = jnp.repeat(cd.reshape(h // 2, 1, 2), HEAD_DIM, axis=2)
    return dec, to_lanes(zeta), to_lanes(xi), cd_l


def _overlap_t(seq):
    ncp = seq // CMP_STRIDE
    nc = (seq - CMP_BLOCK) // CMP_STRIDE + 1
    ns = seq // SLC_BLOCK
    cs = np.arange(ncp) * CMP_STRIDE
    ce = cs + CMP_BLOCK - 1
    ss = np.arange(ns) * SLC_BLOCK
    ov = (cs[None, :] <= ss[:, None] + SLC_BLOCK - 1) & (ce[None, :] >= ss[:, None]) & (np.arange(ncp) < nc)[None, :]
    return jnp.asarray(ov.astype(np.float32), dtype=BF16)


def _layer(x, c, positions, w_ada, b_ada, g_pre, g_post, w_in, w_out,
           pe_k, w1_k, w2_k, pe_v, w1_v, w2_v):
    bsz, seq, d = x.shape
    mod = _ada(c, w_ada, b_ada)
    shift, scale, gate = [t.reshape(bsz, 1, d) for t in jnp.split(mod, 3, axis=-1)]

    offs = np.cumsum((RET_WIDTH,) * 4 + (NSA_WIDTH, 6 * KV_WIDTH, 3 * NSA_HEADS, NSA_WIDTH))
    w_gl = w_in[:, offs[5]:offs[6]]
    w_all = jnp.concatenate(
        [w_in[:, :offs[5]], w_in[:, offs[6]:], w_gl,
         jnp.zeros((d, LANES - w_gl.shape[1]), w_in.dtype)], axis=1).astype(BF16)

    lanes = jnp.arange(LANES)
    half_r = HEAD_DIM // 2
    inv_r = jnp.power(RET_ROPE_BASE, -jnp.arange(half_r, dtype=F32) / half_r)
    half_n = ROPE_DIM // 2
    inv_n = jnp.power(NSA_ROPE_BASE, -jnp.arange(half_n, dtype=F32) / half_n)
    inv_ret_row = inv_r[lanes % half_r].reshape(1, LANES)
    inv_nsa_row = inv_n[lanes % half_n].reshape(1, LANES)
    pos3 = positions.reshape(bsz, seq, 1)

    (rq, rk, rv, rgs, nq, cmps, ksd, vsd, kwd, vwd, ngs, gl) = _proj(
        x, shift, scale, g_pre.reshape(1, d), pos3, inv_ret_row, inv_nsa_row, w_all, ts=256)

    dec, zeta_l, xi_l, cd_l = _retention_tables()
    ret = _retention(rq, rk, rv, rgs, dec, zeta_l, xi_l, cd_l, rows=min(1024, seq))

    ncp = seq // CMP_STRIDE
    nc = (seq - CMP_BLOCK) // CMP_STRIDE + 1
    pad_l = lambda t: jnp.concatenate([t, jnp.zeros_like(t)], axis=-1)
    pe2 = jnp.stack([pad_l(pe_k), pad_l(pe_v)])
    w1r = jnp.stack([w1_k, w1_v]).reshape(2, CMP_BLOCK, HEAD_DIM, CMP_HIDDEN)
    w1p = jnp.concatenate([w1r, jnp.zeros_like(w1r)], axis=2).astype(BF16)
    w2s = jnp.stack([w2_k, w2_v])
    w2d = jnp.concatenate([w2s, w2s], axis=-1).astype(BF16)
    cmp_end = np.arange(nc) * CMP_STRIDE + CMP_BLOCK - 1
    pos_cmp = jnp.pad(positions[:, cmp_end], ((0, 0), (0, ncp - nc))).reshape(bsz, ncp, 1)
    cmpkv = _compress(cmps, pe2, w1p, w2d, pos_cmp, inv_nsa_row)

    o_cmp, selt = _cmpsel(nq, cmpkv, _overlap_t(seq), tq=min(256, seq))
    o_slc = _slc_attention(nq, ksd, vsd, selt, tq=128, tk=512)
    o_win = _win_attention(nq, kwd, vwd, tq=128)

    return _out(x, gate, g_post.reshape(1, d), ret, o_cmp, o_slc, o_win, gl, ngs,
                w_out.astype(BF16), ts=256)


def kernel(x, c, positions, w_ada, b_ada, g_pre, g_post, w_in, w_out, cmp_pe_k, cmp_w1_k, cmp_w2_k, cmp_pe_v, cmp_w1_v, cmp_w2_v):
    for l in range(w_in.shape[0]):
        x = _layer(x, c, positions, w_ada[l], b_ada[l], g_pre[l], g_post[l], w_in[l], w_out[l],
                   cmp_pe_k[l], cmp_w1_k[l], cmp_w2_k[l], cmp_pe_v[l], cmp_w1_v[l], cmp_w2_v[l])
    return x
```

```python
import functools

import numpy as np
import jax
import jax.numpy as jnp
from jax import lax
from jax.experimental import pallas as pl
from jax.experimental.pallas import tpu as pltpu

F32 = jnp.float32
BF16 = jnp.bfloat16

LANES = 128
HEAD_DIM = 64
RET_HEADS = 8
NSA_HEADS = 8
NSA_KV_GROUPS = 2
HEADS_PER_GROUP = NSA_HEADS // NSA_KV_GROUPS
RET_WIDTH = RET_HEADS * HEAD_DIM
NSA_WIDTH = NSA_HEADS * HEAD_DIM
KV_WIDTH = NSA_KV_GROUPS * HEAD_DIM
RET_CHUNK = 128
RET_ROPE_BASE = 10000.0
NSA_ROPE_BASE = 500000.0
ROPE_DIM = HEAD_DIM // 4
CMP_BLOCK = 32
CMP_STRIDE = 16
CMP_HIDDEN = 256
SLC_BLOCK = 64
SLC_TOPK = 16
WINDOW = 512
NEG = -1e30
LOG2E = 1.4426950408889634
BIG = 1e9
EPS = 1e-6
GN_EPS = 1e-5
VMEM_LIMIT = 48 * 1024 * 1024

NT_DIMS = (((1,), (1,)), ((), ()))
TN_DIMS = (((0,), (0,)), ((), ()))


def _dot(a, b):
    return jnp.dot(a, b, preferred_element_type=F32)


def _dot_nt(a, b):
    return lax.dot_general(a, b, NT_DIMS, preferred_element_type=F32)


def _dot_tn(a, b):
    return lax.dot_general(a, b, TN_DIMS, preferred_element_type=F32)


def _silu(z):
    return z * jax.nn.sigmoid(z)


def _params(*sem):
    return pltpu.CompilerParams(dimension_semantics=sem, vmem_limit_bytes=VMEM_LIMIT)


HEAD_SHIFT = 6
SLC_SHIFT = 6
assert 1 << HEAD_SHIFT == HEAD_DIM and 1 << SLC_SHIFT == SLC_BLOCK


def _lane_in_head(shape):
    return lax.broadcasted_iota(jnp.int32, shape, len(shape) - 1) & (HEAD_DIM - 1)


def _ada_kernel(c_ref, w_ref, b_ref, o_ref):
    a = _silu(c_ref[...])
    o_ref[...] = jnp.dot(a, w_ref[...], precision=lax.Precision.HIGHEST,
                         preferred_element_type=F32) + b_ref[...]


def _ada(c, w, b):
    bsz, d = c.shape
    n = w.shape[1]
    tn = 1024
    return pl.pallas_call(
        _ada_kernel,
        grid=(n // tn,),
        in_specs=[pl.BlockSpec((bsz, d), lambda j: (0, 0)),
                  pl.BlockSpec((d, tn), lambda j: (0, j)),
                  pl.BlockSpec((1, tn), lambda j: (0, j))],
        out_specs=pl.BlockSpec((bsz, tn), lambda j: (0, j)),
        out_shape=jax.ShapeDtypeStruct((bsz, n), F32),
        compiler_params=_params("arbitrary"),
        name="ada",
    )(c, w, b.reshape(1, n))


_O_RQ, _O_RK, _O_RV, _O_RG, _O_NQ = 0, 512, 1024, 1536, 2048
_O_KV = 2560
_O_NG = 3328
_O_GL = 3840
_PROJ_COLS = 3968


def _rot_ret(z, cos, sin_signed, m):
    partner = jnp.where(m < HEAD_DIM // 2, pltpu.roll(z, LANES - HEAD_DIM // 2, 1),
                        pltpu.roll(z, HEAD_DIM // 2, 1))
    return z * cos + partner * sin_signed


def _rot_nsa(z, cos, sin_signed, m):
    half = ROPE_DIM // 2
    partner = jnp.where(m < half, pltpu.roll(z, LANES - half, 1), pltpu.roll(z, half, 1))
    return z * cos + partner * sin_signed


def _nsa_tables(posf, inv_row, m):
    half = ROPE_DIM // 2
    ang = posf * inv_row
    cos = jnp.where(m < ROPE_DIM, jnp.cos(ang), 1.0)
    s = jnp.sin(ang)
    sin = jnp.where(m < half, -s, jnp.where(m < ROPE_DIM, s, 0.0))
    return cos, sin


def _dup_groups(z, m_lane):
    r = pltpu.roll(z, HEAD_DIM, 1)
    lo = m_lane < HEAD_DIM
    return jnp.where(lo, z, r), jnp.where(lo, r, z)


def _proj_kernel(x_ref, shift_ref, scale_ref, gpre_ref, pos_ref, invr_ref, invn_ref, w_ref,
                 rq_ref, rk_ref, rv_ref, rgs_ref, nq_ref, cmps_ref, ksd_ref, vsd_ref,
                 kwd_ref, vwd_ref, ngs_ref, gl_ref,
                 cr_s, sr_s, cn_s, sn_s, z_s):
    ts = x_ref.shape[1]
    x = x_ref[0]
    ms = jnp.mean(x * x, axis=-1, keepdims=True)
    y = x * lax.rsqrt(ms + EPS) * gpre_ref[...]
    h = y * (1.0 + scale_ref[0]) + shift_ref[0]
    hb = h.astype(BF16)

    posf = pos_ref[0].astype(F32)
    lane = lax.broadcasted_iota(jnp.int32, (ts, LANES), 1)
    m = lane & (HEAD_DIM - 1)
    ang = posf * invr_ref[...]
    cr_s[...] = jnp.cos(ang)
    s = jnp.sin(ang)
    sr_s[...] = jnp.where(m < HEAD_DIM // 2, -s, s)
    cn, sn = _nsa_tables(posf, invn_ref[...], m)
    cn_s[...] = cn
    sn_s[...] = sn

    def mm(off, width):
        z_s[:, :width] = _dot(hb, w_ref[:, off:off + width])
        return lambda c: z_s[:, c * LANES:(c + 1) * LANES]

    scale_q = HEAD_DIM ** -0.5
    tiles = lambda width: [(c, slice(c * LANES, (c + 1) * LANES)) for c in range(width // LANES)]
    z = mm(_O_RQ, RET_WIDTH)
    for c, sl in tiles(RET_WIDTH):
        rq_ref[0, :, sl] = (_rot_ret(z(c), cr_s[...], sr_s[...], m) * scale_q).astype(BF16)
    z = mm(_O_RK, RET_WIDTH)
    for c, sl in tiles(RET_WIDTH):
        rk_ref[0, :, sl] = _rot_ret(z(c), cr_s[...], sr_s[...], m).astype(BF16)
    z = mm(_O_RV, RET_WIDTH)
    for c, sl in tiles(RET_WIDTH):
        rv_ref[0, :, sl] = z(c).astype(BF16)
    z = mm(_O_RG, RET_WIDTH)
    for c, sl in tiles(RET_WIDTH):
        rgs_ref[0, :, sl] = _silu(z(c)).astype(BF16)
    z = mm(_O_NQ, NSA_WIDTH)
    for c, sl in tiles(NSA_WIDTH):
        nq_ref[0, :, sl] = (_rot_nsa(z(c), cn_s[...], sn_s[...], m) * (scale_q * LOG2E)).astype(BF16)

    z = mm(_O_KV, 6 * KV_WIDTH)
    kc0, kc1 = _dup_groups(z(0), lane)
    vc0, vc1 = _dup_groups(z(1), lane)
    for i, t in enumerate((kc0, kc1, vc0, vc1)):
        cmps_ref[0, :, i * LANES:(i + 1) * LANES] = t
    lo = lane < HEAD_DIM
    tok = pl.program_id(1) * ts + lax.broadcasted_iota(jnp.int32, (ts, LANES), 0)
    onehot = jnp.where(lane - HEAD_DIM == (tok >> SLC_SHIFT), 1.0, 0.0)
    ones = jnp.ones((ts, LANES), F32)
    for c, ref, rotate, upper in ((2, ksd_ref, True, onehot), (3, vsd_ref, False, ones),
                                  (4, kwd_ref, True, None), (5, vwd_ref, False, ones)):
        t = _rot_nsa(z(c), cn_s[...], sn_s[...], m) if rotate else z(c)
        if upper is None:
            d0, d1 = _dup_groups(t, lane)
        else:
            d0, d1 = jnp.where(lo, t, upper), jnp.where(lo, pltpu.roll(t, HEAD_DIM, 1), upper)
        ref[0, :, :LANES] = d0.astype(BF16)
        ref[0, :, LANES:] = d1.astype(BF16)

    z = mm(_O_NG, NSA_WIDTH + LANES)
    for c, sl in tiles(NSA_WIDTH):
        ngs_ref[0, :, sl] = _silu(z(c)).astype(BF16)
    gl_ref[0] = jax.nn.sigmoid(z(NSA_WIDTH // LANES))


def _proj(x, shift, scale, g_pre, pos3, inv_ret_row, inv_nsa_row, w_all, ts):
    bsz, seq, d = x.shape
    row = lambda w: pl.BlockSpec((1, ts, w), lambda b, s: (b, s, 0))
    vec = pl.BlockSpec((1, 1, d), lambda b, s: (b, 0, 0))
    const2 = lambda shp: pl.BlockSpec(shp, lambda b, s: (0, 0))
    outs = [(RET_WIDTH, BF16), (RET_WIDTH, BF16), (RET_WIDTH, BF16), (RET_WIDTH, BF16),
            (NSA_WIDTH, BF16), (4 * LANES, F32), (2 * LANES, BF16), (2 * LANES, BF16),
            (2 * LANES, BF16), (2 * LANES, BF16), (NSA_WIDTH, BF16), (LANES, F32)]
    return pl.pallas_call(
        _proj_kernel,
        grid=(bsz, seq // ts),
        in_specs=[row(d), vec, vec, const2((1, d)), row(1), const2((1, LANES)), const2((1, LANES)),
                  const2((d, _PROJ_COLS))],
        out_specs=[row(w) for w, _ in outs],
        out_shape=[jax.ShapeDtypeStruct((bsz, seq, w), dt) for w, dt in outs],
        scratch_shapes=[pltpu.VMEM((ts, LANES), F32)] * 4 + [pltpu.VMEM((ts, 6 * KV_WIDTH), F32)],
        compiler_params=_params("parallel", "arbitrary"),
        name="proj",
    )(x, shift, scale, g_pre, pos3, inv_ret_row, inv_nsa_row, w_all)


def _ret_kernel(q_ref, k_ref, v_ref, g_ref, dec_ref, zeta_ref, xi_ref, cd_ref, o_ref, r_s):
    rows = q_ref.shape[1]
    cc = RET_CHUNK

    @pl.when(pl.program_id(2) == 0)
    def _():
        r_s[...] = jnp.zeros_like(r_s)

    lane = lax.broadcasted_iota(jnp.int32, (cc, LANES), 1)
    rowi = lax.broadcasted_iota(jnp.int32, (cc, LANES), 0)
    lo = lane < HEAD_DIM
    blockdiag = (rowi < HEAD_DIM) == lo
    inv_n = 1.0 / HEAD_DIM
    chunks = [slice(c * cc, (c + 1) * cc) for c in range(rows // cc)]
    s_ab, kvs = [], []
    for sl in chunks:
        q, k, v = q_ref[0, sl, :], k_ref[0, sl, :], v_ref[0, sl, :]
        zero = jnp.zeros_like(q)
        s_ab.append((_dot_nt(jnp.where(lo, q, zero), k), _dot_nt(jnp.where(lo, zero, q), k)))
        kz = (k.astype(F32) * zeta_ref[0]).astype(BF16)
        kvs.append(jnp.where(blockdiag, _dot_tn(kz, v), 0.0))
    inners = []
    for sl, (s_a, s_b) in zip(chunks, s_ab):
        v = v_ref[0, sl, :]
        inners.append(jnp.where(lo, _dot((s_a * dec_ref[0]).astype(BF16), v),
                                _dot((s_b * dec_ref[1]).astype(BF16), v)))
    r = r_s[...]
    outs = []
    for sl, inner, kv in zip(chunks, inners, kvs):
        outs.append(inner + _dot(q_ref[0, sl, :], r.astype(BF16)) * xi_ref[0])
        r = r * cd_ref[0] + kv
    r_s[...] = r
    head_ones = jnp.where(blockdiag, 1.0, 0.0).astype(BF16)

    def head_sums(z):
        hi = z.astype(BF16)
        lo_part = (z - hi.astype(F32)).astype(BF16)
        return _dot(hi, head_ones) + _dot(lo_part, head_ones)

    devs = [o - s * inv_n for o, s in zip(outs, [head_sums(o) for o in outs])]
    variances = [head_sums(d * d) * inv_n for d in devs]
    for sl, d, var in zip(chunks, devs, variances):
        o_ref[0, sl, :] = (d * lax.rsqrt(var + GN_EPS) * g_ref[0, sl, :]).astype(o_ref.dtype)


def _retention(rq, rk, rv, rgs, dec, zeta_l, xi_l, cd_l, rows):
    bsz, seq, _ = rq.shape
    npair = RET_HEADS // 2
    blk = pl.BlockSpec((1, rows, LANES), lambda b, p, s: (b, s, p))
    return pl.pallas_call(
        _ret_kernel,
        grid=(bsz, npair, seq // rows),
        in_specs=[blk, blk, blk, blk,
                  pl.BlockSpec((2, RET_CHUNK, RET_CHUNK), lambda b, p, s: (p, 0, 0)),
                  pl.BlockSpec((1, RET_CHUNK, LANES), lambda b, p, s: (p, 0, 0)),
                  pl.BlockSpec((1, RET_CHUNK, LANES), lambda b, p, s: (p, 0, 0)),
                  pl.BlockSpec((1, 1, LANES), lambda b, p, s: (p, 0, 0))],
        out_specs=blk,
        out_shape=jax.ShapeDtypeStruct((bsz, seq, RET_WIDTH), BF16),
        scratch_shapes=[pltpu.VMEM((LANES, LANES), F32)],
        compiler_params=_params("parallel", "parallel", "arbitrary"),
        name="retention",
    )(rq, rk, rv, rgs, dec, zeta_l, xi_l, cd_l)


def _cmp_kernel(src_ref, pe_ref, w1_ref, w2_ref, pos_ref, invn_ref, o_ref):
    ncp = o_ref.shape[3]
    half = CMP_BLOCK // 2
    p_lo = jnp.zeros((ncp, CMP_HIDDEN), F32)
    p_hi = jnp.zeros((ncp, CMP_HIDDEN), F32)
    for l in range(half):
        t = src_ref[0, pl.ds(l, ncp, stride=CMP_STRIDE), :]
        p_lo = p_lo + _dot((t + pe_ref[0, l:l + 1, :]).astype(BF16), w1_ref[0, l])
        p_hi = p_hi + _dot((t + pe_ref[0, half + l:half + l + 1, :]).astype(BF16), w1_ref[0, half + l])
    hidden = p_lo + pltpu.roll(p_hi, ncp - 1, 0)
    comp = _dot(_silu(hidden).astype(BF16), w2_ref[0])
    m = _lane_in_head(comp.shape)
    cos, sin = _nsa_tables(pos_ref[0].astype(F32), invn_ref[...], m)
    rotated = _rot_nsa(comp, cos, sin, m)
    is_key = pl.program_id(1) == 0
    o_ref[0, 0, 0] = jnp.where(is_key, rotated, comp).astype(BF16)


def _compress(cmps, pe2, w1p, w2d, pos_cmp, inv_nsa_row):
    bsz, seq, _ = cmps.shape
    ncp = seq // CMP_STRIDE
    g = NSA_KV_GROUPS
    return pl.pallas_call(
        _cmp_kernel,
        grid=(bsz, 2, g),
        in_specs=[pl.BlockSpec((1, seq, LANES), lambda b, j, gi: (b, 0, j * g + gi)),
                  pl.BlockSpec((1, CMP_BLOCK, LANES), lambda b, j, gi: (j, 0, 0)),
                  pl.BlockSpec((1, CMP_BLOCK, LANES, CMP_HIDDEN), lambda b, j, gi: (j, 0, 0, 0)),
                  pl.BlockSpec((1, CMP_HIDDEN, LANES), lambda b, j, gi: (j, 0, 0)),
                  pl.BlockSpec((1, ncp, 1), lambda b, j, gi: (b, 0, 0)),
                  pl.BlockSpec((1, LANES), lambda b, j, gi: (0, 0))],
        out_specs=pl.BlockSpec((1, 1, 1, ncp, LANES), lambda b, j, gi: (b, j, gi, 0, 0)),
        out_shape=jax.ShapeDtypeStruct((bsz, 2, g, ncp, LANES), BF16),
        compiler_params=_params("parallel", "arbitrary", "arbitrary"),
        name="compress",
    )(cmps, pe2, w1p, w2d, pos_cmp, inv_nsa_row)


def _cmpsel_kernel(q_ref, kc_ref, vc_ref, ovt_ref, o_ref, selt_ref, qs_s, score_s, rank_s):
    nh = HEADS_PER_GROUP
    sub = 8
    tq = q_ref.shape[1]
    ncp = kc_ref.shape[3]
    nblk = ovt_ref.shape[0]
    t0 = pl.program_id(2) * tq
    kc = kc_ref[0, 0, 0]
    vc = vc_ref[0, 0, 0]

    tok = t0 + lax.broadcasted_iota(jnp.int32, (tq, LANES), 0)
    lane = lax.broadcasted_iota(jnp.int32, (tq, LANES), 1)
    biases = [jnp.where((lane + c * LANES) * CMP_STRIDE + (CMP_BLOCK - 1) <= tok, 0.0, NEG)
              for c in range(ncp // LANES)]
    has_key = tok >= CMP_BLOCK - 1
    _stack_head_queries(q_ref, qs_s, tq)
    s_all = _dot_nt(qs_s[...], kc)
    psum, probs = None, []
    for h in range(nh):
        tiles = [t + b for b, t in zip(biases, _lane_tiles(s_all[h * tq:(h + 1) * tq, :]))]
        mx = _row_stat(tiles, jnp.maximum, jnp.max)
        es = [jnp.exp2(t - mx) for t in tiles]
        inv = jnp.where(has_key, 1.0 / _row_stat(es, jnp.add, jnp.sum), 0.0)
        ps = [e * inv for e in es]
        psum = ps if psum is None else [a + b for a, b in zip(psum, ps)]
        probs.append(jnp.concatenate(ps, axis=1).astype(BF16))
    pv = _dot(jnp.concatenate(probs, axis=0), vc)
    _store_heads(o_ref, [pv[h * tq:(h + 1) * tq, :] for h in range(nh)], tq)

    psum = jnp.concatenate(psum, axis=1)
    p_hi = psum.astype(BF16)
    p_lo = (psum - p_hi.astype(F32)).astype(BF16)
    imp = _dot_nt(ovt_ref[...], p_hi) + _dot_nt(ovt_ref[...], p_lo)
    blk = lax.broadcasted_iota(jnp.int32, (nblk, tq), 0)
    cur = (t0 + lax.broadcasted_iota(jnp.int32, (nblk, tq), 1)) >> SLC_SHIFT
    forced = (blk == 0) | (blk == cur) | (blk == cur - 1)
    score_s[...] = jnp.where(forced, BIG, jnp.where(blk <= cur, imp, NEG))

    rank_s[...] = jnp.zeros_like(rank_s)
    last_blk = (t0 + tq - 1) >> SLC_SHIFT
    row_in_group = lax.broadcasted_iota(jnp.int32, (sub, tq), 0)
    for ig in range(nblk // sub):
        @pl.when(ig * sub <= last_blk)
        def _():
            groups = [slice(r * sub, (r + 1) * sub) for r in range(nblk // sub)]
            scs = [score_s[g, :] for g in groups]
            cnts = [rank_s[g, :] for g in groups]
            for i in range(ig * sub, (ig + 1) * sub):
                ri = jnp.broadcast_to(score_s[i:i + 1, :], (sub, tq))
                for r, sc in enumerate(scs):
                    if r * sub > i:
                        beats = ri >= sc
                    elif (r + 1) * sub - 1 <= i:
                        beats = ri > sc
                    else:
                        beats = (ri > sc) | ((ri == sc) & (row_in_group > i - r * sub))
                    cnts[r] = cnts[r] + jnp.where(beats, 1.0, 0.0)
            for g, cnt in zip(groups, cnts):
                rank_s[g, :] = cnt
    sel = jnp.where((rank_s[...] < float(min(SLC_TOPK, nblk))) & (score_s[...] > 0.5 * NEG), 1.0, 0.0)
    pads = [jnp.zeros((HEAD_DIM, tq), F32), sel]
    if nblk < HEAD_DIM:
        pads.append(jnp.zeros((HEAD_DIM - nblk, tq), F32))
    selt_ref[0, 0] = jnp.concatenate(pads, axis=0).T.astype(selt_ref.dtype)


def _cmpsel(nq, cmpkv, ovt, tq):
    bsz, seq, _ = nq.shape
    g = NSA_KV_GROUPS
    ncp = cmpkv.shape[3]
    nblk = ovt.shape[0]
    gw = HEADS_PER_GROUP * HEAD_DIM
    return pl.pallas_call(
        _cmpsel_kernel,
        grid=(bsz, g, seq // tq),
        in_specs=[pl.BlockSpec((1, tq, gw), lambda b, gi, i: (b, i, gi)),
                  pl.BlockSpec((1, 1, 1, ncp, LANES), lambda b, gi, i: (b, 0, gi, 0, 0)),
                  pl.BlockSpec((1, 1, 1, ncp, LANES), lambda b, gi, i: (b, 1, gi, 0, 0)),
                  pl.BlockSpec((nblk, ncp), lambda b, gi, i: (0, 0))],
        out_specs=[pl.BlockSpec((1, tq, gw), lambda b, gi, i: (b, i, gi)),
                   pl.BlockSpec((1, 1, tq, LANES), lambda b, gi, i: (b, gi, i, 0))],
        out_shape=[jax.ShapeDtypeStruct((bsz, seq, NSA_WIDTH), BF16),
                   jax.ShapeDtypeStruct((bsz, g, seq, LANES), BF16)],
        scratch_shapes=[pltpu.VMEM((HEADS_PER_GROUP * tq, LANES), BF16),
                        pltpu.VMEM((nblk, tq), F32), pltpu.VMEM((nblk, tq), F32)],
        compiler_params=_params("parallel", "parallel", "arbitrary"),
        name="cmpsel",
    )(nq, cmpkv, cmpkv, ovt)


def _stack_head_queries(q_ref, qs_s, rows, src=None):
    src = slice(0, rows) if src is None else src
    lo = lax.broadcasted_iota(jnp.int32, (rows, LANES), 1) < HEAD_DIM
    for c in range(HEADS_PER_GROUP // 2):
        q = q_ref[0, src, c * LANES:(c + 1) * LANES]
        zero = jnp.zeros_like(q)
        qs_s[(2 * c) * rows:(2 * c + 1) * rows, :] = jnp.where(lo, q, zero)
        qs_s[(2 * c + 1) * rows:(2 * c + 2) * rows, :] = jnp.where(lo, zero, q)


def _store_heads(o_ref, outs, tq):
    lo = lax.broadcasted_iota(jnp.int32, (tq, LANES), 1) < HEAD_DIM
    for c in range(HEADS_PER_GROUP // 2):
        o_ref[0, :, c * LANES:(c + 1) * LANES] = jnp.where(lo, outs[2 * c], outs[2 * c + 1]).astype(o_ref.dtype)


def _store_heads_normalized(o_ref, accs, tq, rows=slice(None)):
    lo = lax.broadcasted_iota(jnp.int32, (tq, LANES), 1) < HEAD_DIM
    outs = [a * (1.0 / jnp.where(lo, pltpu.roll(a, HEAD_DIM, 1), 1.0)) for a in accs]
    for c in range(HEADS_PER_GROUP // 2):
        o_ref[0, rows, c * LANES:(c + 1) * LANES] = jnp.where(
            lo, outs[2 * c], pltpu.roll(outs[2 * c + 1], HEAD_DIM, 1)).astype(o_ref.dtype)


def _lane_tiles(z):
    return [z[:, c * LANES:(c + 1) * LANES] for c in range(z.shape[1] // LANES)]


def _row_stat(tiles, combine, reduce):
    acc = functools.reduce(combine, tiles)
    return jnp.broadcast_to(reduce(acc, axis=-1, keepdims=True), acc.shape)


WIN_UNROLL = 4


def _win_kernel(q_ref, k_ref, v_ref, o_ref, qs0_s, qs1_s, s0_s, s1_s, p0_s, p1_s, *, tq):
    nh = HEADS_PER_GROUP
    span = WINDOW + tq
    n_qt = q_ref.shape[1] // tq
    qs_b, s_b, p_b = (qs0_s, qs1_s), (s0_s, s1_s), (p0_s, p1_s)

    def q_rows(i):
        return pl.ds(pl.multiple_of(i * tq, tq), tq)

    def k_rows(i):
        return pl.ds(pl.multiple_of(jnp.maximum(i * tq - WINDOW, 0), tq), span)

    def stack_queries(i, slot):
        _stack_head_queries(q_ref, qs_b[slot], tq, src=q_rows(jnp.minimum(i, n_qt - 1)))

    def scores(i, slot):
        i = jnp.minimum(i, n_qt - 1)
        s_b[slot][...] = _dot_nt(qs_b[slot][...], k_ref[0, k_rows(i), :])

    def values(i, slot):
        i = jnp.maximum(i, 0)
        pv = _dot(p_b[slot][...], v_ref[0, k_rows(i), :])
        _store_heads_normalized(o_ref, [pv[h * tq:(h + 1) * tq, :] for h in range(nh)], tq, q_rows(i))

    def softmax(i, slot):
        tok = i * tq + lax.broadcasted_iota(jnp.int32, (tq, span), 0)
        key = jnp.maximum(i * tq - WINDOW, 0) + lax.broadcasted_iota(jnp.int32, (tq, span), 1)
        bias = jnp.where((key <= tok) & (tok - key < WINDOW), 0.0, NEG)
        for h in range(nh):
            rows = slice(h * tq, (h + 1) * tq)
            tiles = _lane_tiles(s_b[slot][rows, :] + bias)
            m = _row_stat(tiles, jnp.maximum, jnp.max)
            p_b[slot][rows, :] = jnp.concatenate([jnp.exp2(t - m) for t in tiles], axis=1).astype(BF16)

    def stages(first, count):
        for t in range(count):
            i, slot = first + t, t % 2
            values(i - 1, 1 - slot)
            scores(i + 1, 1 - slot)
            stack_queries(i + 2, slot)
            softmax(i, slot)

    p1_s[...] = jnp.ones_like(p1_s)
    stack_queries(0, 0)
    stack_queries(1, 1)
    scores(0, 0)

    def body(b, carry):
        stages(b * WIN_UNROLL, WIN_UNROLL)
        return carry

    lax.fori_loop(0, n_qt // WIN_UNROLL, body, 0)
    stages(n_qt - n_qt % WIN_UNROLL, n_qt % WIN_UNROLL)
    values(n_qt - 1, (n_qt - 1) % 2)


SLC_UNROLL = 4
SLC_STATE_BUFFERS = 8
assert SLC_UNROLL % 2 == 0 and SLC_UNROLL + 1 <= SLC_STATE_BUFFERS


def _slc_kernel(q_ref, k_ref, v_ref, selt_ref, o_ref, qs_s, s0_s, s1_s, p0_s, p1_s, a0_s, a1_s,
                m_s, acc_s, *, tq, tk):
    nh = HEADS_PER_GROUP
    seq = q_ref.shape[1]
    n_qt = seq // tq
    s_b, p_b, a_b = (s0_s, s1_s), (p0_s, p1_s), (a0_s, a1_s)
    lo = lax.broadcasted_iota(jnp.int32, (tq, LANES), 1) < HEAD_DIM

    def n_tiles(qi):
        return (qi * tq + tq - 1) // tk + 1

    n_items = sum((qi * tq + tq - 1) // tk + 1 for qi in range(n_qt))

    def advance(qi, j):
        wrap = j + 1 >= n_tiles(qi)
        return jnp.where(wrap, jnp.minimum(qi + 1, n_qt - 1), qi), jnp.where(wrap, 0, j + 1)

    def q_rows(qi):
        return pl.ds(pl.multiple_of(qi * tq, tq), tq)

    def k_rows(j):
        return pl.ds(pl.multiple_of(j * tk, tk), tk)

    def stack_queries(qi):
        par = qi % SLC_STATE_BUFFERS
        unselected = jnp.where(selt_ref[0, 0, q_rows(qi), :].astype(F32) > 0.5, 0.0, NEG)
        for c in range(nh // 2):
            q = q_ref[0, q_rows(qi), c * LANES:(c + 1) * LANES].astype(F32)
            qs_s[par, (2 * c) * tq:(2 * c + 1) * tq, :] = jnp.where(lo, q, unselected).astype(BF16)
            qs_s[par, (2 * c + 1) * tq:(2 * c + 2) * tq, :] = jnp.where(
                lo, pltpu.roll(q, HEAD_DIM, 1), unselected).astype(BF16)

    def scores(qi, j, slot):
        s_b[slot][...] = _dot_nt(qs_s[qi % SLC_STATE_BUFFERS], k_ref[0, k_rows(j), :])

    def values(qi, j, slot):
        pv = _dot(p_b[slot][...], v_ref[0, k_rows(j), :])
        par = qi % SLC_STATE_BUFFERS
        for h in range(nh):
            acc_s[par, h] = a_b[slot][h] * acc_s[par, h] + pv[h * tq:(h + 1) * tq, :]

    def softmax(qi, j, slot):
        tok = qi * tq + lax.broadcasted_iota(jnp.int32, (tq, tk), 0)
        key = j * tk + lax.broadcasted_iota(jnp.int32, (tq, tk), 1)
        bias = jnp.where(key <= tok, 0.0, NEG)
        par = qi % SLC_STATE_BUFFERS
        for h in range(nh):
            rows = slice(h * tq, (h + 1) * tq)
            tiles = _lane_tiles(s_b[slot][rows, :] + bias)
            m_prev = jnp.where(j == 0, NEG, m_s[par, h])
            m_new = jnp.maximum(m_prev, _row_stat(tiles, jnp.maximum, jnp.max))
            m_s[par, h] = m_new
            a_b[slot][h] = jnp.exp2(m_prev - m_new)
            p_b[slot][rows, :] = jnp.concatenate([jnp.exp2(t - m_new) for t in tiles], axis=1).astype(BF16)

    def finalize(qi):
        _store_heads_normalized(o_ref, [acc_s[qi % SLC_STATE_BUFFERS, h] for h in range(nh)], tq, q_rows(qi))

    def is_last(qi, j):
        return (j + 1 >= n_tiles(qi)).astype(jnp.int32)

    def finalize_if(flag, qi):
        @pl.when(flag == 1)
        def _():
            finalize(qi)

    def block(carry, n_stages):
        fin, (qp, jp, p_last), (qc, jc), (qn, jn) = carry
        for flag, qi in fin:
            finalize_if(flag, qi)
        items = [(qp, jp, p_last), (qc, jc, is_last(qc, jc)), (qn, jn, is_last(qn, jn))]
        for _ in range(n_stages - 1):
            qx, jx = advance(items[-1][0], items[-1][1])
            items.append((qx, jx, is_last(qx, jx)))
        for qx, jx, _ in items[2:]:
            @pl.when(jx == 0)
            def _():
                stack_queries(qx)
        for t in range(n_stages):
            slot = t % 2
            values(items[t][0], items[t][1], 1 - slot)
            scores(items[t + 2][0], items[t + 2][1], 1 - slot)
            softmax(items[t + 1][0], items[t + 1][1], slot)
        fin = tuple((items[t][2], items[t][0]) for t in range(n_stages))
        nxt = advance(items[-1][0], items[-1][1])
        return fin, items[n_stages], items[n_stages + 1][:2], nxt

    m_s[...] = jnp.full_like(m_s, NEG)
    acc_s[...] = jnp.zeros_like(acc_s)
    p1_s[...] = jnp.zeros_like(p1_s)
    a1_s[...] = jnp.ones_like(a1_s)
    zero = jnp.int32(0)
    stack_queries(zero)
    scores(zero, zero, 0)
    unroll = SLC_UNROLL
    carry = (((zero, zero),) * unroll, (zero, zero, zero), (zero, zero), advance(zero, zero))
    carry = lax.fori_loop(0, n_items // unroll, lambda i, c: block(c, unroll), carry)
    if n_items % unroll:
        carry = block(carry, n_items % unroll)
    fin, (qp, jp, _), _, _ = carry
    for flag, qi in fin:
        finalize_if(flag, qi)
    values(qp, jp, (n_items - 1) % 2)
    finalize(qp)


def _slc_attention(nq, kd, vd, selt, *, tq, tk):
    bsz, seq, _ = nq.shape
    g = NSA_KV_GROUPS
    gw = HEADS_PER_GROUP * HEAD_DIM
    nh = HEADS_PER_GROUP
    assert seq % tk == 0 and tk % tq == 0
    assert seq // SLC_BLOCK <= HEAD_DIM
    whole = lambda w: pl.BlockSpec((1, seq, w), lambda b, gi: (b, 0, gi))
    stat = pltpu.VMEM((nh, tq, LANES), F32)
    state = pltpu.VMEM((SLC_STATE_BUFFERS, nh, tq, LANES), F32)
    return pl.pallas_call(
        functools.partial(_slc_kernel, tq=tq, tk=tk),
        grid=(bsz, g),
        in_specs=[whole(gw), whole(LANES), whole(LANES),
                  pl.BlockSpec((1, 1, seq, LANES), lambda b, gi: (b, gi, 0, 0))],
        out_specs=whole(gw),
        out_shape=jax.ShapeDtypeStruct((bsz, seq, NSA_WIDTH), BF16),
        scratch_shapes=([pltpu.VMEM((SLC_STATE_BUFFERS, nh * tq, LANES), BF16)] + [pltpu.VMEM((nh * tq, tk), F32)] * 2 +
                        [pltpu.VMEM((nh * tq, tk), BF16)] * 2 + [stat] * 2 + [state] * 2),
        compiler_params=_params("parallel", "arbitrary"),
        name="slc_attn",
    )(nq, kd, vd, selt)


def _win_attention(nq, kd, vd, *, tq):
    bsz, seq, _ = nq.shape
    g = NSA_KV_GROUPS
    gw = HEADS_PER_GROUP * HEAD_DIM
    nh = HEADS_PER_GROUP
    assert WINDOW % tq == 0 and seq >= WINDOW + tq and seq // tq >= 2 and WIN_UNROLL % 2 == 0
    whole = lambda w: pl.BlockSpec((1, seq, w), lambda b, gi: (b, 0, gi))
    span = WINDOW + tq
    return pl.pallas_call(
        functools.partial(_win_kernel, tq=tq),
        grid=(bsz, g),
        in_specs=[whole(gw), whole(LANES), whole(LANES)],
        out_specs=whole(gw),
        out_shape=jax.ShapeDtypeStruct((bsz, seq, NSA_WIDTH), BF16),
        scratch_shapes=([pltpu.VMEM((nh * tq, LANES), BF16)] * 2 + [pltpu.VMEM((nh * tq, span), F32)] * 2 +
                        [pltpu.VMEM((nh * tq, span), BF16)] * 2),
        compiler_params=_params("parallel", "arbitrary"),
        name="win_attn",
    )(nq, kd, vd)


def _out_kernel(x_ref, gate_ref, gpost_ref, ret_ref, oc_ref, os_ref, ow_ref, gl_ref, ngs_ref,
                wo_ref, o_ref):
    ts = x_ref.shape[1]
    gates = gl_ref[0]
    lo_off = 32
    assert 3 * NSA_HEADS <= lo_off
    g_hi = gates.astype(BF16).astype(F32)
    g_lo = (gates - g_hi).astype(BF16).astype(F32)
    lane = lax.broadcasted_iota(jnp.int32, (ts, LANES), 1)
    g_split = jnp.where(lane < lo_off, g_hi, pltpu.roll(g_lo, lo_off, 1)).astype(BF16)
    src = lax.broadcasted_iota(jnp.int32, (LANES, NSA_WIDTH), 0)
    head = lax.broadcasted_iota(jnp.int32, (LANES, NSA_WIDTH), 1) >> HEAD_SHIFT
    att = jnp.zeros((ts, NSA_WIDTH), F32)
    for i, br in enumerate((oc_ref, os_ref, ow_ref)):
        expand = jnp.where((src == 3 * head + i) | (src == lo_off + 3 * head + i), 1.0, 0.0).astype(BF16)
        att = att + _dot(g_split, expand) * br[0]
    att = att * ngs_ref[0]
    y = _dot(ret_ref[0].astype(BF16), wo_ref[:RET_WIDTH, :]) + _dot(att.astype(BF16), wo_ref[RET_WIDTH:, :])
    ms = jnp.mean(y * y, axis=-1, keepdims=True)
    y = y * lax.rsqrt(ms + EPS) * gpost_ref[...]
    o_ref[0] = x_ref[0] + gate_ref[0] * y


def _out(x, gate, g_post, ret, o_cmp, o_slc, o_win, gl, ngs, w_out, ts):
    bsz, seq, d = x.shape
    row = lambda w: pl.BlockSpec((1, ts, w), lambda b, s: (b, s, 0))
    return pl.pallas_call(
        _out_kernel,
        grid=(bsz, seq // ts),
        in_specs=[row(d), pl.BlockSpec((1, 1, d), lambda b, s: (b, 0, 0)),
                  pl.BlockSpec((1, d), lambda b, s: (0, 0)),
                  row(RET_WIDTH), row(NSA_WIDTH), row(NSA_WIDTH), row(NSA_WIDTH), row(LANES),
                  row(NSA_WIDTH), pl.BlockSpec(w_out.shape, lambda b, s: (0, 0))],
        out_specs=row(d),
        out_shape=jax.ShapeDtypeStruct((bsz, seq, d), x.dtype),
        compiler_params=_params("parallel", "arbitrary"),
        name="out",
    )(x, gate, g_post, ret, o_cmp, o_slc, o_win, gl, ngs, w_out)


def _retention_tables():
    h = RET_HEADS
    c = RET_CHUNK
    log_g = jnp.log1p(-jnp.power(2.0, -5.0 - jnp.arange(h, dtype=F32)))
    idx = jnp.arange(c, dtype=F32)
    diff = idx[:, None] - idx[None, :]
    dec = jnp.where(diff[None] >= 0, jnp.exp(jnp.maximum(diff, 0.0)[None] * log_g[:, None, None]), 0.0)
    zeta = jnp.exp((c - 1 - idx)[None, :] * log_g[:, None])
    xi = jnp.exp((idx + 1.0)[None, :] * log_g[:, None])
    cd = jnp.exp(c * log_g)
    to_lanes = lambda t: jnp.repeat(t.reshape(h // 2, 2, c).transpose(0, 2, 1), HEAD_DIM, axis=2)
    cd_l = jnp.repeat(cd.reshape(h // 2, 1, 2), HEAD_DIM, axis=2)
    return dec, to_lanes(zeta), to_lanes(xi), cd_l


def _overlap_t(seq):
    ncp = seq // CMP_STRIDE
    nc = (seq - CMP_BLOCK) // CMP_STRIDE + 1
    ns = seq // SLC_BLOCK
    cs = np.arange(ncp) * CMP_STRIDE
    ce = cs + CMP_BLOCK - 1
    ss = np.arange(ns) * SLC_BLOCK
    ov = (cs[None, :] <= ss[:, None] + SLC_BLOCK - 1) & (ce[None, :] >= ss[:, None]) & (np.arange(ncp) < nc)[None, :]
    return jnp.asarray(ov.astype(np.float32), dtype=BF16)


def _layer(x, c, positions, w_ada, b_ada, g_pre, g_post, w_in, w_out,
           pe_k, w1_k, w2_k, pe_v, w1_v, w2_v):
    bsz, seq, d = x.shape
    mod = _ada(c, w_ada, b_ada)
    shift, scale, gate = [t.reshape(bsz, 1, d) for t in jnp.split(mod, 3, axis=-1)]

    offs = np.cumsum((RET_WIDTH,) * 4 + (NSA_WIDTH, 6 * KV_WIDTH, 3 * NSA_HEADS, NSA_WIDTH))
    w_gl = w_in[:, offs[5]:offs[6]]
    w_all = jnp.concatenate(
        [w_in[:, :offs[5]], w_in[:, offs[6]:], w_gl,
         jnp.zeros((d, LANES - w_gl.shape[1]), w_in.dtype)], axis=1).astype(BF16)

    lanes = jnp.arange(LANES)
    half_r = HEAD_DIM // 2
    inv_r = jnp.power(RET_ROPE_BASE, -jnp.arange(half_r, dtype=F32) / half_r)
    half_n = ROPE_DIM // 2
    inv_n = jnp.power(NSA_ROPE_BASE, -jnp.arange(half_n, dtype=F32) / half_n)
    inv_ret_row = inv_r[lanes % half_r].reshape(1, LANES)
    inv_nsa_row = inv_n[lanes % half_n].reshape(1, LANES)
    pos3 = positions.reshape(bsz, seq, 1)

    (rq, rk, rv, rgs, nq, cmps, ksd, vsd, kwd, vwd, ngs, gl) = _proj(
        x, shift, scale, g_pre.reshape(1, d), pos3, inv_ret_row, inv_nsa_row, w_all, ts=256)

    dec, zeta_l, xi_l, cd_l = _retention_tables()
    ret = _retention(rq, rk, rv, rgs, dec, zeta_l, xi_l, cd_l, rows=min(1024, seq))

    ncp = seq // CMP_STRIDE
    nc = (seq - CMP_BLOCK) // CMP_STRIDE + 1
    pad_l = lambda t: jnp.concatenate([t, jnp.zeros_like(t)], axis=-1)
    pe2 = jnp.stack([pad_l(pe_k), pad_l(pe_v)])
    w1r = jnp.stack([w1_k, w1_v]).reshape(2, CMP_BLOCK, HEAD_DIM, CMP_HIDDEN)
    w1p = jnp.concatenate([w1r, jnp.zeros_like(w1r)], axis=2).astype(BF16)
    w2s = jnp.stack([w2_k, w2_v])
    w2d = jnp.concatenate([w2s, w2s], axis=-1).astype(BF16)
    cmp_end = np.arange(nc) * CMP_STRIDE + CMP_BLOCK - 1
    pos_cmp = jnp.pad(positions[:, cmp_end], ((0, 0), (0, ncp - nc))).reshape(bsz, ncp, 1)
    cmpkv = _compress(cmps, pe2, w1p, w2d, pos_cmp, inv_nsa_row)

    o_cmp, selt = _cmpsel(nq, cmpkv, _overlap_t(seq), tq=min(256, seq))
    o_slc = _slc_attention(nq, ksd, vsd, selt, tq=128, tk=512)
    o_win = _win_attention(nq, kwd, vwd, tq=128)

    return _out(x, gate, g_post.reshape(1, d), ret, o_cmp, o_slc, o_win, gl, ngs,
                w_out.astype(BF16), ts=256)


def kernel(x, c, positions, w_ada, b_ada, g_pre, g_post, w_in, w_out, cmp_pe_k, cmp_w1_k, cmp_w2_k, cmp_pe_v, cmp_w1_v, cmp_w2_v):
    for l in range(w_in.shape[0]):
        x = _layer(x, c, positions, w_ada[l], b_ada[l], g_pre[l], g_post[l], w_in[l], w_out[l],
                   cmp_pe_k[l], cmp_w1_k[l], cmp_w2_k[l], cmp_pe_v[l], cmp_w1_v[l], cmp_w2_v[l])
    return x
```

```python
import functools

import numpy as np
import jax
import jax.numpy as jnp
from jax import lax
from jax.experimental import pallas as pl
from jax.experimental.pallas import tpu as pltpu

F32 = jnp.float32
BF16 = jnp.bfloat16

LANES = 128
HEAD_DIM = 64
RET_HEADS = 8
NSA_HEADS = 8
NSA_KV_GROUPS = 2
HEADS_PER_GROUP = NSA_HEADS // NSA_KV_GROUPS
RET_WIDTH = RET_HEADS * HEAD_DIM
NSA_WIDTH = NSA_HEADS * HEAD_DIM
KV_WIDTH = NSA_KV_GROUPS * HEAD_DIM
RET_CHUNK = 128
RET_ROPE_BASE = 10000.0
NSA_ROPE_BASE = 500000.0
ROPE_DIM = HEAD_DIM // 4
CMP_BLOCK = 32
CMP_STRIDE = 16
CMP_HIDDEN = 256
SLC_BLOCK = 64
SLC_TOPK = 16
WINDOW = 512
NEG = -1e30
LOG2E = 1.4426950408889634
BIG = 1e9
EPS = 1e-6
GN_EPS = 1e-5
VMEM_LIMIT = 48 * 1024 * 1024

NT_DIMS = (((1,), (1,)), ((), ()))
TN_DIMS = (((0,), (0,)), ((), ()))


def _dot(a, b):
    return jnp.dot(a, b, preferred_element_type=F32)


def _dot_nt(a, b):
    return lax.dot_general(a, b, NT_DIMS, preferred_element_type=F32)


def _dot_tn(a, b):
    return lax.dot_general(a, b, TN_DIMS, preferred_element_type=F32)


def _silu(z):
    return z * jax.nn.sigmoid(z)


def _params(*sem):
    return pltpu.CompilerParams(dimension_semantics=sem, vmem_limit_bytes=VMEM_LIMIT)


HEAD_SHIFT = 6
SLC_SHIFT = 6
assert 1 << HEAD_SHIFT == HEAD_DIM and 1 << SLC_SHIFT == SLC_BLOCK


def _lane_in_head(shape):
    return lax.broadcasted_iota(jnp.int32, shape, len(shape) - 1) & (HEAD_DIM - 1)


def _ada_kernel(c_ref, w_ref, b_ref, o_ref):
    a = _silu(c_ref[...])
    o_ref[...] = jnp.dot(a, w_ref[...], precision=lax.Precision.HIGHEST,
                         preferred_element_type=F32) + b_ref[...]


def _ada(c, w, b):
    bsz, d = c.shape
    n = w.shape[1]
    tn = 1024
    return pl.pallas_call(
        _ada_kernel,
        grid=(n // tn,),
        in_specs=[pl.BlockSpec((bsz, d), lambda j: (0, 0)),
                  pl.BlockSpec((d, tn), lambda j: (0, j)),
                  pl.BlockSpec((1, tn), lambda j: (0, j))],
        out_specs=pl.BlockSpec((bsz, tn), lambda j: (0, j)),
        out_shape=jax.ShapeDtypeStruct((bsz, n), F32),
        compiler_params=_params("arbitrary"),
        name="ada",
    )(c, w, b.reshape(1, n))


_O_RQ, _O_RK, _O_RV, _O_RG, _O_NQ = 0, 512, 1024, 1536, 2048
_O_KV = 2560
_O_NG = 3328
_O_GL = 3840
_PROJ_COLS = 3968


def _rot_ret(z, cos, sin_signed, m):
    partner = jnp.where(m < HEAD_DIM // 2, pltpu.roll(z, LANES - HEAD_DIM // 2, 1),
                        pltpu.roll(z, HEAD_DIM // 2, 1))
    return z * cos + partner * sin_signed


def _rot_nsa(z, cos, sin_signed, m):
    half = ROPE_DIM // 2
    partner = jnp.where(m < half, pltpu.roll(z, LANES - half, 1), pltpu.roll(z, half, 1))
    return z * cos + partner * sin_signed


def _nsa_tables(posf, inv_row, m):
    half = ROPE_DIM // 2
    ang = posf * inv_row
    cos = jnp.where(m < ROPE_DIM, jnp.cos(ang), 1.0)
    s = jnp.sin(ang)
    sin = jnp.where(m < half, -s, jnp.where(m < ROPE_DIM, s, 0.0))
    return cos, sin


def _dup_groups(z, m_lane):
    r = pltpu.roll(z, HEAD_DIM, 1)
    lo = m_lane < HEAD_DIM
    return jnp.where(lo, z, r), jnp.where(lo, r, z)


def _proj_kernel(x_ref, shift_ref, scale_ref, gpre_ref, pos_ref, invr_ref, invn_ref, w_ref,
                 rq_ref, rk_ref, rv_ref, rgs_ref, nq_ref, cmps_ref, ksd_ref, vsd_ref,
                 kwd_ref, vwd_ref, ngs_ref, gl_ref,
                 cr_s, sr_s, cn_s, sn_s, z0_s, z1_s):
    ts = x_ref.shape[1]
    x = x_ref[0]
    ms = jnp.mean(x * x, axis=-1, keepdims=True)
    y = x * lax.rsqrt(ms + EPS) * gpre_ref[...]
    h = y * (1.0 + scale_ref[0]) + shift_ref[0]
    hb = h.astype(BF16)

    posf = pos_ref[0].astype(F32)
    lane = lax.broadcasted_iota(jnp.int32, (ts, LANES), 1)
    m = lane & (HEAD_DIM - 1)
    ang = posf * invr_ref[...]
    cr_s[...] = jnp.cos(ang)
    s = jnp.sin(ang)
    sr_s[...] = jnp.where(m < HEAD_DIM // 2, -s, s)
    cn, sn = _nsa_tables(posf, invn_ref[...], m)
    cn_s[...] = cn
    sn_s[...] = sn

    staging = [z0_s, z1_s]

    def mm(off, width):
        z_s = staging[0]
        staging.reverse()
        z_s[:, :width] = _dot(hb, w_ref[:, off:off + width])
        return lambda c: z_s[:, c * LANES:(c + 1) * LANES]

    scale_q = HEAD_DIM ** -0.5
    tiles = lambda width: [(c, slice(c * LANES, (c + 1) * LANES)) for c in range(width // LANES)]
    z = mm(_O_RQ, RET_WIDTH)
    for c, sl in tiles(RET_WIDTH):
        rq_ref[0, :, sl] = (_rot_ret(z(c), cr_s[...], sr_s[...], m) * scale_q).astype(BF16)
    z = mm(_O_RK, RET_WIDTH)
    for c, sl in tiles(RET_WIDTH):
        rk_ref[0, :, sl] = _rot_ret(z(c), cr_s[...], sr_s[...], m).astype(BF16)
    z = mm(_O_RV, RET_WIDTH)
    for c, sl in tiles(RET_WIDTH):
        rv_ref[0, :, sl] = z(c).astype(BF16)
    z = mm(_O_RG, RET_WIDTH)
    for c, sl in tiles(RET_WIDTH):
        rgs_ref[0, :, sl] = _silu(z(c)).astype(BF16)
    z = mm(_O_NQ, NSA_WIDTH)
    for c, sl in tiles(NSA_WIDTH):
        nq_ref[0, :, sl] = (_rot_nsa(z(c), cn_s[...], sn_s[...], m) * (scale_q * LOG2E)).astype(BF16)

    z = mm(_O_KV, 6 * KV_WIDTH)
    kc0, kc1 = _dup_groups(z(0), lane)
    vc0, vc1 = _dup_groups(z(1), lane)
    for i, t in enumerate((kc0, kc1, vc0, vc1)):
        cmps_ref[0, :, i * LANES:(i + 1) * LANES] = t
    lo = lane < HEAD_DIM
    tok = pl.program_id(1) * ts + lax.broadcasted_iota(jnp.int32, (ts, LANES), 0)
    onehot = jnp.where(lane - HEAD_DIM == (tok >> SLC_SHIFT), 1.0, 0.0)
    ones = jnp.ones((ts, LANES), F32)
    for c, ref, rotate, upper in ((2, ksd_ref, True, onehot), (3, vsd_ref, False, ones),
                                  (4, kwd_ref, True, None), (5, vwd_ref, False, ones)):
        t = _rot_nsa(z(c), cn_s[...], sn_s[...], m) if rotate else z(c)
        if upper is None:
            d0, d1 = _dup_groups(t, lane)
        else:
            d0, d1 = jnp.where(lo, t, upper), jnp.where(lo, pltpu.roll(t, HEAD_DIM, 1), upper)
        ref[0, :, :LANES] = d0.astype(BF16)
        ref[0, :, LANES:] = d1.astype(BF16)

    z = mm(_O_NG, NSA_WIDTH + LANES)
    for c, sl in tiles(NSA_WIDTH):
        ngs_ref[0, :, sl] = _silu(z(c)).astype(BF16)
    gl_ref[0] = jax.nn.sigmoid(z(NSA_WIDTH // LANES))


def _proj(x, shift, scale, g_pre, pos3, inv_ret_row, inv_nsa_row, w_all, ts):
    bsz, seq, d = x.shape
    row = lambda w: pl.BlockSpec((1, ts, w), lambda b, s: (b, s, 0))
    vec = pl.BlockSpec((1, 1, d), lambda b, s: (b, 0, 0))
    const2 = lambda shp: pl.BlockSpec(shp, lambda b, s: (0, 0))
    outs = [(RET_WIDTH, BF16), (RET_WIDTH, BF16), (RET_WIDTH, BF16), (RET_WIDTH, BF16),
            (NSA_WIDTH, BF16), (4 * LANES, F32), (2 * LANES, BF16), (2 * LANES, BF16),
            (2 * LANES, BF16), (2 * LANES, BF16), (NSA_WIDTH, BF16), (LANES, F32)]
    return pl.pallas_call(
        _proj_kernel,
        grid=(bsz, seq // ts),
        in_specs=[row(d), vec, vec, const2((1, d)), row(1), const2((1, LANES)), const2((1, LANES)),
                  const2((d, _PROJ_COLS))],
        out_specs=[row(w) for w, _ in outs],
        out_shape=[jax.ShapeDtypeStruct((bsz, seq, w), dt) for w, dt in outs],
        scratch_shapes=[pltpu.VMEM((ts, LANES), F32)] * 4 + [pltpu.VMEM((ts, 6 * KV_WIDTH), F32)] * 2,
        compiler_params=_params("parallel", "arbitrary"),
        name="proj",
    )(x, shift, scale, g_pre, pos3, inv_ret_row, inv_nsa_row, w_all)


def _ret_kernel(q_ref, k_ref, v_ref, g_ref, dec_ref, zeta_ref, xi_ref, cd_ref, o_ref, r_s):
    rows = q_ref.shape[1]
    cc = RET_CHUNK

    @pl.when(pl.program_id(2) == 0)
    def _():
        r_s[...] = jnp.zeros_like(r_s)

    lane = lax.broadcasted_iota(jnp.int32, (cc, LANES), 1)
    rowi = lax.broadcasted_iota(jnp.int32, (cc, LANES), 0)
    lo = lane < HEAD_DIM
    blockdiag = (rowi < HEAD_DIM) == lo
    inv_n = 1.0 / HEAD_DIM
    chunks = [slice(c * cc, (c + 1) * cc) for c in range(rows // cc)]
    s_ab, kvs = [], []
    for sl in chunks:
        q, k, v = q_ref[0, sl, :], k_ref[0, sl, :], v_ref[0, sl, :]
        zero = jnp.zeros_like(q)
        s_ab.append((_dot_nt(jnp.where(lo, q, zero), k), _dot_nt(jnp.where(lo, zero, q), k)))
        kz = (k.astype(F32) * zeta_ref[0]).astype(BF16)
        kvs.append(jnp.where(blockdiag, _dot_tn(kz, v), 0.0))
    inners = []
    for sl, (s_a, s_b) in zip(chunks, s_ab):
        v = v_ref[0, sl, :]
        inners.append(jnp.where(lo, _dot((s_a * dec_ref[0]).astype(BF16), v),
                                _dot((s_b * dec_ref[1]).astype(BF16), v)))
    r = r_s[...]
    outs = []
    for sl, inner, kv in zip(chunks, inners, kvs):
        outs.append(inner + _dot(q_ref[0, sl, :], r.astype(BF16)) * xi_ref[0])
        r = r * cd_ref[0] + kv
    r_s[...] = r
    head_ones = jnp.where(blockdiag, 1.0, 0.0).astype(BF16)

    def head_sums(z):
        hi = z.astype(BF16)
        lo_part = (z - hi.astype(F32)).astype(BF16)
        return _dot(hi, head_ones) + _dot(lo_part, head_ones)

    devs = [o - s * inv_n for o, s in zip(outs, [head_sums(o) for o in outs])]
    variances = [head_sums(d * d) * inv_n for d in devs]
    for sl, d, var in zip(chunks, devs, variances):
        o_ref[0, sl, :] = (d * lax.rsqrt(var + GN_EPS) * g_ref[0, sl, :]).astype(o_ref.dtype)


def _retention(rq, rk, rv, rgs, dec, zeta_l, xi_l, cd_l, rows):
    bsz, seq, _ = rq.shape
    npair = RET_HEADS // 2
    blk = pl.BlockSpec((1, rows, LANES), lambda b, p, s: (b, s, p))
    return pl.pallas_call(
        _ret_kernel,
        grid=(bsz, npair, seq // rows),
        in_specs=[blk, blk, blk, blk,
                  pl.BlockSpec((2, RET_CHUNK, RET_CHUNK), lambda b, p, s: (p, 0, 0)),
                  pl.BlockSpec((1, RET_CHUNK, LANES), lambda b, p, s: (p, 0, 0)),
                  pl.BlockSpec((1, RET_CHUNK, LANES), lambda b, p, s: (p, 0, 0)),
                  pl.BlockSpec((1, 1, LANES), lambda b, p, s: (p, 0, 0))],
        out_specs=blk,
        out_shape=jax.ShapeDtypeStruct((bsz, seq, RET_WIDTH), BF16),
        scratch_shapes=[pltpu.VMEM((LANES, LANES), F32)],
        compiler_params=_params("parallel", "parallel", "arbitrary"),
        name="retention",
    )(rq, rk, rv, rgs, dec, zeta_l, xi_l, cd_l)


def _cmp_kernel(src_ref, pe_ref, w1_ref, w2_ref, pos_ref, invn_ref, o_ref):
    ncp = o_ref.shape[3]
    half = CMP_BLOCK // 2
    p_lo = jnp.zeros((ncp, CMP_HIDDEN), F32)
    p_hi = jnp.zeros((ncp, CMP_HIDDEN), F32)
    for l in range(half):
        t = src_ref[0, pl.ds(l, ncp, stride=CMP_STRIDE), :]
        p_lo = p_lo + _dot((t + pe_ref[0, l:l + 1, :]).astype(BF16), w1_ref[0, l])
        p_hi = p_hi + _dot((t + pe_ref[0, half + l:half + l + 1, :]).astype(BF16), w1_ref[0, half + l])
    hidden = p_lo + pltpu.roll(p_hi, ncp - 1, 0)
    comp = _dot(_silu(hidden).astype(BF16), w2_ref[0])
    m = _lane_in_head(comp.shape)
    cos, sin = _nsa_tables(pos_ref[0].astype(F32), invn_ref[...], m)
    rotated = _rot_nsa(comp, cos, sin, m)
    is_key = pl.program_id(1) == 0
    o_ref[0, 0, 0] = jnp.where(is_key, rotated, comp).astype(BF16)


def _compress(cmps, pe2, w1p, w2d, pos_cmp, inv_nsa_row):
    bsz, seq, _ = cmps.shape
    ncp = seq // CMP_STRIDE
    g = NSA_KV_GROUPS
    return pl.pallas_call(
        _cmp_kernel,
        grid=(bsz, 2, g),
        in_specs=[pl.BlockSpec((1, seq, LANES), lambda b, j, gi: (b, 0, j * g + gi)),
                  pl.BlockSpec((1, CMP_BLOCK, LANES), lambda b, j, gi: (j, 0, 0)),
                  pl.BlockSpec((1, CMP_BLOCK, LANES, CMP_HIDDEN), lambda b, j, gi: (j, 0, 0, 0)),
                  pl.BlockSpec((1, CMP_HIDDEN, LANES), lambda b, j, gi: (j, 0, 0)),
                  pl.BlockSpec((1, ncp, 1), lambda b, j, gi: (b, 0, 0)),
                  pl.BlockSpec((1, LANES), lambda b, j, gi: (0, 0))],
        out_specs=pl.BlockSpec((1, 1, 1, ncp, LANES), lambda b, j, gi: (b, j, gi, 0, 0)),
        out_shape=jax.ShapeDtypeStruct((bsz, 2, g, ncp, LANES), BF16),
        compiler_params=_params("parallel", "arbitrary", "arbitrary"),
        name="compress",
    )(cmps, pe2, w1p, w2d, pos_cmp, inv_nsa_row)


def _cmpsel_kernel(q_ref, kc_ref, vc_ref, ovt_ref, o_ref, selt_ref, qs_s, score_s, rank_s):
    nh = HEADS_PER_GROUP
    sub = 8
    tq = q_ref.shape[1]
    ncp = kc_ref.shape[3]
    nblk = ovt_ref.shape[0]
    t0 = pl.program_id(2) * tq
    kc = kc_ref[0, 0, 0]
    vc = vc_ref[0, 0, 0]

    tok = t0 + lax.broadcasted_iota(jnp.int32, (tq, LANES), 0)
    lane = lax.broadcasted_iota(jnp.int32, (tq, LANES), 1)
    biases = [jnp.where((lane + c * LANES) * CMP_STRIDE + (CMP_BLOCK - 1) <= tok, 0.0, NEG)
              for c in range(ncp // LANES)]
    has_key = tok >= CMP_BLOCK - 1
    _stack_head_queries(q_ref, qs_s, tq)
    s_all = _dot_nt(qs_s[...], kc)
    psum, probs = None, []
    for h in range(nh):
        tiles = [t + b for b, t in zip(biases, _lane_tiles(s_all[h * tq:(h + 1) * tq, :]))]
        mx = _row_stat(tiles, jnp.maximum, jnp.max)
        es = [jnp.exp2(t - mx) for t in tiles]
        inv = jnp.where(has_key, 1.0 / _row_stat(es, jnp.add, jnp.sum), 0.0)
        ps = [e * inv for e in es]
        psum = ps if psum is None else [a + b for a, b in zip(psum, ps)]
        probs.append(jnp.concatenate(ps, axis=1).astype(BF16))
    pv = _dot(jnp.concatenate(probs, axis=0), vc)
    _store_heads(o_ref, [pv[h * tq:(h + 1) * tq, :] for h in range(nh)], tq)

    psum = jnp.concatenate(psum, axis=1)
    p_hi = psum.astype(BF16)
    p_lo = (psum - p_hi.astype(F32)).astype(BF16)
    imp = _dot_nt(ovt_ref[...], p_hi) + _dot_nt(ovt_ref[...], p_lo)
    blk = lax.broadcasted_iota(jnp.int32, (nblk, tq), 0)
    cur = (t0 + lax.broadcasted_iota(jnp.int32, (nblk, tq), 1)) >> SLC_SHIFT
    forced = (blk == 0) | (blk == cur) | (blk == cur - 1)
    score_s[...] = jnp.where(forced, BIG, jnp.where(blk <= cur, imp, NEG))

    rank_s[...] = jnp.zeros_like(rank_s)
    last_blk = (t0 + tq - 1) >> SLC_SHIFT
    row_in_group = lax.broadcasted_iota(jnp.int32, (sub, tq), 0)
    for ig in range(nblk // sub):
        @pl.when(ig * sub <= last_blk)
        def _():
            groups = [slice(r * sub, (r + 1) * sub) for r in range(nblk // sub)]
            scs = [score_s[g, :] for g in groups]
            cnts = [rank_s[g, :] for g in groups]
            for i in range(ig * sub, (ig + 1) * sub):
                ri = jnp.broadcast_to(score_s[i:i + 1, :], (sub, tq))
                for r, sc in enumerate(scs):
                    if r * sub > i:
                        beats = ri >= sc
                    elif (r + 1) * sub - 1 <= i:
                        beats = ri > sc
                    else:
                        beats = (ri > sc) | ((ri == sc) & (row_in_group > i - r * sub))
                    cnts[r] = cnts[r] + jnp.where(beats, 1.0, 0.0)
            for g, cnt in zip(groups, cnts):
                rank_s[g, :] = cnt
    sel = jnp.where((rank_s[...] < float(min(SLC_TOPK, nblk))) & (score_s[...] > 0.5 * NEG), 1.0, 0.0)
    pads = [jnp.zeros((HEAD_DIM, tq), F32), sel]
    if nblk < HEAD_DIM:
        pads.append(jnp.zeros((HEAD_DIM - nblk, tq), F32))
    selt_ref[0, 0] = jnp.concatenate(pads, axis=0).T.astype(selt_ref.dtype)


def _cmpsel(nq, cmpkv, ovt, tq):
    bsz, seq, _ = nq.shape
    g = NSA_KV_GROUPS
    ncp = cmpkv.shape[3]
    nblk = ovt.shape[0]
    gw = HEADS_PER_GROUP * HEAD_DIM
    return pl.pallas_call(
        _cmpsel_kernel,
        grid=(bsz, g, seq // tq),
        in_specs=[pl.BlockSpec((1, tq, gw), lambda b, gi, i: (b, i, gi)),
                  pl.BlockSpec((1, 1, 1, ncp, LANES), lambda b, gi, i: (b, 0, gi, 0, 0)),
                  pl.BlockSpec((1, 1, 1, ncp, LANES), lambda b, gi, i: (b, 1, gi, 0, 0)),
                  pl.BlockSpec((nblk, ncp), lambda b, gi, i: (0, 0))],
        out_specs=[pl.BlockSpec((1, tq, gw), lambda b, gi, i: (b, i, gi)),
                   pl.BlockSpec((1, 1, tq, LANES), lambda b, gi, i: (b, gi, i, 0))],
        out_shape=[jax.ShapeDtypeStruct((bsz, seq, NSA_WIDTH), BF16),
                   jax.ShapeDtypeStruct((bsz, g, seq, LANES), BF16)],
        scratch_shapes=[pltpu.VMEM((HEADS_PER_GROUP * tq, LANES), BF16),
                        pltpu.VMEM((nblk, tq), F32), pltpu.VMEM((nblk, tq), F32)],
        compiler_params=_params("parallel", "parallel", "arbitrary"),
        name="cmpsel",
    )(nq, cmpkv, cmpkv, ovt)


def _stack_head_queries(q_ref, qs_s, rows, src=None):
    src = slice(0, rows) if src is None else src
    lo = lax.broadcasted_iota(jnp.int32, (rows, LANES), 1) < HEAD_DIM
    for c in range(HEADS_PER_GROUP // 2):
        q = q_ref[0, src, c * LANES:(c + 1) * LANES]
        zero = jnp.zeros_like(q)
        qs_s[(2 * c) * rows:(2 * c + 1) * rows, :] = jnp.where(lo, q, zero)
        qs_s[(2 * c + 1) * rows:(2 * c + 2) * rows, :] = jnp.where(lo, zero, q)


def _store_heads(o_ref, outs, tq):
    lo = lax.broadcasted_iota(jnp.int32, (tq, LANES), 1) < HEAD_DIM
    for c in range(HEADS_PER_GROUP // 2):
        o_ref[0, :, c * LANES:(c + 1) * LANES] = jnp.where(lo, outs[2 * c], outs[2 * c + 1]).astype(o_ref.dtype)


def _store_heads_normalized(o_ref, accs, tq, rows=slice(None)):
    lo = lax.broadcasted_iota(jnp.int32, (tq, LANES), 1) < HEAD_DIM
    outs = [a * (1.0 / jnp.where(lo, pltpu.roll(a, HEAD_DIM, 1), 1.0)) for a in accs]
    for c in range(HEADS_PER_GROUP // 2):
        o_ref[0, rows, c * LANES:(c + 1) * LANES] = jnp.where(
            lo, outs[2 * c], pltpu.roll(outs[2 * c + 1], HEAD_DIM, 1)).astype(o_ref.dtype)


def _lane_tiles(z):
    return [z[:, c * LANES:(c + 1) * LANES] for c in range(z.shape[1] // LANES)]


def _row_stat(tiles, combine, reduce):
    acc = functools.reduce(combine, tiles)
    return jnp.broadcast_to(reduce(acc, axis=-1, keepdims=True), acc.shape)


WIN_UNROLL = 4


def _win_kernel(q_ref, k_ref, v_ref, o_ref, qs0_s, qs1_s, s0_s, s1_s, p0_s, p1_s, *, tq):
    nh = HEADS_PER_GROUP
    span = WINDOW + tq
    n_qt = q_ref.shape[1] // tq
    qs_b, s_b, p_b = (qs0_s, qs1_s), (s0_s, s1_s), (p0_s, p1_s)

    def q_rows(i):
        return pl.ds(pl.multiple_of(i * tq, tq), tq)

    def k_rows(i):
        return pl.ds(pl.multiple_of(jnp.maximum(i * tq - WINDOW, 0), tq), span)

    def stack_queries(i, slot):
        _stack_head_queries(q_ref, qs_b[slot], tq, src=q_rows(jnp.minimum(i, n_qt - 1)))

    def scores(i, slot):
        i = jnp.minimum(i, n_qt - 1)
        s_b[slot][...] = _dot_nt(qs_b[slot][...], k_ref[0, k_rows(i), :])

    def values(i, slot):
        i = jnp.maximum(i, 0)
        pv = _dot(p_b[slot][...], v_ref[0, k_rows(i), :])
        _store_heads_normalized(o_ref, [pv[h * tq:(h + 1) * tq, :] for h in range(nh)], tq, q_rows(i))

    def softmax(i, slot):
        tok = i * tq + lax.broadcasted_iota(jnp.int32, (tq, span), 0)
        key = jnp.maximum(i * tq - WINDOW, 0) + lax.broadcasted_iota(jnp.int32, (tq, span), 1)
        bias = jnp.where((key <= tok) & (tok - key < WINDOW), 0.0, NEG)
        for h in range(nh):
            rows = slice(h * tq, (h + 1) * tq)
            tiles = _lane_tiles(s_b[slot][rows, :] + bias)
            m = _row_stat(tiles, jnp.maximum, jnp.max)
            p_b[slot][rows, :] = jnp.concatenate([jnp.exp2(t - m) for t in tiles], axis=1).astype(BF16)

    def stages(first, count):
        for t in range(count):
            i, slot = first + t, t % 2
            values(i - 1, 1 - slot)
            scores(i + 1, 1 - slot)
            stack_queries(i + 2, slot)
            softmax(i, slot)

    p1_s[...] = jnp.ones_like(p1_s)
    stack_queries(0, 0)
    stack_queries(1, 1)
    scores(0, 0)

    def body(b, carry):
        stages(b * WIN_UNROLL, WIN_UNROLL)
        return carry

    lax.fori_loop(0, n_qt // WIN_UNROLL, body, 0)
    stages(n_qt - n_qt % WIN_UNROLL, n_qt % WIN_UNROLL)
    values(n_qt - 1, (n_qt - 1) % 2)


SLC_UNROLL = 4
SLC_STATE_BUFFERS = 8
assert SLC_UNROLL % 2 == 0 and SLC_UNROLL + 1 <= SLC_STATE_BUFFERS


def _slc_kernel(q_ref, k_ref, v_ref, selt_ref, o_ref, qs_s, s0_s, s1_s, p0_s, p1_s, a0_s, a1_s,
                m_s, acc_s, *, tq, tk):
    nh = HEADS_PER_GROUP
    seq = q_ref.shape[1]
    n_qt = seq // tq
    s_b, p_b, a_b = (s0_s, s1_s), (p0_s, p1_s), (a0_s, a1_s)
    lo = lax.broadcasted_iota(jnp.int32, (tq, LANES), 1) < HEAD_DIM

    def n_tiles(qi):
        return (qi * tq + tq - 1) // tk + 1

    n_items = sum((qi * tq + tq - 1) // tk + 1 for qi in range(n_qt))

    def advance(qi, j):
        wrap = j + 1 >= n_tiles(qi)
        return jnp.where(wrap, jnp.minimum(qi + 1, n_qt - 1), qi), jnp.where(wrap, 0, j + 1)

    def q_rows(qi):
        return pl.ds(pl.multiple_of(qi * tq, tq), tq)

    def k_rows(j):
        return pl.ds(pl.multiple_of(j * tk, tk), tk)

    def stack_queries(qi):
        par = qi % SLC_STATE_BUFFERS
        unselected = jnp.where(selt_ref[0, 0, q_rows(qi), :].astype(F32) > 0.5, 0.0, NEG)
        for c in range(nh // 2):
            q = q_ref[0, q_rows(qi), c * LANES:(c + 1) * LANES].astype(F32)
            qs_s[par, (2 * c) * tq:(2 * c + 1) * tq, :] = jnp.where(lo, q, unselected).astype(BF16)
            qs_s[par, (2 * c + 1) * tq:(2 * c + 2) * tq, :] = jnp.where(
                lo, pltpu.roll(q, HEAD_DIM, 1), unselected).astype(BF16)

    def scores(qi, j, slot):
        s_b[slot][...] = _dot_nt(qs_s[qi % SLC_STATE_BUFFERS], k_ref[0, k_rows(j), :])

    def values(qi, j, slot):
        pv = _dot(p_b[slot][...], v_ref[0, k_rows(j), :])
        par = qi % SLC_STATE_BUFFERS
        for h in range(nh):
            acc_s[par, h] = a_b[slot][h] * acc_s[par, h] + pv[h * tq:(h + 1) * tq, :]

    def softmax(qi, j, slot):
        tok = qi * tq + lax.broadcasted_iota(jnp.int32, (tq, tk), 0)
        key = j * tk + lax.broadcasted_iota(jnp.int32, (tq, tk), 1)
        bias = jnp.where(key <= tok, 0.0, NEG)
        par = qi % SLC_STATE_BUFFERS
        for h in range(nh):
            rows = slice(h * tq, (h + 1) * tq)
            tiles = _lane_tiles(s_b[slot][rows, :] + bias)
            m_prev = jnp.where(j == 0, NEG, m_s[par, h])
            m_new = jnp.maximum(m_prev, _row_stat(tiles, jnp.maximum, jnp.max))
            m_s[par, h] = m_new
            a_b[slot][h] = jnp.exp2(m_prev - m_new)
            p_b[slot][rows, :] = jnp.concatenate([jnp.exp2(t - m_new) for t in tiles], axis=1).astype(BF16)

    def finalize(qi):
        _store_heads_normalized(o_ref, [acc_s[qi % SLC_STATE_BUFFERS, h] for h in range(nh)], tq, q_rows(qi))

    def is_last(qi, j):
        return (j + 1 >= n_tiles(qi)).astype(jnp.int32)

    def finalize_if(flag, qi):
        @pl.when(flag == 1)
        def _():
            finalize(qi)

    def block(carry, n_stages):
        fin, (qp, jp, p_last), (qc, jc), (qn, jn) = carry
        for flag, qi in fin:
            finalize_if(flag, qi)
        items = [(qp, jp, p_last), (qc, jc, is_last(qc, jc)), (qn, jn, is_last(qn, jn))]
        for _ in range(n_stages - 1):
            qx, jx = advance(items[-1][0], items[-1][1])
            items.append((qx, jx, is_last(qx, jx)))
        for qx, jx, _ in items[2:]:
            @pl.when(jx == 0)
            def _():
                stack_queries(qx)
        for t in range(n_stages):
            slot = t % 2
            values(items[t][0], items[t][1], 1 - slot)
            scores(items[t + 2][0], items[t + 2][1], 1 - slot)
            softmax(items[t + 1][0], items[t + 1][1], slot)
        fin = tuple((items[t][2], items[t][0]) for t in range(n_stages))
        nxt = advance(items[-1][0], items[-1][1])
        return fin, items[n_stages], items[n_stages + 1][:2], nxt

    m_s[...] = jnp.full_like(m_s, NEG)
    acc_s[...] = jnp.zeros_like(acc_s)
    p1_s[...] = jnp.zeros_like(p1_s)
    a1_s[...] = jnp.ones_like(a1_s)
    zero = jnp.int32(0)
    stack_queries(zero)
    scores(zero, zero, 0)
    unroll = SLC_UNROLL
    carry = (((zero, zero),) * unroll, (zero, zero, zero), (zero, zero), advance(zero, zero))
    carry = lax.fori_loop(0, n_items // unroll, lambda i, c: block(c, unroll), carry)
    if n_items % unroll:
        carry = block(carry, n_items % unroll)
    fin, (qp, jp, _), _, _ = carry
    for flag, qi in fin:
        finalize_if(flag, qi)
    values(qp, jp, (n_items - 1) % 2)
    finalize(qp)


def _slc_attention(nq, kd, vd, selt, *, tq, tk):
    bsz, seq, _ = nq.shape
    g = NSA_KV_GROUPS
    gw = HEADS_PER_GROUP * HEAD_DIM
    nh = HEADS_PER_GROUP
    assert seq % tk == 0 and tk % tq == 0
    assert seq // SLC_BLOCK <= HEAD_DIM
    whole = lambda w: pl.BlockSpec((1, seq, w), lambda b, gi: (b, 0, gi))
    stat = pltpu.VMEM((nh, tq, LANES), F32)
    state = pltpu.VMEM((SLC_STATE_BUFFERS, nh, tq, LANES), F32)
    return pl.pallas_call(
        functools.partial(_slc_kernel, tq=tq, tk=tk),
        grid=(bsz, g),
        in_specs=[whole(gw), whole(LANES), whole(LANES),
                  pl.BlockSpec((1, 1, seq, LANES), lambda b, gi: (b, gi, 0, 0))],
        out_specs=whole(gw),
        out_shape=jax.ShapeDtypeStruct((bsz, seq, NSA_WIDTH), BF16),
        scratch_shapes=([pltpu.VMEM((SLC_STATE_BUFFERS, nh * tq, LANES), BF16)] + [pltpu.VMEM((nh * tq, tk), F32)] * 2 +
                        [pltpu.VMEM((nh * tq, tk), BF16)] * 2 + [stat] * 2 + [state] * 2),
        compiler_params=_params("parallel", "arbitrary"),
        name="slc_attn",
    )(nq, kd, vd, selt)


def _win_attention(nq, kd, vd, *, tq):
    bsz, seq, _ = nq.shape
    g = NSA_KV_GROUPS
    gw = HEADS_PER_GROUP * HEAD_DIM
    nh = HEADS_PER_GROUP
    assert WINDOW % tq == 0 and seq >= WINDOW + tq and seq // tq >= 2 and WIN_UNROLL % 2 == 0
    whole = lambda w: pl.BlockSpec((1, seq, w), lambda b, gi: (b, 0, gi))
    span = WINDOW + tq
    return pl.pallas_call(
        functools.partial(_win_kernel, tq=tq),
        grid=(bsz, g),
        in_specs=[whole(gw), whole(LANES), whole(LANES)],
        out_specs=whole(gw),
        out_shape=jax.ShapeDtypeStruct((bsz, seq, NSA_WIDTH), BF16),
        scratch_shapes=([pltpu.VMEM((nh * tq, LANES), BF16)] * 2 + [pltpu.VMEM((nh * tq, span), F32)] * 2 +
                        [pltpu.VMEM((nh * tq, span), BF16)] * 2),
        compiler_params=_params("parallel", "arbitrary"),
        name="win_attn",
    )(nq, kd, vd)


def _out_kernel(x_ref, gate_ref, gpost_ref, ret_ref, oc_ref, os_ref, ow_ref, gl_ref, ngs_ref,
                wo_ref, o_ref):
    ts = x_ref.shape[1]
    gates = gl_ref[0]
    lo_off = 32
    assert 3 * NSA_HEADS <= lo_off
    g_hi = gates.astype(BF16).astype(F32)
    g_lo = (gates - g_hi).astype(BF16).astype(F32)
    lane = lax.broadcasted_iota(jnp.int32, (ts, LANES), 1)
    g_split = jnp.where(lane < lo_off, g_hi, pltpu.roll(g_lo, lo_off, 1)).astype(BF16)
    src = lax.broadcasted_iota(jnp.int32, (LANES, NSA_WIDTH), 0)
    head = lax.broadcasted_iota(jnp.int32, (LANES, NSA_WIDTH), 1) >> HEAD_SHIFT
    att = jnp.zeros((ts, NSA_WIDTH), F32)
    for i, br in enumerate((oc_ref, os_ref, ow_ref)):
        expand = jnp.where((src == 3 * head + i) | (src == lo_off + 3 * head + i), 1.0, 0.0).astype(BF16)
        att = att + _dot(g_split, expand) * br[0]
    att = att * ngs_ref[0]
    y = _dot(ret_ref[0].astype(BF16), wo_ref[:RET_WIDTH, :]) + _dot(att.astype(BF16), wo_ref[RET_WIDTH:, :])
    ms = jnp.mean(y * y, axis=-1, keepdims=True)
    y = y * lax.rsqrt(ms + EPS) * gpost_ref[...]
    o_ref[0] = x_ref[0] + gate_ref[0] * y


def _out(x, gate, g_post, ret, o_cmp, o_slc, o_win, gl, ngs, w_out, ts):
    bsz, seq, d = x.shape
    row = lambda w: pl.BlockSpec((1, ts, w), lambda b, s: (b, s, 0))
    return pl.pallas_call(
        _out_kernel,
        grid=(bsz, seq // ts),
        in_specs=[row(d), pl.BlockSpec((1, 1, d), lambda b, s: (b, 0, 0)),
                  pl.BlockSpec((1, d), lambda b, s: (0, 0)),
                  row(RET_WIDTH), row(NSA_WIDTH), row(NSA_WIDTH), row(NSA_WIDTH), row(LANES),
                  row(NSA_WIDTH), pl.BlockSpec(w_out.shape, lambda b, s: (0, 0))],
        out_specs=row(d),
        out_shape=jax.ShapeDtypeStruct((bsz, seq, d), x.dtype),
        compiler_params=_params("parallel", "arbitrary"),
        name="out",
    )(x, gate, g_post, ret, o_cmp, o_slc, o_win, gl, ngs, w_out)


def _retention_tables():
    h = RET_HEADS
    c = RET_CHUNK
    log_g = jnp.log1p(-jnp.power(2.0, -5.0 - jnp.arange(h, dtype=F32)))
    idx = jnp.arange(c, dtype=F32)
    diff = idx[:, None] - idx[None, :]
    dec = jnp.where(diff[None] >= 0, jnp.exp(jnp.maximum(diff, 0.0)[None] * log_g[:, None, None]), 0.0)
    zeta = jnp.exp((c - 1 - idx)[None, :] * log_g[:, None])
    xi = jnp.exp((idx + 1.0)[None, :] * log_g[:, None])
    cd = jnp.exp(c * log_g)
    to_lanes = lambda t: jnp.repeat(t.reshape(h // 2, 2, c).transpose(0, 2, 1), HEAD_DIM, axis=2)
    cd_l = jnp.repeat(cd.reshape(h // 2, 1, 2), HEAD_DIM, axis=2)
    return dec, to_lanes(zeta), to_lanes(xi), cd_l


def _overlap_t(seq):
    ncp = seq // CMP_STRIDE
    nc = (seq - CMP_BLOCK) // CMP_STRIDE + 1
    ns = seq // SLC_BLOCK
    cs = np.arange(ncp) * CMP_STRIDE
    ce = cs + CMP_BLOCK - 1
    ss = np.arange(ns) * SLC_BLOCK
    ov = (cs[None, :] <= ss[:, None] + SLC_BLOCK - 1) & (ce[None, :] >= ss[:, None]) & (np.arange(ncp) < nc)[None, :]
    return jnp.asarray(ov.astype(np.float32), dtype=BF16)


def _layer(x, c, positions, w_ada, b_ada, g_pre, g_post, w_in, w_out,
           pe_k, w1_k, w2_k, pe_v, w1_v, w2_v):
    bsz, seq, d = x.shape
    mod = _ada(c, w_ada, b_ada)
    shift, scale, gate = [t.reshape(bsz, 1, d) for t in jnp.split(mod, 3, axis=-1)]

    offs = np.cumsum((RET_WIDTH,) * 4 + (NSA_WIDTH, 6 * KV_WIDTH, 3 * NSA_HEADS, NSA_WIDTH))
    w_gl = w_in[:, offs[5]:offs[6]]
    w_all = jnp.concatenate(
        [w_in[:, :offs[5]], w_in[:, offs[6]:], w_gl,
         jnp.zeros((d, LANES - w_gl.shape[1]), w_in.dtype)], axis=1).astype(BF16)

    lanes = jnp.arange(LANES)
    half_r = HEAD_DIM // 2
    inv_r = jnp.power(RET_ROPE_BASE, -jnp.arange(half_r, dtype=F32) / half_r)
    half_n = ROPE_DIM // 2
    inv_n = jnp.power(NSA_ROPE_BASE, -jnp.arange(half_n, dtype=F32) / half_n)
    inv_ret_row = inv_r[lanes % half_r].reshape(1, LANES)
    inv_nsa_row = inv_n[lanes % half_n].reshape(1, LANES)
    pos3 = positions.reshape(bsz, seq, 1)

    (rq, rk, rv, rgs, nq, cmps, ksd, vsd, kwd, vwd, ngs, gl) = _proj(
        x, shift, scale, g_pre.reshape(1, d), pos3, inv_ret_row, inv_nsa_row, w_all, ts=512)

    dec, zeta_l, xi_l, cd_l = _retention_tables()
    ret = _retention(rq, rk, rv, rgs, dec, zeta_l, xi_l, cd_l, rows=min(1024, seq))

    ncp = seq // CMP_STRIDE
    nc = (seq - CMP_BLOCK) // CMP_STRIDE + 1
    pad_l = lambda t: jnp.concatenate([t, jnp.zeros_like(t)], axis=-1)
    pe2 = jnp.stack([pad_l(pe_k), pad_l(pe_v)])
    w1r = jnp.stack([w1_k, w1_v]).reshape(2, CMP_BLOCK, HEAD_DIM, CMP_HIDDEN)
    w1p = jnp.concatenate([w1r, jnp.zeros_like(w1r)], axis=2).astype(BF16)
    w2s = jnp.stack([w2_k, w2_v])
    w2d = jnp.concatenate([w2s, w2s], axis=-1).astype(BF16)
    cmp_end = np.arange(nc) * CMP_STRIDE + CMP_BLOCK - 1
    pos_cmp = jnp.pad(positions[:, cmp_end], ((0, 0), (0, ncp - nc))).reshape(bsz, ncp, 1)
    cmpkv = _compress(cmps, pe2, w1p, w2d, pos_cmp, inv_nsa_row)

    o_cmp, selt = _cmpsel(nq, cmpkv, _overlap_t(seq), tq=min(256, seq))
    o_slc = _slc_attention(nq, ksd, vsd, selt, tq=256, tk=512)
    o_win = _win_attention(nq, kwd, vwd, tq=128)

    return _out(x, gate, g_post.reshape(1, d), ret, o_cmp, o_slc, o_win, gl, ngs,
                w_out.astype(BF16), ts=512)


def kernel(x, c, positions, w_ada, b_ada, g_pre, g_post, w_in, w_out, cmp_pe_k, cmp_w1_k, cmp_w2_k, cmp_pe_v, cmp_w1_v, cmp_w2_v):
    for l in range(w_in.shape[0]):
        x = _layer(x, c, positions, w_ada[l], b_ada[l], g_pre[l], g_post[l], w_in[l], w_out[l],
                   cmp_pe_k[l], cmp_w1_k[l], cmp_w2_k[l], cmp_pe_v[l], cmp_w1_v[l], cmp_w2_v[l])
    return x
```

```python
import functools

import numpy as np
import jax
import jax.numpy as jnp
from jax import lax
from jax.experimental import pallas as pl
from jax.experimental.pallas import tpu as pltpu

F32 = jnp.float32
BF16 = jnp.bfloat16

LANES = 128
HEAD_DIM = 64
RET_HEADS = 8
NSA_HEADS = 8
NSA_KV_GROUPS = 2
HEADS_PER_GROUP = NSA_HEADS // NSA_KV_GROUPS
RET_WIDTH = RET_HEADS * HEAD_DIM
NSA_WIDTH = NSA_HEADS * HEAD_DIM
KV_WIDTH = NSA_KV_GROUPS * HEAD_DIM
RET_CHUNK = 128
RET_ROPE_BASE = 10000.0
NSA_ROPE_BASE = 500000.0
ROPE_DIM = HEAD_DIM // 4
CMP_BLOCK = 32
CMP_STRIDE = 16
CMP_HIDDEN = 256
SLC_BLOCK = 64
SLC_TOPK = 16
WINDOW = 512
NEG = -1e30
LOG2E = 1.4426950408889634
BIG = 1e9
EPS = 1e-6
GN_EPS = 1e-5
VMEM_LIMIT = 48 * 1024 * 1024

NT_DIMS = (((1,), (1,)), ((), ()))
TN_DIMS = (((0,), (0,)), ((), ()))


def _dot(a, b):
    return jnp.dot(a, b, preferred_element_type=F32)


def _dot_nt(a, b):
    return lax.dot_general(a, b, NT_DIMS, preferred_element_type=F32)


def _dot_tn(a, b):
    return lax.dot_general(a, b, TN_DIMS, preferred_element_type=F32)


def _silu(z):
    return z * jax.nn.sigmoid(z)


def _params(*sem):
    return pltpu.CompilerParams(dimension_semantics=sem, vmem_limit_bytes=VMEM_LIMIT)


HEAD_SHIFT = 6
SLC_SHIFT = 6
assert 1 << HEAD_SHIFT == HEAD_DIM and 1 << SLC_SHIFT == SLC_BLOCK


def _lane_in_head(shape):
    return lax.broadcasted_iota(jnp.int32, shape, len(shape) - 1) & (HEAD_DIM - 1)


def _ada_kernel(c_ref, w_ref, b_ref, o_ref):
    a = _silu(c_ref[...])
    o_ref[...] = jnp.dot(a, w_ref[...], precision=lax.Precision.HIGHEST,
                         preferred_element_type=F32) + b_ref[...]


def _ada(c, w, b):
    bsz, d = c.shape
    n = w.shape[1]
    tn = 1024
    return pl.pallas_call(
        _ada_kernel,
        grid=(n // tn,),
        in_specs=[pl.BlockSpec((bsz, d), lambda j: (0, 0)),
                  pl.BlockSpec((d, tn), lambda j: (0, j)),
                  pl.BlockSpec((1, tn), lambda j: (0, j))],
        out_specs=pl.BlockSpec((bsz, tn), lambda j: (0, j)),
        out_shape=jax.ShapeDtypeStruct((bsz, n), F32),
        compiler_params=_params("arbitrary"),
        name="ada",
    )(c, w, b.reshape(1, n))


_O_RQ, _O_RK, _O_RV, _O_RG, _O_NQ = 0, 512, 1024, 1536, 2048
_O_KV = 2560
_O_NG = 3328
_O_GL = 3840
_PROJ_COLS = 3968


def _rot_ret(z, cos, sin_signed, m):
    partner = jnp.where(m < HEAD_DIM // 2, pltpu.roll(z, LANES - HEAD_DIM // 2, 1),
                        pltpu.roll(z, HEAD_DIM // 2, 1))
    return z * cos + partner * sin_signed


def _rot_nsa(z, cos, sin_signed, m):
    half = ROPE_DIM // 2
    partner = jnp.where(m < half, pltpu.roll(z, LANES - half, 1), pltpu.roll(z, half, 1))
    return z * cos + partner * sin_signed


def _nsa_tables(posf, inv_row, m):
    half = ROPE_DIM // 2
    ang = posf * inv_row
    cos = jnp.where(m < ROPE_DIM, jnp.cos(ang), 1.0)
    s = jnp.sin(ang)
    sin = jnp.where(m < half, -s, jnp.where(m < ROPE_DIM, s, 0.0))
    return cos, sin


def _dup_groups(z, m_lane):
    r = pltpu.roll(z, HEAD_DIM, 1)
    lo = m_lane < HEAD_DIM
    return jnp.where(lo, z, r), jnp.where(lo, r, z)


def _proj_kernel(x_ref, shift_ref, scale_ref, gpre_ref, pos_ref, invr_ref, invn_ref, w_ref,
                 rq_ref, rk_ref, rv_ref, rgs_ref, nq_ref, cmps_ref, ksd_ref, vsd_ref,
                 kwd_ref, vwd_ref, ngs_ref, gl_ref,
                 cr_s, sr_s, cn_s, sn_s, z0_s, z1_s):
    ts = x_ref.shape[1]
    x = x_ref[0]
    ms = jnp.mean(x * x, axis=-1, keepdims=True)
    y = x * lax.rsqrt(ms + EPS) * gpre_ref[...]
    h = y * (1.0 + scale_ref[0]) + shift_ref[0]
    hb = h.astype(BF16)

    posf = pos_ref[0].astype(F32)
    lane = lax.broadcasted_iota(jnp.int32, (ts, LANES), 1)
    m = lane & (HEAD_DIM - 1)
    ang = posf * invr_ref[...]
    cr_s[...] = jnp.cos(ang)
    s = jnp.sin(ang)
    sr_s[...] = jnp.where(m < HEAD_DIM // 2, -s, s)
    cn, sn = _nsa_tables(posf, invn_ref[...], m)
    cn_s[...] = cn
    sn_s[...] = sn

    staging = [z0_s, z1_s]

    def mm(off, width):
        z_s = staging[0]
        staging.reverse()
        z_s[:, :width] = _dot(hb, w_ref[:, off:off + width])
        return lambda c: z_s[:, c * LANES:(c + 1) * LANES]

    scale_q = HEAD_DIM ** -0.5
    tiles = lambda width: [(c, slice(c * LANES, (c + 1) * LANES)) for c in range(width // LANES)]
    z = mm(_O_RQ, RET_WIDTH)
    for c, sl in tiles(RET_WIDTH):
        rq_ref[0, :, sl] = (_rot_ret(z(c), cr_s[...], sr_s[...], m) * scale_q).astype(BF16)
    z = mm(_O_RK, RET_WIDTH)
    for c, sl in tiles(RET_WIDTH):
        rk_ref[0, :, sl] = _rot_ret(z(c), cr_s[...], sr_s[...], m).astype(BF16)
    z = mm(_O_RV, RET_WIDTH)
    for c, sl in tiles(RET_WIDTH):
        rv_ref[0, :, sl] = z(c).astype(BF16)
    z = mm(_O_RG, RET_WIDTH)
    for c, sl in tiles(RET_WIDTH):
        rgs_ref[0, :, sl] = _silu(z(c)).astype(BF16)
    z = mm(_O_NQ, NSA_WIDTH)
    for c, sl in tiles(NSA_WIDTH):
        nq_ref[0, :, sl] = (_rot_nsa(z(c), cn_s[...], sn_s[...], m) * (scale_q * LOG2E)).astype(BF16)

    z = mm(_O_KV, 6 * KV_WIDTH)
    kc0, kc1 = _dup_groups(z(0), lane)
    vc0, vc1 = _dup_groups(z(1), lane)
    for i, t in enumerate((kc0, kc1, vc0, vc1)):
        cmps_ref[0, :, i * LANES:(i + 1) * LANES] = t
    lo = lane < HEAD_DIM
    tok = pl.program_id(1) * ts + lax.broadcasted_iota(jnp.int32, (ts, LANES), 0)
    onehot = jnp.where(lane - HEAD_DIM == (tok >> SLC_SHIFT), 1.0, 0.0)
    ones = jnp.ones((ts, LANES), F32)
    for c, ref, rotate, upper in ((2, ksd_ref, True, onehot), (3, vsd_ref, False, ones),
                                  (4, kwd_ref, True, None), (5, vwd_ref, False, ones)):
        t = _rot_nsa(z(c), cn_s[...], sn_s[...], m) if rotate else z(c)
        if upper is None:
            d0, d1 = _dup_groups(t, lane)
        else:
            d0, d1 = jnp.where(lo, t, upper), jnp.where(lo, pltpu.roll(t, HEAD_DIM, 1), upper)
        ref[0, :, :LANES] = d0.astype(BF16)
        ref[0, :, LANES:] = d1.astype(BF16)

    z = mm(_O_NG, NSA_WIDTH + LANES)
    for c, sl in tiles(NSA_WIDTH):
        ngs_ref[0, :, sl] = _silu(z(c)).astype(BF16)
    gl_ref[0] = jax.nn.sigmoid(z(NSA_WIDTH // LANES))


def _proj(x, shift, scale, g_pre, pos3, inv_ret_row, inv_nsa_row, w_all, ts):
    bsz, seq, d = x.shape
    row = lambda w: pl.BlockSpec((1, ts, w), lambda b, s: (b, s, 0))
    vec = pl.BlockSpec((1, 1, d), lambda b, s: (b, 0, 0))
    const2 = lambda shp: pl.BlockSpec(shp, lambda b, s: (0, 0))
    outs = [(RET_WIDTH, BF16), (RET_WIDTH, BF16), (RET_WIDTH, BF16), (RET_WIDTH, BF16),
            (NSA_WIDTH, BF16), (4 * LANES, F32), (2 * LANES, BF16), (2 * LANES, BF16),
            (2 * LANES, BF16), (2 * LANES, BF16), (NSA_WIDTH, BF16), (LANES, F32)]
    return pl.pallas_call(
        _proj_kernel,
        grid=(bsz, seq // ts),
        in_specs=[row(d), vec, vec, const2((1, d)), row(1), const2((1, LANES)), const2((1, LANES)),
                  const2((d, _PROJ_COLS))],
        out_specs=[row(w) for w, _ in outs],
        out_shape=[jax.ShapeDtypeStruct((bsz, seq, w), dt) for w, dt in outs],
        scratch_shapes=[pltpu.VMEM((ts, LANES), F32)] * 4 + [pltpu.VMEM((ts, 6 * KV_WIDTH), F32)] * 2,
        compiler_params=_params("parallel", "arbitrary"),
        name="proj",
    )(x, shift, scale, g_pre, pos3, inv_ret_row, inv_nsa_row, w_all)


def _ret_kernel(q_ref, k_ref, v_ref, g_ref, dec_ref, zeta_ref, xi_ref, cd_ref, o_ref, r_s):
    rows = q_ref.shape[1]
    cc = RET_CHUNK

    @pl.when(pl.program_id(2) == 0)
    def _():
        r_s[...] = jnp.zeros_like(r_s)

    lane = lax.broadcasted_iota(jnp.int32, (cc, LANES), 1)
    rowi = lax.broadcasted_iota(jnp.int32, (cc, LANES), 0)
    lo = lane < HEAD_DIM
    blockdiag = (rowi < HEAD_DIM) == lo
    inv_n = 1.0 / HEAD_DIM
    chunks = [slice(c * cc, (c + 1) * cc) for c in range(rows // cc)]
    s_ab, kvs = [], []
    for sl in chunks:
        q, k, v = q_ref[0, sl, :], k_ref[0, sl, :], v_ref[0, sl, :]
        zero = jnp.zeros_like(q)
        s_ab.append((_dot_nt(jnp.where(lo, q, zero), k), _dot_nt(jnp.where(lo, zero, q), k)))
        kz = (k.astype(F32) * zeta_ref[0]).astype(BF16)
        kvs.append(jnp.where(blockdiag, _dot_tn(kz, v), 0.0))
    inners = []
    for sl, (s_a, s_b) in zip(chunks, s_ab):
        v = v_ref[0, sl, :]
        inners.append(jnp.where(lo, _dot((s_a * dec_ref[0]).astype(BF16), v),
                                _dot((s_b * dec_ref[1]).astype(BF16), v)))
    r = r_s[...]
    outs = []
    for sl, inner, kv in zip(chunks, inners, kvs):
        outs.append(inner + _dot(q_ref[0, sl, :], r.astype(BF16)) * xi_ref[0])
        r = r * cd_ref[0] + kv
    r_s[...] = r
    head_ones = jnp.where(blockdiag, 1.0, 0.0).astype(BF16)

    def head_sums(z):
        hi = z.astype(BF16)
        lo_part = (z - hi.astype(F32)).astype(BF16)
        return _dot(hi, head_ones) + _dot(lo_part, head_ones)

    devs = [o - s * inv_n for o, s in zip(outs, [head_sums(o) for o in outs])]
    variances = [head_sums(d * d) * inv_n for d in devs]
    for sl, d, var in zip(chunks, devs, variances):
        o_ref[0, sl, :] = (d * lax.rsqrt(var + GN_EPS) * g_ref[0, sl, :]).astype(o_ref.dtype)


def _retention(rq, rk, rv, rgs, dec, zeta_l, xi_l, cd_l, rows):
    bsz, seq, _ = rq.shape
    npair = RET_HEADS // 2
    blk = pl.BlockSpec((1, rows, LANES), lambda b, p, s: (b, s, p))
    return pl.pallas_call(
        _ret_kernel,
        grid=(bsz, npair, seq // rows),
        in_specs=[blk, blk, blk, blk,
                  pl.BlockSpec((2, RET_CHUNK, RET_CHUNK), lambda b, p, s: (p, 0, 0)),
                  pl.BlockSpec((1, RET_CHUNK, LANES), lambda b, p, s: (p, 0, 0)),
                  pl.BlockSpec((1, RET_CHUNK, LANES), lambda b, p, s: (p, 0, 0)),
                  pl.BlockSpec((1, 1, LANES), lambda b, p, s: (p, 0, 0))],
        out_specs=blk,
        out_shape=jax.ShapeDtypeStruct((bsz, seq, RET_WIDTH), BF16),
        scratch_shapes=[pltpu.VMEM((LANES, LANES), F32)],
        compiler_params=_params("parallel", "parallel", "arbitrary"),
        name="retention",
    )(rq, rk, rv, rgs, dec, zeta_l, xi_l, cd_l)


def _cmp_kernel(src_ref, pe_ref, w1_ref, w2_ref, pos_ref, invn_ref, o_ref):
    ncp = o_ref.shape[3]
    half = CMP_BLOCK // 2
    p_lo = jnp.zeros((ncp, CMP_HIDDEN), F32)
    p_hi = jnp.zeros((ncp, CMP_HIDDEN), F32)
    for l in range(half):
        t = src_ref[0, pl.ds(l, ncp, stride=CMP_STRIDE), :]
        p_lo = p_lo + _dot((t + pe_ref[0, l:l + 1, :]).astype(BF16), w1_ref[0, l])
        p_hi = p_hi + _dot((t + pe_ref[0, half + l:half + l + 1, :]).astype(BF16), w1_ref[0, half + l])
    hidden = p_lo + pltpu.roll(p_hi, ncp - 1, 0)
    comp = _dot(_silu(hidden).astype(BF16), w2_ref[0])
    m = _lane_in_head(comp.shape)
    cos, sin = _nsa_tables(pos_ref[0].astype(F32), invn_ref[...], m)
    rotated = _rot_nsa(comp, cos, sin, m)
    is_key = pl.program_id(1) == 0
    o_ref[0, 0, 0] = jnp.where(is_key, rotated, comp).astype(BF16)


def _compress(cmps, pe2, w1p, w2d, pos_cmp, inv_nsa_row):
    bsz, seq, _ = cmps.shape
    ncp = seq // CMP_STRIDE
    g = NSA_KV_GROUPS
    return pl.pallas_call(
        _cmp_kernel,
        grid=(bsz, 2, g),
        in_specs=[pl.BlockSpec((1, seq, LANES), lambda b, j, gi: (b, 0, j * g + gi)),
                  pl.BlockSpec((1, CMP_BLOCK, LANES), lambda b, j, gi: (j, 0, 0)),
                  pl.BlockSpec((1, CMP_BLOCK, LANES, CMP_HIDDEN), lambda b, j, gi: (j, 0, 0, 0)),
                  pl.BlockSpec((1, CMP_HIDDEN, LANES), lambda b, j, gi: (j, 0, 0)),
                  pl.BlockSpec((1, ncp, 1), lambda b, j, gi: (b, 0, 0)),
                  pl.BlockSpec((1, LANES), lambda b, j, gi: (0, 0))],
        out_specs=pl.BlockSpec((1, 1, 1, ncp, LANES), lambda b, j, gi: (b, j, gi, 0, 0)),
        out_shape=jax.ShapeDtypeStruct((bsz, 2, g, ncp, LANES), BF16),
        compiler_params=_params("parallel", "arbitrary", "arbitrary"),
        name="compress",
    )(cmps, pe2, w1p, w2d, pos_cmp, inv_nsa_row)


def _cmpsel_kernel(q_ref, kc_ref, vc_ref, ovt_ref, o_ref, selt_ref, qs_s, score_s, rank_s):
    nh = HEADS_PER_GROUP
    sub = 8
    tq = q_ref.shape[1]
    ncp = kc_ref.shape[3]
    nblk = ovt_ref.shape[0]
    t0 = pl.program_id(2) * tq
    kc = kc_ref[0, 0, 0]
    vc = vc_ref[0, 0, 0]

    tok = t0 + lax.broadcasted_iota(jnp.int32, (tq, LANES), 0)
    lane = lax.broadcasted_iota(jnp.int32, (tq, LANES), 1)
    biases = [jnp.where((lane + c * LANES) * CMP_STRIDE + (CMP_BLOCK - 1) <= tok, 0.0, NEG)
              for c in range(ncp // LANES)]
    has_key = tok >= CMP_BLOCK - 1
    _stack_head_queries(q_ref, qs_s, tq)
    s_all = _dot_nt(qs_s[...], kc)
    psum, probs = None, []
    for h in range(nh):
        tiles = [t + b for b, t in zip(biases, _lane_tiles(s_all[h * tq:(h + 1) * tq, :]))]
        mx = _row_stat(tiles, jnp.maximum, jnp.max)
        es = [jnp.exp2(t - mx) for t in tiles]
        inv = jnp.where(has_key, 1.0 / _row_stat(es, jnp.add, jnp.sum), 0.0)
        ps = [e * inv for e in es]
        psum = ps if psum is None else [a + b for a, b in zip(psum, ps)]
        probs.append(jnp.concatenate(ps, axis=1).astype(BF16))
    pv = _dot(jnp.concatenate(probs, axis=0), vc)
    _store_heads(o_ref, [pv[h * tq:(h + 1) * tq, :] for h in range(nh)], tq)

    psum = jnp.concatenate(psum, axis=1)
    p_hi = psum.astype(BF16)
    p_lo = (psum - p_hi.astype(F32)).astype(BF16)
    imp = _dot_nt(ovt_ref[...], p_hi) + _dot_nt(ovt_ref[...], p_lo)
    blk = lax.broadcasted_iota(jnp.int32, (nblk, tq), 0)
    cur = (t0 + lax.broadcasted_iota(jnp.int32, (nblk, tq), 1)) >> SLC_SHIFT
    forced = (blk == 0) | (blk == cur) | (blk == cur - 1)
    score_s[...] = jnp.where(forced, BIG, jnp.where(blk <= cur, imp, NEG))

    rank_s[...] = jnp.zeros_like(rank_s)
    last_blk = (t0 + tq - 1) >> SLC_SHIFT
    row_in_group = lax.broadcasted_iota(jnp.int32, (sub, tq), 0)
    for ig in range(nblk // sub):
        @pl.when(ig * sub <= last_blk)
        def _():
            groups = [slice(r * sub, (r + 1) * sub) for r in range(nblk // sub)]
            scs = [score_s[g, :] for g in groups]
            cnts = [rank_s[g, :] for g in groups]
            for i in range(ig * sub, (ig + 1) * sub):
                ri = jnp.broadcast_to(score_s[i:i + 1, :], (sub, tq))
                for r, sc in enumerate(scs):
                    if r * sub > i:
                        beats = ri >= sc
                    elif (r + 1) * sub - 1 <= i:
                        beats = ri > sc
                    else:
                        beats = (ri > sc) | ((ri == sc) & (row_in_group > i - r * sub))
                    cnts[r] = cnts[r] + jnp.where(beats, 1.0, 0.0)
            for g, cnt in zip(groups, cnts):
                rank_s[g, :] = cnt
    sel = jnp.where((rank_s[...] < float(min(SLC_TOPK, nblk))) & (score_s[...] > 0.5 * NEG), 1.0, 0.0)
    pads = [jnp.zeros((HEAD_DIM, tq), F32), sel]
    if nblk < HEAD_DIM:
        pads.append(jnp.zeros((HEAD_DIM - nblk, tq), F32))
    selt_ref[0, 0] = jnp.concatenate(pads, axis=0).T.astype(selt_ref.dtype)


def _cmpsel(nq, cmpkv, ovt, tq):
    bsz, seq, _ = nq.shape
    g = NSA_KV_GROUPS
    ncp = cmpkv.shape[3]
    nblk = ovt.shape[0]
    gw = HEADS_PER_GROUP * HEAD_DIM
    return pl.pallas_call(
        _cmpsel_kernel,
        grid=(bsz, g, seq // tq),
        in_specs=[pl.BlockSpec((1, tq, gw), lambda b, gi, i: (b, i, gi)),
                  pl.BlockSpec((1, 1, 1, ncp, LANES), lambda b, gi, i: (b, 0, gi, 0, 0)),
                  pl.BlockSpec((1, 1, 1, ncp, LANES), lambda b, gi, i: (b, 1, gi, 0, 0)),
                  pl.BlockSpec((nblk, ncp), lambda b, gi, i: (0, 0))],
        out_specs=[pl.BlockSpec((1, tq, gw), lambda b, gi, i: (b, i, gi)),
                   pl.BlockSpec((1, 1, tq, LANES), lambda b, gi, i: (b, gi, i, 0))],
        out_shape=[jax.ShapeDtypeStruct((bsz, seq, NSA_WIDTH), BF16),
                   jax.ShapeDtypeStruct((bsz, g, seq, LANES), BF16)],
        scratch_shapes=[pltpu.VMEM((HEADS_PER_GROUP * tq, LANES), BF16),
                        pltpu.VMEM((nblk, tq), F32), pltpu.VMEM((nblk, tq), F32)],
        compiler_params=_params("parallel", "parallel", "arbitrary"),
        name="cmpsel",
    )(nq, cmpkv, cmpkv, ovt)


def _stack_head_queries(q_ref, qs_s, rows, src=None):
    src = slice(0, rows) if src is None else src
    lo = lax.broadcasted_iota(jnp.int32, (rows, LANES), 1) < HEAD_DIM
    for c in range(HEADS_PER_GROUP // 2):
        q = q_ref[0, src, c * LANES:(c + 1) * LANES]
        zero = jnp.zeros_like(q)
        qs_s[(2 * c) * rows:(2 * c + 1) * rows, :] = jnp.where(lo, q, zero)
        qs_s[(2 * c + 1) * rows:(2 * c + 2) * rows, :] = jnp.where(lo, zero, q)


def _store_heads(o_ref, outs, tq):
    lo = lax.broadcasted_iota(jnp.int32, (tq, LANES), 1) < HEAD_DIM
    for c in range(HEADS_PER_GROUP // 2):
        o_ref[0, :, c * LANES:(c + 1) * LANES] = jnp.where(lo, outs[2 * c], outs[2 * c + 1]).astype(o_ref.dtype)


def _store_heads_normalized(o_ref, accs, tq, rows=slice(None)):
    lo = lax.broadcasted_iota(jnp.int32, (tq, LANES), 1) < HEAD_DIM
    outs = [a * (1.0 / jnp.where(lo, pltpu.roll(a, HEAD_DIM, 1), 1.0)) for a in accs]
    for c in range(HEADS_PER_GROUP // 2):
        o_ref[0, rows, c * LANES:(c + 1) * LANES] = jnp.where(
            lo, outs[2 * c], pltpu.roll(outs[2 * c + 1], HEAD_DIM, 1)).astype(o_ref.dtype)


def _lane_tiles(z):
    return [z[:, c * LANES:(c + 1) * LANES] for c in range(z.shape[1] // LANES)]


def _row_stat(tiles, combine, reduce):
    acc = functools.reduce(combine, tiles)
    return jnp.broadcast_to(reduce(acc, axis=-1, keepdims=True), acc.shape)


WIN_UNROLL = 4


def _win_kernel(q_ref, k_ref, v_ref, o_ref, qs0_s, qs1_s, s0_s, s1_s, p0_s, p1_s, *, tq):
    nh = HEADS_PER_GROUP
    span = WINDOW + tq
    n_qt = q_ref.shape[1] // tq
    qs_b, s_b, p_b = (qs0_s, qs1_s), (s0_s, s1_s), (p0_s, p1_s)

    def q_rows(i):
        return pl.ds(pl.multiple_of(i * tq, tq), tq)

    def k_rows(i):
        return pl.ds(pl.multiple_of(jnp.maximum(i * tq - WINDOW, 0), tq), span)

    def stack_queries(i, slot):
        _stack_head_queries(q_ref, qs_b[slot], tq, src=q_rows(jnp.minimum(i, n_qt - 1)))

    def scores(i, slot):
        i = jnp.minimum(i, n_qt - 1)
        s_b[slot][...] = _dot_nt(qs_b[slot][...], k_ref[0, k_rows(i), :])

    def values(i, slot):
        i = jnp.maximum(i, 0)
        pv = _dot(p_b[slot][...], v_ref[0, k_rows(i), :])
        _store_heads_normalized(o_ref, [pv[h * tq:(h + 1) * tq, :] for h in range(nh)], tq, q_rows(i))

    def softmax(i, slot):
        tok = i * tq + lax.broadcasted_iota(jnp.int32, (tq, span), 0)
        key = jnp.maximum(i * tq - WINDOW, 0) + lax.broadcasted_iota(jnp.int32, (tq, span), 1)
        bias = jnp.where((key <= tok) & (tok - key < WINDOW), 0.0, NEG)
        for h in range(nh):
            rows = slice(h * tq, (h + 1) * tq)
            tiles = _lane_tiles(s_b[slot][rows, :] + bias)
            m = _row_stat(tiles, jnp.maximum, jnp.max)
            p_b[slot][rows, :] = jnp.concatenate([jnp.exp2(t - m) for t in tiles], axis=1).astype(BF16)

    def stages(first, count):
        for t in range(count):
            i, slot = first + t, t % 2
            values(i - 1, 1 - slot)
            scores(i + 1, 1 - slot)
            stack_queries(i + 2, slot)
            softmax(i, slot)

    p1_s[...] = jnp.ones_like(p1_s)
    stack_queries(0, 0)
    stack_queries(1, 1)
    scores(0, 0)

    def body(b, carry):
        stages(b * WIN_UNROLL, WIN_UNROLL)
        return carry

    lax.fori_loop(0, n_qt // WIN_UNROLL, body, 0)
    stages(n_qt - n_qt % WIN_UNROLL, n_qt % WIN_UNROLL)
    values(n_qt - 1, (n_qt - 1) % 2)


SLC_UNROLL = 8
SLC_STATE_BUFFERS = SLC_UNROLL + 1
assert SLC_UNROLL % 2 == 0


def _slc_kernel(q_ref, k_ref, v_ref, selt_ref, o_ref, qs_s, s0_s, s1_s, p0_s, p1_s, a0_s, a1_s,
                m_s, acc_s, *, tq, tk):
    nh = HEADS_PER_GROUP
    seq = q_ref.shape[1]
    n_qt = seq // tq
    s_b, p_b, a_b = (s0_s, s1_s), (p0_s, p1_s), (a0_s, a1_s)
    lo = lax.broadcasted_iota(jnp.int32, (tq, LANES), 1) < HEAD_DIM

    def n_tiles(qi):
        return (qi * tq + tq - 1) // tk + 1

    n_items = sum((qi * tq + tq - 1) // tk + 1 for qi in range(n_qt))

    def advance(qi, j):
        wrap = j + 1 >= n_tiles(qi)
        return jnp.where(wrap, jnp.minimum(qi + 1, n_qt - 1), qi), jnp.where(wrap, 0, j + 1)

    def q_rows(qi):
        return pl.ds(pl.multiple_of(qi * tq, tq), tq)

    def k_rows(j):
        return pl.ds(pl.multiple_of(j * tk, tk), tk)

    def stack_queries(qi):
        par = qi % SLC_STATE_BUFFERS
        unselected = jnp.where(selt_ref[0, 0, q_rows(qi), :].astype(F32) > 0.5, 0.0, NEG)
        for c in range(nh // 2):
            q = q_ref[0, q_rows(qi), c * LANES:(c + 1) * LANES].astype(F32)
            qs_s[par, (2 * c) * tq:(2 * c + 1) * tq, :] = jnp.where(lo, q, unselected).astype(BF16)
            qs_s[par, (2 * c + 1) * tq:(2 * c + 2) * tq, :] = jnp.where(
                lo, pltpu.roll(q, HEAD_DIM, 1), unselected).astype(BF16)

    def scores(qi, j, slot):
        s_b[slot][...] = _dot_nt(qs_s[qi % SLC_STATE_BUFFERS], k_ref[0, k_rows(j), :])

    def values(qi, j, slot):
        pv = _dot(p_b[slot][...], v_ref[0, k_rows(j), :])
        par = qi % SLC_STATE_BUFFERS
        for h in range(nh):
            acc_s[par, h] = a_b[slot][h] * acc_s[par, h] + pv[h * tq:(h + 1) * tq, :]

    def softmax(qi, j, slot):
        tok = qi * tq + lax.broadcasted_iota(jnp.int32, (tq, tk), 0)
        key = j * tk + lax.broadcasted_iota(jnp.int32, (tq, tk), 1)
        bias = jnp.where(key <= tok, 0.0, NEG)
        for h in range(nh):
            rows = slice(h * tq, (h + 1) * tq)
            tiles = _lane_tiles(s_b[slot][rows, :] + bias)
            m_prev = jnp.where(j == 0, NEG, m_s[h])
            m_new = jnp.maximum(m_prev, _row_stat(tiles, jnp.maximum, jnp.max))
            m_s[h] = m_new
            a_b[slot][h] = jnp.exp2(m_prev - m_new)
            p_b[slot][rows, :] = jnp.concatenate([jnp.exp2(t - m_new) for t in tiles], axis=1).astype(BF16)

    def finalize(qi):
        _store_heads_normalized(o_ref, [acc_s[qi % SLC_STATE_BUFFERS, h] for h in range(nh)], tq, q_rows(qi))

    def is_last(qi, j):
        return (j + 1 >= n_tiles(qi)).astype(jnp.int32)

    def finalize_if(flag, qi):
        @pl.when(flag == 1)
        def _():
            finalize(qi)

    def block(carry, n_stages):
        fin, (qp, jp, p_last), (qc, jc), (qn, jn) = carry
        for flag, qi in fin:
            finalize_if(flag, qi)
        items = [(qp, jp, p_last), (qc, jc, is_last(qc, jc)), (qn, jn, is_last(qn, jn))]
        for _ in range(n_stages - 1):
            qx, jx = advance(items[-1][0], items[-1][1])
            items.append((qx, jx, is_last(qx, jx)))
        for qx, jx, _ in items[2:]:
            @pl.when(jx == 0)
            def _():
                stack_queries(qx)
        for t in range(n_stages):
            slot = t % 2
            values(items[t][0], items[t][1], 1 - slot)
            scores(items[t + 2][0], items[t + 2][1], 1 - slot)
            softmax(items[t + 1][0], items[t + 1][1], slot)
        fin = tuple((items[t][2], items[t][0]) for t in range(n_stages))
        nxt = advance(items[-1][0], items[-1][1])
        return fin, items[n_stages], items[n_stages + 1][:2], nxt

    m_s[...] = jnp.full_like(m_s, NEG)
    acc_s[...] = jnp.zeros_like(acc_s)
    p1_s[...] = jnp.zeros_like(p1_s)
    a1_s[...] = jnp.ones_like(a1_s)
    zero = jnp.int32(0)
    stack_queries(zero)
    scores(zero, zero, 0)
    unroll = SLC_UNROLL
    carry = (((zero, zero),) * unroll, (zero, zero, zero), (zero, zero), advance(zero, zero))
    carry = lax.fori_loop(0, n_items // unroll, lambda i, c: block(c, unroll), carry)
    if n_items % unroll:
        carry = block(carry, n_items % unroll)
    fin, (qp, jp, _), _, _ = carry
    for flag, qi in fin:
        finalize_if(flag, qi)
    values(qp, jp, (n_items - 1) % 2)
    finalize(qp)


def _slc_attention(nq, kd, vd, selt, *, tq, tk):
    bsz, seq, _ = nq.shape
    g = NSA_KV_GROUPS
    gw = HEADS_PER_GROUP * HEAD_DIM
    nh = HEADS_PER_GROUP
    assert seq % tk == 0 and tk % tq == 0
    assert seq // SLC_BLOCK <= HEAD_DIM
    whole = lambda w: pl.BlockSpec((1, seq, w), lambda b, gi: (b, 0, gi))
    stat = pltpu.VMEM((nh, tq, LANES), F32)
    state = pltpu.VMEM((SLC_STATE_BUFFERS, nh, tq, LANES), F32)
    return pl.pallas_call(
        functools.partial(_slc_kernel, tq=tq, tk=tk),
        grid=(bsz, g),
        in_specs=[whole(gw), whole(LANES), whole(LANES),
                  pl.BlockSpec((1, 1, seq, LANES), lambda b, gi: (b, gi, 0, 0))],
        out_specs=whole(gw),
        out_shape=jax.ShapeDtypeStruct((bsz, seq, NSA_WIDTH), BF16),
        scratch_shapes=([pltpu.VMEM((SLC_STATE_BUFFERS, nh * tq, LANES), BF16)] + [pltpu.VMEM((nh * tq, tk), F32)] * 2 +
                        [pltpu.VMEM((nh * tq, tk), BF16)] * 2 + [stat] * 3 + [state]),
        compiler_params=_params("parallel", "arbitrary"),
        name="slc_attn",
    )(nq, kd, vd, selt)


def _win_attention(nq, kd, vd, *, tq):
    bsz, seq, _ = nq.shape
    g = NSA_KV_GROUPS
    gw = HEADS_PER_GROUP * HEAD_DIM
    nh = HEADS_PER_GROUP
    assert WINDOW % tq == 0 and seq >= WINDOW + tq and seq // tq >= 2 and WIN_UNROLL % 2 == 0
    whole = lambda w: pl.BlockSpec((1, seq, w), lambda b, gi: (b, 0, gi))
    span = WINDOW + tq
    return pl.pallas_call(
        functools.partial(_win_kernel, tq=tq),
        grid=(bsz, g),
        in_specs=[whole(gw), whole(LANES), whole(LANES)],
        out_specs=whole(gw),
        out_shape=jax.ShapeDtypeStruct((bsz, seq, NSA_WIDTH), BF16),
        scratch_shapes=([pltpu.VMEM((nh * tq, LANES), BF16)] * 2 + [pltpu.VMEM((nh * tq, span), F32)] * 2 +
                        [pltpu.VMEM((nh * tq, span), BF16)] * 2),
        compiler_params=_params("parallel", "arbitrary"),
        name="win_attn",
    )(nq, kd, vd)


def _out_kernel(x_ref, gate_ref, gpost_ref, ret_ref, oc_ref, os_ref, ow_ref, gl_ref, ngs_ref,
                wo_ref, o_ref):
    ts = x_ref.shape[1]
    gates = gl_ref[0]
    lo_off = 32
    assert 3 * NSA_HEADS <= lo_off
    g_hi = gates.astype(BF16).astype(F32)
    g_lo = (gates - g_hi).astype(BF16).astype(F32)
    lane = lax.broadcasted_iota(jnp.int32, (ts, LANES), 1)
    g_split = jnp.where(lane < lo_off, g_hi, pltpu.roll(g_lo, lo_off, 1)).astype(BF16)
    src = lax.broadcasted_iota(jnp.int32, (LANES, NSA_WIDTH), 0)
    head = lax.broadcasted_iota(jnp.int32, (LANES, NSA_WIDTH), 1) >> HEAD_SHIFT
    att = jnp.zeros((ts, NSA_WIDTH), F32)
    for i, br in enumerate((oc_ref, os_ref, ow_ref)):
        expand = jnp.where((src == 3 * head + i) | (src == lo_off + 3 * head + i), 1.0, 0.0).astype(BF16)
        att = att + _dot(g_split, expand) * br[0]
    att = att * ngs_ref[0]
    y = _dot(ret_ref[0].astype(BF16), wo_ref[:RET_WIDTH, :]) + _dot(att.astype(BF16), wo_ref[RET_WIDTH:, :])
    ms = jnp.mean(y * y, axis=-1, keepdims=True)
    y = y * lax.rsqrt(ms + EPS) * gpost_ref[...]
    o_ref[0] = x_ref[0] + gate_ref[0] * y


def _out(x, gate, g_post, ret, o_cmp, o_slc, o_win, gl, ngs, w_out, ts):
    bsz, seq, d = x.shape
    row = lambda w: pl.BlockSpec((1, ts, w), lambda b, s: (b, s, 0))
    return pl.pallas_call(
        _out_kernel,
        grid=(bsz, seq // ts),
        in_specs=[row(d), pl.BlockSpec((1, 1, d), lambda b, s: (b, 0, 0)),
                  pl.BlockSpec((1, d), lambda b, s: (0, 0)),
                  row(RET_WIDTH), row(NSA_WIDTH), row(NSA_WIDTH), row(NSA_WIDTH), row(LANES),
                  row(NSA_WIDTH), pl.BlockSpec(w_out.shape, lambda b, s: (0, 0))],
        out_specs=row(d),
        out_shape=jax.ShapeDtypeStruct((bsz, seq, d), x.dtype),
        compiler_params=_params("parallel", "arbitrary"),
        name="out",
    )(x, gate, g_post, ret, o_cmp, o_slc, o_win, gl, ngs, w_out)


def _retention_tables():
    h = RET_HEADS
    c = RET_CHUNK
    log_g = jnp.log1p(-jnp.power(2.0, -5.0 - jnp.arange(h, dtype=F32)))
    idx = jnp.arange(c, dtype=F32)
    diff = idx[:, None] - idx[None, :]
    dec = jnp.where(diff[None] >= 0, jnp.exp(jnp.maximum(diff, 0.0)[None] * log_g[:, None, None]), 0.0)
    zeta = jnp.exp((c - 1 - idx)[None, :] * log_g[:, None])
    xi = jnp.exp((idx + 1.0)[None, :] * log_g[:, None])
    cd = jnp.exp(c * log_g)
    to_lanes = lambda t: jnp.repeat(t.reshape(h // 2, 2, c).transpose(0, 2, 1), HEAD_DIM, axis=2)
    cd_l = jnp.repeat(cd.reshape(h // 2, 1, 2), HEAD_DIM, axis=2)
    return dec, to_lanes(zeta), to_lanes(xi), cd_l


def _overlap_t(seq):
    ncp = seq // CMP_STRIDE
    nc = (seq - CMP_BLOCK) // CMP_STRIDE + 1
    ns = seq // SLC_BLOCK
    cs = np.arange(ncp) * CMP_STRIDE
    ce = cs + CMP_BLOCK - 1
    ss = np.arange(ns) * SLC_BLOCK
    ov = (cs[None, :] <= ss[:, None] + SLC_BLOCK - 1) & (ce[None, :] >= ss[:, None]) & (np.arange(ncp) < nc)[None, :]
    return jnp.asarray(ov.astype(np.float32), dtype=BF16)


def _layer(x, c, positions, w_ada, b_ada, g_pre, g_post, w_in, w_out,
           pe_k, w1_k, w2_k, pe_v, w1_v, w2_v):
    bsz, seq, d = x.shape
    mod = _ada(c, w_ada, b_ada)
    shift, scale, gate = [t.reshape(bsz, 1, d) for t in jnp.split(mod, 3, axis=-1)]

    offs = np.cumsum((RET_WIDTH,) * 4 + (NSA_WIDTH, 6 * KV_WIDTH, 3 * NSA_HEADS, NSA_WIDTH))
    w_gl = w_in[:, offs[5]:offs[6]]
    w_all = jnp.concatenate(
        [w_in[:, :offs[5]], w_in[:, offs[6]:], w_gl,
         jnp.zeros((d, LANES - w_gl.shape[1]), w_in.dtype)], axis=1).astype(BF16)

    lanes = jnp.arange(LANES)
    half_r = HEAD_DIM // 2
    inv_r = jnp.power(RET_ROPE_BASE, -jnp.arange(half_r, dtype=F32) / half_r)
    half_n = ROPE_DIM // 2
    inv_n = jnp.power(NSA_ROPE_BASE, -jnp.arange(half_n, dtype=F32) / half_n)
    inv_ret_row = inv_r[lanes % half_r].reshape(1, LANES)
    inv_nsa_row = inv_n[lanes % half_n].reshape(1, LANES)
    pos3 = positions.reshape(bsz, seq, 1)

    (rq, rk, rv, rgs, nq, cmps, ksd, vsd, kwd, vwd, ngs, gl) = _proj(
        x, shift, scale, g_pre.reshape(1, d), pos3, inv_ret_row, inv_nsa_row, w_all, ts=512)

    dec, zeta_l, xi_l, cd_l = _retention_tables()
    ret = _retention(rq, rk, rv, rgs, dec, zeta_l, xi_l, cd_l, rows=min(1024, seq))

    ncp = seq // CMP_STRIDE
    nc = (seq - CMP_BLOCK) // CMP_STRIDE + 1
    pad_l = lambda t: jnp.concatenate([t, jnp.zeros_like(t)], axis=-1)
    pe2 = jnp.stack([pad_l(pe_k), pad_l(pe_v)])
    w1r = jnp.stack([w1_k, w1_v]).reshape(2, CMP_BLOCK, HEAD_DIM, CMP_HIDDEN)
    w1p = jnp.concatenate([w1r, jnp.zeros_like(w1r)], axis=2).astype(BF16)
    w2s = jnp.stack([w2_k, w2_v])
    w2d = jnp.concatenate([w2s, w2s], axis=-1).astype(BF16)
    cmp_end = np.arange(nc) * CMP_STRIDE + CMP_BLOCK - 1
    pos_cmp = jnp.pad(positions[:, cmp_end], ((0, 0), (0, ncp - nc))).reshape(bsz, ncp, 1)
    cmpkv = _compress(cmps, pe2, w1p, w2d, pos_cmp, inv_nsa_row)

    o_cmp, selt = _cmpsel(nq, cmpkv, _overlap_t(seq), tq=min(256, seq))
    o_slc = _slc_attention(nq, ksd, vsd, selt, tq=256, tk=512)
    o_win = _win_attention(nq, kwd, vwd, tq=128)

    return _out(x, gate, g_post.reshape(1, d), ret, o_cmp, o_slc, o_win, gl, ngs,
                w_out.astype(BF16), ts=512)


def kernel(x, c, positions, w_ada, b_ada, g_pre, g_post, w_in, w_out, cmp_pe_k, cmp_w1_k, cmp_w2_k, cmp_pe_v, cmp_w1_v, cmp_w2_v):
    for l in range(w_in.shape[0]):
        x = _layer(x, c, positions, w_ada[l], b_ada[l], g_pre[l], g_post[l], w_in[l], w_out[l],
                   cmp_pe_k[l], cmp_w1_k[l], cmp_w2_k[l], cmp_pe_v[l], cmp_w1_v[l], cmp_w2_v[l])
    return x
```

```python
import functools

import numpy as np
import jax
import jax.numpy as jnp
from jax import lax
from jax.experimental import pallas as pl
from jax.experimental.pallas import tpu as pltpu

F32 = jnp.float32
BF16 = jnp.bfloat16

LANES = 128
HEAD_DIM = 64
RET_HEADS = 8
NSA_HEADS = 8
NSA_KV_GROUPS = 2
HEADS_PER_GROUP = NSA_HEADS // NSA_KV_GROUPS
RET_WIDTH = RET_HEADS * HEAD_DIM
NSA_WIDTH = NSA_HEADS * HEAD_DIM
KV_WIDTH = NSA_KV_GROUPS * HEAD_DIM
RET_CHUNK = 128
RET_ROPE_BASE = 10000.0
NSA_ROPE_BASE = 500000.0
ROPE_DIM = HEAD_DIM // 4
CMP_BLOCK = 32
CMP_STRIDE = 16
CMP_HIDDEN = 256
SLC_BLOCK = 64
SLC_TOPK = 16
WINDOW = 512
NEG = -1e30
LOG2E = 1.4426950408889634
BIG = 1e9
EPS = 1e-6
GN_EPS = 1e-5
VMEM_LIMIT = 48 * 1024 * 1024

NT_DIMS = (((1,), (1,)), ((), ()))
TN_DIMS = (((0,), (0,)), ((), ()))


def _dot(a, b):
    return jnp.dot(a, b, preferred_element_type=F32)


def _dot_nt(a, b):
    return lax.dot_general(a, b, NT_DIMS, preferred_element_type=F32)


def _dot_tn(a, b):
    return lax.dot_general(a, b, TN_DIMS, preferred_element_type=F32)


def _silu(z):
    return z * jax.nn.sigmoid(z)


def _params(*sem):
    return pltpu.CompilerParams(dimension_semantics=sem, vmem_limit_bytes=VMEM_LIMIT)


HEAD_SHIFT = 6
SLC_SHIFT = 6
assert 1 << HEAD_SHIFT == HEAD_DIM and 1 << SLC_SHIFT == SLC_BLOCK


def _lane_in_head(shape):
    return lax.broadcasted_iota(jnp.int32, shape, len(shape) - 1) & (HEAD_DIM - 1)


def _ada_kernel(c_ref, w_ref, b_ref, o_ref):
    a = _silu(c_ref[...])
    o_ref[...] = jnp.dot(a, w_ref[0], precision=lax.Precision.HIGHEST,
                         preferred_element_type=F32) + b_ref[0]


def _ada(c, w_layers, b_layers, layer):
    bsz, d = c.shape
    depth, _, n = w_layers.shape
    tn = d
    return pl.pallas_call(
        _ada_kernel,
        grid=(n // tn,),
        in_specs=[pl.BlockSpec((bsz, d), lambda j: (0, 0)),
                  pl.BlockSpec((1, d, tn), lambda j: (layer, 0, j)),
                  pl.BlockSpec((1, 1, tn), lambda j: (layer, 0, j))],
        out_specs=pl.BlockSpec((bsz, tn), lambda j: (0, j)),
        out_shape=jax.ShapeDtypeStruct((bsz, n), F32),
        compiler_params=_params("arbitrary"),
        name="ada",
    )(c, w_layers, b_layers.reshape(depth, 1, n))


_O_RQ, _O_RK, _O_RV, _O_RG, _O_NQ = 0, 512, 1024, 1536, 2048
_O_KV = 2560
_O_NG = 3328
_O_GL = 3840
_PROJ_COLS = 3968


def _rot_ret(z, cos, sin_signed, m):
    partner = jnp.where(m < HEAD_DIM // 2, pltpu.roll(z, LANES - HEAD_DIM // 2, 1),
                        pltpu.roll(z, HEAD_DIM // 2, 1))
    return z * cos + partner * sin_signed


def _rot_nsa(z, cos, sin_signed, m):
    half = ROPE_DIM // 2
    partner = jnp.where(m < half, pltpu.roll(z, LANES - half, 1), pltpu.roll(z, half, 1))
    return z * cos + partner * sin_signed


def _nsa_tables(posf, inv_row, m):
    half = ROPE_DIM // 2
    ang = posf * inv_row
    cos = jnp.where(m < ROPE_DIM, jnp.cos(ang), 1.0)
    s = jnp.sin(ang)
    sin = jnp.where(m < half, -s, jnp.where(m < ROPE_DIM, s, 0.0))
    return cos, sin


def _dup_groups(z, m_lane):
    r = pltpu.roll(z, HEAD_DIM, 1)
    lo = m_lane < HEAD_DIM
    return jnp.where(lo, z, r), jnp.where(lo, r, z)


def _proj_kernel(x_ref, shift_ref, scale_ref, gpre_ref, pos_ref, invr_ref, invn_ref, w_ref,
                 rq_ref, rk_ref, rv_ref, rgs_ref, nq_ref, cmps_ref, ksd_ref, vsd_ref,
                 kwd_ref, vwd_ref, ngs_ref, gl_ref,
                 cr_s, sr_s, cn_s, sn_s, z0_s, z1_s):
    ts = x_ref.shape[1]
    x = x_ref[0]
    ms = jnp.mean(x * x, axis=-1, keepdims=True)
    y = x * lax.rsqrt(ms + EPS) * gpre_ref[...]
    h = y * (1.0 + scale_ref[0]) + shift_ref[0]
    hb = h.astype(BF16)

    posf = pos_ref[0].astype(F32)
    lane = lax.broadcasted_iota(jnp.int32, (ts, LANES), 1)
    m = lane & (HEAD_DIM - 1)
    ang = posf * invr_ref[...]
    cr_s[...] = jnp.cos(ang)
    s = jnp.sin(ang)
    sr_s[...] = jnp.where(m < HEAD_DIM // 2, -s, s)
    cn, sn = _nsa_tables(posf, invn_ref[...], m)
    cn_s[...] = cn
    sn_s[...] = sn

    staging = [z0_s, z1_s]

    def mm(off, width):
        z_s = staging[0]
        staging.reverse()
        z_s[:, :width] = _dot(hb, w_ref[:, off:off + width])
        return lambda c: z_s[:, c * LANES:(c + 1) * LANES]

    scale_q = HEAD_DIM ** -0.5
    tiles = lambda width: [(c, slice(c * LANES, (c + 1) * LANES)) for c in range(width // LANES)]
    z = mm(_O_RQ, RET_WIDTH)
    for c, sl in tiles(RET_WIDTH):
        rq_ref[0, :, sl] = (_rot_ret(z(c), cr_s[...], sr_s[...], m) * scale_q).astype(BF16)
    z = mm(_O_RK, RET_WIDTH)
    for c, sl in tiles(RET_WIDTH):
        rk_ref[0, :, sl] = _rot_ret(z(c), cr_s[...], sr_s[...], m).astype(BF16)
    z = mm(_O_RV, RET_WIDTH)
    for c, sl in tiles(RET_WIDTH):
        rv_ref[0, :, sl] = z(c).astype(BF16)
    z = mm(_O_RG, RET_WIDTH)
    for c, sl in tiles(RET_WIDTH):
        rgs_ref[0, :, sl] = _silu(z(c)).astype(BF16)
    z = mm(_O_NQ, NSA_WIDTH)
    for c, sl in tiles(NSA_WIDTH):
        nq_ref[0, :, sl] = (_rot_nsa(z(c), cn_s[...], sn_s[...], m) * (scale_q * LOG2E)).astype(BF16)

    z = mm(_O_KV, 6 * KV_WIDTH)
    kc0, kc1 = _dup_groups(z(0), lane)
    vc0, vc1 = _dup_groups(z(1), lane)
    for i, t in enumerate((kc0, kc1, vc0, vc1)):
        cmps_ref[0, :, i * LANES:(i + 1) * LANES] = t
    lo = lane < HEAD_DIM
    tok = pl.program_id(1) * ts + lax.broadcasted_iota(jnp.int32, (ts, LANES), 0)
    onehot = jnp.where(lane - HEAD_DIM == (tok >> SLC_SHIFT), 1.0, 0.0)
    ones = jnp.ones((ts, LANES), F32)
    for c, ref, rotate, upper in ((2, ksd_ref, True, onehot), (3, vsd_ref, False, ones),
                                  (4, kwd_ref, True, None), (5, vwd_ref, False, ones)):
        t = _rot_nsa(z(c), cn_s[...], sn_s[...], m) if rotate else z(c)
        if upper is None:
            d0, d1 = _dup_groups(t, lane)
        else:
            d0, d1 = jnp.where(lo, t, upper), jnp.where(lo, pltpu.roll(t, HEAD_DIM, 1), upper)
        ref[0, :, :LANES] = d0.astype(BF16)
        ref[0, :, LANES:] = d1.astype(BF16)

    z = mm(_O_NG, NSA_WIDTH + LANES)
    for c, sl in tiles(NSA_WIDTH):
        ngs_ref[0, :, sl] = _silu(z(c)).astype(BF16)
    gl_ref[0] = jax.nn.sigmoid(z(NSA_WIDTH // LANES))


def _proj(x, shift, scale, g_pre, pos3, inv_ret_row, inv_nsa_row, w_all, ts):
    bsz, seq, d = x.shape
    row = lambda w: pl.BlockSpec((1, ts, w), lambda b, s: (b, s, 0))
    vec = pl.BlockSpec((1, 1, d), lambda b, s: (b, 0, 0))
    const2 = lambda shp: pl.BlockSpec(shp, lambda b, s: (0, 0))
    outs = [(RET_WIDTH, BF16), (RET_WIDTH, BF16), (RET_WIDTH, BF16), (RET_WIDTH, BF16),
            (NSA_WIDTH, BF16), (4 * LANES, F32), (2 * LANES, BF16), (2 * LANES, BF16),
            (2 * LANES, BF16), (2 * LANES, BF16), (NSA_WIDTH, BF16), (LANES, F32)]
    return pl.pallas_call(
        _proj_kernel,
        grid=(bsz, seq // ts),
        in_specs=[row(d), vec, vec, const2((1, d)), row(1), const2((1, LANES)), const2((1, LANES)),
                  const2((d, _PROJ_COLS))],
        out_specs=[row(w) for w, _ in outs],
        out_shape=[jax.ShapeDtypeStruct((bsz, seq, w), dt) for w, dt in outs],
        scratch_shapes=[pltpu.VMEM((ts, LANES), F32)] * 4 + [pltpu.VMEM((ts, 6 * KV_WIDTH), F32)] * 2,
        compiler_params=_params("parallel", "arbitrary"),
        name="proj",
    )(x, shift, scale, g_pre, pos3, inv_ret_row, inv_nsa_row, w_all)


def _ret_kernel(q_ref, k_ref, v_ref, g_ref, dec_ref, zeta_ref, xi_ref, cd_ref, o_ref, r_s):
    rows = q_ref.shape[1]
    cc = RET_CHUNK

    @pl.when(pl.program_id(2) == 0)
    def _():
        r_s[...] = jnp.zeros_like(r_s)

    lane = lax.broadcasted_iota(jnp.int32, (cc, LANES), 1)
    rowi = lax.broadcasted_iota(jnp.int32, (cc, LANES), 0)
    lo = lane < HEAD_DIM
    blockdiag = (rowi < HEAD_DIM) == lo
    inv_n = 1.0 / HEAD_DIM
    chunks = [slice(c * cc, (c + 1) * cc) for c in range(rows // cc)]
    s_ab, kvs = [], []
    for sl in chunks:
        q, k, v = q_ref[0, sl, :], k_ref[0, sl, :], v_ref[0, sl, :]
        zero = jnp.zeros_like(q)
        s_ab.append((_dot_nt(jnp.where(lo, q, zero), k), _dot_nt(jnp.where(lo, zero, q), k)))
        kz = (k.astype(F32) * zeta_ref[0]).astype(BF16)
        kvs.append(jnp.where(blockdiag, _dot_tn(kz, v), 0.0))
    inners = []
    for sl, (s_a, s_b) in zip(chunks, s_ab):
        v = v_ref[0, sl, :]
        inners.append(jnp.where(lo, _dot((s_a * dec_ref[0]).astype(BF16), v),
                                _dot((s_b * dec_ref[1]).astype(BF16), v)))
    r = r_s[...]
    outs = []
    for sl, inner, kv in zip(chunks, inners, kvs):
        outs.append(inner + _dot(q_ref[0, sl, :], r.astype(BF16)) * xi_ref[0])
        r = r * cd_ref[0] + kv
    r_s[...] = r
    head_ones = jnp.where(blockdiag, 1.0, 0.0).astype(BF16)

    def head_sums(z):
        hi = z.astype(BF16)
        lo_part = (z - hi.astype(F32)).astype(BF16)
        return _dot(hi, head_ones) + _dot(lo_part, head_ones)

    devs = [o - s * inv_n for o, s in zip(outs, [head_sums(o) for o in outs])]
    variances = [head_sums(d * d) * inv_n for d in devs]
    for sl, d, var in zip(chunks, devs, variances):
        o_ref[0, sl, :] = (d * lax.rsqrt(var + GN_EPS) * g_ref[0, sl, :]).astype(o_ref.dtype)


def _retention(rq, rk, rv, rgs, dec, zeta_l, xi_l, cd_l, rows):
    bsz, seq, _ = rq.shape
    npair = RET_HEADS // 2
    blk = pl.BlockSpec((1, rows, LANES), lambda b, p, s: (b, s, p))
    return pl.pallas_call(
        _ret_kernel,
        grid=(bsz, npair, seq // rows),
        in_specs=[blk, blk, blk, blk,
                  pl.BlockSpec((2, RET_CHUNK, RET_CHUNK), lambda b, p, s: (p, 0, 0)),
                  pl.BlockSpec((1, RET_CHUNK, LANES), lambda b, p, s: (p, 0, 0)),
                  pl.BlockSpec((1, RET_CHUNK, LANES), lambda b, p, s: (p, 0, 0)),
                  pl.BlockSpec((1, 1, LANES), lambda b, p, s: (p, 0, 0))],
        out_specs=blk,
        out_shape=jax.ShapeDtypeStruct((bsz, seq, RET_WIDTH), BF16),
        scratch_shapes=[pltpu.VMEM((LANES, LANES), F32)],
        compiler_params=_params("parallel", "parallel", "arbitrary"),
        name="retention",
    )(rq, rk, rv, rgs, dec, zeta_l, xi_l, cd_l)


def _cmp_kernel(src_ref, pe_ref, w1_ref, w2_ref, pos_ref, invn_ref, o_ref):
    ncp = o_ref.shape[3]
    pairs = CMP_BLOCK // 4
    lo = lax.broadcasted_iota(jnp.int32, (ncp, LANES), 1) < HEAD_DIM
    p_lo = jnp.zeros((ncp, CMP_HIDDEN), F32)
    p_hi = jnp.zeros((ncp, CMP_HIDDEN), F32)
    for m in range(pairs):
        t = jnp.where(lo, src_ref[0, pl.ds(2 * m, ncp, stride=CMP_STRIDE), :],
                      src_ref[0, pl.ds(2 * m + 1, ncp, stride=CMP_STRIDE), :])
        p_lo = p_lo + _dot((t + pe_ref[0, m:m + 1, :]).astype(BF16), w1_ref[0, m])
        p_hi = p_hi + _dot((t + pe_ref[0, pairs + m:pairs + m + 1, :]).astype(BF16), w1_ref[0, pairs + m])
    hidden = p_lo + pltpu.roll(p_hi, ncp - 1, 0)
    comp = _dot(_silu(hidden).astype(BF16), w2_ref[0])
    m = _lane_in_head(comp.shape)
    cos, sin = _nsa_tables(pos_ref[0].astype(F32), invn_ref[...], m)
    rotated = _rot_nsa(comp, cos, sin, m)
    is_key = pl.program_id(1) == 0
    o_ref[0, 0, 0] = jnp.where(is_key, rotated, comp).astype(BF16)


def _compress(cmps, pe2, w1p, w2d, pos_cmp, inv_nsa_row):
    bsz, seq, _ = cmps.shape
    ncp = seq // CMP_STRIDE
    g = NSA_KV_GROUPS
    return pl.pallas_call(
        _cmp_kernel,
        grid=(bsz, 2, g),
        in_specs=[pl.BlockSpec((1, seq, LANES), lambda b, j, gi: (b, 0, j * g + gi)),
                  pl.BlockSpec((1, CMP_BLOCK // 2, LANES), lambda b, j, gi: (j, 0, 0)),
                  pl.BlockSpec((1, CMP_BLOCK // 2, LANES, CMP_HIDDEN), lambda b, j, gi: (j, 0, 0, 0)),
                  pl.BlockSpec((1, CMP_HIDDEN, LANES), lambda b, j, gi: (j, 0, 0)),
                  pl.BlockSpec((1, ncp, 1), lambda b, j, gi: (b, 0, 0)),
                  pl.BlockSpec((1, LANES), lambda b, j, gi: (0, 0))],
        out_specs=pl.BlockSpec((1, 1, 1, ncp, LANES), lambda b, j, gi: (b, j, gi, 0, 0)),
        out_shape=jax.ShapeDtypeStruct((bsz, 2, g, ncp, LANES), BF16),
        compiler_params=_params("parallel", "arbitrary", "arbitrary"),
        name="compress",
    )(cmps, pe2, w1p, w2d, pos_cmp, inv_nsa_row)


def _cmpsel_kernel(q_ref, kc_ref, vc_ref, ovt_ref, o_ref, selt_ref, qs_s, score_s, rank_s):
    nh = HEADS_PER_GROUP
    sub = 8
    tq = q_ref.shape[1]
    ncp = kc_ref.shape[3]
    nblk = ovt_ref.shape[0]
    t0 = pl.program_id(2) * tq
    kc = kc_ref[0, 0, 0]
    vc = vc_ref[0, 0, 0]

    tok = t0 + lax.broadcasted_iota(jnp.int32, (tq, LANES), 0)
    lane = lax.broadcasted_iota(jnp.int32, (tq, LANES), 1)
    biases = [jnp.where((lane + c * LANES) * CMP_STRIDE + (CMP_BLOCK - 1) <= tok, 0.0, NEG)
              for c in range(ncp // LANES)]
    has_key = tok >= CMP_BLOCK - 1
    _stack_head_queries(q_ref, qs_s, tq)
    s_all = _dot_nt(qs_s[...], kc)
    psum, probs = None, []
    for h in range(nh):
        tiles = [t + b for b, t in zip(biases, _lane_tiles(s_all[h * tq:(h + 1) * tq, :]))]
        mx = _row_stat(tiles, jnp.maximum, jnp.max)
        es = [jnp.exp2(t - mx) for t in tiles]
        inv = jnp.where(has_key, 1.0 / _row_stat(es, jnp.add, jnp.sum), 0.0)
        ps = [e * inv for e in es]
        psum = ps if psum is None else [a + b for a, b in zip(psum, ps)]
        probs.append(jnp.concatenate(ps, axis=1).astype(BF16))
    pv = _dot(jnp.concatenate(probs, axis=0), vc)
    _store_heads(o_ref, [pv[h * tq:(h + 1) * tq, :] for h in range(nh)], tq)

    psum = jnp.concatenate(psum, axis=1)
    p_hi = psum.astype(BF16)
    p_lo = (psum - p_hi.astype(F32)).astype(BF16)
    imp = _dot_nt(ovt_ref[...], p_hi) + _dot_nt(ovt_ref[...], p_lo)
    blk = lax.broadcasted_iota(jnp.int32, (nblk, tq), 0)
    cur = (t0 + lax.broadcasted_iota(jnp.int32, (nblk, tq), 1)) >> SLC_SHIFT
    forced = (blk == 0) | (blk == cur) | (blk == cur - 1)
    score_s[...] = jnp.where(forced, BIG, jnp.where(blk <= cur, imp, NEG))

    rank_s[...] = jnp.zeros_like(rank_s)
    last_blk = (t0 + tq - 1) >> SLC_SHIFT
    row_in_group = lax.broadcasted_iota(jnp.int32, (sub, tq), 0)
    for ig in range(nblk // sub):
        @pl.when(ig * sub <= last_blk)
        def _():
            groups = [slice(r * sub, (r + 1) * sub) for r in range(nblk // sub)]
            scs = [score_s[g, :] for g in groups]
            cnts = [rank_s[g, :] for g in groups]
            for i in range(ig * sub, (ig + 1) * sub):
                ri = jnp.broadcast_to(score_s[i:i + 1, :], (sub, tq))
                for r, sc in enumerate(scs):
                    if r * sub > i:
                        beats = ri >= sc
                    elif (r + 1) * sub - 1 <= i:
                        beats = ri > sc
                    else:
                        beats = (ri > sc) | ((ri == sc) & (row_in_group > i - r * sub))
                    cnts[r] = cnts[r] + jnp.where(beats, 1.0, 0.0)
            for g, cnt in zip(groups, cnts):
                rank_s[g, :] = cnt
    sel = jnp.where((rank_s[...] < float(min(SLC_TOPK, nblk))) & (score_s[...] > 0.5 * NEG), 1.0, 0.0)
    pads = [jnp.zeros((HEAD_DIM, tq), F32), sel]
    if nblk < HEAD_DIM:
        pads.append(jnp.zeros((HEAD_DIM - nblk, tq), F32))
    selt_ref[0, 0] = jnp.concatenate(pads, axis=0).T.astype(selt_ref.dtype)


def _cmpsel(nq, cmpkv, ovt, tq):
    bsz, seq, _ = nq.shape
    g = NSA_KV_GROUPS
    ncp = cmpkv.shape[3]
    nblk = ovt.shape[0]
    gw = HEADS_PER_GROUP * HEAD_DIM
    return pl.pallas_call(
        _cmpsel_kernel,
        grid=(bsz, g, seq // tq),
        in_specs=[pl.BlockSpec((1, tq, gw), lambda b, gi, i: (b, i, gi)),
                  pl.BlockSpec((1, 1, 1, ncp, LANES), lambda b, gi, i: (b, 0, gi, 0, 0)),
                  pl.BlockSpec((1, 1, 1, ncp, LANES), lambda b, gi, i: (b, 1, gi, 0, 0)),
                  pl.BlockSpec((nblk, ncp), lambda b, gi, i: (0, 0))],
        out_specs=[pl.BlockSpec((1, tq, gw), lambda b, gi, i: (b, i, gi)),
                   pl.BlockSpec((1, 1, tq, LANES), lambda b, gi, i: (b, gi, i, 0))],
        out_shape=[jax.ShapeDtypeStruct((bsz, seq, NSA_WIDTH), BF16),
                   jax.ShapeDtypeStruct((bsz, g, seq, LANES), BF16)],
        scratch_shapes=[pltpu.VMEM((HEADS_PER_GROUP * tq, LANES), BF16),
                        pltpu.VMEM((nblk, tq), F32), pltpu.VMEM((nblk, tq), F32)],
        compiler_params=_params("parallel", "parallel", "arbitrary"),
        name="cmpsel",
    )(nq, cmpkv, cmpkv, ovt)


def _stack_head_queries(q_ref, qs_s, rows, src=None):
    src = slice(0, rows) if src is None else src
    lo = lax.broadcasted_iota(jnp.int32, (rows, LANES), 1) < HEAD_DIM
    for c in range(HEADS_PER_GROUP // 2):
        q = q_ref[0, src, c * LANES:(c + 1) * LANES]
        zero = jnp.zeros_like(q)
        qs_s[(2 * c) * rows:(2 * c + 1) * rows, :] = jnp.where(lo, q, zero)
        qs_s[(2 * c + 1) * rows:(2 * c + 2) * rows, :] = jnp.where(lo, zero, q)


def _store_heads(o_ref, outs, tq):
    lo = lax.broadcasted_iota(jnp.int32, (tq, LANES), 1) < HEAD_DIM
    for c in range(HEADS_PER_GROUP // 2):
        o_ref[0, :, c * LANES:(c + 1) * LANES] = jnp.where(lo, outs[2 * c], outs[2 * c + 1]).astype(o_ref.dtype)


def _store_heads_normalized(o_ref, accs, tq, rows=slice(None)):
    lo = lax.broadcasted_iota(jnp.int32, (tq, LANES), 1) < HEAD_DIM
    for c in range(HEADS_PER_GROUP // 2):
        a, b = accs[2 * c], accs[2 * c + 1]
        num = jnp.where(lo, a, pltpu.roll(b, HEAD_DIM, 1))
        den = jnp.where(lo, pltpu.roll(a, HEAD_DIM, 1), b)
        o_ref[0, rows, c * LANES:(c + 1) * LANES] = (num * (1.0 / den)).astype(o_ref.dtype)


def _lane_tiles(z):
    return [z[:, c * LANES:(c + 1) * LANES] for c in range(z.shape[1] // LANES)]


def _row_stat(tiles, combine, reduce):
    acc = functools.reduce(combine, tiles)
    return jnp.broadcast_to(reduce(acc, axis=-1, keepdims=True), acc.shape)


WIN_UNROLL = 8


def _win_kernel(q_ref, k_ref, v_ref, o_ref, qs0_s, qs1_s, s0_s, s1_s, p0_s, p1_s, *, tq):
    nh = HEADS_PER_GROUP
    span = WINDOW + tq
    n_qt = q_ref.shape[1] // tq
    qs_b, s_b, p_b = (qs0_s, qs1_s), (s0_s, s1_s), (p0_s, p1_s)

    def q_rows(i):
        return pl.ds(pl.multiple_of(i * tq, tq), tq)

    def k_rows(i):
        return pl.ds(pl.multiple_of(jnp.maximum(i * tq - WINDOW, 0), tq), span)

    def stack_queries(i, slot):
        _stack_head_queries(q_ref, qs_b[slot], tq, src=q_rows(jnp.minimum(i, n_qt - 1)))

    def scores(i, slot):
        i = jnp.minimum(i, n_qt - 1)
        s_b[slot][...] = _dot_nt(qs_b[slot][...], k_ref[0, k_rows(i), :])

    def values(i, slot):
        i = jnp.maximum(i, 0)
        pv = _dot(p_b[slot][...], v_ref[0, k_rows(i), :])
        _store_heads_normalized(o_ref, [pv[h * tq:(h + 1) * tq, :] for h in range(nh)], tq, q_rows(i))

    def softmax(i, slot):
        tok = i * tq + lax.broadcasted_iota(jnp.int32, (tq, span), 0)
        key = jnp.maximum(i * tq - WINDOW, 0) + lax.broadcasted_iota(jnp.int32, (tq, span), 1)
        bias = jnp.where((key <= tok) & (tok - key < WINDOW), 0.0, NEG)
        for h in range(nh):
            rows = slice(h * tq, (h + 1) * tq)
            tiles = _lane_tiles(s_b[slot][rows, :] + bias)
            m = _row_stat(tiles, jnp.maximum, jnp.max)
            p_b[slot][rows, :] = jnp.concatenate([jnp.exp2(t - m) for t in tiles], axis=1).astype(BF16)

    def stages(first, count):
        for t in range(count):
            i, slot = first + t, t % 2
            values(i - 1, 1 - slot)
            scores(i + 1, 1 - slot)
            stack_queries(i + 2, slot)
            softmax(i, slot)

    p1_s[...] = jnp.ones_like(p1_s)
    stack_queries(0, 0)
    stack_queries(1, 1)
    scores(0, 0)

    def body(b, carry):
        stages(b * WIN_UNROLL, WIN_UNROLL)
        return carry

    lax.fori_loop(0, n_qt // WIN_UNROLL, body, 0)
    stages(n_qt - n_qt % WIN_UNROLL, n_qt % WIN_UNROLL)
    values(n_qt - 1, (n_qt - 1) % 2)


SLC_UNROLL = 8
SLC_STATE_BUFFERS = SLC_UNROLL + 1
assert SLC_UNROLL % 2 == 0


def _slc_kernel(q_ref, k_ref, v_ref, selt_ref, o_ref, qs_s, s0_s, s1_s, p0_s, p1_s, a0_s, a1_s,
                m_s, acc_s, *, tq, tk):
    nh = HEADS_PER_GROUP
    seq = q_ref.shape[1]
    n_qt = seq // tq
    s_b, p_b, a_b = (s0_s, s1_s), (p0_s, p1_s), (a0_s, a1_s)
    lo = lax.broadcasted_iota(jnp.int32, (tq, LANES), 1) < HEAD_DIM

    def n_tiles(qi):
        return (qi * tq + tq - 1) // tk + 1

    n_items = sum((qi * tq + tq - 1) // tk + 1 for qi in range(n_qt))

    def advance(qi, j):
        wrap = j + 1 >= n_tiles(qi)
        return jnp.where(wrap, jnp.minimum(qi + 1, n_qt - 1), qi), jnp.where(wrap, 0, j + 1)

    def q_rows(qi):
        return pl.ds(pl.multiple_of(qi * tq, tq), tq)

    def k_rows(j):
        return pl.ds(pl.multiple_of(j * tk, tk), tk)

    def stack_queries(qi):
        par = qi % SLC_STATE_BUFFERS
        unselected = jnp.where(selt_ref[0, 0, q_rows(qi), :].astype(F32) > 0.5, 0.0, NEG)
        for c in range(nh // 2):
            q = q_ref[0, q_rows(qi), c * LANES:(c + 1) * LANES].astype(F32)
            qs_s[par, (2 * c) * tq:(2 * c + 1) * tq, :] = jnp.where(lo, q, unselected).astype(BF16)
            qs_s[par, (2 * c + 1) * tq:(2 * c + 2) * tq, :] = jnp.where(
                lo, pltpu.roll(q, HEAD_DIM, 1), unselected).astype(BF16)

    def scores(qi, j, slot):
        s_b[slot][...] = _dot_nt(qs_s[qi % SLC_STATE_BUFFERS], k_ref[0, k_rows(j), :])

    def values(qi, j, slot):
        pv = _dot(p_b[slot][...], v_ref[0, k_rows(j), :])
        par = qi % SLC_STATE_BUFFERS
        for h in range(nh):
            acc_s[par, h] = a_b[slot][h] * acc_s[par, h] + pv[h * tq:(h + 1) * tq, :]

    def softmax(qi, j, slot):
        tok = qi * tq + lax.broadcasted_iota(jnp.int32, (tq, tk), 0)
        key = j * tk + lax.broadcasted_iota(jnp.int32, (tq, tk), 1)
        bias = jnp.where(key <= tok, 0.0, NEG)
        for h in range(nh):
            rows = slice(h * tq, (h + 1) * tq)
            tiles = _lane_tiles(s_b[slot][rows, :] + bias)
            m_prev = jnp.where(j == 0, NEG, m_s[h])
            m_new = jnp.maximum(m_prev, _row_stat(tiles, jnp.maximum, jnp.max))
            m_s[h] = m_new
            a_b[slot][h] = jnp.exp2(m_prev - m_new)
            p_b[slot][rows, :] = jnp.concatenate([jnp.exp2(t - m_new) for t in tiles], axis=1).astype(BF16)

    def finalize(qi):
        _store_heads_normalized(o_ref, [acc_s[qi % SLC_STATE_BUFFERS, h] for h in range(nh)], tq, q_rows(qi))

    def is_last(qi, j):
        return (j + 1 >= n_tiles(qi)).astype(jnp.int32)

    def finalize_if(flag, qi):
        @pl.when(flag == 1)
        def _():
            finalize(qi)

    def block(carry, n_stages):
        fin, (qp, jp, p_last), (qc, jc), (qn, jn) = carry
        for flag, qi in fin:
            finalize_if(flag, qi)
        items = [(qp, jp, p_last), (qc, jc, is_last(qc, jc)), (qn, jn, is_last(qn, jn))]
        for _ in range(n_stages - 1):
            qx, jx = advance(items[-1][0], items[-1][1])
            items.append((qx, jx, is_last(qx, jx)))
        for qx, jx, _ in items[2:]:
            @pl.when(jx == 0)
            def _():
                stack_queries(qx)
        for t in range(n_stages):
            slot = t % 2
            values(items[t][0], items[t][1], 1 - slot)
            scores(items[t + 2][0], items[t + 2][1], 1 - slot)
            softmax(items[t + 1][0], items[t + 1][1], slot)
        fin = tuple((items[t][2], items[t][0]) for t in range(n_stages))
        nxt = advance(items[-1][0], items[-1][1])
        return fin, items[n_stages], items[n_stages + 1][:2], nxt

    m_s[...] = jnp.full_like(m_s, NEG)
    acc_s[...] = jnp.zeros_like(acc_s)
    p1_s[...] = jnp.zeros_like(p1_s)
    a1_s[...] = jnp.ones_like(a1_s)
    zero = jnp.int32(0)
    stack_queries(zero)
    scores(zero, zero, 0)
    unroll = SLC_UNROLL
    carry = (((zero, zero),) * unroll, (zero, zero, zero), (zero, zero), advance(zero, zero))
    carry = lax.fori_loop(0, n_items // unroll, lambda i, c: block(c, unroll), carry)
    if n_items % unroll:
        carry = block(carry, n_items % unroll)
    fin, (qp, jp, _), _, _ = carry
    for flag, qi in fin:
        finalize_if(flag, qi)
    values(qp, jp, (n_items - 1) % 2)
    finalize(qp)


def _slc_attention(nq, kd, vd, selt, *, tq, tk):
    bsz, seq, _ = nq.shape
    g = NSA_KV_GROUPS
    gw = HEADS_PER_GROUP * HEAD_DIM
    nh = HEADS_PER_GROUP
    assert seq % tk == 0 and tk % tq == 0
    assert seq // SLC_BLOCK <= HEAD_DIM
    whole = lambda w: pl.BlockSpec((1, seq, w), lambda b, gi: (b, 0, gi))
    stat = pltpu.VMEM((nh, tq, LANES), F32)
    state = pltpu.VMEM((SLC_STATE_BUFFERS, nh, tq, LANES), F32)
    return pl.pallas_call(
        functools.partial(_slc_kernel, tq=tq, tk=tk),
        grid=(bsz, g),
        in_specs=[whole(gw), whole(LANES), whole(LANES),
                  pl.BlockSpec((1, 1, seq, LANES), lambda b, gi: (b, gi, 0, 0))],
        out_specs=whole(gw),
        out_shape=jax.ShapeDtypeStruct((bsz, seq, NSA_WIDTH), BF16),
        scratch_shapes=([pltpu.VMEM((SLC_STATE_BUFFERS, nh * tq, LANES), BF16)] + [pltpu.VMEM((nh * tq, tk), F32)] * 2 +
                        [pltpu.VMEM((nh * tq, tk), BF16)] * 2 + [stat] * 3 + [state]),
        compiler_params=_params("parallel", "arbitrary"),
        name="slc_attn",
    )(nq, kd, vd, selt)


def _win_attention(nq, kd, vd, *, tq):
    bsz, seq, _ = nq.shape
    g = NSA_KV_GROUPS
    gw = HEADS_PER_GROUP * HEAD_DIM
    nh = HEADS_PER_GROUP
    assert WINDOW % tq == 0 and seq >= WINDOW + tq and seq // tq >= 2 and WIN_UNROLL % 2 == 0
    whole = lambda w: pl.BlockSpec((1, seq, w), lambda b, gi: (b, 0, gi))
    span = WINDOW + tq
    return pl.pallas_call(
        functools.partial(_win_kernel, tq=tq),
        grid=(bsz, g),
        in_specs=[whole(gw), whole(LANES), whole(LANES)],
        out_specs=whole(gw),
        out_shape=jax.ShapeDtypeStruct((bsz, seq, NSA_WIDTH), BF16),
        scratch_shapes=([pltpu.VMEM((nh * tq, LANES), BF16)] * 2 + [pltpu.VMEM((nh * tq, span), F32)] * 2 +
                        [pltpu.VMEM((nh * tq, span), BF16)] * 2),
        compiler_params=_params("parallel", "arbitrary"),
        name="win_attn",
    )(nq, kd, vd)


def _out_kernel(x_ref, gate_ref, gpost_ref, ret_ref, oc_ref, os_ref, ow_ref, gl_ref, ngs_ref,
                wo_ref, o_ref):
    ts = x_ref.shape[1]
    gates = gl_ref[0]
    lo_off = 32
    assert 3 * NSA_HEADS <= lo_off
    g_hi = gates.astype(BF16).astype(F32)
    g_lo = (gates - g_hi).astype(BF16).astype(F32)
    lane = lax.broadcasted_iota(jnp.int32, (ts, LANES), 1)
    g_split = jnp.where(lane < lo_off, g_hi, pltpu.roll(g_lo, lo_off, 1)).astype(BF16)
    src = lax.broadcasted_iota(jnp.int32, (LANES, NSA_WIDTH), 0)
    head = lax.broadcasted_iota(jnp.int32, (LANES, NSA_WIDTH), 1) >> HEAD_SHIFT
    att = jnp.zeros((ts, NSA_WIDTH), F32)
    for i, br in enumerate((oc_ref, os_ref, ow_ref)):
        expand = jnp.where((src == 3 * head + i) | (src == lo_off + 3 * head + i), 1.0, 0.0).astype(BF16)
        att = att + _dot(g_split, expand) * br[0]
    att = att * ngs_ref[0]
    y = _dot(ret_ref[0].astype(BF16), wo_ref[:RET_WIDTH, :]) + _dot(att.astype(BF16), wo_ref[RET_WIDTH:, :])
    ms = jnp.mean(y * y, axis=-1, keepdims=True)
    y = y * lax.rsqrt(ms + EPS) * gpost_ref[...]
    o_ref[0] = x_ref[0] + gate_ref[0] * y


def _out(x, gate, g_post, ret, o_cmp, o_slc, o_win, gl, ngs, w_out, ts):
    bsz, seq, d = x.shape
    row = lambda w: pl.BlockSpec((1, ts, w), lambda b, s: (b, s, 0))
    return pl.pallas_call(
        _out_kernel,
        grid=(bsz, seq // ts),
        in_specs=[row(d), pl.BlockSpec((1, 1, d), lambda b, s: (b, 0, 0)),
                  pl.BlockSpec((1, d), lambda b, s: (0, 0)),
                  row(RET_WIDTH), row(NSA_WIDTH), row(NSA_WIDTH), row(NSA_WIDTH), row(LANES),
                  row(NSA_WIDTH), pl.BlockSpec(w_out.shape, lambda b, s: (0, 0))],
        out_specs=row(d),
        out_shape=jax.ShapeDtypeStruct((bsz, seq, d), x.dtype),
        compiler_params=_params("parallel", "arbitrary"),
        name="out",
    )(x, gate, g_post, ret, o_cmp, o_slc, o_win, gl, ngs, w_out)


def _retention_tables():
    h = RET_HEADS
    c = RET_CHUNK
    log_g = jnp.log1p(-jnp.power(2.0, -5.0 - jnp.arange(h, dtype=F32)))
    idx = jnp.arange(c, dtype=F32)
    diff = idx[:, None] - idx[None, :]
    dec = jnp.where(diff[None] >= 0, jnp.exp(jnp.maximum(diff, 0.0)[None] * log_g[:, None, None]), 0.0)
    zeta = jnp.exp((c - 1 - idx)[None, :] * log_g[:, None])
    xi = jnp.exp((idx + 1.0)[None, :] * log_g[:, None])
    cd = jnp.exp(c * log_g)
    to_lanes = lambda t: jnp.repeat(t.reshape(h // 2, 2, c).transpose(0, 2, 1), HEAD_DIM, axis=2)
    cd_l = jnp.repeat(cd.reshape(h // 2, 1, 2), HEAD_DIM, axis=2)
    return dec, to_lanes(zeta), to_lanes(xi), cd_l


def _overlap_t(seq):
    ncp = seq // CMP_STRIDE
    nc = (seq - CMP_BLOCK) // CMP_STRIDE + 1
    ns = seq // SLC_BLOCK
    cs = np.arange(ncp) * CMP_STRIDE
    ce = cs + CMP_BLOCK - 1
    ss = np.arange(ns) * SLC_BLOCK
    ov = (cs[None, :] <= ss[:, None] + SLC_BLOCK - 1) & (ce[None, :] >= ss[:, None]) & (np.arange(ncp) < nc)[None, :]
    return jnp.asarray(ov.astype(np.float32), dtype=BF16)


def _layer(x, c, positions, mod, g_pre, g_post, w_in, w_out,
           pe_k, w1_k, w2_k, pe_v, w1_v, w2_v):
    bsz, seq, d = x.shape
    shift, scale, gate = [t.reshape(bsz, 1, d) for t in jnp.split(mod, 3, axis=-1)]

    offs = np.cumsum((RET_WIDTH,) * 4 + (NSA_WIDTH, 6 * KV_WIDTH, 3 * NSA_HEADS, NSA_WIDTH))
    w_gl = w_in[:, offs[5]:offs[6]]
    w_all = jnp.concatenate(
        [w_in[:, :offs[5]], w_in[:, offs[6]:], w_gl,
         jnp.zeros((d, LANES - w_gl.shape[1]), w_in.dtype)], axis=1).astype(BF16)

    lanes = jnp.arange(LANES)
    half_r = HEAD_DIM // 2
    inv_r = jnp.power(RET_ROPE_BASE, -jnp.arange(half_r, dtype=F32) / half_r)
    half_n = ROPE_DIM // 2
    inv_n = jnp.power(NSA_ROPE_BASE, -jnp.arange(half_n, dtype=F32) / half_n)
    inv_ret_row = inv_r[lanes % half_r].reshape(1, LANES)
    inv_nsa_row = inv_n[lanes % half_n].reshape(1, LANES)
    pos3 = positions.reshape(bsz, seq, 1)

    (rq, rk, rv, rgs, nq, cmps, ksd, vsd, kwd, vwd, ngs, gl) = _proj(
        x, shift, scale, g_pre.reshape(1, d), pos3, inv_ret_row, inv_nsa_row, w_all, ts=512)

    dec, zeta_l, xi_l, cd_l = _retention_tables()
    ret = _retention(rq, rk, rv, rgs, dec, zeta_l, xi_l, cd_l, rows=min(1024, seq))

    ncp = seq // CMP_STRIDE
    nc = (seq - CMP_BLOCK) // CMP_STRIDE + 1
    pe2 = jnp.stack([pe_k, pe_v]).reshape(2, CMP_BLOCK // 2, LANES)
    w1p = jnp.stack([w1_k, w1_v]).reshape(2, CMP_BLOCK // 2, LANES, CMP_HIDDEN).astype(BF16)
    w2s = jnp.stack([w2_k, w2_v])
    w2d = jnp.concatenate([w2s, w2s], axis=-1).astype(BF16)
    cmp_end = np.arange(nc) * CMP_STRIDE + CMP_BLOCK - 1
    pos_cmp = jnp.pad(positions[:, cmp_end], ((0, 0), (0, ncp - nc))).reshape(bsz, ncp, 1)
    cmpkv = _compress(cmps, pe2, w1p, w2d, pos_cmp, inv_nsa_row)

    o_cmp, selt = _cmpsel(nq, cmpkv, _overlap_t(seq), tq=min(256, seq))
    o_slc = _slc_attention(nq, ksd, vsd, selt, tq=256, tk=512)
    o_win = _win_attention(nq, kwd, vwd, tq=128)

    return _out(x, gate, g_post.reshape(1, d), ret, o_cmp, o_slc, o_win, gl, ngs,
                w_out.astype(BF16), ts=512)


def kernel(x, c, positions, w_ada, b_ada, g_pre, g_post, w_in, w_out, cmp_pe_k, cmp_w1_k, cmp_w2_k, cmp_pe_v, cmp_w1_v, cmp_w2_v):
    for l in range(w_in.shape[0]):
        x = _layer(x, c, positions, _ada(c, w_ada, b_ada, l), g_pre[l], g_post[l], w_in[l], w_out[l],
                   cmp_pe_k[l], cmp_w1_k[l], cmp_w2_k[l], cmp_pe_v[l], cmp_w1_v[l], cmp_w2_v[l])
    return x
```

```python
import functools

import numpy as np
import jax
import jax.numpy as jnp
from jax import lax
from jax.experimental import pallas as pl
from jax.experimental.pallas import tpu as pltpu

F32 = jnp.float32
BF16 = jnp.bfloat16

LANES = 128
HEAD_DIM = 64
RET_HEADS = 8
NSA_HEADS = 8
NSA_KV_GROUPS = 2
HEADS_PER_GROUP = NSA_HEADS // NSA_KV_GROUPS
RET_WIDTH = RET_HEADS * HEAD_DIM
NSA_WIDTH = NSA_HEADS * HEAD_DIM
KV_WIDTH = NSA_KV_GROUPS * HEAD_DIM
RET_CHUNK = 128
RET_ROPE_BASE = 10000.0
NSA_ROPE_BASE = 500000.0
ROPE_DIM = HEAD_DIM // 4
CMP_BLOCK = 32
CMP_STRIDE = 16
CMP_HIDDEN = 256
SLC_BLOCK = 64
SLC_TOPK = 16
WINDOW = 512
NEG = -1e30
LOG2E = 1.4426950408889634
BIG = 1e9
EPS = 1e-6
GN_EPS = 1e-5
VMEM_LIMIT = 48 * 1024 * 1024

NT_DIMS = (((1,), (1,)), ((), ()))
TN_DIMS = (((0,), (0,)), ((), ()))


def _dot(a, b):
    return jnp.dot(a, b, preferred_element_type=F32)


def _dot_nt(a, b):
    return lax.dot_general(a, b, NT_DIMS, preferred_element_type=F32)


def _dot_tn(a, b):
    return lax.dot_general(a, b, TN_DIMS, preferred_element_type=F32)


def _silu(z):
    return z * jax.nn.sigmoid(z)


def _params(*sem):
    return pltpu.CompilerParams(dimension_semantics=sem, vmem_limit_bytes=VMEM_LIMIT)


HEAD_SHIFT = 6
SLC_SHIFT = 6
assert 1 << HEAD_SHIFT == HEAD_DIM and 1 << SLC_SHIFT == SLC_BLOCK


def _lane_in_head(shape):
    return lax.broadcasted_iota(jnp.int32, shape, len(shape) - 1) & (HEAD_DIM - 1)


def _ada_kernel(c_ref, w_ref, b_ref, o_ref):
    a = _silu(c_ref[...])
    o_ref[...] = jnp.dot(a, w_ref[0], precision=lax.Precision.HIGHEST,
                         preferred_element_type=F32) + b_ref[0]


def _ada(c, w_layers, b_layers, layer):
    bsz, d = c.shape
    depth, _, n = w_layers.shape
    tn = d
    return pl.pallas_call(
        _ada_kernel,
        grid=(n // tn,),
        in_specs=[pl.BlockSpec((bsz, d), lambda j: (0, 0)),
                  pl.BlockSpec((1, d, tn), lambda j: (layer, 0, j)),
                  pl.BlockSpec((1, 1, tn), lambda j: (layer, 0, j))],
        out_specs=pl.BlockSpec((bsz, tn), lambda j: (0, j)),
        out_shape=jax.ShapeDtypeStruct((bsz, n), F32),
        compiler_params=_params("arbitrary"),
        name="ada",
    )(c, w_layers, b_layers.reshape(depth, 1, n))


_O_RQ, _O_RK, _O_RV, _O_RG, _O_NQ = 0, 512, 1024, 1536, 2048
_O_KV = 2560
_O_NG = 3328
_O_GL = 3840
_PROJ_COLS = 3968


def _rot_ret(z, cos, sin_signed, m):
    partner = jnp.where(m < HEAD_DIM // 2, pltpu.roll(z, LANES - HEAD_DIM // 2, 1),
                        pltpu.roll(z, HEAD_DIM // 2, 1))
    return z * cos + partner * sin_signed


def _rot_nsa(z, cos, sin_signed, m):
    half = ROPE_DIM // 2
    partner = jnp.where(m < half, pltpu.roll(z, LANES - half, 1), pltpu.roll(z, half, 1))
    return z * cos + partner * sin_signed


def _nsa_tables(posf, inv_row, m):
    half = ROPE_DIM // 2
    ang = posf * inv_row
    cos = jnp.where(m < ROPE_DIM, jnp.cos(ang), 1.0)
    s = jnp.sin(ang)
    sin = jnp.where(m < half, -s, jnp.where(m < ROPE_DIM, s, 0.0))
    return cos, sin


def _dup_groups(z, m_lane):
    r = pltpu.roll(z, HEAD_DIM, 1)
    lo = m_lane < HEAD_DIM
    return jnp.where(lo, z, r), jnp.where(lo, r, z)


def _proj_kernel(x_ref, shift_ref, scale_ref, gpre_ref, pos_ref, invr_ref, invn_ref, w_ref,
                 rq_ref, rk_ref, rv_ref, rgs_ref, nq_ref, cmps_ref, ksd_ref, vsd_ref,
                 kwd_ref, vwd_ref, ngs_ref, gl_ref,
                 cr_s, sr_s, cn_s, sn_s, z0_s, z1_s):
    ts = x_ref.shape[1]
    x = x_ref[0]
    ms = jnp.mean(x * x, axis=-1, keepdims=True)
    y = x * lax.rsqrt(ms + EPS) * gpre_ref[...]
    h = y * (1.0 + scale_ref[0]) + shift_ref[0]
    hb = h.astype(BF16)

    posf = pos_ref[0].astype(F32)
    lane = lax.broadcasted_iota(jnp.int32, (ts, LANES), 1)
    m = lane & (HEAD_DIM - 1)
    ang = posf * invr_ref[...]
    cr_s[...] = jnp.cos(ang)
    s = jnp.sin(ang)
    sr_s[...] = jnp.where(m < HEAD_DIM // 2, -s, s)
    cn, sn = _nsa_tables(posf, invn_ref[...], m)
    cn_s[...] = cn
    sn_s[...] = sn

    staging = [z0_s, z1_s]

    def mm(off, width):
        z_s = staging[0]
        staging.reverse()
        z_s[:, :width] = _dot(hb, w_ref[:, off:off + width])
        return lambda c: z_s[:, c * LANES:(c + 1) * LANES]

    scale_q = HEAD_DIM ** -0.5
    tiles = lambda width: [(c, slice(c * LANES, (c + 1) * LANES)) for c in range(width // LANES)]
    z = mm(_O_RQ, RET_WIDTH)
    for c, sl in tiles(RET_WIDTH):
        rq_ref[0, :, sl] = (_rot_ret(z(c), cr_s[...], sr_s[...], m) * scale_q).astype(BF16)
    z = mm(_O_RK, RET_WIDTH)
    for c, sl in tiles(RET_WIDTH):
        rk_ref[0, :, sl] = _rot_ret(z(c), cr_s[...], sr_s[...], m).astype(BF16)
    z = mm(_O_RV, RET_WIDTH)
    for c, sl in tiles(RET_WIDTH):
        rv_ref[0, :, sl] = z(c).astype(BF16)
    z = mm(_O_RG, RET_WIDTH)
    for c, sl in tiles(RET_WIDTH):
        rgs_ref[0, :, sl] = _silu(z(c)).astype(BF16)
    z = mm(_O_NQ, NSA_WIDTH)
    for c, sl in tiles(NSA_WIDTH):
        nq_ref[0, :, sl] = (_rot_nsa(z(c), cn_s[...], sn_s[...], m) * (scale_q * LOG2E)).astype(BF16)

    z = mm(_O_KV, 6 * KV_WIDTH)
    kc0, kc1 = _dup_groups(z(0), lane)
    vc0, vc1 = _dup_groups(z(1), lane)
    for i, t in enumerate((kc0, kc1, vc0, vc1)):
        cmps_ref[0, :, i * LANES:(i + 1) * LANES] = t
    lo = lane < HEAD_DIM
    tok = pl.program_id(1) * ts + lax.broadcasted_iota(jnp.int32, (ts, LANES), 0)
    onehot = jnp.where(lane - HEAD_DIM == (tok >> SLC_SHIFT), 1.0, 0.0)
    ones = jnp.ones((ts, LANES), F32)
    for c, ref, rotate, upper in ((2, ksd_ref, True, onehot), (3, vsd_ref, False, ones),
                                  (4, kwd_ref, True, None), (5, vwd_ref, False, ones)):
        t = _rot_nsa(z(c), cn_s[...], sn_s[...], m) if rotate else z(c)
        if upper is None:
            d0, d1 = _dup_groups(t, lane)
        else:
            d0, d1 = jnp.where(lo, t, upper), jnp.where(lo, pltpu.roll(t, HEAD_DIM, 1), upper)
        ref[0, :, :LANES] = d0.astype(BF16)
        ref[0, :, LANES:] = d1.astype(BF16)

    z = mm(_O_NG, NSA_WIDTH + LANES)
    for c, sl in tiles(NSA_WIDTH):
        ngs_ref[0, :, sl] = _silu(z(c)).astype(BF16)
    gl_ref[0] = jax.nn.sigmoid(z(NSA_WIDTH // LANES))


def _proj(x, shift, scale, g_pre, pos3, inv_ret_row, inv_nsa_row, w_all, ts):
    bsz, seq, d = x.shape
    row = lambda w: pl.BlockSpec((1, ts, w), lambda b, s: (b, s, 0))
    vec = pl.BlockSpec((1, 1, d), lambda b, s: (b, 0, 0))
    const2 = lambda shp: pl.BlockSpec(shp, lambda b, s: (0, 0))
    outs = [(RET_WIDTH, BF16), (RET_WIDTH, BF16), (RET_WIDTH, BF16), (RET_WIDTH, BF16),
            (NSA_WIDTH, BF16), (4 * LANES, F32), (2 * LANES, BF16), (2 * LANES, BF16),
            (2 * LANES, BF16), (2 * LANES, BF16), (NSA_WIDTH, BF16), (LANES, F32)]
    return pl.pallas_call(
        _proj_kernel,
        grid=(bsz, seq // ts),
        in_specs=[row(d), vec, vec, const2((1, d)), row(1), const2((1, LANES)), const2((1, LANES)),
                  const2((d, _PROJ_COLS))],
        out_specs=[row(w) for w, _ in outs],
        out_shape=[jax.ShapeDtypeStruct((bsz, seq, w), dt) for w, dt in outs],
        scratch_shapes=[pltpu.VMEM((ts, LANES), F32)] * 4 + [pltpu.VMEM((ts, 6 * KV_WIDTH), F32)] * 2,
        compiler_params=_params("parallel", "arbitrary"),
        name="proj",
    )(x, shift, scale, g_pre, pos3, inv_ret_row, inv_nsa_row, w_all)


def _ret_kernel(q_ref, k_ref, v_ref, g_ref, dec_ref, zeta_ref, xi_ref, cd_ref, o_ref, r_s):
    rows = q_ref.shape[1]
    cc = RET_CHUNK

    @pl.when(pl.program_id(2) == 0)
    def _():
        r_s[...] = jnp.zeros_like(r_s)

    lane = lax.broadcasted_iota(jnp.int32, (cc, LANES), 1)
    rowi = lax.broadcasted_iota(jnp.int32, (cc, LANES), 0)
    lo = lane < HEAD_DIM
    blockdiag = (rowi < HEAD_DIM) == lo
    inv_n = 1.0 / HEAD_DIM
    chunks = [slice(c * cc, (c + 1) * cc) for c in range(rows // cc)]
    s_ab, kvs = [], []
    for sl in chunks:
        q, k, v = q_ref[0, sl, :], k_ref[0, sl, :], v_ref[0, sl, :]
        zero = jnp.zeros_like(q)
        s_ab.append((_dot_nt(jnp.where(lo, q, zero), k), _dot_nt(jnp.where(lo, zero, q), k)))
        kz = (k.astype(F32) * zeta_ref[0]).astype(BF16)
        kvs.append(jnp.where(blockdiag, _dot_tn(kz, v), 0.0))
    inners = []
    for sl, (s_a, s_b) in zip(chunks, s_ab):
        v = v_ref[0, sl, :]
        inners.append(jnp.where(lo, _dot((s_a * dec_ref[0]).astype(BF16), v),
                                _dot((s_b * dec_ref[1]).astype(BF16), v)))
    r = r_s[...]
    outs = []
    for sl, inner, kv in zip(chunks, inners, kvs):
        outs.append(inner + _dot(q_ref[0, sl, :], r.astype(BF16)) * xi_ref[0])
        r = r * cd_ref[0] + kv
    r_s[...] = r
    head_ones = jnp.where(blockdiag, 1.0, 0.0).astype(BF16)

    def head_sums(z):
        hi = z.astype(BF16)
        lo_part = (z - hi.astype(F32)).astype(BF16)
        return _dot(hi, head_ones) + _dot(lo_part, head_ones)

    devs = [o - s * inv_n for o, s in zip(outs, [head_sums(o) for o in outs])]
    variances = [head_sums(d * d) * inv_n for d in devs]
    for sl, d, var in zip(chunks, devs, variances):
        o_ref[0, sl, :] = (d * lax.rsqrt(var + GN_EPS) * g_ref[0, sl, :]).astype(o_ref.dtype)


def _retention(rq, rk, rv, rgs, dec, zeta_l, xi_l, cd_l, rows):
    bsz, seq, _ = rq.shape
    npair = RET_HEADS // 2
    blk = pl.BlockSpec((1, rows, LANES), lambda b, p, s: (b, s, p))
    return pl.pallas_call(
        _ret_kernel,
        grid=(bsz, npair, seq // rows),
        in_specs=[blk, blk, blk, blk,
                  pl.BlockSpec((2, RET_CHUNK, RET_CHUNK), lambda b, p, s: (p, 0, 0)),
                  pl.BlockSpec((1, RET_CHUNK, LANES), lambda b, p, s: (p, 0, 0)),
                  pl.BlockSpec((1, RET_CHUNK, LANES), lambda b, p, s: (p, 0, 0)),
                  pl.BlockSpec((1, 1, LANES), lambda b, p, s: (p, 0, 0))],
        out_specs=blk,
        out_shape=jax.ShapeDtypeStruct((bsz, seq, RET_WIDTH), BF16),
        scratch_shapes=[pltpu.VMEM((LANES, LANES), F32)],
        compiler_params=_params("parallel", "parallel", "arbitrary"),
        name="retention",
    )(rq, rk, rv, rgs, dec, zeta_l, xi_l, cd_l)


def _cmp_kernel(src_ref, pe_ref, w1_ref, w2_ref, pos_ref, invn_ref, o_ref):
    ncp = o_ref.shape[3]
    pairs = CMP_BLOCK // 4
    lo = lax.broadcasted_iota(jnp.int32, (ncp, LANES), 1) < HEAD_DIM
    p_lo = jnp.zeros((ncp, CMP_HIDDEN), F32)
    p_hi = jnp.zeros((ncp, CMP_HIDDEN), F32)
    for m in range(pairs):
        t = jnp.where(lo, src_ref[0, pl.ds(2 * m, ncp, stride=CMP_STRIDE), :],
                      src_ref[0, pl.ds(2 * m + 1, ncp, stride=CMP_STRIDE), :])
        p_lo = p_lo + _dot((t + pe_ref[0, m:m + 1, :]).astype(BF16), w1_ref[0, m])
        p_hi = p_hi + _dot((t + pe_ref[0, pairs + m:pairs + m + 1, :]).astype(BF16), w1_ref[0, pairs + m])
    hidden = p_lo + pltpu.roll(p_hi, ncp - 1, 0)
    comp = _dot(_silu(hidden).astype(BF16), w2_ref[0])
    m = _lane_in_head(comp.shape)
    cos, sin = _nsa_tables(pos_ref[0].astype(F32), invn_ref[...], m)
    rotated = _rot_nsa(comp, cos, sin, m)
    is_key = pl.program_id(1) == 0
    o_ref[0, 0, 0] = jnp.where(is_key, rotated, comp).astype(BF16)


def _compress(cmps, pe2, w1p, w2d, pos_cmp, inv_nsa_row):
    bsz, seq, _ = cmps.shape
    ncp = seq // CMP_STRIDE
    g = NSA_KV_GROUPS
    return pl.pallas_call(
        _cmp_kernel,
        grid=(bsz, 2, g),
        in_specs=[pl.BlockSpec((1, seq, LANES), lambda b, j, gi: (b, 0, j * g + gi)),
                  pl.BlockSpec((1, CMP_BLOCK // 2, LANES), lambda b, j, gi: (j, 0, 0)),
                  pl.BlockSpec((1, CMP_BLOCK // 2, LANES, CMP_HIDDEN), lambda b, j, gi: (j, 0, 0, 0)),
                  pl.BlockSpec((1, CMP_HIDDEN, LANES), lambda b, j, gi: (j, 0, 0)),
                  pl.BlockSpec((1, ncp, 1), lambda b, j, gi: (b, 0, 0)),
                  pl.BlockSpec((1, LANES), lambda b, j, gi: (0, 0))],
        out_specs=pl.BlockSpec((1, 1, 1, ncp, LANES), lambda b, j, gi: (b, j, gi, 0, 0)),
        out_shape=jax.ShapeDtypeStruct((bsz, 2, g, ncp, LANES), BF16),
        compiler_params=_params("parallel", "arbitrary", "arbitrary"),
        name="compress",
    )(cmps, pe2, w1p, w2d, pos_cmp, inv_nsa_row)


def _cmpsel_kernel(q_ref, kc_ref, vc_ref, ovt_ref, o_ref, selt_ref,
                   qs0_s, qs1_s, s0_s, s1_s, p0_s, p1_s, hi0_s, hi1_s, lo0_s, lo1_s, score_s, *, tq):
    nh = HEADS_PER_GROUP
    sub = 8
    n_qt = q_ref.shape[1] // tq
    ncp = kc_ref.shape[3]
    nblk = ovt_ref.shape[0]
    kc = kc_ref[0, 0, 0]
    vc = vc_ref[0, 0, 0]
    qs_b, s_b, p_b, hi_b, lo_b = (qs0_s, qs1_s), (s0_s, s1_s), (p0_s, p1_s), (hi0_s, hi1_s), (lo0_s, lo1_s)

    def rows_of(i):
        return slice(i * tq, (i + 1) * tq)

    def last_block(i):
        return ((i + 1) * tq - 1) >> SLC_SHIFT

    def n_rows(i):
        return min(nblk, -(-(last_block(i) + 1) // sub) * sub)

    def stack(i, slot):
        _stack_head_queries(q_ref, qs_b[slot], tq, src=rows_of(i))

    def scores(i, slot):
        s_b[slot][...] = _dot_nt(qs_b[slot][...], kc)

    def softmax(i, slot):
        tok = i * tq + lax.broadcasted_iota(jnp.int32, (tq, LANES), 0)
        lane = lax.broadcasted_iota(jnp.int32, (tq, LANES), 1)
        biases = [jnp.where((lane + c * LANES) * CMP_STRIDE + (CMP_BLOCK - 1) <= tok, 0.0, NEG)
                  for c in range(ncp // LANES)]
        has_key = tok >= CMP_BLOCK - 1
        psum = None
        for h in range(nh):
            rows = slice(h * tq, (h + 1) * tq)
            tiles = [t + b for b, t in zip(biases, _lane_tiles(s_b[slot][rows, :]))]
            mx = _row_stat(tiles, jnp.maximum, jnp.max)
            es = [jnp.exp2(t - mx) for t in tiles]
            inv = jnp.where(has_key, 1.0 / _row_stat(es, jnp.add, jnp.sum), 0.0)
            ps = [e * inv for e in es]
            psum = ps if psum is None else [a + b for a, b in zip(psum, ps)]
            p_b[slot][rows, :] = jnp.concatenate(ps, axis=1).astype(BF16)
        psum = jnp.concatenate(psum, axis=1)
        hi = psum.astype(BF16)
        hi_b[slot][...] = hi
        lo_b[slot][...] = (psum - hi.astype(F32)).astype(BF16)

    def values(i, slot):
        pv = _dot(p_b[slot][...], vc)
        _store_heads(o_ref, [pv[h * tq:(h + 1) * tq, :] for h in range(nh)], tq, rows_of(i))
        ov = ovt_ref[0:n_rows(i), :]
        return _dot_nt(ov, hi_b[slot][...]) + _dot_nt(ov, lo_b[slot][...])

    def select(i, imp):
        nr = n_rows(i)
        blk = lax.broadcasted_iota(jnp.int32, (nr, tq), 0)
        cur = (i * tq + lax.broadcasted_iota(jnp.int32, (nr, tq), 1)) >> SLC_SHIFT
        forced = (blk == 0) | (blk == cur) | (blk == cur - 1)
        score_s[0:nr, :] = jnp.where(forced, BIG, jnp.where(blk <= cur, imp, NEG))
        row_in_group = lax.broadcasted_iota(jnp.int32, (sub, tq), 0)
        scs = [score_s[r * sub:(r + 1) * sub, :] for r in range(nr // sub)]
        cnts = [jnp.zeros((sub, tq), F32) for _ in scs]
        for src in range(last_block(i) + 1):
            ri = jnp.broadcast_to(score_s[src:src + 1, :], (sub, tq))
            for r, sc in enumerate(scs):
                if r * sub > src:
                    beats = ri >= sc
                elif (r + 1) * sub - 1 <= src:
                    beats = ri > sc
                else:
                    beats = (ri > sc) | ((ri == sc) & (row_in_group > src - r * sub))
                cnts[r] = cnts[r] + jnp.where(beats, 1.0, 0.0)
        sels = [jnp.where((cnt < float(min(SLC_TOPK, nblk))) & (sc > 0.5 * NEG), 1.0, 0.0)
                for cnt, sc in zip(cnts, scs)]
        pads = [jnp.zeros((HEAD_DIM, tq), F32)] + sels
        if nr < HEAD_DIM:
            pads.append(jnp.zeros((HEAD_DIM - nr, tq), F32))
        selt_ref[0, 0, rows_of(i), :] = jnp.concatenate(pads, axis=0).T.astype(selt_ref.dtype)

    stack(0, 0)
    scores(0, 0)
    if n_qt > 1:
        stack(1, 1)
    for i in range(n_qt):
        slot = i % 2
        imp = values(i - 1, 1 - slot) if i >= 1 else None
        if i + 1 < n_qt:
            scores(i + 1, 1 - slot)
        if i + 2 < n_qt:
            stack(i + 2, slot)
        softmax(i, slot)
        if i >= 1:
            select(i - 1, imp)
    select(n_qt - 1, values(n_qt - 1, (n_qt - 1) % 2))


def _cmpsel(nq, cmpkv, ovt, tq):
    bsz, seq, _ = nq.shape
    g = NSA_KV_GROUPS
    ncp = cmpkv.shape[3]
    nblk = ovt.shape[0]
    gw = HEADS_PER_GROUP * HEAD_DIM
    rows = HEADS_PER_GROUP * tq
    assert nblk <= HEAD_DIM and nblk % 8 == 0
    return pl.pallas_call(
        functools.partial(_cmpsel_kernel, tq=tq),
        grid=(bsz, g),
        in_specs=[pl.BlockSpec((1, seq, gw), lambda b, gi: (b, 0, gi)),
                  pl.BlockSpec((1, 1, 1, ncp, LANES), lambda b, gi: (b, 0, gi, 0, 0)),
                  pl.BlockSpec((1, 1, 1, ncp, LANES), lambda b, gi: (b, 1, gi, 0, 0)),
                  pl.BlockSpec((nblk, ncp), lambda b, gi: (0, 0))],
        out_specs=[pl.BlockSpec((1, seq, gw), lambda b, gi: (b, 0, gi)),
                   pl.BlockSpec((1, 1, seq, LANES), lambda b, gi: (b, gi, 0, 0))],
        out_shape=[jax.ShapeDtypeStruct((bsz, seq, NSA_WIDTH), BF16),
                   jax.ShapeDtypeStruct((bsz, g, seq, LANES), BF16)],
        scratch_shapes=([pltpu.VMEM((rows, LANES), BF16)] * 2 + [pltpu.VMEM((rows, ncp), F32)] * 2 +
                        [pltpu.VMEM((rows, ncp), BF16)] * 2 + [pltpu.VMEM((tq, ncp), BF16)] * 4 +
                        [pltpu.VMEM((nblk, tq), F32)]),
        compiler_params=_params("parallel", "arbitrary"),
        name="cmpsel",
    )(nq, cmpkv, cmpkv, ovt)


def _stack_head_queries(q_ref, qs_s, rows, src=None):
    src = slice(0, rows) if src is None else src
    lo = lax.broadcasted_iota(jnp.int32, (rows, LANES), 1) < HEAD_DIM
    for c in range(HEADS_PER_GROUP // 2):
        q = q_ref[0, src, c * LANES:(c + 1) * LANES]
        zero = jnp.zeros_like(q)
        qs_s[(2 * c) * rows:(2 * c + 1) * rows, :] = jnp.where(lo, q, zero)
        qs_s[(2 * c + 1) * rows:(2 * c + 2) * rows, :] = jnp.where(lo, zero, q)


def _store_heads(o_ref, outs, tq, rows=slice(None)):
    lo = lax.broadcasted_iota(jnp.int32, (tq, LANES), 1) < HEAD_DIM
    for c in range(HEADS_PER_GROUP // 2):
        o_ref[0, rows, c * LANES:(c + 1) * LANES] = jnp.where(lo, outs[2 * c], outs[2 * c + 1]).astype(o_ref.dtype)


def _store_heads_normalized(o_ref, accs, tq, rows=slice(None)):
    lo = lax.broadcasted_iota(jnp.int32, (tq, LANES), 1) < HEAD_DIM
    for c in range(HEADS_PER_GROUP // 2):
        a, b = accs[2 * c], accs[2 * c + 1]
        num = jnp.where(lo, a, pltpu.roll(b, HEAD_DIM, 1))
        den = jnp.where(lo, pltpu.roll(a, HEAD_DIM, 1), b)
        o_ref[0, rows, c * LANES:(c + 1) * LANES] = (num * (1.0 / den)).astype(o_ref.dtype)


def _lane_tiles(z):
    return [z[:, c * LANES:(c + 1) * LANES] for c in range(z.shape[1] // LANES)]


def _row_stat(tiles, combine, reduce):
    acc = functools.reduce(combine, tiles)
    return jnp.broadcast_to(reduce(acc, axis=-1, keepdims=True), acc.shape)


WIN_UNROLL = 8


def _win_kernel(q_ref, k_ref, v_ref, o_ref, qs0_s, qs1_s, s0_s, s1_s, p0_s, p1_s, *, tq):
    nh = HEADS_PER_GROUP
    span = WINDOW + tq
    n_qt = q_ref.shape[1] // tq
    qs_b, s_b, p_b = (qs0_s, qs1_s), (s0_s, s1_s), (p0_s, p1_s)

    def q_rows(i):
        return pl.ds(pl.multiple_of(i * tq, tq), tq)

    def k_rows(i):
        return pl.ds(pl.multiple_of(jnp.maximum(i * tq - WINDOW, 0), tq), span)

    def stack_queries(i, slot):
        _stack_head_queries(q_ref, qs_b[slot], tq, src=q_rows(jnp.minimum(i, n_qt - 1)))

    def scores(i, slot):
        i = jnp.minimum(i, n_qt - 1)
        s_b[slot][...] = _dot_nt(qs_b[slot][...], k_ref[0, k_rows(i), :])

    def values(i, slot):
        i = jnp.maximum(i, 0)
        pv = _dot(p_b[slot][...], v_ref[0, k_rows(i), :])
        _store_heads_normalized(o_ref, [pv[h * tq:(h + 1) * tq, :] for h in range(nh)], tq, q_rows(i))

    def softmax(i, slot):
        tok = i * tq + lax.broadcasted_iota(jnp.int32, (tq, span), 0)
        key = jnp.maximum(i * tq - WINDOW, 0) + lax.broadcasted_iota(jnp.int32, (tq, span), 1)
        bias = jnp.where((key <= tok) & (tok - key < WINDOW), 0.0, NEG)
        for h in range(nh):
            rows = slice(h * tq, (h + 1) * tq)
            tiles = _lane_tiles(s_b[slot][rows, :] + bias)
            m = _row_stat(tiles, jnp.maximum, jnp.max)
            p_b[slot][rows, :] = jnp.concatenate([jnp.exp2(t - m) for t in tiles], axis=1).astype(BF16)

    def stages(first, count):
        for t in range(count):
            i, slot = first + t, t % 2
            values(i - 1, 1 - slot)
            scores(i + 1, 1 - slot)
            stack_queries(i + 2, slot)
            softmax(i, slot)

    p1_s[...] = jnp.ones_like(p1_s)
    stack_queries(0, 0)
    stack_queries(1, 1)
    scores(0, 0)

    def body(b, carry):
        stages(b * WIN_UNROLL, WIN_UNROLL)
        return carry

    lax.fori_loop(0, n_qt // WIN_UNROLL, body, 0)
    stages(n_qt - n_qt % WIN_UNROLL, n_qt % WIN_UNROLL)
    values(n_qt - 1, (n_qt - 1) % 2)


SLC_UNROLL = 8
SLC_STATE_BUFFERS = SLC_UNROLL + 1
assert SLC_UNROLL % 2 == 0


def _slc_kernel(q_ref, k_ref, v_ref, selt_ref, o_ref, qs_s, s0_s, s1_s, p0_s, p1_s, a0_s, a1_s,
                m_s, acc_s, *, tq, tk):
    nh = HEADS_PER_GROUP
    seq = q_ref.shape[1]
    n_qt = seq // tq
    s_b, p_b, a_b = (s0_s, s1_s), (p0_s, p1_s), (a0_s, a1_s)
    lo = lax.broadcasted_iota(jnp.int32, (tq, LANES), 1) < HEAD_DIM

    def n_tiles(qi):
        return (qi * tq + tq - 1) // tk + 1

    n_items = sum((qi * tq + tq - 1) // tk + 1 for qi in range(n_qt))

    def advance(qi, j):
        wrap = j + 1 >= n_tiles(qi)
        return jnp.where(wrap, jnp.minimum(qi + 1, n_qt - 1), qi), jnp.where(wrap, 0, j + 1)

    def q_rows(qi):
        return pl.ds(pl.multiple_of(qi * tq, tq), tq)

    def k_rows(j):
        return pl.ds(pl.multiple_of(j * tk, tk), tk)

    def stack_queries(qi):
        par = qi % SLC_STATE_BUFFERS
        unselected = jnp.where(selt_ref[0, 0, q_rows(qi), :].astype(F32) > 0.5, 0.0, NEG)
        for c in range(nh // 2):
            q = q_ref[0, q_rows(qi), c * LANES:(c + 1) * LANES].astype(F32)
            qs_s[par, (2 * c) * tq:(2 * c + 1) * tq, :] = jnp.where(lo, q, unselected).astype(BF16)
            qs_s[par, (2 * c + 1) * tq:(2 * c + 2) * tq, :] = jnp.where(
                lo, pltpu.roll(q, HEAD_DIM, 1), unselected).astype(BF16)

    def scores(qi, j, slot):
        s_b[slot][...] = _dot_nt(qs_s[qi % SLC_STATE_BUFFERS], k_ref[0, k_rows(j), :])

    def values(qi, j, slot):
        pv = _dot(p_b[slot][...], v_ref[0, k_rows(j), :])
        par = qi % SLC_STATE_BUFFERS
        for h in range(nh):
            acc_s[par, h] = a_b[slot][h] * acc_s[par, h] + pv[h * tq:(h + 1) * tq, :]

    def softmax(qi, j, slot):
        tok = qi * tq + lax.broadcasted_iota(jnp.int32, (tq, tk), 0)
        key = j * tk + lax.broadcasted_iota(jnp.int32, (tq, tk), 1)
        bias = jnp.where(key <= tok, 0.0, NEG)
        for h in range(nh):
            rows = slice(h * tq, (h + 1) * tq)
            tiles = _lane_tiles(s_b[slot][rows, :] + bias)
            m_prev = jnp.where(j == 0, NEG, m_s[h])
            m_new = jnp.maximum(m_prev, _row_stat(tiles, jnp.maximum, jnp.max))
            m_s[h] = m_new
            a_b[slot][h] = jnp.exp2(m_prev - m_new)
            p_b[slot][rows, :] = jnp.concatenate([jnp.exp2(t - m_new) for t in tiles], axis=1).astype(BF16)

    def finalize(qi):
        _store_heads_normalized(o_ref, [acc_s[qi % SLC_STATE_BUFFERS, h] for h in range(nh)], tq, q_rows(qi))

    def is_last(qi, j):
        return (j + 1 >= n_tiles(qi)).astype(jnp.int32)

    def finalize_if(flag, qi):
        @pl.when(flag == 1)
        def _():
            finalize(qi)

    def block(carry, n_stages):
        fin, (qp, jp, p_last), (qc, jc), (qn, jn) = carry
        for flag, qi in fin:
            finalize_if(flag, qi)
        items = [(qp, jp, p_last), (qc, jc, is_last(qc, jc)), (qn, jn, is_last(qn, jn))]
        for _ in range(n_stages - 1):
            qx, jx = advance(items[-1][0], items[-1][1])
            items.append((qx, jx, is_last(qx, jx)))
        for qx, jx, _ in items[2:]:
            @pl.when(jx == 0)
            def _():
                stack_queries(qx)
        for t in range(n_stages):
            slot = t % 2
            values(items[t][0], items[t][1], 1 - slot)
            scores(items[t + 2][0], items[t + 2][1], 1 - slot)
            softmax(items[t + 1][0], items[t + 1][1], slot)
        fin = tuple((items[t][2], items[t][0]) for t in range(n_stages))
        nxt = advance(items[-1][0], items[-1][1])
        return fin, items[n_stages], items[n_stages + 1][:2], nxt

    m_s[...] = jnp.full_like(m_s, NEG)
    acc_s[...] = jnp.zeros_like(acc_s)
    p1_s[...] = jnp.zeros_like(p1_s)
    a1_s[...] = jnp.ones_like(a1_s)
    zero = jnp.int32(0)
    stack_queries(zero)
    scores(zero, zero, 0)
    unroll = SLC_UNROLL
    carry = (((zero, zero),) * unroll, (zero, zero, zero), (zero, zero), advance(zero, zero))
    carry = lax.fori_loop(0, n_items // unroll, lambda i, c: block(c, unroll), carry)
    if n_items % unroll:
        carry = block(carry, n_items % unroll)
    fin, (qp, jp, _), _, _ = carry
    for flag, qi in fin:
        finalize_if(flag, qi)
    values(qp, jp, (n_items - 1) % 2)
    finalize(qp)


def _slc_attention(nq, kd, vd, selt, *, tq, tk):
    bsz, seq, _ = nq.shape
    g = NSA_KV_GROUPS
    gw = HEADS_PER_GROUP * HEAD_DIM
    nh = HEADS_PER_GROUP
    assert seq % tk == 0 and tk % tq == 0
    assert seq // SLC_BLOCK <= HEAD_DIM
    whole = lambda w: pl.BlockSpec((1, seq, w), lambda b, gi: (b, 0, gi))
    stat = pltpu.VMEM((nh, tq, LANES), F32)
    state = pltpu.VMEM((SLC_STATE_BUFFERS, nh, tq, LANES), F32)
    return pl.pallas_call(
        functools.partial(_slc_kernel, tq=tq, tk=tk),
        grid=(bsz, g),
        in_specs=[whole(gw), whole(LANES), whole(LANES),
                  pl.BlockSpec((1, 1, seq, LANES), lambda b, gi: (b, gi, 0, 0))],
        out_specs=whole(gw),
        out_shape=jax.ShapeDtypeStruct((bsz, seq, NSA_WIDTH), BF16),
        scratch_shapes=([pltpu.VMEM((SLC_STATE_BUFFERS, nh * tq, LANES), BF16)] + [pltpu.VMEM((nh * tq, tk), F32)] * 2 +
                        [pltpu.VMEM((nh * tq, tk), BF16)] * 2 + [stat] * 3 + [state]),
        compiler_params=_params("parallel", "arbitrary"),
        name="slc_attn",
    )(nq, kd, vd, selt)


def _win_attention(nq, kd, vd, *, tq):
    bsz, seq, _ = nq.shape
    g = NSA_KV_GROUPS
    gw = HEADS_PER_GROUP * HEAD_DIM
    nh = HEADS_PER_GROUP
    assert WINDOW % tq == 0 and seq >= WINDOW + tq and seq // tq >= 2 and WIN_UNROLL % 2 == 0
    whole = lambda w: pl.BlockSpec((1, seq, w), lambda b, gi: (b, 0, gi))
    span = WINDOW + tq
    return pl.pallas_call(
        functools.partial(_win_kernel, tq=tq),
        grid=(bsz, g),
        in_specs=[whole(gw), whole(LANES), whole(LANES)],
        out_specs=whole(gw),
        out_shape=jax.ShapeDtypeStruct((bsz, seq, NSA_WIDTH), BF16),
        scratch_shapes=([pltpu.VMEM((nh * tq, LANES), BF16)] * 2 + [pltpu.VMEM((nh * tq, span), F32)] * 2 +
                        [pltpu.VMEM((nh * tq, span), BF16)] * 2),
        compiler_params=_params("parallel", "arbitrary"),
        name="win_attn",
    )(nq, kd, vd)


def _out_kernel(x_ref, gate_ref, gpost_ref, ret_ref, oc_ref, os_ref, ow_ref, gl_ref, ngs_ref,
                wo_ref, o_ref):
    ts = x_ref.shape[1]
    gates = gl_ref[0]
    lo_off = 32
    assert 3 * NSA_HEADS <= lo_off
    g_hi = gates.astype(BF16).astype(F32)
    g_lo = (gates - g_hi).astype(BF16).astype(F32)
    lane = lax.broadcasted_iota(jnp.int32, (ts, LANES), 1)
    g_split = jnp.where(lane < lo_off, g_hi, pltpu.roll(g_lo, lo_off, 1)).astype(BF16)
    src = lax.broadcasted_iota(jnp.int32, (LANES, NSA_WIDTH), 0)
    head = lax.broadcasted_iota(jnp.int32, (LANES, NSA_WIDTH), 1) >> HEAD_SHIFT
    att = jnp.zeros((ts, NSA_WIDTH), F32)
    for i, br in enumerate((oc_ref, os_ref, ow_ref)):
        expand = jnp.where((src == 3 * head + i) | (src == lo_off + 3 * head + i), 1.0, 0.0).astype(BF16)
        att = att + _dot(g_split, expand) * br[0]
    att = att * ngs_ref[0]
    y = _dot(ret_ref[0].astype(BF16), wo_ref[:RET_WIDTH, :]) + _dot(att.astype(BF16), wo_ref[RET_WIDTH:, :])
    ms = jnp.mean(y * y, axis=-1, keepdims=True)
    y = y * lax.rsqrt(ms + EPS) * gpost_ref[...]
    o_ref[0] = x_ref[0] + gate_ref[0] * y


def _out(x, gate, g_post, ret, o_cmp, o_slc, o_win, gl, ngs, w_out, ts):
    bsz, seq, d = x.shape
    row = lambda w: pl.BlockSpec((1, ts, w), lambda b, s: (b, s, 0))
    return pl.pallas_call(
        _out_kernel,
        grid=(bsz, seq // ts),
        in_specs=[row(d), pl.BlockSpec((1, 1, d), lambda b, s: (b, 0, 0)),
                  pl.BlockSpec((1, d), lambda b, s: (0, 0)),
                  row(RET_WIDTH), row(NSA_WIDTH), row(NSA_WIDTH), row(NSA_WIDTH), row(LANES),
                  row(NSA_WIDTH), pl.BlockSpec(w_out.shape, lambda b, s: (0, 0))],
        out_specs=row(d),
        out_shape=jax.ShapeDtypeStruct((bsz, seq, d), x.dtype),
        compiler_params=_params("parallel", "arbitrary"),
        name="out",
    )(x, gate, g_post, ret, o_cmp, o_slc, o_win, gl, ngs, w_out)


def _retention_tables():
    h = RET_HEADS
    c = RET_CHUNK
    log_g = jnp.log1p(-jnp.power(2.0, -5.0 - jnp.arange(h, dtype=F32)))
    idx = jnp.arange(c, dtype=F32)
    diff = idx[:, None] - idx[None, :]
    dec = jnp.where(diff[None] >= 0, jnp.exp(jnp.maximum(diff, 0.0)[None] * log_g[:, None, None]), 0.0)
    zeta = jnp.exp((c - 1 - idx)[None, :] * log_g[:, None])
    xi = jnp.exp((idx + 1.0)[None, :] * log_g[:, None])
    cd = jnp.exp(c * log_g)
    to_lanes = lambda t: jnp.repeat(t.reshape(h // 2, 2, c).transpose(0, 2, 1), HEAD_DIM, axis=2)
    cd_l = jnp.repeat(cd.reshape(h // 2, 1, 2), HEAD_DIM, axis=2)
    return dec, to_lanes(zeta), to_lanes(xi), cd_l


def _overlap_t(seq):
    ncp = seq // CMP_STRIDE
    nc = (seq - CMP_BLOCK) // CMP_STRIDE + 1
    ns = seq // SLC_BLOCK
    cs = np.arange(ncp) * CMP_STRIDE
    ce = cs + CMP_BLOCK - 1
    ss = np.arange(ns) * SLC_BLOCK
    ov = (cs[None, :] <= ss[:, None] + SLC_BLOCK - 1) & (ce[None, :] >= ss[:, None]) & (np.arange(ncp) < nc)[None, :]
    return jnp.asarray(ov.astype(np.float32), dtype=BF16)


def _layer(x, c, positions, mod, g_pre, g_post, w_in, w_out,
           pe_k, w1_k, w2_k, pe_v, w1_v, w2_v):
    bsz, seq, d = x.shape
    shift, scale, gate = [t.reshape(bsz, 1, d) for t in jnp.split(mod, 3, axis=-1)]

    offs = np.cumsum((RET_WIDTH,) * 4 + (NSA_WIDTH, 6 * KV_WIDTH, 3 * NSA_HEADS, NSA_WIDTH))
    w_gl = w_in[:, offs[5]:offs[6]]
    w_all = jnp.concatenate(
        [w_in[:, :offs[5]], w_in[:, offs[6]:], w_gl,
         jnp.zeros((d, LANES - w_gl.shape[1]), w_in.dtype)], axis=1).astype(BF16)

    lanes = jnp.arange(LANES)
    half_r = HEAD_DIM // 2
    inv_r = jnp.power(RET_ROPE_BASE, -jnp.arange(half_r, dtype=F32) / half_r)
    half_n = ROPE_DIM // 2
    inv_n = jnp.power(NSA_ROPE_BASE, -jnp.arange(half_n, dtype=F32) / half_n)
    inv_ret_row = inv_r[lanes % half_r].reshape(1, LANES)
    inv_nsa_row = inv_n[lanes % half_n].reshape(1, LANES)
    pos3 = positions.reshape(bsz, seq, 1)

    (rq, rk, rv, rgs, nq, cmps, ksd, vsd, kwd, vwd, ngs, gl) = _proj(
        x, shift, scale, g_pre.reshape(1, d), pos3, inv_ret_row, inv_nsa_row, w_all, ts=512)

    dec, zeta_l, xi_l, cd_l = _retention_tables()
    ret = _retention(rq, rk, rv, rgs, dec, zeta_l, xi_l, cd_l, rows=min(1024, seq))

    ncp = seq // CMP_STRIDE
    nc = (seq - CMP_BLOCK) // CMP_STRIDE + 1
    pe2 = jnp.stack([pe_k, pe_v]).reshape(2, CMP_BLOCK // 2, LANES)
    w1p = jnp.stack([w1_k, w1_v]).reshape(2, CMP_BLOCK // 2, LANES, CMP_HIDDEN).astype(BF16)
    w2s = jnp.stack([w2_k, w2_v])
    w2d = jnp.concatenate([w2s, w2s], axis=-1).astype(BF16)
    cmp_end = np.arange(nc) * CMP_STRIDE + CMP_BLOCK - 1
    pos_cmp = jnp.pad(positions[:, cmp_end], ((0, 0), (0, ncp - nc))).reshape(bsz, ncp, 1)
    cmpkv = _compress(cmps, pe2, w1p, w2d, pos_cmp, inv_nsa_row)

    o_cmp, selt = _cmpsel(nq, cmpkv, _overlap_t(seq), tq=min(256, seq))
    o_slc = _slc_attention(nq, ksd, vsd, selt, tq=256, tk=512)
    o_win = _win_attention(nq, kwd, vwd, tq=128)

    return _out(x, gate, g_post.reshape(1, d), ret, o_cmp, o_slc, o_win, gl, ngs,
                w_out.astype(BF16), ts=512)


def kernel(x, c, positions, w_ada, b_ada, g_pre, g_post, w_in, w_out, cmp_pe_k, cmp_w1_k, cmp_w2_k, cmp_pe_v, cmp_w1_v, cmp_w2_v):
    for l in range(w_in.shape[0]):
        x = _layer(x, c, positions, _ada(c, w_ada, b_ada, l), g_pre[l], g_post[l], w_in[l], w_out[l],
                   cmp_pe_k[l], cmp_w1_k[l], cmp_w2_k[l], cmp_pe_v[l], cmp_w1_v[l], cmp_w2_v[l])
    return x
```

```python
import functools

import numpy as np
import jax
import jax.numpy as jnp
from jax import lax
from jax.experimental import pallas as pl
from jax.experimental.pallas import tpu as pltpu

F32 = jnp.float32
BF16 = jnp.bfloat16

LANES = 128
HEAD_DIM = 64
RET_HEADS = 8
NSA_HEADS = 8
NSA_KV_GROUPS = 2
HEADS_PER_GROUP = NSA_HEADS // NSA_KV_GROUPS
RET_WIDTH = RET_HEADS * HEAD_DIM
NSA_WIDTH = NSA_HEADS * HEAD_DIM
KV_WIDTH = NSA_KV_GROUPS * HEAD_DIM
RET_CHUNK = 128
RET_ROPE_BASE = 10000.0
NSA_ROPE_BASE = 500000.0
ROPE_DIM = HEAD_DIM // 4
CMP_BLOCK = 32
CMP_STRIDE = 16
CMP_HIDDEN = 256
SLC_BLOCK = 64
SLC_TOPK = 16
WINDOW = 512
NEG = -1e30
LOG2E = 1.4426950408889634
BIG = 1e9
EPS = 1e-6
GN_EPS = 1e-5
VMEM_LIMIT = 48 * 1024 * 1024

PROJ_ROWS = 512
RET_ROWS = 1024
CMPSEL_TQ = 256
SLC_TQ, SLC_TK = 256, 512
WIN_TQ = 128

NT_DIMS = (((1,), (1,)), ((), ()))
TN_DIMS = (((0,), (0,)), ((), ()))


def _dot(a, b):
    return jnp.dot(a, b, preferred_element_type=F32)


def _dot_nt(a, b):
    return lax.dot_general(a, b, NT_DIMS, preferred_element_type=F32)


def _dot_tn(a, b):
    return lax.dot_general(a, b, TN_DIMS, preferred_element_type=F32)


def _silu(z):
    return z * jax.nn.sigmoid(z)


def _params(*sem):
    return pltpu.CompilerParams(dimension_semantics=sem, vmem_limit_bytes=VMEM_LIMIT)


HEAD_SHIFT = 6
SLC_SHIFT = 6
assert 1 << HEAD_SHIFT == HEAD_DIM and 1 << SLC_SHIFT == SLC_BLOCK


def _lane_in_head(shape):
    return lax.broadcasted_iota(jnp.int32, shape, len(shape) - 1) & (HEAD_DIM - 1)


def _ada_kernel(c_ref, w_ref, b_ref, o_ref):
    a = _silu(c_ref[...])
    o_ref[...] = jnp.dot(a, w_ref[0], precision=lax.Precision.HIGHEST,
                         preferred_element_type=F32) + b_ref[0]


def _ada(c, w_layers, b_layers, layer):
    bsz, d = c.shape
    depth, _, n = w_layers.shape
    tn = d
    return pl.pallas_call(
        _ada_kernel,
        grid=(n // tn,),
        in_specs=[pl.BlockSpec((bsz, d), lambda j: (0, 0)),
                  pl.BlockSpec((1, d, tn), lambda j: (layer, 0, j)),
                  pl.BlockSpec((1, 1, tn), lambda j: (layer, 0, j))],
        out_specs=pl.BlockSpec((bsz, tn), lambda j: (0, j)),
        out_shape=jax.ShapeDtypeStruct((bsz, n), F32),
        compiler_params=_params("arbitrary"),
        name="ada",
    )(c, w_layers, b_layers.reshape(depth, 1, n))


_O_RQ, _O_RK, _O_RV, _O_RG, _O_NQ = 0, 512, 1024, 1536, 2048
_O_KV = 2560
_O_NG = 3328
_O_GL = 3840
_PROJ_COLS = 3968


def _rot_ret(z, cos, sin_signed, m):
    partner = jnp.where(m < HEAD_DIM // 2, pltpu.roll(z, LANES - HEAD_DIM // 2, 1),
                        pltpu.roll(z, HEAD_DIM // 2, 1))
    return z * cos + partner * sin_signed


def _rot_nsa(z, cos, sin_signed, m):
    half = ROPE_DIM // 2
    partner = jnp.where(m < half, pltpu.roll(z, LANES - half, 1), pltpu.roll(z, half, 1))
    return z * cos + partner * sin_signed


def _nsa_tables(posf, inv_row, m):
    half = ROPE_DIM // 2
    ang = posf * inv_row
    cos = jnp.where(m < ROPE_DIM, jnp.cos(ang), 1.0)
    s = jnp.sin(ang)
    sin = jnp.where(m < half, -s, jnp.where(m < ROPE_DIM, s, 0.0))
    return cos, sin


def _dup_groups(z, m_lane):
    r = pltpu.roll(z, HEAD_DIM, 1)
    lo = m_lane < HEAD_DIM
    return jnp.where(lo, z, r), jnp.where(lo, r, z)


def _proj_kernel(x_ref, shift_ref, scale_ref, gpre_ref, pos_ref, invr_ref, invn_ref, w_ref,
                 rq_ref, rk_ref, rv_ref, rgs_ref, nq_ref, cmps_ref, ksd_ref, vsd_ref,
                 kwd_ref, vwd_ref, ngs_ref, gl_ref,
                 cr_s, sr_s, cn_s, sn_s, z0_s, z1_s):
    ts = x_ref.shape[1]
    x = x_ref[0]
    ms = jnp.mean(x * x, axis=-1, keepdims=True)
    y = x * lax.rsqrt(ms + EPS) * gpre_ref[...]
    h = y * (1.0 + scale_ref[0]) + shift_ref[0]
    hb = h.astype(BF16)

    posf = pos_ref[0].astype(F32)
    lane = lax.broadcasted_iota(jnp.int32, (ts, LANES), 1)
    m = lane & (HEAD_DIM - 1)
    ang = posf * invr_ref[...]
    cr_s[...] = jnp.cos(ang)
    s = jnp.sin(ang)
    sr_s[...] = jnp.where(m < HEAD_DIM // 2, -s, s)
    cn, sn = _nsa_tables(posf, invn_ref[...], m)
    cn_s[...] = cn
    sn_s[...] = sn

    staging = [z0_s, z1_s]

    def mm(off, width):
        z_s = staging[0]
        staging.reverse()
        z_s[:, :width] = _dot(hb, w_ref[:, off:off + width])
        return lambda c: z_s[:, c * LANES:(c + 1) * LANES]

    scale_q = HEAD_DIM ** -0.5
    tiles = lambda width: [(c, slice(c * LANES, (c + 1) * LANES)) for c in range(width // LANES)]
    z = mm(_O_RQ, RET_WIDTH)
    for c, sl in tiles(RET_WIDTH):
        rq_ref[0, :, sl] = (_rot_ret(z(c), cr_s[...], sr_s[...], m) * scale_q).astype(BF16)
    z = mm(_O_RK, RET_WIDTH)
    for c, sl in tiles(RET_WIDTH):
        rk_ref[0, :, sl] = _rot_ret(z(c), cr_s[...], sr_s[...], m).astype(BF16)
    z = mm(_O_RV, RET_WIDTH)
    for c, sl in tiles(RET_WIDTH):
        rv_ref[0, :, sl] = z(c).astype(BF16)
    z = mm(_O_RG, RET_WIDTH)
    for c, sl in tiles(RET_WIDTH):
        rgs_ref[0, :, sl] = _silu(z(c)).astype(BF16)
    z = mm(_O_NQ, NSA_WIDTH)
    for c, sl in tiles(NSA_WIDTH):
        nq_ref[0, :, sl] = (_rot_nsa(z(c), cn_s[...], sn_s[...], m) * (scale_q * LOG2E)).astype(BF16)

    z = mm(_O_KV, 6 * KV_WIDTH)
    kc0, kc1 = _dup_groups(z(0), lane)
    vc0, vc1 = _dup_groups(z(1), lane)
    for i, t in enumerate((kc0, kc1, vc0, vc1)):
        cmps_ref[0, :, i * LANES:(i + 1) * LANES] = t
    lo = lane < HEAD_DIM
    tok = pl.program_id(1) * ts + lax.broadcasted_iota(jnp.int32, (ts, LANES), 0)
    onehot = jnp.where(lane - HEAD_DIM == (tok >> SLC_SHIFT), 1.0, 0.0)
    ones = jnp.ones((ts, LANES), F32)
    for c, ref, rotate, upper in ((2, ksd_ref, True, onehot), (3, vsd_ref, False, ones),
                                  (4, kwd_ref, True, None), (5, vwd_ref, False, ones)):
        t = _rot_nsa(z(c), cn_s[...], sn_s[...], m) if rotate else z(c)
        if upper is None:
            d0, d1 = _dup_groups(t, lane)
        else:
            d0, d1 = jnp.where(lo, t, upper), jnp.where(lo, pltpu.roll(t, HEAD_DIM, 1), upper)
        ref[0, :, :LANES] = d0.astype(BF16)
        ref[0, :, LANES:] = d1.astype(BF16)

    z = mm(_O_NG, NSA_WIDTH + LANES)
    for c, sl in tiles(NSA_WIDTH):
        ngs_ref[0, :, sl] = _silu(z(c)).astype(BF16)
    gl_ref[0] = jax.nn.sigmoid(z(NSA_WIDTH // LANES))


def _proj(x, shift, scale, g_pre, pos3, inv_ret_row, inv_nsa_row, w_all, ts):
    bsz, seq, d = x.shape
    row = lambda w: pl.BlockSpec((1, ts, w), lambda b, s: (b, s, 0))
    vec = pl.BlockSpec((1, 1, d), lambda b, s: (b, 0, 0))
    const2 = lambda shp: pl.BlockSpec(shp, lambda b, s: (0, 0))
    outs = [(RET_WIDTH, BF16), (RET_WIDTH, BF16), (RET_WIDTH, BF16), (RET_WIDTH, BF16),
            (NSA_WIDTH, BF16), (4 * LANES, F32), (2 * LANES, BF16), (2 * LANES, BF16),
            (2 * LANES, BF16), (2 * LANES, BF16), (NSA_WIDTH, BF16), (LANES, F32)]
    return pl.pallas_call(
        _proj_kernel,
        grid=(bsz, seq // ts),
        in_specs=[row(d), vec, vec, const2((1, d)), row(1), const2((1, LANES)), const2((1, LANES)),
                  const2((d, _PROJ_COLS))],
        out_specs=[row(w) for w, _ in outs],
        out_shape=[jax.ShapeDtypeStruct((bsz, seq, w), dt) for w, dt in outs],
        scratch_shapes=[pltpu.VMEM((ts, LANES), F32)] * 4 + [pltpu.VMEM((ts, 6 * KV_WIDTH), F32)] * 2,
        compiler_params=_params("parallel", "arbitrary"),
        name="proj",
    )(x, shift, scale, g_pre, pos3, inv_ret_row, inv_nsa_row, w_all)


def _ret_kernel(q_ref, k_ref, v_ref, g_ref, dec_ref, zeta_ref, xi_ref, cd_ref, o_ref, r_s):
    rows = q_ref.shape[1]
    cc = RET_CHUNK

    @pl.when(pl.program_id(2) == 0)
    def _():
        r_s[...] = jnp.zeros_like(r_s)

    lane = lax.broadcasted_iota(jnp.int32, (cc, LANES), 1)
    rowi = lax.broadcasted_iota(jnp.int32, (cc, LANES), 0)
    lo = lane < HEAD_DIM
    blockdiag = (rowi < HEAD_DIM) == lo
    inv_n = 1.0 / HEAD_DIM
    chunks = [slice(c * cc, (c + 1) * cc) for c in range(rows // cc)]
    s_ab, kvs = [], []
    for sl in chunks:
        q, k, v = q_ref[0, sl, :], k_ref[0, sl, :], v_ref[0, sl, :]
        zero = jnp.zeros_like(q)
        s_ab.append((_dot_nt(jnp.where(lo, q, zero), k), _dot_nt(jnp.where(lo, zero, q), k)))
        kz = (k.astype(F32) * zeta_ref[0]).astype(BF16)
        kvs.append(jnp.where(blockdiag, _dot_tn(kz, v), 0.0))
    inners = []
    for sl, (s_a, s_b) in zip(chunks, s_ab):
        v = v_ref[0, sl, :]
        inners.append(jnp.where(lo, _dot((s_a * dec_ref[0]).astype(BF16), v),
                                _dot((s_b * dec_ref[1]).astype(BF16), v)))
    r = r_s[...]
    outs = []
    for sl, inner, kv in zip(chunks, inners, kvs):
        outs.append(inner + _dot(q_ref[0, sl, :], r.astype(BF16)) * xi_ref[0])
        r = r * cd_ref[0] + kv
    r_s[...] = r
    head_ones = jnp.where(blockdiag, 1.0, 0.0).astype(BF16)

    def head_sums(z):
        hi = z.astype(BF16)
        lo_part = (z - hi.astype(F32)).astype(BF16)
        return _dot(hi, head_ones) + _dot(lo_part, head_ones)

    devs = [o - s * inv_n for o, s in zip(outs, [head_sums(o) for o in outs])]
    variances = [head_sums(d * d) * inv_n for d in devs]
    for sl, d, var in zip(chunks, devs, variances):
        o_ref[0, sl, :] = (d * lax.rsqrt(var + GN_EPS) * g_ref[0, sl, :]).astype(o_ref.dtype)


def _retention(rq, rk, rv, rgs, dec, zeta_l, xi_l, cd_l, rows):
    bsz, seq, _ = rq.shape
    npair = RET_HEADS // 2
    blk = pl.BlockSpec((1, rows, LANES), lambda b, p, s: (b, s, p))
    return pl.pallas_call(
        _ret_kernel,
        grid=(bsz, npair, seq // rows),
        in_specs=[blk, blk, blk, blk,
                  pl.BlockSpec((2, RET_CHUNK, RET_CHUNK), lambda b, p, s: (p, 0, 0)),
                  pl.BlockSpec((1, RET_CHUNK, LANES), lambda b, p, s: (p, 0, 0)),
                  pl.BlockSpec((1, RET_CHUNK, LANES), lambda b, p, s: (p, 0, 0)),
                  pl.BlockSpec((1, 1, LANES), lambda b, p, s: (p, 0, 0))],
        out_specs=blk,
        out_shape=jax.ShapeDtypeStruct((bsz, seq, RET_WIDTH), BF16),
        scratch_shapes=[pltpu.VMEM((LANES, LANES), F32)],
        compiler_params=_params("parallel", "parallel", "arbitrary"),
        name="retention",
    )(rq, rk, rv, rgs, dec, zeta_l, xi_l, cd_l)


def _cmp_kernel(src_ref, pe_ref, w1_ref, w2_ref, pos_ref, invn_ref, o_ref):
    ncp = o_ref.shape[3]
    pairs = CMP_BLOCK // 4
    lo = lax.broadcasted_iota(jnp.int32, (ncp, LANES), 1) < HEAD_DIM
    p_lo = jnp.zeros((ncp, CMP_HIDDEN), F32)
    p_hi = jnp.zeros((ncp, CMP_HIDDEN), F32)
    for m in range(pairs):
        t = jnp.where(lo, src_ref[0, pl.ds(2 * m, ncp, stride=CMP_STRIDE), :],
                      src_ref[0, pl.ds(2 * m + 1, ncp, stride=CMP_STRIDE), :])
        p_lo = p_lo + _dot((t + pe_ref[0, m:m + 1, :]).astype(BF16), w1_ref[0, m])
        p_hi = p_hi + _dot((t + pe_ref[0, pairs + m:pairs + m + 1, :]).astype(BF16), w1_ref[0, pairs + m])
    hidden = p_lo + pltpu.roll(p_hi, ncp - 1, 0)
    comp = _dot(_silu(hidden).astype(BF16), w2_ref[0])
    m = _lane_in_head(comp.shape)
    cos, sin = _nsa_tables(pos_ref[0].astype(F32), invn_ref[...], m)
    rotated = _rot_nsa(comp, cos, sin, m)
    is_key = pl.program_id(1) == 0
    o_ref[0, 0, 0] = jnp.where(is_key, rotated, comp).astype(BF16)


def _compress(cmps, pe2, w1p, w2d, pos_cmp, inv_nsa_row):
    bsz, seq, _ = cmps.shape
    ncp = seq // CMP_STRIDE
    g = NSA_KV_GROUPS
    return pl.pallas_call(
        _cmp_kernel,
        grid=(bsz, 2, g),
        in_specs=[pl.BlockSpec((1, seq, LANES), lambda b, j, gi: (b, 0, j * g + gi)),
                  pl.BlockSpec((1, CMP_BLOCK // 2, LANES), lambda b, j, gi: (j, 0, 0)),
                  pl.BlockSpec((1, CMP_BLOCK // 2, LANES, CMP_HIDDEN), lambda b, j, gi: (j, 0, 0, 0)),
                  pl.BlockSpec((1, CMP_HIDDEN, LANES), lambda b, j, gi: (j, 0, 0)),
                  pl.BlockSpec((1, ncp, 1), lambda b, j, gi: (b, 0, 0)),
                  pl.BlockSpec((1, LANES), lambda b, j, gi: (0, 0))],
        out_specs=pl.BlockSpec((1, 1, 1, ncp, LANES), lambda b, j, gi: (b, j, gi, 0, 0)),
        out_shape=jax.ShapeDtypeStruct((bsz, 2, g, ncp, LANES), BF16),
        compiler_params=_params("parallel", "arbitrary", "arbitrary"),
        name="compress",
    )(cmps, pe2, w1p, w2d, pos_cmp, inv_nsa_row)


def _cmpsel_kernel(q_ref, kc_ref, vc_ref, ovt_ref, o_ref, selt_ref,
                   qs0_s, qs1_s, s0_s, s1_s, p0_s, p1_s, hi0_s, hi1_s, lo0_s, lo1_s, score_s, *, tq):
    nh = HEADS_PER_GROUP
    sub = 8
    n_qt = q_ref.shape[1] // tq
    ncp = kc_ref.shape[3]
    nblk = ovt_ref.shape[0]
    kc = kc_ref[0, 0, 0]
    vc = vc_ref[0, 0, 0]
    qs_b, s_b, p_b, hi_b, lo_b = (qs0_s, qs1_s), (s0_s, s1_s), (p0_s, p1_s), (hi0_s, hi1_s), (lo0_s, lo1_s)

    def rows_of(i):
        return slice(i * tq, (i + 1) * tq)

    def last_block(i):
        return ((i + 1) * tq - 1) >> SLC_SHIFT

    def n_rows(i):
        return min(nblk, -(-(last_block(i) + 1) // sub) * sub)

    def stack(i, slot):
        _stack_head_queries(q_ref, qs_b[slot], tq, src=rows_of(i))

    def scores(i, slot):
        s_b[slot][...] = _dot_nt(qs_b[slot][...], kc)

    def softmax(i, slot):
        tok = i * tq + lax.broadcasted_iota(jnp.int32, (tq, LANES), 0)
        lane = lax.broadcasted_iota(jnp.int32, (tq, LANES), 1)
        biases = [jnp.where((lane + c * LANES) * CMP_STRIDE + (CMP_BLOCK - 1) <= tok, 0.0, NEG)
                  for c in range(ncp // LANES)]
        has_key = tok >= CMP_BLOCK - 1
        psum = None
        for h in range(nh):
            rows = slice(h * tq, (h + 1) * tq)
            tiles = [t + b for b, t in zip(biases, _lane_tiles(s_b[slot][rows, :]))]
            mx = _row_stat(tiles, jnp.maximum, jnp.max)
            es = [jnp.exp2(t - mx) for t in tiles]
            inv = jnp.where(has_key, 1.0 / _row_stat(es, jnp.add, jnp.sum), 0.0)
            ps = [e * inv for e in es]
            psum = ps if psum is None else [a + b for a, b in zip(psum, ps)]
            p_b[slot][rows, :] = jnp.concatenate(ps, axis=1).astype(BF16)
        psum = jnp.concatenate(psum, axis=1)
        hi = psum.astype(BF16)
        hi_b[slot][...] = hi
        lo_b[slot][...] = (psum - hi.astype(F32)).astype(BF16)

    def values(i, slot):
        pv = _dot(p_b[slot][...], vc)
        _store_heads(o_ref, [pv[h * tq:(h + 1) * tq, :] for h in range(nh)], tq, rows_of(i))
        ov = ovt_ref[0:n_rows(i), :]
        return _dot_nt(ov, hi_b[slot][...]) + _dot_nt(ov, lo_b[slot][...])

    def select(i, imp):
        nr = n_rows(i)
        blk = lax.broadcasted_iota(jnp.int32, (nr, tq), 0)
        cur = (i * tq + lax.broadcasted_iota(jnp.int32, (nr, tq), 1)) >> SLC_SHIFT
        forced = (blk == 0) | (blk == cur) | (blk == cur - 1)
        score_s[0:nr, :] = jnp.where(forced, BIG, jnp.where(blk <= cur, imp, NEG))
        row_in_group = lax.broadcasted_iota(jnp.int32, (sub, tq), 0)
        scs = [score_s[r * sub:(r + 1) * sub, :] for r in range(nr // sub)]
        cnts = [jnp.zeros((sub, tq), F32) for _ in scs]
        for src in range(last_block(i) + 1):
            ri = jnp.broadcast_to(score_s[src:src + 1, :], (sub, tq))
            for r, sc in enumerate(scs):
                if r * sub > src:
                    beats = ri >= sc
                elif (r + 1) * sub - 1 <= src:
                    beats = ri > sc
                else:
                    beats = (ri > sc) | ((ri == sc) & (row_in_group > src - r * sub))
                cnts[r] = cnts[r] + jnp.where(beats, 1.0, 0.0)
        sels = [jnp.where((cnt < float(min(SLC_TOPK, nblk))) & (sc > 0.5 * NEG), 1.0, 0.0)
                for cnt, sc in zip(cnts, scs)]
        pads = [jnp.zeros((HEAD_DIM, tq), F32)] + sels
        if nr < HEAD_DIM:
            pads.append(jnp.zeros((HEAD_DIM - nr, tq), F32))
        selt_ref[0, 0, rows_of(i), :] = jnp.concatenate(pads, axis=0).T.astype(selt_ref.dtype)

    stack(0, 0)
    scores(0, 0)
    if n_qt > 1:
        stack(1, 1)
    for i in range(n_qt):
        slot = i % 2
        imp = values(i - 1, 1 - slot) if i >= 1 else None
        if i + 1 < n_qt:
            scores(i + 1, 1 - slot)
        if i + 2 < n_qt:
            stack(i + 2, slot)
        softmax(i, slot)
        if i >= 1:
            select(i - 1, imp)
    select(n_qt - 1, values(n_qt - 1, (n_qt - 1) % 2))


def _cmpsel(nq, cmpkv, ovt, tq):
    bsz, seq, _ = nq.shape
    g = NSA_KV_GROUPS
    ncp = cmpkv.shape[3]
    nblk = ovt.shape[0]
    gw = HEADS_PER_GROUP * HEAD_DIM
    rows = HEADS_PER_GROUP * tq
    assert nblk <= HEAD_DIM and nblk % 8 == 0
    return pl.pallas_call(
        functools.partial(_cmpsel_kernel, tq=tq),
        grid=(bsz, g),
        in_specs=[pl.BlockSpec((1, seq, gw), lambda b, gi: (b, 0, gi)),
                  pl.BlockSpec((1, 1, 1, ncp, LANES), lambda b, gi: (b, 0, gi, 0, 0)),
                  pl.BlockSpec((1, 1, 1, ncp, LANES), lambda b, gi: (b, 1, gi, 0, 0)),
                  pl.BlockSpec((nblk, ncp), lambda b, gi: (0, 0))],
        out_specs=[pl.BlockSpec((1, seq, gw), lambda b, gi: (b, 0, gi)),
                   pl.BlockSpec((1, 1, seq, LANES), lambda b, gi: (b, gi, 0, 0))],
        out_shape=[jax.ShapeDtypeStruct((bsz, seq, NSA_WIDTH), BF16),
                   jax.ShapeDtypeStruct((bsz, g, seq, LANES), BF16)],
        scratch_shapes=([pltpu.VMEM((rows, LANES), BF16)] * 2 + [pltpu.VMEM((rows, ncp), F32)] * 2 +
                        [pltpu.VMEM((rows, ncp), BF16)] * 2 + [pltpu.VMEM((tq, ncp), BF16)] * 4 +
                        [pltpu.VMEM((nblk, tq), F32)]),
        compiler_params=_params("parallel", "arbitrary"),
        name="cmpsel",
    )(nq, cmpkv, cmpkv, ovt)


def _stack_head_queries(q_ref, qs_s, rows, src=None):
    src = slice(0, rows) if src is None else src
    lo = lax.broadcasted_iota(jnp.int32, (rows, LANES), 1) < HEAD_DIM
    for c in range(HEADS_PER_GROUP // 2):
        q = q_ref[0, src, c * LANES:(c + 1) * LANES]
        zero = jnp.zeros_like(q)
        qs_s[(2 * c) * rows:(2 * c + 1) * rows, :] = jnp.where(lo, q, zero)
        qs_s[(2 * c + 1) * rows:(2 * c + 2) * rows, :] = jnp.where(lo, zero, q)


def _store_heads(o_ref, outs, tq, rows=slice(None)):
    lo = lax.broadcasted_iota(jnp.int32, (tq, LANES), 1) < HEAD_DIM
    for c in range(HEADS_PER_GROUP // 2):
        o_ref[0, rows, c * LANES:(c + 1) * LANES] = jnp.where(lo, outs[2 * c], outs[2 * c + 1]).astype(o_ref.dtype)


def _store_heads_normalized(o_ref, accs, tq, rows=slice(None)):
    lo = lax.broadcasted_iota(jnp.int32, (tq, LANES), 1) < HEAD_DIM
    for c in range(HEADS_PER_GROUP // 2):
        a, b = accs[2 * c], accs[2 * c + 1]
        num = jnp.where(lo, a, pltpu.roll(b, HEAD_DIM, 1))
        den = jnp.where(lo, pltpu.roll(a, HEAD_DIM, 1), b)
        o_ref[0, rows, c * LANES:(c + 1) * LANES] = (num * (1.0 / den)).astype(o_ref.dtype)


def _lane_tiles(z):
    return [z[:, c * LANES:(c + 1) * LANES] for c in range(z.shape[1] // LANES)]


def _row_stat(tiles, combine, reduce):
    acc = functools.reduce(combine, tiles)
    return jnp.broadcast_to(reduce(acc, axis=-1, keepdims=True), acc.shape)


WIN_UNROLL = 8


def _win_kernel(q_ref, k_ref, v_ref, o_ref, qs0_s, qs1_s, s0_s, s1_s, p0_s, p1_s, *, tq):
    nh = HEADS_PER_GROUP
    span = WINDOW + tq
    n_qt = q_ref.shape[1] // tq
    qs_b, s_b, p_b = (qs0_s, qs1_s), (s0_s, s1_s), (p0_s, p1_s)

    def q_rows(i):
        return pl.ds(pl.multiple_of(i * tq, tq), tq)

    def k_rows(i):
        return pl.ds(pl.multiple_of(jnp.maximum(i * tq - WINDOW, 0), tq), span)

    def stack_queries(i, slot):
        _stack_head_queries(q_ref, qs_b[slot], tq, src=q_rows(jnp.minimum(i, n_qt - 1)))

    def scores(i, slot):
        i = jnp.minimum(i, n_qt - 1)
        s_b[slot][...] = _dot_nt(qs_b[slot][...], k_ref[0, k_rows(i), :])

    def values(i, slot):
        i = jnp.maximum(i, 0)
        pv = _dot(p_b[slot][...], v_ref[0, k_rows(i), :])
        _store_heads_normalized(o_ref, [pv[h * tq:(h + 1) * tq, :] for h in range(nh)], tq, q_rows(i))

    def softmax(i, slot):
        tok = i * tq + lax.broadcasted_iota(jnp.int32, (tq, span), 0)
        key = jnp.maximum(i * tq - WINDOW, 0) + lax.broadcasted_iota(jnp.int32, (tq, span), 1)
        bias = jnp.where((key <= tok) & (tok - key < WINDOW), 0.0, NEG)
        for h in range(nh):
            rows = slice(h * tq, (h + 1) * tq)
            tiles = _lane_tiles(s_b[slot][rows, :] + bias)
            m = _row_stat(tiles, jnp.maximum, jnp.max)
            p_b[slot][rows, :] = jnp.concatenate([jnp.exp2(t - m) for t in tiles], axis=1).astype(BF16)

    def stages(first, count):
        for t in range(count):
            i, slot = first + t, t % 2
            values(i - 1, 1 - slot)
            scores(i + 1, 1 - slot)
            stack_queries(i + 2, slot)
            softmax(i, slot)

    p1_s[...] = jnp.ones_like(p1_s)
    stack_queries(0, 0)
    stack_queries(1, 1)
    scores(0, 0)

    def body(b, carry):
        stages(b * WIN_UNROLL, WIN_UNROLL)
        return carry

    lax.fori_loop(0, n_qt // WIN_UNROLL, body, 0)
    stages(n_qt - n_qt % WIN_UNROLL, n_qt % WIN_UNROLL)
    values(n_qt - 1, (n_qt - 1) % 2)


SLC_UNROLL = 8
SLC_STATE_BUFFERS = SLC_UNROLL + 1
assert SLC_UNROLL % 2 == 0


def _slc_kernel(q_ref, k_ref, v_ref, selt_ref, o_ref, qs_s, s0_s, s1_s, p0_s, p1_s, a0_s, a1_s,
                m_s, acc_s, *, tq, tk):
    nh = HEADS_PER_GROUP
    seq = q_ref.shape[1]
    n_qt = seq // tq
    s_b, p_b, a_b = (s0_s, s1_s), (p0_s, p1_s), (a0_s, a1_s)
    lo = lax.broadcasted_iota(jnp.int32, (tq, LANES), 1) < HEAD_DIM

    def n_tiles(qi):
        return (qi * tq + tq - 1) // tk + 1

    n_items = sum((qi * tq + tq - 1) // tk + 1 for qi in range(n_qt))

    def advance(qi, j):
        wrap = j + 1 >= n_tiles(qi)
        return jnp.where(wrap, jnp.minimum(qi + 1, n_qt - 1), qi), jnp.where(wrap, 0, j + 1)

    def q_rows(qi):
        return pl.ds(pl.multiple_of(qi * tq, tq), tq)

    def k_rows(j):
        return pl.ds(pl.multiple_of(j * tk, tk), tk)

    def stack_queries(qi):
        par = qi % SLC_STATE_BUFFERS
        unselected = jnp.where(selt_ref[0, 0, q_rows(qi), :].astype(F32) > 0.5, 0.0, NEG)
        for c in range(nh // 2):
            q = q_ref[0, q_rows(qi), c * LANES:(c + 1) * LANES].astype(F32)
            qs_s[par, (2 * c) * tq:(2 * c + 1) * tq, :] = jnp.where(lo, q, unselected).astype(BF16)
            qs_s[par, (2 * c + 1) * tq:(2 * c + 2) * tq, :] = jnp.where(
                lo, pltpu.roll(q, HEAD_DIM, 1), unselected).astype(BF16)

    def scores(qi, j, slot):
        s_b[slot][...] = _dot_nt(qs_s[qi % SLC_STATE_BUFFERS], k_ref[0, k_rows(j), :])

    def values(qi, j, slot):
        pv = _dot(p_b[slot][...], v_ref[0, k_rows(j), :])
        par = qi % SLC_STATE_BUFFERS
        for h in range(nh):
            acc_s[par, h] = a_b[slot][h] * acc_s[par, h] + pv[h * tq:(h + 1) * tq, :]

    def softmax(qi, j, slot):
        tok = qi * tq + lax.broadcasted_iota(jnp.int32, (tq, tk), 0)
        key = j * tk + lax.broadcasted_iota(jnp.int32, (tq, tk), 1)
        bias = jnp.where(key <= tok, 0.0, NEG)
        for h in range(nh):
            rows = slice(h * tq, (h + 1) * tq)
            tiles = _lane_tiles(s_b[slot][rows, :] + bias)
            m_prev = jnp.where(j == 0, NEG, m_s[h])
            m_new = jnp.maximum(m_prev, _row_stat(tiles, jnp.maximum, jnp.max))
            m_s[h] = m_new
            a_b[slot][h] = jnp.exp2(m_prev - m_new)
            p_b[slot][rows, :] = jnp.concatenate([jnp.exp2(t - m_new) for t in tiles], axis=1).astype(BF16)

    def finalize(qi):
        _store_heads_normalized(o_ref, [acc_s[qi % SLC_STATE_BUFFERS, h] for h in range(nh)], tq, q_rows(qi))

    def is_last(qi, j):
        return (j + 1 >= n_tiles(qi)).astype(jnp.int32)

    def finalize_if(flag, qi):
        @pl.when(flag == 1)
        def _():
            finalize(qi)

    def block(carry, n_stages):
        fin, (qp, jp, p_last), (qc, jc), (qn, jn) = carry
        for flag, qi in fin:
            finalize_if(flag, qi)
        items = [(qp, jp, p_last), (qc, jc, is_last(qc, jc)), (qn, jn, is_last(qn, jn))]
        for _ in range(n_stages - 1):
            qx, jx = advance(items[-1][0], items[-1][1])
            items.append((qx, jx, is_last(qx, jx)))
        for qx, jx, _ in items[2:]:
            @pl.when(jx == 0)
            def _():
                stack_queries(qx)
        for t in range(n_stages):
            slot = t % 2
            values(items[t][0], items[t][1], 1 - slot)
            scores(items[t + 2][0], items[t + 2][1], 1 - slot)
            softmax(items[t + 1][0], items[t + 1][1], slot)
        fin = tuple((items[t][2], items[t][0]) for t in range(n_stages))
        nxt = advance(items[-1][0], items[-1][1])
        return fin, items[n_stages], items[n_stages + 1][:2], nxt

    m_s[...] = jnp.full_like(m_s, NEG)
    acc_s[...] = jnp.zeros_like(acc_s)
    p1_s[...] = jnp.zeros_like(p1_s)
    a1_s[...] = jnp.ones_like(a1_s)
    zero = jnp.int32(0)
    stack_queries(zero)
    scores(zero, zero, 0)
    unroll = SLC_UNROLL
    carry = (((zero, zero),) * unroll, (zero, zero, zero), (zero, zero), advance(zero, zero))
    carry = lax.fori_loop(0, n_items // unroll, lambda i, c: block(c, unroll), carry)
    if n_items % unroll:
        carry = block(carry, n_items % unroll)
    fin, (qp, jp, _), _, _ = carry
    for flag, qi in fin:
        finalize_if(flag, qi)
    values(qp, jp, (n_items - 1) % 2)
    finalize(qp)


def _slc_attention(nq, kd, vd, selt, *, tq, tk):
    bsz, seq, _ = nq.shape
    g = NSA_KV_GROUPS
    gw = HEADS_PER_GROUP * HEAD_DIM
    nh = HEADS_PER_GROUP
    assert seq % tk == 0 and tk % tq == 0
    assert seq // SLC_BLOCK <= HEAD_DIM
    whole = lambda w: pl.BlockSpec((1, seq, w), lambda b, gi: (b, 0, gi))
    stat = pltpu.VMEM((nh, tq, LANES), F32)
    state = pltpu.VMEM((SLC_STATE_BUFFERS, nh, tq, LANES), F32)
    return pl.pallas_call(
        functools.partial(_slc_kernel, tq=tq, tk=tk),
        grid=(bsz, g),
        in_specs=[whole(gw), whole(LANES), whole(LANES),
                  pl.BlockSpec((1, 1, seq, LANES), lambda b, gi: (b, gi, 0, 0))],
        out_specs=whole(gw),
        out_shape=jax.ShapeDtypeStruct((bsz, seq, NSA_WIDTH), BF16),
        scratch_shapes=([pltpu.VMEM((SLC_STATE_BUFFERS, nh * tq, LANES), BF16)] + [pltpu.VMEM((nh * tq, tk), F32)] * 2 +
                        [pltpu.VMEM((nh * tq, tk), BF16)] * 2 + [stat] * 3 + [state]),
        compiler_params=_params("parallel", "arbitrary"),
        name="slc_attn",
    )(nq, kd, vd, selt)


def _win_attention(nq, kd, vd, *, tq):
    bsz, seq, _ = nq.shape
    g = NSA_KV_GROUPS
    gw = HEADS_PER_GROUP * HEAD_DIM
    nh = HEADS_PER_GROUP
    assert WINDOW % tq == 0 and seq >= WINDOW + tq and seq // tq >= 2 and WIN_UNROLL % 2 == 0
    whole = lambda w: pl.BlockSpec((1, seq, w), lambda b, gi: (b, 0, gi))
    span = WINDOW + tq
    return pl.pallas_call(
        functools.partial(_win_kernel, tq=tq),
        grid=(bsz, g),
        in_specs=[whole(gw), whole(LANES), whole(LANES)],
        out_specs=whole(gw),
        out_shape=jax.ShapeDtypeStruct((bsz, seq, NSA_WIDTH), BF16),
        scratch_shapes=([pltpu.VMEM((nh * tq, LANES), BF16)] * 2 + [pltpu.VMEM((nh * tq, span), F32)] * 2 +
                        [pltpu.VMEM((nh * tq, span), BF16)] * 2),
        compiler_params=_params("parallel", "arbitrary"),
        name="win_attn",
    )(nq, kd, vd)


def _out_kernel(x_ref, gate_ref, gpost_ref, ret_ref, oc_ref, os_ref, ow_ref, gl_ref, ngs_ref,
                wo_ref, o_ref):
    ts = x_ref.shape[1]
    gates = gl_ref[0]
    lo_off = 32
    assert 3 * NSA_HEADS <= lo_off
    g_hi = gates.astype(BF16).astype(F32)
    g_lo = (gates - g_hi).astype(BF16).astype(F32)
    lane = lax.broadcasted_iota(jnp.int32, (ts, LANES), 1)
    g_split = jnp.where(lane < lo_off, g_hi, pltpu.roll(g_lo, lo_off, 1)).astype(BF16)
    src = lax.broadcasted_iota(jnp.int32, (LANES, NSA_WIDTH), 0)
    head = lax.broadcasted_iota(jnp.int32, (LANES, NSA_WIDTH), 1) >> HEAD_SHIFT
    att = jnp.zeros((ts, NSA_WIDTH), F32)
    for i, br in enumerate((oc_ref, os_ref, ow_ref)):
        expand = jnp.where((src == 3 * head + i) | (src == lo_off + 3 * head + i), 1.0, 0.0).astype(BF16)
        att = att + _dot(g_split, expand) * br[0]
    att = att * ngs_ref[0]
    y = _dot(ret_ref[0].astype(BF16), wo_ref[:RET_WIDTH, :]) + _dot(att.astype(BF16), wo_ref[RET_WIDTH:, :])
    ms = jnp.mean(y * y, axis=-1, keepdims=True)
    y = y * lax.rsqrt(ms + EPS) * gpost_ref[...]
    o_ref[0] = x_ref[0] + gate_ref[0] * y


def _out(x, gate, g_post, ret, o_cmp, o_slc, o_win, gl, ngs, w_out, ts):
    bsz, seq, d = x.shape
    row = lambda w: pl.BlockSpec((1, ts, w), lambda b, s: (b, s, 0))
    return pl.pallas_call(
        _out_kernel,
        grid=(bsz, seq // ts),
        in_specs=[row(d), pl.BlockSpec((1, 1, d), lambda b, s: (b, 0, 0)),
                  pl.BlockSpec((1, d), lambda b, s: (0, 0)),
                  row(RET_WIDTH), row(NSA_WIDTH), row(NSA_WIDTH), row(NSA_WIDTH), row(LANES),
                  row(NSA_WIDTH), pl.BlockSpec(w_out.shape, lambda b, s: (0, 0))],
        out_specs=row(d),
        out_shape=jax.ShapeDtypeStruct((bsz, seq, d), x.dtype),
        compiler_params=_params("parallel", "arbitrary"),
        name="out",
    )(x, gate, g_post, ret, o_cmp, o_slc, o_win, gl, ngs, w_out)


def _retention_tables():
    h = RET_HEADS
    c = RET_CHUNK
    log_g = jnp.log1p(-jnp.power(2.0, -5.0 - jnp.arange(h, dtype=F32)))
    idx = jnp.arange(c, dtype=F32)
    diff = idx[:, None] - idx[None, :]
    dec = jnp.where(diff[None] >= 0, jnp.exp(jnp.maximum(diff, 0.0)[None] * log_g[:, None, None]), 0.0)
    zeta = jnp.exp((c - 1 - idx)[None, :] * log_g[:, None])
    xi = jnp.exp((idx + 1.0)[None, :] * log_g[:, None])
    cd = jnp.exp(c * log_g)
    to_lanes = lambda t: jnp.repeat(t.reshape(h // 2, 2, c).transpose(0, 2, 1), HEAD_DIM, axis=2)
    cd_l = jnp.repeat(cd.reshape(h // 2, 1, 2), HEAD_DIM, axis=2)
    return dec, to_lanes(zeta), to_lanes(xi), cd_l


def _overlap_t(seq):
    ncp = seq // CMP_STRIDE
    nc = (seq - CMP_BLOCK) // CMP_STRIDE + 1
    ns = seq // SLC_BLOCK
    cs = np.arange(ncp) * CMP_STRIDE
    ce = cs + CMP_BLOCK - 1
    ss = np.arange(ns) * SLC_BLOCK
    ov = (cs[None, :] <= ss[:, None] + SLC_BLOCK - 1) & (ce[None, :] >= ss[:, None]) & (np.arange(ncp) < nc)[None, :]
    return jnp.asarray(ov.astype(np.float32), dtype=BF16)


def _layer(x, c, positions, mod, g_pre, g_post, w_in, w_out,
           pe_k, w1_k, w2_k, pe_v, w1_v, w2_v):
    bsz, seq, d = x.shape
    shift, scale, gate = [t.reshape(bsz, 1, d) for t in jnp.split(mod, 3, axis=-1)]

    offs = np.cumsum((RET_WIDTH,) * 4 + (NSA_WIDTH, 6 * KV_WIDTH, 3 * NSA_HEADS, NSA_WIDTH))
    w_gl = w_in[:, offs[5]:offs[6]]
    w_all = jnp.concatenate(
        [w_in[:, :offs[5]], w_in[:, offs[6]:], w_gl,
         jnp.zeros((d, LANES - w_gl.shape[1]), w_in.dtype)], axis=1).astype(BF16)

    lanes = jnp.arange(LANES)
    half_r = HEAD_DIM // 2
    inv_r = jnp.power(RET_ROPE_BASE, -jnp.arange(half_r, dtype=F32) / half_r)
    half_n = ROPE_DIM // 2
    inv_n = jnp.power(NSA_ROPE_BASE, -jnp.arange(half_n, dtype=F32) / half_n)
    inv_ret_row = inv_r[lanes % half_r].reshape(1, LANES)
    inv_nsa_row = inv_n[lanes % half_n].reshape(1, LANES)
    pos3 = positions.reshape(bsz, seq, 1)

    (rq, rk, rv, rgs, nq, cmps, ksd, vsd, kwd, vwd, ngs, gl) = _proj(
        x, shift, scale, g_pre.reshape(1, d), pos3, inv_ret_row, inv_nsa_row, w_all, ts=min(PROJ_ROWS, seq))

    dec, zeta_l, xi_l, cd_l = _retention_tables()
    ret = _retention(rq, rk, rv, rgs, dec, zeta_l, xi_l, cd_l, rows=min(RET_ROWS, seq))

    ncp = seq // CMP_STRIDE
    nc = (seq - CMP_BLOCK) // CMP_STRIDE + 1
    pe2 = jnp.stack([pe_k, pe_v]).reshape(2, CMP_BLOCK // 2, LANES)
    w1p = jnp.stack([w1_k, w1_v]).reshape(2, CMP_BLOCK // 2, LANES, CMP_HIDDEN).astype(BF16)
    w2s = jnp.stack([w2_k, w2_v])
    w2d = jnp.concatenate([w2s, w2s], axis=-1).astype(BF16)
    cmp_end = np.arange(nc) * CMP_STRIDE + CMP_BLOCK - 1
    pos_cmp = jnp.pad(positions[:, cmp_end], ((0, 0), (0, ncp - nc))).reshape(bsz, ncp, 1)
    cmpkv = _compress(cmps, pe2, w1p, w2d, pos_cmp, inv_nsa_row)

    o_cmp, selt = _cmpsel(nq, cmpkv, _overlap_t(seq), tq=min(CMPSEL_TQ, seq))
    o_slc = _slc_attention(nq, ksd, vsd, selt, tq=SLC_TQ, tk=SLC_TK)
    o_win = _win_attention(nq, kwd, vwd, tq=WIN_TQ)

    return _out(x, gate, g_post.reshape(1, d), ret, o_cmp, o_slc, o_win, gl, ngs,
                w_out.astype(BF16), ts=min(PROJ_ROWS, seq))


def kernel(x, c, positions, w_ada, b_ada, g_pre, g_post, w_in, w_out, cmp_pe_k, cmp_w1_k, cmp_w2_k, cmp_pe_v, cmp_w1_v, cmp_w2_v):
    for l in range(w_in.shape[0]):
        x = _layer(x, c, positions, _ada(c, w_ada, b_ada, l), g_pre[l], g_post[l], w_in[l], w_out[l],
                   cmp_pe_k[l], cmp_w1_k[l], cmp_w2_k[l], cmp_pe_v[l], cmp_w1_v[l], cmp_w2_v[l])
    return x
```

```python
import functools

import numpy as np
import jax
import jax.numpy as jnp
from jax import lax
from jax.experimental import pallas as pl
from jax.experimental.pallas import tpu as pltpu

F32 = jnp.float32
BF16 = jnp.bfloat16

LANES = 128
HEAD_DIM = 64
RET_HEADS = 8
NSA_HEADS = 8
NSA_KV_GROUPS = 2
HEADS_PER_GROUP = NSA_HEADS // NSA_KV_GROUPS
RET_WIDTH = RET_HEADS * HEAD_DIM
NSA_WIDTH = NSA_HEADS * HEAD_DIM
KV_WIDTH = NSA_KV_GROUPS * HEAD_DIM
RET_CHUNK = 128
RET_ROPE_BASE = 10000.0
NSA_ROPE_BASE = 500000.0
ROPE_DIM = HEAD_DIM // 4
CMP_BLOCK = 32
CMP_STRIDE = 16
CMP_HIDDEN = 256
SLC_BLOCK = 64
SLC_TOPK = 16
WINDOW = 512
NEG = -1e30
LOG2E = 1.4426950408889634
BIG = 1e9
EPS = 1e-6
GN_EPS = 1e-5
VMEM_LIMIT = 48 * 1024 * 1024

PROJ_ROWS = 512
RET_ROWS = 1024
CMPSEL_TQ = 256
SLC_TQ, SLC_TK = 256, 512
WIN_TQ = 128

NT_DIMS = (((1,), (1,)), ((), ()))
TN_DIMS = (((0,), (0,)), ((), ()))


def _dot(a, b):
    return jnp.dot(a, b, preferred_element_type=F32)


def _dot_nt(a, b):
    return lax.dot_general(a, b, NT_DIMS, preferred_element_type=F32)


def _dot_tn(a, b):
    return lax.dot_general(a, b, TN_DIMS, preferred_element_type=F32)


def _silu(z):
    return z * jax.nn.sigmoid(z)


def _params(*sem):
    return pltpu.CompilerParams(dimension_semantics=sem, vmem_limit_bytes=VMEM_LIMIT)


HEAD_SHIFT = 6
SLC_SHIFT = 6
assert 1 << HEAD_SHIFT == HEAD_DIM and 1 << SLC_SHIFT == SLC_BLOCK


def _lane_in_head(shape):
    return lax.broadcasted_iota(jnp.int32, shape, len(shape) - 1) & (HEAD_DIM - 1)


def _ada_kernel(c_ref, w_ref, b_ref, o_ref):
    a = _silu(c_ref[...])
    o_ref[...] = jnp.dot(a, w_ref[0], precision=lax.Precision.HIGHEST,
                         preferred_element_type=F32) + b_ref[0]


def _ada(c, w_layers, b_layers, layer):
    bsz, d = c.shape
    depth, _, n = w_layers.shape
    tn = d
    return pl.pallas_call(
        _ada_kernel,
        grid=(n // tn,),
        in_specs=[pl.BlockSpec((bsz, d), lambda j: (0, 0)),
                  pl.BlockSpec((1, d, tn), lambda j: (layer, 0, j)),
                  pl.BlockSpec((1, 1, tn), lambda j: (layer, 0, j))],
        out_specs=pl.BlockSpec((bsz, tn), lambda j: (0, j)),
        out_shape=jax.ShapeDtypeStruct((bsz, n), F32),
        compiler_params=_params("arbitrary"),
        name="ada",
    )(c, w_layers, b_layers.reshape(depth, 1, n))


_O_RQ, _O_RK, _O_RV, _O_RG, _O_NQ = 0, 512, 1024, 1536, 2048
_O_KV = 2560
_O_NG = 3328
_O_GL = 3840
_PROJ_COLS = 3968


def _rot_ret(z, cos, sin_signed, m):
    partner = jnp.where(m < HEAD_DIM // 2, pltpu.roll(z, LANES - HEAD_DIM // 2, 1),
                        pltpu.roll(z, HEAD_DIM // 2, 1))
    return z * cos + partner * sin_signed


def _rot_nsa(z, cos, sin_signed, m):
    half = ROPE_DIM // 2
    partner = jnp.where(m < half, pltpu.roll(z, LANES - half, 1), pltpu.roll(z, half, 1))
    return z * cos + partner * sin_signed


def _nsa_tables(posf, inv_row, m):
    half = ROPE_DIM // 2
    ang = posf * inv_row
    cos = jnp.where(m < ROPE_DIM, jnp.cos(ang), 1.0)
    s = jnp.sin(ang)
    sin = jnp.where(m < half, -s, jnp.where(m < ROPE_DIM, s, 0.0))
    return cos, sin


def _dup_groups(z, m_lane):
    r = pltpu.roll(z, HEAD_DIM, 1)
    lo = m_lane < HEAD_DIM
    return jnp.where(lo, z, r), jnp.where(lo, r, z)


def _proj_kernel(x_ref, shift_ref, scale_ref, gpre_ref, pos_ref, invr_ref, invn_ref, w_ref,
                 rq_ref, rk_ref, rv_ref, rgs_ref, nq_ref, cmps_ref, ksd_ref, vsd_ref,
                 kwd_ref, vwd_ref, ngs_ref, gl_ref,
                 cr_s, sr_s, cn_s, sn_s, z0_s, z1_s):
    ts = x_ref.shape[1]
    x = x_ref[0]
    ms = jnp.mean(x * x, axis=-1, keepdims=True)
    y = x * lax.rsqrt(ms + EPS) * gpre_ref[...]
    h = y * (1.0 + scale_ref[0]) + shift_ref[0]
    hb = h.astype(BF16)

    posf = pos_ref[0].astype(F32)
    lane = lax.broadcasted_iota(jnp.int32, (ts, LANES), 1)
    m = lane & (HEAD_DIM - 1)
    ang = posf * invr_ref[...]
    cr_s[...] = jnp.cos(ang)
    s = jnp.sin(ang)
    sr_s[...] = jnp.where(m < HEAD_DIM // 2, -s, s)
    cn, sn = _nsa_tables(posf, invn_ref[...], m)
    cn_s[...] = cn
    sn_s[...] = sn

    staging = [z0_s, z1_s]

    def mm(off, width):
        z_s = staging[0]
        staging.reverse()
        z_s[:, :width] = _dot(hb, w_ref[:, off:off + width])
        return lambda c: z_s[:, c * LANES:(c + 1) * LANES]

    scale_q = HEAD_DIM ** -0.5
    tiles = lambda width: [(c, slice(c * LANES, (c + 1) * LANES)) for c in range(width // LANES)]
    z = mm(_O_RQ, RET_WIDTH)
    for c, sl in tiles(RET_WIDTH):
        rq_ref[0, :, sl] = (_rot_ret(z(c), cr_s[...], sr_s[...], m) * scale_q).astype(BF16)
    z = mm(_O_RK, RET_WIDTH)
    for c, sl in tiles(RET_WIDTH):
        rk_ref[0, :, sl] = _rot_ret(z(c), cr_s[...], sr_s[...], m).astype(BF16)
    z = mm(_O_RV, RET_WIDTH)
    for c, sl in tiles(RET_WIDTH):
        rv_ref[0, :, sl] = z(c).astype(BF16)
    z = mm(_O_RG, RET_WIDTH)
    for c, sl in tiles(RET_WIDTH):
        rgs_ref[0, :, sl] = _silu(z(c)).astype(BF16)
    z = mm(_O_NQ, NSA_WIDTH)
    for c, sl in tiles(NSA_WIDTH):
        nq_ref[0, :, sl] = (_rot_nsa(z(c), cn_s[...], sn_s[...], m) * (scale_q * LOG2E)).astype(BF16)

    z = mm(_O_KV, 6 * KV_WIDTH)
    kc0, kc1 = _dup_groups(z(0), lane)
    vc0, vc1 = _dup_groups(z(1), lane)
    for i, t in enumerate((kc0, kc1, vc0, vc1)):
        cmps_ref[0, :, i * LANES:(i + 1) * LANES] = t
    lo = lane < HEAD_DIM
    tok = pl.program_id(1) * ts + lax.broadcasted_iota(jnp.int32, (ts, LANES), 0)
    onehot = jnp.where(lane - HEAD_DIM == (tok >> SLC_SHIFT), 1.0, 0.0)
    ones = jnp.ones((ts, LANES), F32)
    for c, ref, rotate, upper in ((2, ksd_ref, True, onehot), (3, vsd_ref, False, ones),
                                  (4, kwd_ref, True, None), (5, vwd_ref, False, ones)):
        t = _rot_nsa(z(c), cn_s[...], sn_s[...], m) if rotate else z(c)
        if upper is None:
            d0, d1 = _dup_groups(t, lane)
        else:
            d0, d1 = jnp.where(lo, t, upper), jnp.where(lo, pltpu.roll(t, HEAD_DIM, 1), upper)
        ref[0, :, :LANES] = d0.astype(BF16)
        ref[0, :, LANES:] = d1.astype(BF16)

    z = mm(_O_NG, NSA_WIDTH + LANES)
    for c, sl in tiles(NSA_WIDTH):
        ngs_ref[0, :, sl] = _silu(z(c)).astype(BF16)
    gl_ref[0] = jax.nn.sigmoid(z(NSA_WIDTH // LANES))


def _proj(x, shift, scale, g_pre, pos3, inv_ret_row, inv_nsa_row, w_all, ts):
    bsz, seq, d = x.shape
    row = lambda w: pl.BlockSpec((1, ts, w), lambda b, s: (b, s, 0))
    vec = pl.BlockSpec((1, 1, d), lambda b, s: (b, 0, 0))
    const2 = lambda shp: pl.BlockSpec(shp, lambda b, s: (0, 0))
    outs = [(RET_WIDTH, BF16), (RET_WIDTH, BF16), (RET_WIDTH, BF16), (RET_WIDTH, BF16),
            (NSA_WIDTH, BF16), (4 * LANES, F32), (2 * LANES, BF16), (2 * LANES, BF16),
            (2 * LANES, BF16), (2 * LANES, BF16), (NSA_WIDTH, BF16), (LANES, F32)]
    return pl.pallas_call(
        _proj_kernel,
        grid=(bsz, seq // ts),
        in_specs=[row(d), vec, vec, const2((1, d)), row(1), const2((1, LANES)), const2((1, LANES)),
                  const2((d, _PROJ_COLS))],
        out_specs=[row(w) for w, _ in outs],
        out_shape=[jax.ShapeDtypeStruct((bsz, seq, w), dt) for w, dt in outs],
        scratch_shapes=[pltpu.VMEM((ts, LANES), F32)] * 4 + [pltpu.VMEM((ts, 6 * KV_WIDTH), F32)] * 2,
        compiler_params=_params("parallel", "arbitrary"),
        name="proj",
    )(x, shift, scale, g_pre, pos3, inv_ret_row, inv_nsa_row, w_all)


def _ret_kernel(q_ref, k_ref, v_ref, g_ref, dec_ref, zeta_ref, xi_ref, cd_ref, o_ref, r_s):
    rows = q_ref.shape[1]
    cc = RET_CHUNK

    @pl.when(pl.program_id(2) == 0)
    def _():
        r_s[...] = jnp.zeros_like(r_s)

    lane = lax.broadcasted_iota(jnp.int32, (cc, LANES), 1)
    rowi = lax.broadcasted_iota(jnp.int32, (cc, LANES), 0)
    lo = lane < HEAD_DIM
    blockdiag = (rowi < HEAD_DIM) == lo
    inv_n = 1.0 / HEAD_DIM
    chunks = [slice(c * cc, (c + 1) * cc) for c in range(rows // cc)]
    s_ab, kvs = [], []
    for sl in chunks:
        q, k, v = q_ref[0, sl, :], k_ref[0, sl, :], v_ref[0, sl, :]
        zero = jnp.zeros_like(q)
        s_ab.append((_dot_nt(jnp.where(lo, q, zero), k), _dot_nt(jnp.where(lo, zero, q), k)))
        kz = (k.astype(F32) * zeta_ref[0]).astype(BF16)
        kvs.append(jnp.where(blockdiag, _dot_tn(kz, v), 0.0))
    inners = []
    for sl, (s_a, s_b) in zip(chunks, s_ab):
        v = v_ref[0, sl, :]
        inners.append(jnp.where(lo, _dot((s_a * dec_ref[0]).astype(BF16), v),
                                _dot((s_b * dec_ref[1]).astype(BF16), v)))
    r = r_s[...]
    outs = []
    for sl, inner, kv in zip(chunks, inners, kvs):
        outs.append(inner + _dot(q_ref[0, sl, :], r.astype(BF16)) * xi_ref[0])
        r = r * cd_ref[0] + kv
    r_s[...] = r
    head_ones = jnp.where(blockdiag, 1.0, 0.0).astype(BF16)

    def head_sums(z):
        hi = z.astype(BF16)
        lo_part = (z - hi.astype(F32)).astype(BF16)
        return _dot(hi, head_ones) + _dot(lo_part, head_ones)

    devs = [o - s * inv_n for o, s in zip(outs, [head_sums(o) for o in outs])]
    variances = [head_sums(d * d) * inv_n for d in devs]
    for sl, d, var in zip(chunks, devs, variances):
        o_ref[0, sl, :] = (d * lax.rsqrt(var + GN_EPS) * g_ref[0, sl, :]).astype(o_ref.dtype)


def _retention(rq, rk, rv, rgs, dec, zeta_l, xi_l, cd_l, rows):
    bsz, seq, _ = rq.shape
    npair = RET_HEADS // 2
    blk = pl.BlockSpec((1, rows, LANES), lambda b, p, s: (b, s, p))
    return pl.pallas_call(
        _ret_kernel,
        grid=(bsz, npair, seq // rows),
        in_specs=[blk, blk, blk, blk,
                  pl.BlockSpec((2, RET_CHUNK, RET_CHUNK), lambda b, p, s: (p, 0, 0)),
                  pl.BlockSpec((1, RET_CHUNK, LANES), lambda b, p, s: (p, 0, 0)),
                  pl.BlockSpec((1, RET_CHUNK, LANES), lambda b, p, s: (p, 0, 0)),
                  pl.BlockSpec((1, 1, LANES), lambda b, p, s: (p, 0, 0))],
        out_specs=blk,
        out_shape=jax.ShapeDtypeStruct((bsz, seq, RET_WIDTH), BF16),
        scratch_shapes=[pltpu.VMEM((LANES, LANES), F32)],
        compiler_params=_params("parallel", "parallel", "arbitrary"),
        name="retention",
    )(rq, rk, rv, rgs, dec, zeta_l, xi_l, cd_l)


def _cmp_kernel(src_ref, pe_ref, w1_ref, w2_ref, pos_ref, invn_ref, o_ref):
    ncp = o_ref.shape[3]
    pairs = CMP_BLOCK // 4
    lo = lax.broadcasted_iota(jnp.int32, (ncp, LANES), 1) < HEAD_DIM
    p_lo = jnp.zeros((ncp, CMP_HIDDEN), F32)
    p_hi = jnp.zeros((ncp, CMP_HIDDEN), F32)
    for m in range(pairs):
        t = jnp.where(lo, src_ref[0, pl.ds(2 * m, ncp, stride=CMP_STRIDE), :],
                      src_ref[0, pl.ds(2 * m + 1, ncp, stride=CMP_STRIDE), :])
        p_lo = p_lo + _dot((t + pe_ref[0, m:m + 1, :]).astype(BF16), w1_ref[0, m])
        p_hi = p_hi + _dot((t + pe_ref[0, pairs + m:pairs + m + 1, :]).astype(BF16), w1_ref[0, pairs + m])
    hidden = p_lo + pltpu.roll(p_hi, ncp - 1, 0)
    comp = _dot(_silu(hidden).astype(BF16), w2_ref[0])
    m = _lane_in_head(comp.shape)
    cos, sin = _nsa_tables(pos_ref[0].astype(F32), invn_ref[...], m)
    rotated = _rot_nsa(comp, cos, sin, m)
    is_key = pl.program_id(1) == 0
    o_ref[0, 0, 0] = jnp.where(is_key, rotated, comp).astype(BF16)


def _compress(cmps, pe2, w1p, w2d, pos_cmp, inv_nsa_row):
    bsz, seq, _ = cmps.shape
    ncp = seq // CMP_STRIDE
    g = NSA_KV_GROUPS
    return pl.pallas_call(
        _cmp_kernel,
        grid=(bsz, 2, g),
        in_specs=[pl.BlockSpec((1, seq, LANES), lambda b, j, gi: (b, 0, j * g + gi)),
                  pl.BlockSpec((1, CMP_BLOCK // 2, LANES), lambda b, j, gi: (j, 0, 0)),
                  pl.BlockSpec((1, CMP_BLOCK // 2, LANES, CMP_HIDDEN), lambda b, j, gi: (j, 0, 0, 0)),
                  pl.BlockSpec((1, CMP_HIDDEN, LANES), lambda b, j, gi: (j, 0, 0)),
                  pl.BlockSpec((1, ncp, 1), lambda b, j, gi: (b, 0, 0)),
                  pl.BlockSpec((1, LANES), lambda b, j, gi: (0, 0))],
        out_specs=pl.BlockSpec((1, 1, 1, ncp, LANES), lambda b, j, gi: (b, j, gi, 0, 0)),
        out_shape=jax.ShapeDtypeStruct((bsz, 2, g, ncp, LANES), BF16),
        compiler_params=_params("parallel", "arbitrary", "arbitrary"),
        name="compress",
    )(cmps, pe2, w1p, w2d, pos_cmp, inv_nsa_row)


def _cmpsel_kernel(q_ref, kc_ref, vc_ref, ovt_ref, o_ref, selt_ref,
                   qs0_s, qs1_s, s0_s, s1_s, p0_s, p1_s, hi0_s, hi1_s, lo0_s, lo1_s, score_s, *, tq):
    nh = HEADS_PER_GROUP
    sub = 8
    n_qt = q_ref.shape[1] // tq
    ncp = kc_ref.shape[3]
    nblk = ovt_ref.shape[0]
    kc = kc_ref[0, 0, 0]
    vc = vc_ref[0, 0, 0]
    qs_b, s_b, p_b, hi_b, lo_b = (qs0_s, qs1_s), (s0_s, s1_s), (p0_s, p1_s), (hi0_s, hi1_s), (lo0_s, lo1_s)

    def rows_of(i):
        return slice(i * tq, (i + 1) * tq)

    def last_block(i):
        return ((i + 1) * tq - 1) >> SLC_SHIFT

    def n_rows(i):
        return min(nblk, -(-(last_block(i) + 1) // sub) * sub)

    def stack(i, slot):
        _stack_head_queries(q_ref, qs_b[slot], tq, src=rows_of(i))

    def scores(i, slot):
        s_b[slot][...] = _dot_nt(qs_b[slot][...], kc)

    def softmax(i, slot):
        tok = i * tq + lax.broadcasted_iota(jnp.int32, (tq, LANES), 0)
        lane = lax.broadcasted_iota(jnp.int32, (tq, LANES), 1)
        biases = [jnp.where((lane + c * LANES) * CMP_STRIDE + (CMP_BLOCK - 1) <= tok, 0.0, NEG)
                  for c in range(ncp // LANES)]
        has_key = tok >= CMP_BLOCK - 1
        psum = None
        for h in range(nh):
            rows = slice(h * tq, (h + 1) * tq)
            tiles = [t + b for b, t in zip(biases, _lane_tiles(s_b[slot][rows, :]))]
            mx = _row_stat(tiles, jnp.maximum, jnp.max)
            es = [jnp.exp2(t - mx) for t in tiles]
            inv = jnp.where(has_key, 1.0 / _row_stat(es, jnp.add, jnp.sum), 0.0)
            ps = [e * inv for e in es]
            psum = ps if psum is None else [a + b for a, b in zip(psum, ps)]
            p_b[slot][rows, :] = jnp.concatenate(ps, axis=1).astype(BF16)
        psum = jnp.concatenate(psum, axis=1)
        hi = psum.astype(BF16)
        hi_b[slot][...] = hi
        lo_b[slot][...] = (psum - hi.astype(F32)).astype(BF16)

    def values(i, slot):
        pv = _dot(p_b[slot][...], vc)
        _store_heads(o_ref, [pv[h * tq:(h + 1) * tq, :] for h in range(nh)], tq, rows_of(i))
        ov = ovt_ref[0:n_rows(i), :]
        return _dot_nt(ov, hi_b[slot][...]) + _dot_nt(ov, lo_b[slot][...])

    def select(i, imp):
        nr = n_rows(i)
        blk = lax.broadcasted_iota(jnp.int32, (nr, tq), 0)
        cur = (i * tq + lax.broadcasted_iota(jnp.int32, (nr, tq), 1)) >> SLC_SHIFT
        forced = (blk == 0) | (blk == cur) | (blk == cur - 1)
        score_s[0:nr, :] = jnp.where(forced, BIG, jnp.where(blk <= cur, imp, NEG))
        row_in_group = lax.broadcasted_iota(jnp.int32, (sub, tq), 0)
        scs = [score_s[r * sub:(r + 1) * sub, :] for r in range(nr // sub)]
        cnts = [jnp.zeros((sub, tq), F32) for _ in scs]
        for src in range(last_block(i) + 1):
            ri = jnp.broadcast_to(score_s[src:src + 1, :], (sub, tq))
            for r, sc in enumerate(scs):
                if r * sub > src:
                    beats = ri >= sc
                elif (r + 1) * sub - 1 <= src:
                    beats = ri > sc
                else:
                    beats = (ri > sc) | ((ri == sc) & (row_in_group > src - r * sub))
                cnts[r] = cnts[r] + jnp.where(beats, 1.0, 0.0)
        sels = [jnp.where((cnt < float(min(SLC_TOPK, nblk))) & (sc > 0.5 * NEG), 1.0, 0.0)
                for cnt, sc in zip(cnts, scs)]
        pads = [jnp.zeros((HEAD_DIM, tq), F32)] + sels
        if nr < HEAD_DIM:
            pads.append(jnp.zeros((HEAD_DIM - nr, tq), F32))
        selt_ref[0, 0, rows_of(i), :] = jnp.concatenate(pads, axis=0).T.astype(selt_ref.dtype)

    stack(0, 0)
    scores(0, 0)
    if n_qt > 1:
        stack(1, 1)
    for i in range(n_qt):
        slot = i % 2
        imp = values(i - 1, 1 - slot) if i >= 1 else None
        if i + 1 < n_qt:
            scores(i + 1, 1 - slot)
        if i + 2 < n_qt:
            stack(i + 2, slot)
        softmax(i, slot)
        if i >= 1:
            select(i - 1, imp)
    select(n_qt - 1, values(n_qt - 1, (n_qt - 1) % 2))


def _cmpsel(nq, cmpkv, ovt, tq):
    bsz, seq, _ = nq.shape
    g = NSA_KV_GROUPS
    ncp = cmpkv.shape[3]
    nblk = ovt.shape[0]
    gw = HEADS_PER_GROUP * HEAD_DIM
    rows = HEADS_PER_GROUP * tq
    assert nblk <= HEAD_DIM and nblk % 8 == 0
    return pl.pallas_call(
        functools.partial(_cmpsel_kernel, tq=tq),
        grid=(bsz, g),
        in_specs=[pl.BlockSpec((1, seq, gw), lambda b, gi: (b, 0, gi)),
                  pl.BlockSpec((1, 1, 1, ncp, LANES), lambda b, gi: (b, 0, gi, 0, 0)),
                  pl.BlockSpec((1, 1, 1, ncp, LANES), lambda b, gi: (b, 1, gi, 0, 0)),
                  pl.BlockSpec((nblk, ncp), lambda b, gi: (0, 0))],
        out_specs=[pl.BlockSpec((1, seq, gw), lambda b, gi: (b, 0, gi)),
                   pl.BlockSpec((1, 1, seq, LANES), lambda b, gi: (b, gi, 0, 0))],
        out_shape=[jax.ShapeDtypeStruct((bsz, seq, NSA_WIDTH), BF16),
                   jax.ShapeDtypeStruct((bsz, g, seq, LANES), BF16)],
        scratch_shapes=([pltpu.VMEM((rows, LANES), BF16)] * 2 + [pltpu.VMEM((rows, ncp), F32)] * 2 +
                        [pltpu.VMEM((rows, ncp), BF16)] * 2 + [pltpu.VMEM((tq, ncp), BF16)] * 4 +
                        [pltpu.VMEM((nblk, tq), F32)]),
        compiler_params=_params("parallel", "arbitrary"),
        name="cmpsel",
    )(nq, cmpkv, cmpkv, ovt)


def _stack_head_queries(q_ref, qs_s, rows, src=None):
    src = slice(0, rows) if src is None else src
    lo = lax.broadcasted_iota(jnp.int32, (rows, LANES), 1) < HEAD_DIM
    for c in range(HEADS_PER_GROUP // 2):
        q = q_ref[0, src, c * LANES:(c + 1) * LANES]
        zero = jnp.zeros_like(q)
        qs_s[(2 * c) * rows:(2 * c + 1) * rows, :] = jnp.where(lo, q, zero)
        qs_s[(2 * c + 1) * rows:(2 * c + 2) * rows, :] = jnp.where(lo, zero, q)


def _store_heads(o_ref, outs, tq, rows=slice(None)):
    lo = lax.broadcasted_iota(jnp.int32, (tq, LANES), 1) < HEAD_DIM
    for c in range(HEADS_PER_GROUP // 2):
        o_ref[0, rows, c * LANES:(c + 1) * LANES] = jnp.where(lo, outs[2 * c], outs[2 * c + 1]).astype(o_ref.dtype)


def _store_head_sums(o_ref, accs, rows=slice(None)):
    for h, a in enumerate(accs):
        o_ref[0, rows, h * LANES:(h + 1) * LANES] = a.astype(o_ref.dtype)


def _normalized_heads(ref):
    rows, width = ref.shape[1], ref.shape[2]
    lo = lax.broadcasted_iota(jnp.int32, (rows, LANES), 1) < HEAD_DIM
    outs = []
    for c in range(width // (2 * LANES)):
        a = ref[0, :, (2 * c) * LANES:(2 * c + 1) * LANES].astype(F32)
        b = ref[0, :, (2 * c + 1) * LANES:(2 * c + 2) * LANES].astype(F32)
        num = jnp.where(lo, a, pltpu.roll(b, HEAD_DIM, 1))
        den = jnp.where(lo, pltpu.roll(a, HEAD_DIM, 1), b)
        outs.append(num * (1.0 / den))
    return jnp.concatenate(outs, axis=1)


def _lane_tiles(z):
    return [z[:, c * LANES:(c + 1) * LANES] for c in range(z.shape[1] // LANES)]


def _row_stat(tiles, combine, reduce):
    acc = functools.reduce(combine, tiles)
    return jnp.broadcast_to(reduce(acc, axis=-1, keepdims=True), acc.shape)


WIN_UNROLL = 8


def _win_kernel(q_ref, k_ref, v_ref, o_ref, qs0_s, qs1_s, s0_s, s1_s, p0_s, p1_s, *, tq):
    nh = HEADS_PER_GROUP
    span = WINDOW + tq
    n_qt = q_ref.shape[1] // tq
    qs_b, s_b, p_b = (qs0_s, qs1_s), (s0_s, s1_s), (p0_s, p1_s)

    def q_rows(i):
        return pl.ds(pl.multiple_of(i * tq, tq), tq)

    def k_rows(i):
        return pl.ds(pl.multiple_of(jnp.maximum(i * tq - WINDOW, 0), tq), span)

    def stack_queries(i, slot):
        _stack_head_queries(q_ref, qs_b[slot], tq, src=q_rows(jnp.minimum(i, n_qt - 1)))

    def scores(i, slot):
        i = jnp.minimum(i, n_qt - 1)
        s_b[slot][...] = _dot_nt(qs_b[slot][...], k_ref[0, k_rows(i), :])

    def values(i, slot):
        i = jnp.maximum(i, 0)
        pv = _dot(p_b[slot][...], v_ref[0, k_rows(i), :])
        _store_head_sums(o_ref, [pv[h * tq:(h + 1) * tq, :] for h in range(nh)], q_rows(i))

    def softmax(i, slot):
        tok = i * tq + lax.broadcasted_iota(jnp.int32, (tq, span), 0)
        key = jnp.maximum(i * tq - WINDOW, 0) + lax.broadcasted_iota(jnp.int32, (tq, span), 1)
        bias = jnp.where((key <= tok) & (tok - key < WINDOW), 0.0, NEG)
        for h in range(nh):
            rows = slice(h * tq, (h + 1) * tq)
            tiles = _lane_tiles(s_b[slot][rows, :] + bias)
            m = _row_stat(tiles, jnp.maximum, jnp.max)
            p_b[slot][rows, :] = jnp.concatenate([jnp.exp2(t - m) for t in tiles], axis=1).astype(BF16)

    def stages(first, count):
        for t in range(count):
            i, slot = first + t, t % 2
            values(i - 1, 1 - slot)
            scores(i + 1, 1 - slot)
            stack_queries(i + 2, slot)
            softmax(i, slot)

    p1_s[...] = jnp.ones_like(p1_s)
    stack_queries(0, 0)
    stack_queries(1, 1)
    scores(0, 0)

    def body(b, carry):
        stages(b * WIN_UNROLL, WIN_UNROLL)
        return carry

    lax.fori_loop(0, n_qt // WIN_UNROLL, body, 0)
    stages(n_qt - n_qt % WIN_UNROLL, n_qt % WIN_UNROLL)
    values(n_qt - 1, (n_qt - 1) % 2)


SLC_UNROLL = 8
SLC_STATE_BUFFERS = SLC_UNROLL + 1
assert SLC_UNROLL % 2 == 0


def _slc_kernel(q_ref, k_ref, v_ref, selt_ref, o_ref, qs_s, s0_s, s1_s, p0_s, p1_s, a0_s, a1_s,
                m_s, acc_s, *, tq, tk):
    nh = HEADS_PER_GROUP
    seq = q_ref.shape[1]
    n_qt = seq // tq
    s_b, p_b, a_b = (s0_s, s1_s), (p0_s, p1_s), (a0_s, a1_s)
    lo = lax.broadcasted_iota(jnp.int32, (tq, LANES), 1) < HEAD_DIM

    def n_tiles(qi):
        return (qi * tq + tq - 1) // tk + 1

    n_items = sum((qi * tq + tq - 1) // tk + 1 for qi in range(n_qt))

    def advance(qi, j):
        wrap = j + 1 >= n_tiles(qi)
        return jnp.where(wrap, jnp.minimum(qi + 1, n_qt - 1), qi), jnp.where(wrap, 0, j + 1)

    def q_rows(qi):
        return pl.ds(pl.multiple_of(qi * tq, tq), tq)

    def k_rows(j):
        return pl.ds(pl.multiple_of(j * tk, tk), tk)

    def stack_queries(qi):
        par = qi % SLC_STATE_BUFFERS
        unselected = jnp.where(selt_ref[0, 0, q_rows(qi), :].astype(F32) > 0.5, 0.0, NEG)
        for c in range(nh // 2):
            q = q_ref[0, q_rows(qi), c * LANES:(c + 1) * LANES].astype(F32)
            qs_s[par, (2 * c) * tq:(2 * c + 1) * tq, :] = jnp.where(lo, q, unselected).astype(BF16)
            qs_s[par, (2 * c + 1) * tq:(2 * c + 2) * tq, :] = jnp.where(
                lo, pltpu.roll(q, HEAD_DIM, 1), unselected).astype(BF16)

    def scores(qi, j, slot):
        s_b[slot][...] = _dot_nt(qs_s[qi % SLC_STATE_BUFFERS], k_ref[0, k_rows(j), :])

    def values(qi, j, slot):
        pv = _dot(p_b[slot][...], v_ref[0, k_rows(j), :])
        par = qi % SLC_STATE_BUFFERS
        for h in range(nh):
            acc_s[par, h] = a_b[slot][h] * acc_s[par, h] + pv[h * tq:(h + 1) * tq, :]

    def softmax(qi, j, slot):
        tok = qi * tq + lax.broadcasted_iota(jnp.int32, (tq, tk), 0)
        key = j * tk + lax.broadcasted_iota(jnp.int32, (tq, tk), 1)
        bias = jnp.where(key <= tok, 0.0, NEG)
        for h in range(nh):
            rows = slice(h * tq, (h + 1) * tq)
            tiles = _lane_tiles(s_b[slot][rows, :] + bias)
            m_prev = jnp.where(j == 0, NEG, m_s[h])
            m_new = jnp.maximum(m_prev, _row_stat(tiles, jnp.maximum, jnp.max))
            m_s[h] = m_new
            a_b[slot][h] = jnp.exp2(m_prev - m_new)
            p_b[slot][rows, :] = jnp.concatenate([jnp.exp2(t - m_new) for t in tiles], axis=1).astype(BF16)

    def finalize(qi):
        _store_head_sums(o_ref, [acc_s[qi % SLC_STATE_BUFFERS, h] for h in range(nh)], q_rows(qi))

    def is_last(qi, j):
        return (j + 1 >= n_tiles(qi)).astype(jnp.int32)

    def finalize_if(flag, qi):
        @pl.when(flag == 1)
        def _():
            finalize(qi)

    def block(carry, n_stages):
        fin, (qp, jp, p_last), (qc, jc), (qn, jn) = carry
        for flag, qi in fin:
            finalize_if(flag, qi)
        items = [(qp, jp, p_last), (qc, jc, is_last(qc, jc)), (qn, jn, is_last(qn, jn))]
        for _ in range(n_stages - 1):
            qx, jx = advance(items[-1][0], items[-1][1])
            items.append((qx, jx, is_last(qx, jx)))
        for qx, jx, _ in items[2:]:
            @pl.when(jx == 0)
            def _():
                stack_queries(qx)
        for t in range(n_stages):
            slot = t % 2
            values(items[t][0], items[t][1], 1 - slot)
            scores(items[t + 2][0], items[t + 2][1], 1 - slot)
            softmax(items[t + 1][0], items[t + 1][1], slot)
        fin = tuple((items[t][2], items[t][0]) for t in range(n_stages))
        nxt = advance(items[-1][0], items[-1][1])
        return fin, items[n_stages], items[n_stages + 1][:2], nxt

    m_s[...] = jnp.full_like(m_s, NEG)
    acc_s[...] = jnp.zeros_like(acc_s)
    p1_s[...] = jnp.zeros_like(p1_s)
    a1_s[...] = jnp.ones_like(a1_s)
    zero = jnp.int32(0)
    stack_queries(zero)
    scores(zero, zero, 0)
    unroll = SLC_UNROLL
    carry = (((zero, zero),) * unroll, (zero, zero, zero), (zero, zero), advance(zero, zero))
    carry = lax.fori_loop(0, n_items // unroll, lambda i, c: block(c, unroll), carry)
    if n_items % unroll:
        carry = block(carry, n_items % unroll)
    fin, (qp, jp, _), _, _ = carry
    for flag, qi in fin:
        finalize_if(flag, qi)
    values(qp, jp, (n_items - 1) % 2)
    finalize(qp)


def _slc_attention(nq, kd, vd, selt, *, tq, tk):
    bsz, seq, _ = nq.shape
    g = NSA_KV_GROUPS
    gw = HEADS_PER_GROUP * HEAD_DIM
    nh = HEADS_PER_GROUP
    assert seq % tk == 0 and tk % tq == 0
    assert seq // SLC_BLOCK <= HEAD_DIM
    whole = lambda w: pl.BlockSpec((1, seq, w), lambda b, gi: (b, 0, gi))
    stat = pltpu.VMEM((nh, tq, LANES), F32)
    state = pltpu.VMEM((SLC_STATE_BUFFERS, nh, tq, LANES), F32)
    return pl.pallas_call(
        functools.partial(_slc_kernel, tq=tq, tk=tk),
        grid=(bsz, g),
        in_specs=[whole(gw), whole(LANES), whole(LANES),
                  pl.BlockSpec((1, 1, seq, LANES), lambda b, gi: (b, gi, 0, 0))],
        out_specs=whole(nh * LANES),
        out_shape=jax.ShapeDtypeStruct((bsz, seq, g * nh * LANES), BF16),
        scratch_shapes=([pltpu.VMEM((SLC_STATE_BUFFERS, nh * tq, LANES), BF16)] + [pltpu.VMEM((nh * tq, tk), F32)] * 2 +
                        [pltpu.VMEM((nh * tq, tk), BF16)] * 2 + [stat] * 3 + [state]),
        compiler_params=_params("parallel", "arbitrary"),
        name="slc_attn",
    )(nq, kd, vd, selt)


def _win_attention(nq, kd, vd, *, tq):
    bsz, seq, _ = nq.shape
    g = NSA_KV_GROUPS
    gw = HEADS_PER_GROUP * HEAD_DIM
    nh = HEADS_PER_GROUP
    assert WINDOW % tq == 0 and seq >= WINDOW + tq and seq // tq >= 2 and WIN_UNROLL % 2 == 0
    whole = lambda w: pl.BlockSpec((1, seq, w), lambda b, gi: (b, 0, gi))
    span = WINDOW + tq
    return pl.pallas_call(
        functools.partial(_win_kernel, tq=tq),
        grid=(bsz, g),
        in_specs=[whole(gw), whole(LANES), whole(LANES)],
        out_specs=whole(nh * LANES),
        out_shape=jax.ShapeDtypeStruct((bsz, seq, g * nh * LANES), BF16),
        scratch_shapes=([pltpu.VMEM((nh * tq, LANES), BF16)] * 2 + [pltpu.VMEM((nh * tq, span), F32)] * 2 +
                        [pltpu.VMEM((nh * tq, span), BF16)] * 2),
        compiler_params=_params("parallel", "arbitrary"),
        name="win_attn",
    )(nq, kd, vd)


def _out_kernel(x_ref, gate_ref, gpost_ref, ret_ref, oc_ref, os_ref, ow_ref, gl_ref, ngs_ref,
                wo_ref, o_ref):
    ts = x_ref.shape[1]
    gates = gl_ref[0]
    lo_off = 32
    assert 3 * NSA_HEADS <= lo_off
    g_hi = gates.astype(BF16).astype(F32)
    g_lo = (gates - g_hi).astype(BF16).astype(F32)
    lane = lax.broadcasted_iota(jnp.int32, (ts, LANES), 1)
    g_split = jnp.where(lane < lo_off, g_hi, pltpu.roll(g_lo, lo_off, 1)).astype(BF16)
    src = lax.broadcasted_iota(jnp.int32, (LANES, NSA_WIDTH), 0)
    head = lax.broadcasted_iota(jnp.int32, (LANES, NSA_WIDTH), 1) >> HEAD_SHIFT
    att = jnp.zeros((ts, NSA_WIDTH), F32)
    for i, branch in enumerate((oc_ref[0], _normalized_heads(os_ref), _normalized_heads(ow_ref))):
        expand = jnp.where((src == 3 * head + i) | (src == lo_off + 3 * head + i), 1.0, 0.0).astype(BF16)
        att = att + _dot(g_split, expand) * branch
    att = att * ngs_ref[0]
    y = _dot(ret_ref[0].astype(BF16), wo_ref[:RET_WIDTH, :]) + _dot(att.astype(BF16), wo_ref[RET_WIDTH:, :])
    ms = jnp.mean(y * y, axis=-1, keepdims=True)
    y = y * lax.rsqrt(ms + EPS) * gpost_ref[...]
    o_ref[0] = x_ref[0] + gate_ref[0] * y


def _out(x, gate, g_post, ret, o_cmp, o_slc, o_win, gl, ngs, w_out, ts):
    bsz, seq, d = x.shape
    row = lambda w: pl.BlockSpec((1, ts, w), lambda b, s: (b, s, 0))
    return pl.pallas_call(
        _out_kernel,
        grid=(bsz, seq // ts),
        in_specs=[row(d), pl.BlockSpec((1, 1, d), lambda b, s: (b, 0, 0)),
                  pl.BlockSpec((1, d), lambda b, s: (0, 0)),
                  row(RET_WIDTH), row(NSA_WIDTH), row(o_slc.shape[2]), row(o_win.shape[2]), row(LANES),
                  row(NSA_WIDTH), pl.BlockSpec(w_out.shape, lambda b, s: (0, 0))],
        out_specs=row(d),
        out_shape=jax.ShapeDtypeStruct((bsz, seq, d), x.dtype),
        compiler_params=_params("parallel", "arbitrary"),
        name="out",
    )(x, gate, g_post, ret, o_cmp, o_slc, o_win, gl, ngs, w_out)


def _retention_tables():
    h = RET_HEADS
    c = RET_CHUNK
    log_g = jnp.log1p(-jnp.power(2.0, -5.0 - jnp.arange(h, dtype=F32)))
    idx = jnp.arange(c, dtype=F32)
    diff = idx[:, None] - idx[None, :]
    dec = jnp.where(diff[None] >= 0, jnp.exp(jnp.maximum(diff, 0.0)[None] * log_g[:, None, None]), 0.0)
    zeta = jnp.exp((c - 1 - idx)[None, :] * log_g[:, None])
    xi = jnp.exp((idx + 1.0)[None, :] * log_g[:, None])
    cd = jnp.exp(c * log_g)
    to_lanes = lambda t: jnp.repeat(t.reshape(h // 2, 2, c).transpose(0, 2, 1), HEAD_DIM, axis=2)
    cd_l = jnp.repeat(cd.reshape(h // 2, 1, 2), HEAD_DIM, axis=2)
    return dec, to_lanes(zeta), to_lanes(xi), cd_l


def _overlap_t(seq):
    ncp = seq // CMP_STRIDE
    nc = (seq - CMP_BLOCK) // CMP_STRIDE + 1
    ns = seq // SLC_BLOCK
    cs = np.arange(ncp) * CMP_STRIDE
    ce = cs + CMP_BLOCK - 1
    ss = np.arange(ns) * SLC_BLOCK
    ov = (cs[None, :] <= ss[:, None] + SLC_BLOCK - 1) & (ce[None, :] >= ss[:, None]) & (np.arange(ncp) < nc)[None, :]
    return jnp.asarray(ov.astype(np.float32), dtype=BF16)


def _layer(x, c, positions, mod, g_pre, g_post, w_in, w_out,
           pe_k, w1_k, w2_k, pe_v, w1_v, w2_v):
    bsz, seq, d = x.shape
    shift, scale, gate = [t.reshape(bsz, 1, d) for t in jnp.split(mod, 3, axis=-1)]

    offs = np.cumsum((RET_WIDTH,) * 4 + (NSA_WIDTH, 6 * KV_WIDTH, 3 * NSA_HEADS, NSA_WIDTH))
    w_gl = w_in[:, offs[5]:offs[6]]
    w_all = jnp.concatenate(
        [w_in[:, :offs[5]], w_in[:, offs[6]:], w_gl,
         jnp.zeros((d, LANES - w_gl.shape[1]), w_in.dtype)], axis=1).astype(BF16)

    lanes = jnp.arange(LANES)
    half_r = HEAD_DIM // 2
    inv_r = jnp.power(RET_ROPE_BASE, -jnp.arange(half_r, dtype=F32) / half_r)
    half_n = ROPE_DIM // 2
    inv_n = jnp.power(NSA_ROPE_BASE, -jnp.arange(half_n, dtype=F32) / half_n)
    inv_ret_row = inv_r[lanes % half_r].reshape(1, LANES)
    inv_nsa_row = inv_n[lanes % half_n].reshape(1, LANES)
    pos3 = positions.reshape(bsz, seq, 1)

    (rq, rk, rv, rgs, nq, cmps, ksd, vsd, kwd, vwd, ngs, gl) = _proj(
        x, shift, scale, g_pre.reshape(1, d), pos3, inv_ret_row, inv_nsa_row, w_all, ts=min(PROJ_ROWS, seq))

    dec, zeta_l, xi_l, cd_l = _retention_tables()
    ret = _retention(rq, rk, rv, rgs, dec, zeta_l, xi_l, cd_l, rows=min(RET_ROWS, seq))

    ncp = seq // CMP_STRIDE
    nc = (seq - CMP_BLOCK) // CMP_STRIDE + 1
    pe2 = jnp.stack([pe_k, pe_v]).reshape(2, CMP_BLOCK // 2, LANES)
    w1p = jnp.stack([w1_k, w1_v]).reshape(2, CMP_BLOCK // 2, LANES, CMP_HIDDEN).astype(BF16)
    w2s = jnp.stack([w2_k, w2_v])
    w2d = jnp.concatenate([w2s, w2s], axis=-1).astype(BF16)
    cmp_end = np.arange(nc) * CMP_STRIDE + CMP_BLOCK - 1
    pos_cmp = jnp.pad(positions[:, cmp_end], ((0, 0), (0, ncp - nc))).reshape(bsz, ncp, 1)
    cmpkv = _compress(cmps, pe2, w1p, w2d, pos_cmp, inv_nsa_row)

    o_cmp, selt = _cmpsel(nq, cmpkv, _overlap_t(seq), tq=min(CMPSEL_TQ, seq))
    o_slc = _slc_attention(nq, ksd, vsd, selt, tq=SLC_TQ, tk=SLC_TK)
    o_win = _win_attention(nq, kwd, vwd, tq=WIN_TQ)

    return _out(x, gate, g_post.reshape(1, d), ret, o_cmp, o_slc, o_win, gl, ngs,
                w_out.astype(BF16), ts=min(PROJ_ROWS, seq))


def kernel(x, c, positions, w_ada, b_ada, g_pre, g_post, w_in, w_out, cmp_pe_k, cmp_w1_k, cmp_w2_k, cmp_pe_v, cmp_w1_v, cmp_w2_v):
    for l in range(w_in.shape[0]):
        x = _layer(x, c, positions, _ada(c, w_ada, b_ada, l), g_pre[l], g_post[l], w_in[l], w_out[l],
                   cmp_pe_k[l], cmp_w1_k[l], cmp_w2_k[l], cmp_pe_v[l], cmp_w1_v[l], cmp_w2_v[l])
    return x
```

```python
import functools

import numpy as np
import jax
import jax.numpy as jnp
from jax import lax
from jax.experimental import pallas as pl
from jax.experimental.pallas import tpu as pltpu

F32 = jnp.float32
BF16 = jnp.bfloat16

LANES = 128
HEAD_DIM = 64
RET_HEADS = 8
NSA_HEADS = 8
NSA_KV_GROUPS = 2
HEADS_PER_GROUP = NSA_HEADS // NSA_KV_GROUPS
RET_WIDTH = RET_HEADS * HEAD_DIM
NSA_WIDTH = NSA_HEADS * HEAD_DIM
KV_WIDTH = NSA_KV_GROUPS * HEAD_DIM
RET_CHUNK = 128
RET_ROPE_BASE = 10000.0
NSA_ROPE_BASE = 500000.0
ROPE_DIM = HEAD_DIM // 4
CMP_BLOCK = 32
CMP_STRIDE = 16
CMP_HIDDEN = 256
SLC_BLOCK = 64
SLC_TOPK = 16
WINDOW = 512
NEG = -1e30
LOG2E = 1.4426950408889634
BIG = 1e9
EPS = 1e-6
GN_EPS = 1e-5
VMEM_LIMIT = 48 * 1024 * 1024

PROJ_ROWS = 512
RET_ROWS = 4096
CMPSEL_TQ = 256
SLC_TQ, SLC_TK = 256, 512
WIN_TQ = 128

NT_DIMS = (((1,), (1,)), ((), ()))
TN_DIMS = (((0,), (0,)), ((), ()))


def _dot(a, b):
    return jnp.dot(a, b, preferred_element_type=F32)


def _dot_nt(a, b):
    return lax.dot_general(a, b, NT_DIMS, preferred_element_type=F32)


def _dot_tn(a, b):
    return lax.dot_general(a, b, TN_DIMS, preferred_element_type=F32)


def _silu(z):
    return z * jax.nn.sigmoid(z)


def _params(*sem):
    return pltpu.CompilerParams(dimension_semantics=sem, vmem_limit_bytes=VMEM_LIMIT)


HEAD_SHIFT = 6
SLC_SHIFT = 6
assert 1 << HEAD_SHIFT == HEAD_DIM and 1 << SLC_SHIFT == SLC_BLOCK


def _lane_in_head(shape):
    return lax.broadcasted_iota(jnp.int32, shape, len(shape) - 1) & (HEAD_DIM - 1)


def _ada_kernel(c_ref, w_ref, b_ref, o_ref):
    a = _silu(c_ref[...])
    o_ref[...] = jnp.dot(a, w_ref[0], precision=lax.Precision.HIGHEST,
                         preferred_element_type=F32) + b_ref[0]


def _ada(c, w_layers, b_layers, layer):
    bsz, d = c.shape
    depth, _, n = w_layers.shape
    tn = d
    return pl.pallas_call(
        _ada_kernel,
        grid=(n // tn,),
        in_specs=[pl.BlockSpec((bsz, d), lambda j: (0, 0)),
                  pl.BlockSpec((1, d, tn), lambda j: (layer, 0, j)),
                  pl.BlockSpec((1, 1, tn), lambda j: (layer, 0, j))],
        out_specs=pl.BlockSpec((bsz, tn), lambda j: (0, j)),
        out_shape=jax.ShapeDtypeStruct((bsz, n), F32),
        compiler_params=_params("arbitrary"),
        name="ada",
    )(c, w_layers, b_layers.reshape(depth, 1, n))


_O_RQ, _O_RK, _O_RV, _O_RG, _O_NQ = 0, 512, 1024, 1536, 2048
_O_KV = 2560
_O_NG = 3328
_O_GL = 3840
_PROJ_COLS = 3968


def _rot_ret(z, cos, sin_signed, m):
    partner = jnp.where(m < HEAD_DIM // 2, pltpu.roll(z, LANES - HEAD_DIM // 2, 1),
                        pltpu.roll(z, HEAD_DIM // 2, 1))
    return z * cos + partner * sin_signed


def _rot_nsa(z, cos, sin_signed, m):
    half = ROPE_DIM // 2
    partner = jnp.where(m < half, pltpu.roll(z, LANES - half, 1), pltpu.roll(z, half, 1))
    return z * cos + partner * sin_signed


def _nsa_tables(posf, inv_row, m):
    half = ROPE_DIM // 2
    ang = posf * inv_row
    cos = jnp.where(m < ROPE_DIM, jnp.cos(ang), 1.0)
    s = jnp.sin(ang)
    sin = jnp.where(m < half, -s, jnp.where(m < ROPE_DIM, s, 0.0))
    return cos, sin


def _dup_groups(z, m_lane):
    r = pltpu.roll(z, HEAD_DIM, 1)
    lo = m_lane < HEAD_DIM
    return jnp.where(lo, z, r), jnp.where(lo, r, z)


def _proj_kernel(x_ref, shift_ref, scale_ref, gpre_ref, pos_ref, invr_ref, invn_ref, w_ref,
                 rq_ref, rk_ref, rv_ref, rgs_ref, nq_ref, cmps_ref, ksd_ref, vsd_ref,
                 kwd_ref, vwd_ref, ngs_ref, gl_ref,
                 cr_s, sr_s, cn_s, sn_s, z0_s, z1_s):
    ts = x_ref.shape[1]
    x = x_ref[0]
    ms = jnp.mean(x * x, axis=-1, keepdims=True)
    y = x * lax.rsqrt(ms + EPS) * gpre_ref[...]
    h = y * (1.0 + scale_ref[0]) + shift_ref[0]
    hb = h.astype(BF16)

    posf = pos_ref[0].astype(F32)
    lane = lax.broadcasted_iota(jnp.int32, (ts, LANES), 1)
    m = lane & (HEAD_DIM - 1)
    ang = posf * invr_ref[...]
    cr_s[...] = jnp.cos(ang)
    s = jnp.sin(ang)
    sr_s[...] = jnp.where(m < HEAD_DIM // 2, -s, s)
    cn, sn = _nsa_tables(posf, invn_ref[...], m)
    cn_s[...] = cn
    sn_s[...] = sn

    staging = [z0_s, z1_s]

    def mm(off, width):
        z_s = staging[0]
        staging.reverse()
        z_s[:, :width] = _dot(hb, w_ref[:, off:off + width])
        return lambda c: z_s[:, c * LANES:(c + 1) * LANES]

    scale_q = HEAD_DIM ** -0.5
    tiles = lambda width: [(c, slice(c * LANES, (c + 1) * LANES)) for c in range(width // LANES)]
    z = mm(_O_RQ, RET_WIDTH)
    for c, sl in tiles(RET_WIDTH):
        rq_ref[0, :, sl] = (_rot_ret(z(c), cr_s[...], sr_s[...], m) * scale_q).astype(BF16)
    z = mm(_O_RK, RET_WIDTH)
    for c, sl in tiles(RET_WIDTH):
        rk_ref[0, :, sl] = _rot_ret(z(c), cr_s[...], sr_s[...], m).astype(BF16)
    z = mm(_O_RV, RET_WIDTH)
    for c, sl in tiles(RET_WIDTH):
        rv_ref[0, :, sl] = z(c).astype(BF16)
    z = mm(_O_RG, RET_WIDTH)
    for c, sl in tiles(RET_WIDTH):
        rgs_ref[0, :, sl] = _silu(z(c)).astype(BF16)
    z = mm(_O_NQ, NSA_WIDTH)
    for c, sl in tiles(NSA_WIDTH):
        nq_ref[0, :, sl] = (_rot_nsa(z(c), cn_s[...], sn_s[...], m) * (scale_q * LOG2E)).astype(BF16)

    z = mm(_O_KV, 6 * KV_WIDTH)
    kc0, kc1 = _dup_groups(z(0), lane)
    vc0, vc1 = _dup_groups(z(1), lane)
    for i, t in enumerate((kc0, kc1, vc0, vc1)):
        cmps_ref[0, :, i * LANES:(i + 1) * LANES] = t
    lo = lane < HEAD_DIM
    tok = pl.program_id(1) * ts + lax.broadcasted_iota(jnp.int32, (ts, LANES), 0)
    onehot = jnp.where(lane - HEAD_DIM == (tok >> SLC_SHIFT), 1.0, 0.0)
    ones = jnp.ones((ts, LANES), F32)
    for c, ref, rotate, upper in ((2, ksd_ref, True, onehot), (3, vsd_ref, False, ones),
                                  (4, kwd_ref, True, None), (5, vwd_ref, False, ones)):
        t = _rot_nsa(z(c), cn_s[...], sn_s[...], m) if rotate else z(c)
        if upper is None:
            d0, d1 = _dup_groups(t, lane)
        else:
            d0, d1 = jnp.where(lo, t, upper), jnp.where(lo, pltpu.roll(t, HEAD_DIM, 1), upper)
        ref[0, :, :LANES] = d0.astype(BF16)
        ref[0, :, LANES:] = d1.astype(BF16)

    z = mm(_O_NG, NSA_WIDTH + LANES)
    for c, sl in tiles(NSA_WIDTH):
        ngs_ref[0, :, sl] = _silu(z(c)).astype(BF16)
    gl_ref[0] = jax.nn.sigmoid(z(NSA_WIDTH // LANES))


def _proj(x, shift, scale, g_pre, pos3, inv_ret_row, inv_nsa_row, w_all, ts):
    bsz, seq, d = x.shape
    row = lambda w: pl.BlockSpec((1, ts, w), lambda b, s: (b, s, 0))
    vec = pl.BlockSpec((1, 1, d), lambda b, s: (b, 0, 0))
    const2 = lambda shp: pl.BlockSpec(shp, lambda b, s: (0, 0))
    outs = [(RET_WIDTH, BF16), (RET_WIDTH, BF16), (RET_WIDTH, BF16), (RET_WIDTH, BF16),
            (NSA_WIDTH, BF16), (4 * LANES, F32), (2 * LANES, BF16), (2 * LANES, BF16),
            (2 * LANES, BF16), (2 * LANES, BF16), (NSA_WIDTH, BF16), (LANES, F32)]
    return pl.pallas_call(
        _proj_kernel,
        grid=(bsz, seq // ts),
        in_specs=[row(d), vec, vec, const2((1, d)), row(1), const2((1, LANES)), const2((1, LANES)),
                  const2((d, _PROJ_COLS))],
        out_specs=[row(w) for w, _ in outs],
        out_shape=[jax.ShapeDtypeStruct((bsz, seq, w), dt) for w, dt in outs],
        scratch_shapes=[pltpu.VMEM((ts, LANES), F32)] * 4 + [pltpu.VMEM((ts, 6 * KV_WIDTH), F32)] * 2,
        compiler_params=_params("parallel", "arbitrary"),
        name="proj",
    )(x, shift, scale, g_pre, pos3, inv_ret_row, inv_nsa_row, w_all)


def _ret_kernel(q_ref, k_ref, v_ref, g_ref, dec_ref, zeta_ref, xi_ref, cd_ref, o_ref, r_s):
    rows = q_ref.shape[1]
    cc = RET_CHUNK

    @pl.when(pl.program_id(2) == 0)
    def _():
        r_s[...] = jnp.zeros_like(r_s)

    lane = lax.broadcasted_iota(jnp.int32, (cc, LANES), 1)
    rowi = lax.broadcasted_iota(jnp.int32, (cc, LANES), 0)
    lo = lane < HEAD_DIM
    blockdiag = (rowi < HEAD_DIM) == lo
    inv_n = 1.0 / HEAD_DIM
    chunks = [slice(c * cc, (c + 1) * cc) for c in range(rows // cc)]
    s_ab, kvs = [], []
    for sl in chunks:
        q, k, v = q_ref[0, sl, :], k_ref[0, sl, :], v_ref[0, sl, :]
        zero = jnp.zeros_like(q)
        s_ab.append((_dot_nt(jnp.where(lo, q, zero), k), _dot_nt(jnp.where(lo, zero, q), k)))
        kz = (k.astype(F32) * zeta_ref[0]).astype(BF16)
        kvs.append(jnp.where(blockdiag, _dot_tn(kz, v), 0.0))
    inners = []
    for sl, (s_a, s_b) in zip(chunks, s_ab):
        v = v_ref[0, sl, :]
        inners.append(jnp.where(lo, _dot((s_a * dec_ref[0]).astype(BF16), v),
                                _dot((s_b * dec_ref[1]).astype(BF16), v)))
    r = r_s[...]
    outs = []
    for sl, inner, kv in zip(chunks, inners, kvs):
        outs.append(inner + _dot(q_ref[0, sl, :], r.astype(BF16)) * xi_ref[0])
        r = r * cd_ref[0] + kv
    r_s[...] = r
    head_ones = jnp.where(blockdiag, 1.0, 0.0).astype(BF16)

    def head_sums(z):
        hi = z.astype(BF16)
        lo_part = (z - hi.astype(F32)).astype(BF16)
        return _dot(hi, head_ones) + _dot(lo_part, head_ones)

    devs = [o - s * inv_n for o, s in zip(outs, [head_sums(o) for o in outs])]
    variances = [head_sums(d * d) * inv_n for d in devs]
    for sl, d, var in zip(chunks, devs, variances):
        o_ref[0, sl, :] = (d * lax.rsqrt(var + GN_EPS) * g_ref[0, sl, :]).astype(o_ref.dtype)


def _retention(rq, rk, rv, rgs, dec, zeta_l, xi_l, cd_l, rows):
    bsz, seq, _ = rq.shape
    npair = RET_HEADS // 2
    blk = pl.BlockSpec((1, rows, LANES), lambda b, p, s: (b, s, p))
    return pl.pallas_call(
        _ret_kernel,
        grid=(bsz, npair, seq // rows),
        in_specs=[blk, blk, blk, blk,
                  pl.BlockSpec((2, RET_CHUNK, RET_CHUNK), lambda b, p, s: (p, 0, 0)),
                  pl.BlockSpec((1, RET_CHUNK, LANES), lambda b, p, s: (p, 0, 0)),
                  pl.BlockSpec((1, RET_CHUNK, LANES), lambda b, p, s: (p, 0, 0)),
                  pl.BlockSpec((1, 1, LANES), lambda b, p, s: (p, 0, 0))],
        out_specs=blk,
        out_shape=jax.ShapeDtypeStruct((bsz, seq, RET_WIDTH), BF16),
        scratch_shapes=[pltpu.VMEM((LANES, LANES), F32)],
        compiler_params=_params("parallel", "parallel", "arbitrary"),
        name="retention",
    )(rq, rk, rv, rgs, dec, zeta_l, xi_l, cd_l)


def _cmp_kernel(src_ref, pe_ref, w1_ref, w2_ref, pos_ref, invn_ref, o_ref):
    ncp = o_ref.shape[3]
    pairs = CMP_BLOCK // 4
    lo = lax.broadcasted_iota(jnp.int32, (ncp, LANES), 1) < HEAD_DIM
    p_lo = jnp.zeros((ncp, CMP_HIDDEN), F32)
    p_hi = jnp.zeros((ncp, CMP_HIDDEN), F32)
    for m in range(pairs):
        t = jnp.where(lo, src_ref[0, pl.ds(2 * m, ncp, stride=CMP_STRIDE), :],
                      src_ref[0, pl.ds(2 * m + 1, ncp, stride=CMP_STRIDE), :])
        p_lo = p_lo + _dot((t + pe_ref[0, m:m + 1, :]).astype(BF16), w1_ref[0, m])
        p_hi = p_hi + _dot((t + pe_ref[0, pairs + m:pairs + m + 1, :]).astype(BF16), w1_ref[0, pairs + m])
    hidden = p_lo + pltpu.roll(p_hi, ncp - 1, 0)
    comp = _dot(_silu(hidden).astype(BF16), w2_ref[0])
    m = _lane_in_head(comp.shape)
    cos, sin = _nsa_tables(pos_ref[0].astype(F32), invn_ref[...], m)
    rotated = _rot_nsa(comp, cos, sin, m)
    is_key = pl.program_id(1) == 0
    o_ref[0, 0, 0] = jnp.where(is_key, rotated, comp).astype(BF16)


def _compress(cmps, pe2, w1p, w2d, pos_cmp, inv_nsa_row):
    bsz, seq, _ = cmps.shape
    ncp = seq // CMP_STRIDE
    g = NSA_KV_GROUPS
    return pl.pallas_call(
        _cmp_kernel,
        grid=(bsz, 2, g),
        in_specs=[pl.BlockSpec((1, seq, LANES), lambda b, j, gi: (b, 0, j * g + gi)),
                  pl.BlockSpec((1, CMP_BLOCK // 2, LANES), lambda b, j, gi: (j, 0, 0)),
                  pl.BlockSpec((1, CMP_BLOCK // 2, LANES, CMP_HIDDEN), lambda b, j, gi: (j, 0, 0, 0)),
                  pl.BlockSpec((1, CMP_HIDDEN, LANES), lambda b, j, gi: (j, 0, 0)),
                  pl.BlockSpec((1, ncp, 1), lambda b, j, gi: (b, 0, 0)),
                  pl.BlockSpec((1, LANES), lambda b, j, gi: (0, 0))],
        out_specs=pl.BlockSpec((1, 1, 1, ncp, LANES), lambda b, j, gi: (b, j, gi, 0, 0)),
        out_shape=jax.ShapeDtypeStruct((bsz, 2, g, ncp, LANES), BF16),
        compiler_params=_params("parallel", "arbitrary", "arbitrary"),
        name="compress",
    )(cmps, pe2, w1p, w2d, pos_cmp, inv_nsa_row)


def _cmpsel_kernel(q_ref, kc_ref, vc_ref, ovt_ref, o_ref, selt_ref,
                   qs0_s, qs1_s, s0_s, s1_s, p0_s, p1_s, hi0_s, hi1_s, lo0_s, lo1_s, score_s, *, tq):
    nh = HEADS_PER_GROUP
    sub = 8
    n_qt = q_ref.shape[1] // tq
    ncp = kc_ref.shape[3]
    nblk = ovt_ref.shape[0]
    kc = kc_ref[0, 0, 0]
    vc = vc_ref[0, 0, 0]
    qs_b, s_b, p_b, hi_b, lo_b = (qs0_s, qs1_s), (s0_s, s1_s), (p0_s, p1_s), (hi0_s, hi1_s), (lo0_s, lo1_s)

    def rows_of(i):
        return slice(i * tq, (i + 1) * tq)

    def last_block(i):
        return ((i + 1) * tq - 1) >> SLC_SHIFT

    def n_rows(i):
        return min(nblk, -(-(last_block(i) + 1) // sub) * sub)

    def stack(i, slot):
        _stack_head_queries(q_ref, qs_b[slot], tq, src=rows_of(i))

    def scores(i, slot):
        s_b[slot][...] = _dot_nt(qs_b[slot][...], kc)

    def softmax(i, slot):
        tok = i * tq + lax.broadcasted_iota(jnp.int32, (tq, LANES), 0)
        lane = lax.broadcasted_iota(jnp.int32, (tq, LANES), 1)
        biases = [jnp.where((lane + c * LANES) * CMP_STRIDE + (CMP_BLOCK - 1) <= tok, 0.0, NEG)
                  for c in range(ncp // LANES)]
        has_key = tok >= CMP_BLOCK - 1
        psum = None
        for h in range(nh):
            rows = slice(h * tq, (h + 1) * tq)
            tiles = [t + b for b, t in zip(biases, _lane_tiles(s_b[slot][rows, :]))]
            mx = _row_stat(tiles, jnp.maximum, jnp.max)
            es = [jnp.exp2(t - mx) for t in tiles]
            inv = jnp.where(has_key, 1.0 / _row_stat(es, jnp.add, jnp.sum), 0.0)
            ps = [e * inv for e in es]
            psum = ps if psum is None else [a + b for a, b in zip(psum, ps)]
            p_b[slot][rows, :] = jnp.concatenate(ps, axis=1).astype(BF16)
        psum = jnp.concatenate(psum, axis=1)
        hi = psum.astype(BF16)
        hi_b[slot][...] = hi
        lo_b[slot][...] = (psum - hi.astype(F32)).astype(BF16)

    def values(i, slot):
        pv = _dot(p_b[slot][...], vc)
        _store_heads(o_ref, [pv[h * tq:(h + 1) * tq, :] for h in range(nh)], tq, rows_of(i))
        ov = ovt_ref[0:n_rows(i), :]
        return _dot_nt(ov, hi_b[slot][...]) + _dot_nt(ov, lo_b[slot][...])

    def select(i, imp):
        nr = n_rows(i)
        blk = lax.broadcasted_iota(jnp.int32, (nr, tq), 0)
        cur = (i * tq + lax.broadcasted_iota(jnp.int32, (nr, tq), 1)) >> SLC_SHIFT
        forced = (blk == 0) | (blk == cur) | (blk == cur - 1)
        score_s[0:nr, :] = jnp.where(forced, BIG, jnp.where(blk <= cur, imp, NEG))
        row_in_group = lax.broadcasted_iota(jnp.int32, (sub, tq), 0)
        scs = [score_s[r * sub:(r + 1) * sub, :] for r in range(nr // sub)]
        cnts = [jnp.zeros((sub, tq), F32) for _ in scs]
        for src in range(last_block(i) + 1):
            ri = jnp.broadcast_to(score_s[src:src + 1, :], (sub, tq))
            for r, sc in enumerate(scs):
                if r * sub > src:
                    beats = ri >= sc
                elif (r + 1) * sub - 1 <= src:
                    beats = ri > sc
                else:
                    beats = (ri > sc) | ((ri == sc) & (row_in_group > src - r * sub))
                cnts[r] = cnts[r] + jnp.where(beats, 1.0, 0.0)
        sels = [jnp.where((cnt < float(min(SLC_TOPK, nblk))) & (sc > 0.5 * NEG), 1.0, 0.0)
                for cnt, sc in zip(cnts, scs)]
        pads = [jnp.zeros((HEAD_DIM, tq), F32)] + sels
        if nr < HEAD_DIM:
            pads.append(jnp.zeros((HEAD_DIM - nr, tq), F32))
        selt_ref[0, 0, rows_of(i), :] = jnp.concatenate(pads, axis=0).T.astype(selt_ref.dtype)

    stack(0, 0)
    scores(0, 0)
    if n_qt > 1:
        stack(1, 1)
    for i in range(n_qt):
        slot = i % 2
        imp = values(i - 1, 1 - slot) if i >= 1 else None
        if i + 1 < n_qt:
            scores(i + 1, 1 - slot)
        if i + 2 < n_qt:
            stack(i + 2, slot)
        softmax(i, slot)
        if i >= 1:
            select(i - 1, imp)
    select(n_qt - 1, values(n_qt - 1, (n_qt - 1) % 2))


def _cmpsel(nq, cmpkv, ovt, tq):
    bsz, seq, _ = nq.shape
    g = NSA_KV_GROUPS
    ncp = cmpkv.shape[3]
    nblk = ovt.shape[0]
    gw = HEADS_PER_GROUP * HEAD_DIM
    rows = HEADS_PER_GROUP * tq
    assert nblk <= HEAD_DIM and nblk % 8 == 0
    return pl.pallas_call(
        functools.partial(_cmpsel_kernel, tq=tq),
        grid=(bsz, g),
        in_specs=[pl.BlockSpec((1, seq, gw), lambda b, gi: (b, 0, gi)),
                  pl.BlockSpec((1, 1, 1, ncp, LANES), lambda b, gi: (b, 0, gi, 0, 0)),
                  pl.BlockSpec((1, 1, 1, ncp, LANES), lambda b, gi: (b, 1, gi, 0, 0)),
                  pl.BlockSpec((nblk, ncp), lambda b, gi: (0, 0))],
        out_specs=[pl.BlockSpec((1, seq, gw), lambda b, gi: (b, 0, gi)),
                   pl.BlockSpec((1, 1, seq, LANES), lambda b, gi: (b, gi, 0, 0))],
        out_shape=[jax.ShapeDtypeStruct((bsz, seq, NSA_WIDTH), BF16),
                   jax.ShapeDtypeStruct((bsz, g, seq, LANES), BF16)],
        scratch_shapes=([pltpu.VMEM((rows, LANES), BF16)] * 2 + [pltpu.VMEM((rows, ncp), F32)] * 2 +
                        [pltpu.VMEM((rows, ncp), BF16)] * 2 + [pltpu.VMEM((tq, ncp), BF16)] * 4 +
                        [pltpu.VMEM((nblk, tq), F32)]),
        compiler_params=_params("parallel", "arbitrary"),
        name="cmpsel",
    )(nq, cmpkv, cmpkv, ovt)


def _stack_head_queries(q_ref, qs_s, rows, src=None):
    src = slice(0, rows) if src is None else src
    lo = lax.broadcasted_iota(jnp.int32, (rows, LANES), 1) < HEAD_DIM
    for c in range(HEADS_PER_GROUP // 2):
        q = q_ref[0, src, c * LANES:(c + 1) * LANES]
        zero = jnp.zeros_like(q)
        qs_s[(2 * c) * rows:(2 * c + 1) * rows, :] = jnp.where(lo, q, zero)
        qs_s[(2 * c + 1) * rows:(2 * c + 2) * rows, :] = jnp.where(lo, zero, q)


def _store_heads(o_ref, outs, tq, rows=slice(None)):
    lo = lax.broadcasted_iota(jnp.int32, (tq, LANES), 1) < HEAD_DIM
    for c in range(HEADS_PER_GROUP // 2):
        o_ref[0, rows, c * LANES:(c + 1) * LANES] = jnp.where(lo, outs[2 * c], outs[2 * c + 1]).astype(o_ref.dtype)


def _store_head_sums(o_ref, accs, rows=slice(None)):
    for h, a in enumerate(accs):
        o_ref[0, rows, h * LANES:(h + 1) * LANES] = a.astype(o_ref.dtype)


def _normalized_heads(ref):
    rows, width = ref.shape[1], ref.shape[2]
    lo = lax.broadcasted_iota(jnp.int32, (rows, LANES), 1) < HEAD_DIM
    outs = []
    for c in range(width // (2 * LANES)):
        a = ref[0, :, (2 * c) * LANES:(2 * c + 1) * LANES].astype(F32)
        b = ref[0, :, (2 * c + 1) * LANES:(2 * c + 2) * LANES].astype(F32)
        num = jnp.where(lo, a, pltpu.roll(b, HEAD_DIM, 1))
        den = jnp.where(lo, pltpu.roll(a, HEAD_DIM, 1), b)
        outs.append(num * (1.0 / den))
    return jnp.concatenate(outs, axis=1)


def _lane_tiles(z):
    return [z[:, c * LANES:(c + 1) * LANES] for c in range(z.shape[1] // LANES)]


def _row_stat(tiles, combine, reduce):
    acc = functools.reduce(combine, tiles)
    return jnp.broadcast_to(reduce(acc, axis=-1, keepdims=True), acc.shape)


WIN_UNROLL = 8


def _win_kernel(q_ref, k_ref, v_ref, o_ref, qs0_s, qs1_s, s0_s, s1_s, p0_s, p1_s, *, tq):
    nh = HEADS_PER_GROUP
    span = WINDOW + tq
    n_qt = q_ref.shape[1] // tq
    qs_b, s_b, p_b = (qs0_s, qs1_s), (s0_s, s1_s), (p0_s, p1_s)

    def q_rows(i):
        return pl.ds(pl.multiple_of(i * tq, tq), tq)

    def k_rows(i):
        return pl.ds(pl.multiple_of(jnp.maximum(i * tq - WINDOW, 0), tq), span)

    def stack_queries(i, slot):
        _stack_head_queries(q_ref, qs_b[slot], tq, src=q_rows(jnp.minimum(i, n_qt - 1)))

    def scores(i, slot):
        i = jnp.minimum(i, n_qt - 1)
        s_b[slot][...] = _dot_nt(qs_b[slot][...], k_ref[0, k_rows(i), :])

    def values(i, slot):
        i = jnp.maximum(i, 0)
        pv = _dot(p_b[slot][...], v_ref[0, k_rows(i), :])
        _store_head_sums(o_ref, [pv[h * tq:(h + 1) * tq, :] for h in range(nh)], q_rows(i))

    def softmax(i, slot):
        tok = i * tq + lax.broadcasted_iota(jnp.int32, (tq, span), 0)
        key = jnp.maximum(i * tq - WINDOW, 0) + lax.broadcasted_iota(jnp.int32, (tq, span), 1)
        bias = jnp.where((key <= tok) & (tok - key < WINDOW), 0.0, NEG)
        for h in range(nh):
            rows = slice(h * tq, (h + 1) * tq)
            tiles = _lane_tiles(s_b[slot][rows, :] + bias)
            m = _row_stat(tiles, jnp.maximum, jnp.max)
            p_b[slot][rows, :] = jnp.concatenate([jnp.exp2(t - m) for t in tiles], axis=1).astype(BF16)

    def stages(first, count):
        for t in range(count):
            i, slot = first + t, t % 2
            values(i - 1, 1 - slot)
            scores(i + 1, 1 - slot)
            stack_queries(i + 2, slot)
            softmax(i, slot)

    p1_s[...] = jnp.ones_like(p1_s)
    stack_queries(0, 0)
    stack_queries(1, 1)
    scores(0, 0)

    def body(b, carry):
        stages(b * WIN_UNROLL, WIN_UNROLL)
        return carry

    lax.fori_loop(0, n_qt // WIN_UNROLL, body, 0)
    stages(n_qt - n_qt % WIN_UNROLL, n_qt % WIN_UNROLL)
    values(n_qt - 1, (n_qt - 1) % 2)


SLC_UNROLL = 8
SLC_STATE_BUFFERS = SLC_UNROLL + 1
assert SLC_UNROLL % 2 == 0


def _slc_kernel(q_ref, k_ref, v_ref, selt_ref, o_ref, qs_s, s0_s, s1_s, p0_s, p1_s, a0_s, a1_s,
                m_s, acc_s, *, tq, tk):
    nh = HEADS_PER_GROUP
    seq = q_ref.shape[1]
    n_qt = seq // tq
    s_b, p_b, a_b = (s0_s, s1_s), (p0_s, p1_s), (a0_s, a1_s)
    lo = lax.broadcasted_iota(jnp.int32, (tq, LANES), 1) < HEAD_DIM

    def n_tiles(qi):
        return (qi * tq + tq - 1) // tk + 1

    n_items = sum((qi * tq + tq - 1) // tk + 1 for qi in range(n_qt))

    def advance(qi, j):
        wrap = j + 1 >= n_tiles(qi)
        return jnp.where(wrap, jnp.minimum(qi + 1, n_qt - 1), qi), jnp.where(wrap, 0, j + 1)

    def q_rows(qi):
        return pl.ds(pl.multiple_of(qi * tq, tq), tq)

    def k_rows(j):
        return pl.ds(pl.multiple_of(j * tk, tk), tk)

    def stack_queries(qi):
        par = qi % SLC_STATE_BUFFERS
        unselected = jnp.where(selt_ref[0, 0, q_rows(qi), :].astype(F32) > 0.5, 0.0, NEG)
        for c in range(nh // 2):
            q = q_ref[0, q_rows(qi), c * LANES:(c + 1) * LANES].astype(F32)
            qs_s[par, (2 * c) * tq:(2 * c + 1) * tq, :] = jnp.where(lo, q, unselected).astype(BF16)
            qs_s[par, (2 * c + 1) * tq:(2 * c + 2) * tq, :] = jnp.where(
                lo, pltpu.roll(q, HEAD_DIM, 1), unselected).astype(BF16)

    def scores(qi, j, slot):
        s_b[slot][...] = _dot_nt(qs_s[qi % SLC_STATE_BUFFERS], k_ref[0, k_rows(j), :])

    def values(qi, j, slot):
        pv = _dot(p_b[slot][...], v_ref[0, k_rows(j), :])
        par = qi % SLC_STATE_BUFFERS
        for h in range(nh):
            acc_s[par, h] = a_b[slot][h] * acc_s[par, h] + pv[h * tq:(h + 1) * tq, :]

    def softmax(qi, j, slot):
        tok = qi * tq + lax.broadcasted_iota(jnp.int32, (tq, tk), 0)
        key = j * tk + lax.broadcasted_iota(jnp.int32, (tq, tk), 1)
        bias = jnp.where(key <= tok, 0.0, NEG)
        for h in range(nh):
            rows = slice(h * tq, (h + 1) * tq)
            tiles = _lane_tiles(s_b[slot][rows, :] + bias)
            m_prev = jnp.where(j == 0, NEG, m_s[h])
            m_new = jnp.maximum(m_prev, _row_stat(tiles, jnp.maximum, jnp.max))
            m_s[h] = m_new
            a_b[slot][h] = jnp.exp2(m_prev - m_new)
            p_b[slot][rows, :] = jnp.concatenate([jnp.exp2(t - m_new) for t in tiles], axis=1).astype(BF16)

    def finalize(qi):
        _store_head_sums(o_ref, [acc_s[qi % SLC_STATE_BUFFERS, h] for h in range(nh)], q_rows(qi))

    def is_last(qi, j):
        return (j + 1 >= n_tiles(qi)).astype(jnp.int32)

    def finalize_if(flag, qi):
        @pl.when(flag == 1)
        def _():
            finalize(qi)

    def block(carry, n_stages):
        fin, (qp, jp, p_last), (qc, jc), (qn, jn) = carry
        for flag, qi in fin:
            finalize_if(flag, qi)
        items = [(qp, jp, p_last), (qc, jc, is_last(qc, jc)), (qn, jn, is_last(qn, jn))]
        for _ in range(n_stages - 1):
            qx, jx = advance(items[-1][0], items[-1][1])
            items.append((qx, jx, is_last(qx, jx)))
        for qx, jx, _ in items[2:]:
            @pl.when(jx == 0)
            def _():
                stack_queries(qx)
        for t in range(n_stages):
            slot = t % 2
            values(items[t][0], items[t][1], 1 - slot)
            scores(items[t + 2][0], items[t + 2][1], 1 - slot)
            softmax(items[t + 1][0], items[t + 1][1], slot)
        fin = tuple((items[t][2], items[t][0]) for t in range(n_stages))
        nxt = advance(items[-1][0], items[-1][1])
        return fin, items[n_stages], items[n_stages + 1][:2], nxt

    m_s[...] = jnp.full_like(m_s, NEG)
    acc_s[...] = jnp.zeros_like(acc_s)
    p1_s[...] = jnp.zeros_like(p1_s)
    a1_s[...] = jnp.ones_like(a1_s)
    zero = jnp.int32(0)
    stack_queries(zero)
    scores(zero, zero, 0)
    unroll = SLC_UNROLL
    carry = (((zero, zero),) * unroll, (zero, zero, zero), (zero, zero), advance(zero, zero))
    carry = lax.fori_loop(0, n_items // unroll, lambda i, c: block(c, unroll), carry)
    if n_items % unroll:
        carry = block(carry, n_items % unroll)
    fin, (qp, jp, _), _, _ = carry
    for flag, qi in fin:
        finalize_if(flag, qi)
    values(qp, jp, (n_items - 1) % 2)
    finalize(qp)


def _slc_attention(nq, kd, vd, selt, *, tq, tk):
    bsz, seq, _ = nq.shape
    g = NSA_KV_GROUPS
    gw = HEADS_PER_GROUP * HEAD_DIM
    nh = HEADS_PER_GROUP
    assert seq % tk == 0 and tk % tq == 0
    assert seq // SLC_BLOCK <= HEAD_DIM
    whole = lambda w: pl.BlockSpec((1, seq, w), lambda b, gi: (b, 0, gi))
    stat = pltpu.VMEM((nh, tq, LANES), F32)
    state = pltpu.VMEM((SLC_STATE_BUFFERS, nh, tq, LANES), F32)
    return pl.pallas_call(
        functools.partial(_slc_kernel, tq=tq, tk=tk),
        grid=(bsz, g),
        in_specs=[whole(gw), whole(LANES), whole(LANES),
                  pl.BlockSpec((1, 1, seq, LANES), lambda b, gi: (b, gi, 0, 0))],
        out_specs=whole(nh * LANES),
        out_shape=jax.ShapeDtypeStruct((bsz, seq, g * nh * LANES), BF16),
        scratch_shapes=([pltpu.VMEM((SLC_STATE_BUFFERS, nh * tq, LANES), BF16)] + [pltpu.VMEM((nh * tq, tk), F32)] * 2 +
                        [pltpu.VMEM((nh * tq, tk), BF16)] * 2 + [stat] * 3 + [state]),
        compiler_params=_params("parallel", "arbitrary"),
        name="slc_attn",
    )(nq, kd, vd, selt)


def _win_attention(nq, kd, vd, *, tq):
    bsz, seq, _ = nq.shape
    g = NSA_KV_GROUPS
    gw = HEADS_PER_GROUP * HEAD_DIM
    nh = HEADS_PER_GROUP
    assert WINDOW % tq == 0 and seq >= WINDOW + tq and seq // tq >= 2 and WIN_UNROLL % 2 == 0
    whole = lambda w: pl.BlockSpec((1, seq, w), lambda b, gi: (b, 0, gi))
    span = WINDOW + tq
    return pl.pallas_call(
        functools.partial(_win_kernel, tq=tq),
        grid=(bsz, g),
        in_specs=[whole(gw), whole(LANES), whole(LANES)],
        out_specs=whole(nh * LANES),
        out_shape=jax.ShapeDtypeStruct((bsz, seq, g * nh * LANES), BF16),
        scratch_shapes=([pltpu.VMEM((nh * tq, LANES), BF16)] * 2 + [pltpu.VMEM((nh * tq, span), F32)] * 2 +
                        [pltpu.VMEM((nh * tq, span), BF16)] * 2),
        compiler_params=_params("parallel", "arbitrary"),
        name="win_attn",
    )(nq, kd, vd)


def _out_kernel(x_ref, gate_ref, gpost_ref, ret_ref, oc_ref, os_ref, ow_ref, gl_ref, ngs_ref,
                wo_ref, o_ref):
    ts = x_ref.shape[1]
    gates = gl_ref[0]
    lo_off = 32
    assert 3 * NSA_HEADS <= lo_off
    g_hi = gates.astype(BF16).astype(F32)
    g_lo = (gates - g_hi).astype(BF16).astype(F32)
    lane = lax.broadcasted_iota(jnp.int32, (ts, LANES), 1)
    g_split = jnp.where(lane < lo_off, g_hi, pltpu.roll(g_lo, lo_off, 1)).astype(BF16)
    src = lax.broadcasted_iota(jnp.int32, (LANES, NSA_WIDTH), 0)
    head = lax.broadcasted_iota(jnp.int32, (LANES, NSA_WIDTH), 1) >> HEAD_SHIFT
    att = jnp.zeros((ts, NSA_WIDTH), F32)
    for i, branch in enumerate((oc_ref[0], _normalized_heads(os_ref), _normalized_heads(ow_ref))):
        expand = jnp.where((src == 3 * head + i) | (src == lo_off + 3 * head + i), 1.0, 0.0).astype(BF16)
        att = att + _dot(g_split, expand) * branch
    att = att * ngs_ref[0]
    y = _dot(ret_ref[0].astype(BF16), wo_ref[:RET_WIDTH, :]) + _dot(att.astype(BF16), wo_ref[RET_WIDTH:, :])
    ms = jnp.mean(y * y, axis=-1, keepdims=True)
    y = y * lax.rsqrt(ms + EPS) * gpost_ref[...]
    o_ref[0] = x_ref[0] + gate_ref[0] * y


def _out(x, gate, g_post, ret, o_cmp, o_slc, o_win, gl, ngs, w_out, ts):
    bsz, seq, d = x.shape
    row = lambda w: pl.BlockSpec((1, ts, w), lambda b, s: (b, s, 0))
    return pl.pallas_call(
        _out_kernel,
        grid=(bsz, seq // ts),
        in_specs=[row(d), pl.BlockSpec((1, 1, d), lambda b, s: (b, 0, 0)),
                  pl.BlockSpec((1, d), lambda b, s: (0, 0)),
                  row(RET_WIDTH), row(NSA_WIDTH), row(o_slc.shape[2]), row(o_win.shape[2]), row(LANES),
                  row(NSA_WIDTH), pl.BlockSpec(w_out.shape, lambda b, s: (0, 0))],
        out_specs=row(d),
        out_shape=jax.ShapeDtypeStruct((bsz, seq, d), x.dtype),
        compiler_params=_params("parallel", "arbitrary"),
        name="out",
    )(x, gate, g_post, ret, o_cmp, o_slc, o_win, gl, ngs, w_out)


def _retention_tables():
    h = RET_HEADS
    c = RET_CHUNK
    log_g = jnp.log1p(-jnp.power(2.0, -5.0 - jnp.arange(h, dtype=F32)))
    idx = jnp.arange(c, dtype=F32)
    diff = idx[:, None] - idx[None, :]
    dec = jnp.where(diff[None] >= 0, jnp.exp(jnp.maximum(diff, 0.0)[None] * log_g[:, None, None]), 0.0)
    zeta = jnp.exp((c - 1 - idx)[None, :] * log_g[:, None])
    xi = jnp.exp((idx + 1.0)[None, :] * log_g[:, None])
    cd = jnp.exp(c * log_g)
    to_lanes = lambda t: jnp.repeat(t.reshape(h // 2, 2, c).transpose(0, 2, 1), HEAD_DIM, axis=2)
    cd_l = jnp.repeat(cd.reshape(h // 2, 1, 2), HEAD_DIM, axis=2)
    return dec, to_lanes(zeta), to_lanes(xi), cd_l


def _overlap_t(seq):
    ncp = seq // CMP_STRIDE
    nc = (seq - CMP_BLOCK) // CMP_STRIDE + 1
    ns = seq // SLC_BLOCK
    cs = np.arange(ncp) * CMP_STRIDE
    ce = cs + CMP_BLOCK - 1
    ss = np.arange(ns) * SLC_BLOCK
    ov = (cs[None, :] <= ss[:, None] + SLC_BLOCK - 1) & (ce[None, :] >= ss[:, None]) & (np.arange(ncp) < nc)[None, :]
    return jnp.asarray(ov.astype(np.float32), dtype=BF16)


def _layer(x, c, positions, mod, g_pre, g_post, w_in, w_out,
           pe_k, w1_k, w2_k, pe_v, w1_v, w2_v):
    bsz, seq, d = x.shape
    shift, scale, gate = [t.reshape(bsz, 1, d) for t in jnp.split(mod, 3, axis=-1)]

    offs = np.cumsum((RET_WIDTH,) * 4 + (NSA_WIDTH, 6 * KV_WIDTH, 3 * NSA_HEADS, NSA_WIDTH))
    w_gl = w_in[:, offs[5]:offs[6]]
    w_all = jnp.concatenate(
        [w_in[:, :offs[5]], w_in[:, offs[6]:], w_gl,
         jnp.zeros((d, LANES - w_gl.shape[1]), w_in.dtype)], axis=1).astype(BF16)

    lanes = jnp.arange(LANES)
    half_r = HEAD_DIM // 2
    inv_r = jnp.power(RET_ROPE_BASE, -jnp.arange(half_r, dtype=F32) / half_r)
    half_n = ROPE_DIM // 2
    inv_n = jnp.power(NSA_ROPE_BASE, -jnp.arange(half_n, dtype=F32) / half_n)
    inv_ret_row = inv_r[lanes % half_r].reshape(1, LANES)
    inv_nsa_row = inv_n[lanes % half_n].reshape(1, LANES)
    pos3 = positions.reshape(bsz, seq, 1)

    (rq, rk, rv, rgs, nq, cmps, ksd, vsd, kwd, vwd, ngs, gl) = _proj(
        x, shift, scale, g_pre.reshape(1, d), pos3, inv_ret_row, inv_nsa_row, w_all, ts=min(PROJ_ROWS, seq))

    dec, zeta_l, xi_l, cd_l = _retention_tables()
    ret = _retention(rq, rk, rv, rgs, dec, zeta_l, xi_l, cd_l, rows=min(RET_ROWS, seq))

    ncp = seq // CMP_STRIDE
    nc = (seq - CMP_BLOCK) // CMP_STRIDE + 1
    pe2 = jnp.stack([pe_k, pe_v]).reshape(2, CMP_BLOCK // 2, LANES)
    w1p = jnp.stack([w1_k, w1_v]).reshape(2, CMP_BLOCK // 2, LANES, CMP_HIDDEN).astype(BF16)
    w2s = jnp.stack([w2_k, w2_v])
    w2d = jnp.concatenate([w2s, w2s], axis=-1).astype(BF16)
    cmp_end = np.arange(nc) * CMP_STRIDE + CMP_BLOCK - 1
    pos_cmp = jnp.pad(positions[:, cmp_end], ((0, 0), (0, ncp - nc))).reshape(bsz, ncp, 1)
    cmpkv = _compress(cmps, pe2, w1p, w2d, pos_cmp, inv_nsa_row)

    o_cmp, selt = _cmpsel(nq, cmpkv, _overlap_t(seq), tq=min(CMPSEL_TQ, seq))
    o_slc = _slc_attention(nq, ksd, vsd, selt, tq=SLC_TQ, tk=SLC_TK)
    o_win = _win_attention(nq, kwd, vwd, tq=WIN_TQ)

    return _out(x, gate, g_post.reshape(1, d), ret, o_cmp, o_slc, o_win, gl, ngs,
                w_out.astype(BF16), ts=min(PROJ_ROWS, seq))


def kernel(x, c, positions, w_ada, b_ada, g_pre, g_post, w_in, w_out, cmp_pe_k, cmp_w1_k, cmp_w2_k, cmp_pe_v, cmp_w1_v, cmp_w2_v):
    for l in range(w_in.shape[0]):
        x = _layer(x, c, positions, _ada(c, w_ada, b_ada, l), g_pre[l], g_post[l], w_in[l], w_out[l],
                   cmp_pe_k[l], cmp_w1_k[l], cmp_w2_k[l], cmp_pe_v[l], cmp_w1_v[l], cmp_w2_v[l])
    return x
```

```python
import functools

import numpy as np
import jax
import jax.numpy as jnp
from jax import lax
from jax.experimental import pallas as pl
from jax.experimental.pallas import tpu as pltpu

F32 = jnp.float32
BF16 = jnp.bfloat16

LANES = 128
HEAD_DIM = 64
RET_HEADS = 8
NSA_HEADS = 8
NSA_KV_GROUPS = 2
HEADS_PER_GROUP = NSA_HEADS // NSA_KV_GROUPS
RET_WIDTH = RET_HEADS * HEAD_DIM
NSA_WIDTH = NSA_HEADS * HEAD_DIM
KV_WIDTH = NSA_KV_GROUPS * HEAD_DIM
RET_CHUNK = 128
RET_ROPE_BASE = 10000.0
NSA_ROPE_BASE = 500000.0
ROPE_DIM = HEAD_DIM // 4
CMP_BLOCK = 32
CMP_STRIDE = 16
CMP_HIDDEN = 256
SLC_BLOCK = 64
SLC_TOPK = 16
WINDOW = 512
NEG = -1e30
LOG2E = 1.4426950408889634
BIG = 1e9
EPS = 1e-6
GN_EPS = 1e-5
VMEM_LIMIT = 48 * 1024 * 1024

PROJ_ROWS = 512
OUT_ROWS = 1024
RET_ROWS = 4096
CMPSEL_TQ = 256
SLC_TQ, SLC_TK = 256, 512
WIN_TQ = 128

NT_DIMS = (((1,), (1,)), ((), ()))
TN_DIMS = (((0,), (0,)), ((), ()))


def _dot(a, b):
    return jnp.dot(a, b, preferred_element_type=F32)


def _dot_nt(a, b):
    return lax.dot_general(a, b, NT_DIMS, preferred_element_type=F32)


def _dot_tn(a, b):
    return lax.dot_general(a, b, TN_DIMS, preferred_element_type=F32)


def _silu(z):
    return z * jax.nn.sigmoid(z)


def _params(*sem):
    return pltpu.CompilerParams(dimension_semantics=sem, vmem_limit_bytes=VMEM_LIMIT)


HEAD_SHIFT = 6
SLC_SHIFT = 6
assert 1 << HEAD_SHIFT == HEAD_DIM and 1 << SLC_SHIFT == SLC_BLOCK


def _lane_in_head(shape):
    return lax.broadcasted_iota(jnp.int32, shape, len(shape) - 1) & (HEAD_DIM - 1)


def _ada_kernel(c_ref, w_ref, b_ref, o_ref):
    a = _silu(c_ref[...])
    o_ref[...] = jnp.dot(a, w_ref[0], precision=lax.Precision.HIGHEST,
                         preferred_element_type=F32) + b_ref[0]


def _ada(c, w_layers, b_layers, layer):
    bsz, d = c.shape
    depth, _, n = w_layers.shape
    tn = d
    return pl.pallas_call(
        _ada_kernel,
        grid=(n // tn,),
        in_specs=[pl.BlockSpec((bsz, d), lambda j: (0, 0)),
                  pl.BlockSpec((1, d, tn), lambda j: (layer, 0, j)),
                  pl.BlockSpec((1, 1, tn), lambda j: (layer, 0, j))],
        out_specs=pl.BlockSpec((bsz, tn), lambda j: (0, j)),
        out_shape=jax.ShapeDtypeStruct((bsz, n), F32),
        compiler_params=_params("arbitrary"),
        name="ada",
    )(c, w_layers, b_layers.reshape(depth, 1, n))


_O_RQ, _O_RK, _O_RV, _O_RG, _O_NQ = 0, 512, 1024, 1536, 2048
_O_KV = 2560
_O_NG = 3328
_O_GL = 3840
_PROJ_COLS = 3968


def _rot_ret(z, cos, sin_signed, m):
    partner = jnp.where(m < HEAD_DIM // 2, pltpu.roll(z, LANES - HEAD_DIM // 2, 1),
                        pltpu.roll(z, HEAD_DIM // 2, 1))
    return z * cos + partner * sin_signed


def _rot_nsa(z, cos, sin_signed, m):
    half = ROPE_DIM // 2
    partner = jnp.where(m < half, pltpu.roll(z, LANES - half, 1), pltpu.roll(z, half, 1))
    return z * cos + partner * sin_signed


def _nsa_tables(posf, inv_row, m):
    half = ROPE_DIM // 2
    ang = posf * inv_row
    cos = jnp.where(m < ROPE_DIM, jnp.cos(ang), 1.0)
    s = jnp.sin(ang)
    sin = jnp.where(m < half, -s, jnp.where(m < ROPE_DIM, s, 0.0))
    return cos, sin


def _dup_groups(z, m_lane):
    r = pltpu.roll(z, HEAD_DIM, 1)
    lo = m_lane < HEAD_DIM
    return jnp.where(lo, z, r), jnp.where(lo, r, z)


def _proj_kernel(x_ref, shift_ref, scale_ref, gpre_ref, pos_ref, invr_ref, invn_ref, w_ref,
                 rq_ref, rk_ref, rv_ref, rgs_ref, nq_ref, cmps_ref, ksd_ref, vsd_ref,
                 kwd_ref, vwd_ref, ngs_ref, gl_ref,
                 cr_s, sr_s, cn_s, sn_s, z0_s, z1_s):
    ts = x_ref.shape[1]
    x = x_ref[0]
    ms = jnp.mean(x * x, axis=-1, keepdims=True)
    y = x * lax.rsqrt(ms + EPS) * gpre_ref[...]
    h = y * (1.0 + scale_ref[0]) + shift_ref[0]
    hb = h.astype(BF16)

    posf = pos_ref[0].astype(F32)
    lane = lax.broadcasted_iota(jnp.int32, (ts, LANES), 1)
    m = lane & (HEAD_DIM - 1)
    ang = posf * invr_ref[...]
    cr_s[...] = jnp.cos(ang)
    s = jnp.sin(ang)
    sr_s[...] = jnp.where(m < HEAD_DIM // 2, -s, s)
    cn, sn = _nsa_tables(posf, invn_ref[...], m)
    cn_s[...] = cn
    sn_s[...] = sn

    staging = [z0_s, z1_s]

    def mm(off, width):
        z_s = staging[0]
        staging.reverse()
        z_s[:, :width] = _dot(hb, w_ref[:, off:off + width])
        return lambda c: z_s[:, c * LANES:(c + 1) * LANES]

    scale_q = HEAD_DIM ** -0.5
    tiles = lambda width: [(c, slice(c * LANES, (c + 1) * LANES)) for c in range(width // LANES)]
    z = mm(_O_RQ, RET_WIDTH)
    for c, sl in tiles(RET_WIDTH):
        rq_ref[0, :, sl] = (_rot_ret(z(c), cr_s[...], sr_s[...], m) * scale_q).astype(BF16)
    z = mm(_O_RK, RET_WIDTH)
    for c, sl in tiles(RET_WIDTH):
        rk_ref[0, :, sl] = _rot_ret(z(c), cr_s[...], sr_s[...], m).astype(BF16)
    z = mm(_O_RV, RET_WIDTH)
    for c, sl in tiles(RET_WIDTH):
        rv_ref[0, :, sl] = z(c).astype(BF16)
    z = mm(_O_RG, RET_WIDTH)
    for c, sl in tiles(RET_WIDTH):
        rgs_ref[0, :, sl] = _silu(z(c)).astype(BF16)
    z = mm(_O_NQ, NSA_WIDTH)
    for c, sl in tiles(NSA_WIDTH):
        nq_ref[0, :, sl] = (_rot_nsa(z(c), cn_s[...], sn_s[...], m) * (scale_q * LOG2E)).astype(BF16)

    z = mm(_O_KV, 6 * KV_WIDTH)
    kc0, kc1 = _dup_groups(z(0), lane)
    vc0, vc1 = _dup_groups(z(1), lane)
    for i, t in enumerate((kc0, kc1, vc0, vc1)):
        cmps_ref[0, :, i * LANES:(i + 1) * LANES] = t
    lo = lane < HEAD_DIM
    tok = pl.program_id(1) * ts + lax.broadcasted_iota(jnp.int32, (ts, LANES), 0)
    onehot = jnp.where(lane - HEAD_DIM == (tok >> SLC_SHIFT), 1.0, 0.0)
    ones = jnp.ones((ts, LANES), F32)
    for c, ref, rotate, upper in ((2, ksd_ref, True, onehot), (3, vsd_ref, False, ones),
                                  (4, kwd_ref, True, None), (5, vwd_ref, False, ones)):
        t = _rot_nsa(z(c), cn_s[...], sn_s[...], m) if rotate else z(c)
        if upper is None:
            d0, d1 = _dup_groups(t, lane)
        else:
            d0, d1 = jnp.where(lo, t, upper), jnp.where(lo, pltpu.roll(t, HEAD_DIM, 1), upper)
        ref[0, :, :LANES] = d0.astype(BF16)
        ref[0, :, LANES:] = d1.astype(BF16)

    z = mm(_O_NG, NSA_WIDTH + LANES)
    for c, sl in tiles(NSA_WIDTH):
        ngs_ref[0, :, sl] = _silu(z(c)).astype(BF16)
    gl_ref[0] = jax.nn.sigmoid(z(NSA_WIDTH // LANES))


def _proj(x, shift, scale, g_pre, pos3, inv_ret_row, inv_nsa_row, w_all, ts):
    bsz, seq, d = x.shape
    row = lambda w: pl.BlockSpec((1, ts, w), lambda b, s: (b, s, 0))
    vec = pl.BlockSpec((1, 1, d), lambda b, s: (b, 0, 0))
    const2 = lambda shp: pl.BlockSpec(shp, lambda b, s: (0, 0))
    outs = [(RET_WIDTH, BF16), (RET_WIDTH, BF16), (RET_WIDTH, BF16), (RET_WIDTH, BF16),
            (NSA_WIDTH, BF16), (4 * LANES, F32), (2 * LANES, BF16), (2 * LANES, BF16),
            (2 * LANES, BF16), (2 * LANES, BF16), (NSA_WIDTH, BF16), (LANES, F32)]
    return pl.pallas_call(
        _proj_kernel,
        grid=(bsz, seq // ts),
        in_specs=[row(d), vec, vec, const2((1, d)), row(1), const2((1, LANES)), const2((1, LANES)),
                  const2((d, _PROJ_COLS))],
        out_specs=[row(w) for w, _ in outs],
        out_shape=[jax.ShapeDtypeStruct((bsz, seq, w), dt) for w, dt in outs],
        scratch_shapes=[pltpu.VMEM((ts, LANES), F32)] * 4 + [pltpu.VMEM((ts, 6 * KV_WIDTH), F32)] * 2,
        compiler_params=_params("parallel", "arbitrary"),
        name="proj",
    )(x, shift, scale, g_pre, pos3, inv_ret_row, inv_nsa_row, w_all)


def _ret_kernel(q_ref, k_ref, v_ref, g_ref, dec_ref, zeta_ref, xi_ref, cd_ref, o_ref, r_s):
    rows = q_ref.shape[1]
    cc = RET_CHUNK

    @pl.when(pl.program_id(2) == 0)
    def _():
        r_s[...] = jnp.zeros_like(r_s)

    lane = lax.broadcasted_iota(jnp.int32, (cc, LANES), 1)
    rowi = lax.broadcasted_iota(jnp.int32, (cc, LANES), 0)
    lo = lane < HEAD_DIM
    blockdiag = (rowi < HEAD_DIM) == lo
    inv_n = 1.0 / HEAD_DIM
    chunks = [slice(c * cc, (c + 1) * cc) for c in range(rows // cc)]
    s_ab, kvs = [], []
    for sl in chunks:
        q, k, v = q_ref[0, sl, :], k_ref[0, sl, :], v_ref[0, sl, :]
        zero = jnp.zeros_like(q)
        s_ab.append((_dot_nt(jnp.where(lo, q, zero), k), _dot_nt(jnp.where(lo, zero, q), k)))
        kz = (k.astype(F32) * zeta_ref[0]).astype(BF16)
        kvs.append(jnp.where(blockdiag, _dot_tn(kz, v), 0.0))
    inners = []
    for sl, (s_a, s_b) in zip(chunks, s_ab):
        v = v_ref[0, sl, :]
        inners.append(jnp.where(lo, _dot((s_a * dec_ref[0]).astype(BF16), v),
                                _dot((s_b * dec_ref[1]).astype(BF16), v)))
    r = r_s[...]
    outs = []
    for sl, inner, kv in zip(chunks, inners, kvs):
        outs.append(inner + _dot(q_ref[0, sl, :], r.astype(BF16)) * xi_ref[0])
        r = r * cd_ref[0] + kv
    r_s[...] = r
    head_ones = jnp.where(blockdiag, 1.0, 0.0).astype(BF16)

    def head_sums(z):
        hi = z.astype(BF16)
        lo_part = (z - hi.astype(F32)).astype(BF16)
        return _dot(hi, head_ones) + _dot(lo_part, head_ones)

    devs = [o - s * inv_n for o, s in zip(outs, [head_sums(o) for o in outs])]
    variances = [head_sums(d * d) * inv_n for d in devs]
    for sl, d, var in zip(chunks, devs, variances):
        o_ref[0, sl, :] = (d * lax.rsqrt(var + GN_EPS) * g_ref[0, sl, :]).astype(o_ref.dtype)


def _retention(rq, rk, rv, rgs, dec, zeta_l, xi_l, cd_l, rows):
    bsz, seq, _ = rq.shape
    npair = RET_HEADS // 2
    blk = pl.BlockSpec((1, rows, LANES), lambda b, p, s: (b, s, p))
    return pl.pallas_call(
        _ret_kernel,
        grid=(bsz, npair, seq // rows),
        in_specs=[blk, blk, blk, blk,
                  pl.BlockSpec((2, RET_CHUNK, RET_CHUNK), lambda b, p, s: (p, 0, 0)),
                  pl.BlockSpec((1, RET_CHUNK, LANES), lambda b, p, s: (p, 0, 0)),
                  pl.BlockSpec((1, RET_CHUNK, LANES), lambda b, p, s: (p, 0, 0)),
                  pl.BlockSpec((1, 1, LANES), lambda b, p, s: (p, 0, 0))],
        out_specs=blk,
        out_shape=jax.ShapeDtypeStruct((bsz, seq, RET_WIDTH), BF16),
        scratch_shapes=[pltpu.VMEM((LANES, LANES), F32)],
        compiler_params=_params("parallel", "parallel", "arbitrary"),
        name="retention",
    )(rq, rk, rv, rgs, dec, zeta_l, xi_l, cd_l)


def _cmp_kernel(src_ref, pe_ref, w1_ref, w2_ref, pos_ref, invn_ref, o_ref):
    ncp = o_ref.shape[3]
    pairs = CMP_BLOCK // 4
    lo = lax.broadcasted_iota(jnp.int32, (ncp, LANES), 1) < HEAD_DIM
    p_lo = jnp.zeros((ncp, CMP_HIDDEN), F32)
    p_hi = jnp.zeros((ncp, CMP_HIDDEN), F32)
    for m in range(pairs):
        t = jnp.where(lo, src_ref[0, pl.ds(2 * m, ncp, stride=CMP_STRIDE), :],
                      src_ref[0, pl.ds(2 * m + 1, ncp, stride=CMP_STRIDE), :])
        p_lo = p_lo + _dot((t + pe_ref[0, m:m + 1, :]).astype(BF16), w1_ref[0, m])
        p_hi = p_hi + _dot((t + pe_ref[0, pairs + m:pairs + m + 1, :]).astype(BF16), w1_ref[0, pairs + m])
    hidden = p_lo + pltpu.roll(p_hi, ncp - 1, 0)
    comp = _dot(_silu(hidden).astype(BF16), w2_ref[0])
    m = _lane_in_head(comp.shape)
    cos, sin = _nsa_tables(pos_ref[0].astype(F32), invn_ref[...], m)
    rotated = _rot_nsa(comp, cos, sin, m)
    is_key = pl.program_id(1) == 0
    o_ref[0, 0, 0] = jnp.where(is_key, rotated, comp).astype(BF16)


def _compress(cmps, pe2, w1p, w2d, pos_cmp, inv_nsa_row):
    bsz, seq, _ = cmps.shape
    ncp = seq // CMP_STRIDE
    g = NSA_KV_GROUPS
    return pl.pallas_call(
        _cmp_kernel,
        grid=(bsz, 2, g),
        in_specs=[pl.BlockSpec((1, seq, LANES), lambda b, j, gi: (b, 0, j * g + gi)),
                  pl.BlockSpec((1, CMP_BLOCK // 2, LANES), lambda b, j, gi: (j, 0, 0)),
                  pl.BlockSpec((1, CMP_BLOCK // 2, LANES, CMP_HIDDEN), lambda b, j, gi: (j, 0, 0, 0)),
                  pl.BlockSpec((1, CMP_HIDDEN, LANES), lambda b, j, gi: (j, 0, 0)),
                  pl.BlockSpec((1, ncp, 1), lambda b, j, gi: (b, 0, 0)),
                  pl.BlockSpec((1, LANES), lambda b, j, gi: (0, 0))],
        out_specs=pl.BlockSpec((1, 1, 1, ncp, LANES), lambda b, j, gi: (b, j, gi, 0, 0)),
        out_shape=jax.ShapeDtypeStruct((bsz, 2, g, ncp, LANES), BF16),
        compiler_params=_params("parallel", "arbitrary", "arbitrary"),
        name="compress",
    )(cmps, pe2, w1p, w2d, pos_cmp, inv_nsa_row)


def _cmpsel_kernel(q_ref, kc_ref, vc_ref, ovt_ref, o_ref, selt_ref,
                   qs0_s, qs1_s, s0_s, s1_s, p0_s, p1_s, hi0_s, hi1_s, lo0_s, lo1_s, score_s, *, tq):
    nh = HEADS_PER_GROUP
    sub = 8
    n_qt = q_ref.shape[1] // tq
    ncp = kc_ref.shape[3]
    nblk = ovt_ref.shape[0]
    kc = kc_ref[0, 0, 0]
    vc = vc_ref[0, 0, 0]
    qs_b, s_b, p_b, hi_b, lo_b = (qs0_s, qs1_s), (s0_s, s1_s), (p0_s, p1_s), (hi0_s, hi1_s), (lo0_s, lo1_s)

    def rows_of(i):
        return slice(i * tq, (i + 1) * tq)

    def last_block(i):
        return ((i + 1) * tq - 1) >> SLC_SHIFT

    def n_rows(i):
        return min(nblk, -(-(last_block(i) + 1) // sub) * sub)

    def stack(i, slot):
        _stack_head_queries(q_ref, qs_b[slot], tq, src=rows_of(i))

    def scores(i, slot):
        s_b[slot][...] = _dot_nt(qs_b[slot][...], kc)

    def softmax(i, slot):
        tok = i * tq + lax.broadcasted_iota(jnp.int32, (tq, LANES), 0)
        lane = lax.broadcasted_iota(jnp.int32, (tq, LANES), 1)
        biases = [jnp.where((lane + c * LANES) * CMP_STRIDE + (CMP_BLOCK - 1) <= tok, 0.0, NEG)
                  for c in range(ncp // LANES)]
        has_key = tok >= CMP_BLOCK - 1
        psum = None
        for h in range(nh):
            rows = slice(h * tq, (h + 1) * tq)
            tiles = [t + b for b, t in zip(biases, _lane_tiles(s_b[slot][rows, :]))]
            mx = _row_stat(tiles, jnp.maximum, jnp.max)
            es = [jnp.exp2(t - mx) for t in tiles]
            inv = jnp.where(has_key, 1.0 / _row_stat(es, jnp.add, jnp.sum), 0.0)
            ps = [e * inv for e in es]
            psum = ps if psum is None else [a + b for a, b in zip(psum, ps)]
            p_b[slot][rows, :] = jnp.concatenate(ps, axis=1).astype(BF16)
        psum = jnp.concatenate(psum, axis=1)
        hi = psum.astype(BF16)
        hi_b[slot][...] = hi
        lo_b[slot][...] = (psum - hi.astype(F32)).astype(BF16)

    def values(i, slot):
        pv = _dot(p_b[slot][...], vc)
        _store_heads(o_ref, [pv[h * tq:(h + 1) * tq, :] for h in range(nh)], tq, rows_of(i))
        ov = ovt_ref[0:n_rows(i), :]
        return _dot_nt(ov, hi_b[slot][...]) + _dot_nt(ov, lo_b[slot][...])

    def select(i, imp):
        nr = n_rows(i)
        blk = lax.broadcasted_iota(jnp.int32, (nr, tq), 0)
        cur = (i * tq + lax.broadcasted_iota(jnp.int32, (nr, tq), 1)) >> SLC_SHIFT
        forced = (blk == 0) | (blk == cur) | (blk == cur - 1)
        score_s[0:nr, :] = jnp.where(forced, BIG, jnp.where(blk <= cur, imp, NEG))
        row_in_group = lax.broadcasted_iota(jnp.int32, (sub, tq), 0)
        scs = [score_s[r * sub:(r + 1) * sub, :] for r in range(nr // sub)]
        cnts = [jnp.zeros((sub, tq), F32) for _ in scs]
        for src in range(last_block(i) + 1):
            ri = jnp.broadcast_to(score_s[src:src + 1, :], (sub, tq))
            for r, sc in enumerate(scs):
                if r * sub > src:
                    beats = ri >= sc
                elif (r + 1) * sub - 1 <= src:
                    beats = ri > sc
                else:
                    beats = (ri > sc) | ((ri == sc) & (row_in_group > src - r * sub))
                cnts[r] = cnts[r] + jnp.where(beats, 1.0, 0.0)
        sels = [jnp.where((cnt < float(min(SLC_TOPK, nblk))) & (sc > 0.5 * NEG), 1.0, 0.0)
                for cnt, sc in zip(cnts, scs)]
        pads = [jnp.zeros((HEAD_DIM, tq), F32)] + sels
        if nr < HEAD_DIM:
            pads.append(jnp.zeros((HEAD_DIM - nr, tq), F32))
        selt_ref[0, 0, rows_of(i), :] = jnp.concatenate(pads, axis=0).T.astype(selt_ref.dtype)

    stack(0, 0)
    scores(0, 0)
    if n_qt > 1:
        stack(1, 1)
    for i in range(n_qt):
        slot = i % 2
        imp = values(i - 1, 1 - slot) if i >= 1 else None
        if i + 1 < n_qt:
            scores(i + 1, 1 - slot)
        if i + 2 < n_qt:
            stack(i + 2, slot)
        softmax(i, slot)
        if i >= 1:
            select(i - 1, imp)
    select(n_qt - 1, values(n_qt - 1, (n_qt - 1) % 2))


def _cmpsel(nq, cmpkv, ovt, tq):
    bsz, seq, _ = nq.shape
    g = NSA_KV_GROUPS
    ncp = cmpkv.shape[3]
    nblk = ovt.shape[0]
    gw = HEADS_PER_GROUP * HEAD_DIM
    rows = HEADS_PER_GROUP * tq
    assert nblk <= HEAD_DIM and nblk % 8 == 0
    return pl.pallas_call(
        functools.partial(_cmpsel_kernel, tq=tq),
        grid=(bsz, g),
        in_specs=[pl.BlockSpec((1, seq, gw), lambda b, gi: (b, 0, gi)),
                  pl.BlockSpec((1, 1, 1, ncp, LANES), lambda b, gi: (b, 0, gi, 0, 0)),
                  pl.BlockSpec((1, 1, 1, ncp, LANES), lambda b, gi: (b, 1, gi, 0, 0)),
                  pl.BlockSpec((nblk, ncp), lambda b, gi: (0, 0))],
        out_specs=[pl.BlockSpec((1, seq, gw), lambda b, gi: (b, 0, gi)),
                   pl.BlockSpec((1, 1, seq, LANES), lambda b, gi: (b, gi, 0, 0))],
        out_shape=[jax.ShapeDtypeStruct((bsz, seq, NSA_WIDTH), BF16),
                   jax.ShapeDtypeStruct((bsz, g, seq, LANES), BF16)],
        scratch_shapes=([pltpu.VMEM((rows, LANES), BF16)] * 2 + [pltpu.VMEM((rows, ncp), F32)] * 2 +
                        [pltpu.VMEM((rows, ncp), BF16)] * 2 + [pltpu.VMEM((tq, ncp), BF16)] * 4 +
                        [pltpu.VMEM((nblk, tq), F32)]),
        compiler_params=_params("parallel", "arbitrary"),
        name="cmpsel",
    )(nq, cmpkv, cmpkv, ovt)


def _stack_head_queries(q_ref, qs_s, rows, src=None):
    src = slice(0, rows) if src is None else src
    lo = lax.broadcasted_iota(jnp.int32, (rows, LANES), 1) < HEAD_DIM
    for c in range(HEADS_PER_GROUP // 2):
        q = q_ref[0, src, c * LANES:(c + 1) * LANES]
        zero = jnp.zeros_like(q)
        qs_s[(2 * c) * rows:(2 * c + 1) * rows, :] = jnp.where(lo, q, zero)
        qs_s[(2 * c + 1) * rows:(2 * c + 2) * rows, :] = jnp.where(lo, zero, q)


def _store_heads(o_ref, outs, tq, rows=slice(None)):
    lo = lax.broadcasted_iota(jnp.int32, (tq, LANES), 1) < HEAD_DIM
    for c in range(HEADS_PER_GROUP // 2):
        o_ref[0, rows, c * LANES:(c + 1) * LANES] = jnp.where(lo, outs[2 * c], outs[2 * c + 1]).astype(o_ref.dtype)


def _store_head_sums(o_ref, accs, rows=slice(None)):
    for h, a in enumerate(accs):
        o_ref[0, rows, h * LANES:(h + 1) * LANES] = a.astype(o_ref.dtype)


def _normalized_heads(ref):
    rows, width = ref.shape[1], ref.shape[2]
    lo = lax.broadcasted_iota(jnp.int32, (rows, LANES), 1) < HEAD_DIM
    outs = []
    for c in range(width // (2 * LANES)):
        a = ref[0, :, (2 * c) * LANES:(2 * c + 1) * LANES].astype(F32)
        b = ref[0, :, (2 * c + 1) * LANES:(2 * c + 2) * LANES].astype(F32)
        num = jnp.where(lo, a, pltpu.roll(b, HEAD_DIM, 1))
        den = jnp.where(lo, pltpu.roll(a, HEAD_DIM, 1), b)
        outs.append(num * (1.0 / den))
    return jnp.concatenate(outs, axis=1)


def _lane_tiles(z):
    return [z[:, c * LANES:(c + 1) * LANES] for c in range(z.shape[1] // LANES)]


def _row_stat(tiles, combine, reduce):
    acc = functools.reduce(combine, tiles)
    return jnp.broadcast_to(reduce(acc, axis=-1, keepdims=True), acc.shape)


WIN_UNROLL = 8


def _win_kernel(q_ref, k_ref, v_ref, o_ref, qs0_s, qs1_s, s0_s, s1_s, p0_s, p1_s, *, tq):
    nh = HEADS_PER_GROUP
    span = WINDOW + tq
    n_qt = q_ref.shape[1] // tq
    qs_b, s_b, p_b = (qs0_s, qs1_s), (s0_s, s1_s), (p0_s, p1_s)

    def q_rows(i):
        return pl.ds(pl.multiple_of(i * tq, tq), tq)

    def k_rows(i):
        return pl.ds(pl.multiple_of(jnp.maximum(i * tq - WINDOW, 0), tq), span)

    def stack_queries(i, slot):
        _stack_head_queries(q_ref, qs_b[slot], tq, src=q_rows(jnp.minimum(i, n_qt - 1)))

    def scores(i, slot):
        i = jnp.minimum(i, n_qt - 1)
        s_b[slot][...] = _dot_nt(qs_b[slot][...], k_ref[0, k_rows(i), :])

    def values(i, slot):
        i = jnp.maximum(i, 0)
        pv = _dot(p_b[slot][...], v_ref[0, k_rows(i), :])
        _store_head_sums(o_ref, [pv[h * tq:(h + 1) * tq, :] for h in range(nh)], q_rows(i))

    def softmax(i, slot):
        tok = i * tq + lax.broadcasted_iota(jnp.int32, (tq, span), 0)
        key = jnp.maximum(i * tq - WINDOW, 0) + lax.broadcasted_iota(jnp.int32, (tq, span), 1)
        bias = jnp.where((key <= tok) & (tok - key < WINDOW), 0.0, NEG)
        for h in range(nh):
            rows = slice(h * tq, (h + 1) * tq)
            tiles = _lane_tiles(s_b[slot][rows, :] + bias)
            m = _row_stat(tiles, jnp.maximum, jnp.max)
            p_b[slot][rows, :] = jnp.concatenate([jnp.exp2(t - m) for t in tiles], axis=1).astype(BF16)

    def stages(first, count):
        for t in range(count):
            i, slot = first + t, t % 2
            values(i - 1, 1 - slot)
            scores(i + 1, 1 - slot)
            stack_queries(i + 2, slot)
            softmax(i, slot)

    p1_s[...] = jnp.ones_like(p1_s)
    stack_queries(0, 0)
    stack_queries(1, 1)
    scores(0, 0)

    def body(b, carry):
        stages(b * WIN_UNROLL, WIN_UNROLL)
        return carry

    lax.fori_loop(0, n_qt // WIN_UNROLL, body, 0)
    stages(n_qt - n_qt % WIN_UNROLL, n_qt % WIN_UNROLL)
    values(n_qt - 1, (n_qt - 1) % 2)


SLC_UNROLL = 8
SLC_STATE_BUFFERS = SLC_UNROLL + 1
assert SLC_UNROLL % 2 == 0


def _slc_kernel(q_ref, k_ref, v_ref, selt_ref, o_ref, qs_s, s0_s, s1_s, p0_s, p1_s, a0_s, a1_s,
                m_s, acc_s, *, tq, tk):
    nh = HEADS_PER_GROUP
    seq = q_ref.shape[1]
    n_qt = seq // tq
    s_b, p_b, a_b = (s0_s, s1_s), (p0_s, p1_s), (a0_s, a1_s)
    lo = lax.broadcasted_iota(jnp.int32, (tq, LANES), 1) < HEAD_DIM

    def n_tiles(qi):
        return (qi * tq + tq - 1) // tk + 1

    n_items = sum((qi * tq + tq - 1) // tk + 1 for qi in range(n_qt))

    def advance(qi, j):
        wrap = j + 1 >= n_tiles(qi)
        return jnp.where(wrap, jnp.minimum(qi + 1, n_qt - 1), qi), jnp.where(wrap, 0, j + 1)

    def q_rows(qi):
        return pl.ds(pl.multiple_of(qi * tq, tq), tq)

    def k_rows(j):
        return pl.ds(pl.multiple_of(j * tk, tk), tk)

    def stack_queries(qi):
        par = qi % SLC_STATE_BUFFERS
        unselected = jnp.where(selt_ref[0, 0, q_rows(qi), :].astype(F32) > 0.5, 0.0, NEG)
        for c in range(nh // 2):
            q = q_ref[0, q_rows(qi), c * LANES:(c + 1) * LANES].astype(F32)
            qs_s[par, (2 * c) * tq:(2 * c + 1) * tq, :] = jnp.where(lo, q, unselected).astype(BF16)
            qs_s[par, (2 * c + 1) * tq:(2 * c + 2) * tq, :] = jnp.where(
                lo, pltpu.roll(q, HEAD_DIM, 1), unselected).astype(BF16)

    def scores(qi, j, slot):
        s_b[slot][...] = _dot_nt(qs_s[qi % SLC_STATE_BUFFERS], k_ref[0, k_rows(j), :])

    def values(qi, j, slot):
        pv = _dot(p_b[slot][...], v_ref[0, k_rows(j), :])
        par = qi % SLC_STATE_BUFFERS
        for h in range(nh):
            acc_s[par, h] = a_b[slot][h] * acc_s[par, h] + pv[h * tq:(h + 1) * tq, :]

    def softmax(qi, j, slot):
        tok = qi * tq + lax.broadcasted_iota(jnp.int32, (tq, tk), 0)
        key = j * tk + lax.broadcasted_iota(jnp.int32, (tq, tk), 1)
        bias = jnp.where(key <= tok, 0.0, NEG)
        for h in range(nh):
            rows = slice(h * tq, (h + 1) * tq)
            tiles = _lane_tiles(s_b[slot][rows, :] + bias)
            m_prev = jnp.where(j == 0, NEG, m_s[h])
            m_new = jnp.maximum(m_prev, _row_stat(tiles, jnp.maximum, jnp.max))
            m_s[h] = m_new
            a_b[slot][h] = jnp.exp2(m_prev - m_new)
            p_b[slot][rows, :] = jnp.concatenate([jnp.exp2(t - m_new) for t in tiles], axis=1).astype(BF16)

    def finalize(qi):
        _store_head_sums(o_ref, [acc_s[qi % SLC_STATE_BUFFERS, h] for h in range(nh)], q_rows(qi))

    def is_last(qi, j):
        return (j + 1 >= n_tiles(qi)).astype(jnp.int32)

    def finalize_if(flag, qi):
        @pl.when(flag == 1)
        def _():
            finalize(qi)

    def block(carry, n_stages):
        fin, (qp, jp, p_last), (qc, jc), (qn, jn) = carry
        for flag, qi in fin:
            finalize_if(flag, qi)
        items = [(qp, jp, p_last), (qc, jc, is_last(qc, jc)), (qn, jn, is_last(qn, jn))]
        for _ in range(n_stages - 1):
            qx, jx = advance(items[-1][0], items[-1][1])
            items.append((qx, jx, is_last(qx, jx)))
        for qx, jx, _ in items[2:]:
            @pl.when(jx == 0)
            def _():
                stack_queries(qx)
        for t in range(n_stages):
            slot = t % 2
            values(items[t][0], items[t][1], 1 - slot)
            scores(items[t + 2][0], items[t + 2][1], 1 - slot)
            softmax(items[t + 1][0], items[t + 1][1], slot)
        fin = tuple((items[t][2], items[t][0]) for t in range(n_stages))
        nxt = advance(items[-1][0], items[-1][1])
        return fin, items[n_stages], items[n_stages + 1][:2], nxt

    m_s[...] = jnp.full_like(m_s, NEG)
    acc_s[...] = jnp.zeros_like(acc_s)
    p1_s[...] = jnp.zeros_like(p1_s)
    a1_s[...] = jnp.ones_like(a1_s)
    zero = jnp.int32(0)
    stack_queries(zero)
    scores(zero, zero, 0)
    unroll = SLC_UNROLL
    carry = (((zero, zero),) * unroll, (zero, zero, zero), (zero, zero), advance(zero, zero))
    carry = lax.fori_loop(0, n_items // unroll, lambda i, c: block(c, unroll), carry)
    if n_items % unroll:
        carry = block(carry, n_items % unroll)
    fin, (qp, jp, _), _, _ = carry
    for flag, qi in fin:
        finalize_if(flag, qi)
    values(qp, jp, (n_items - 1) % 2)
    finalize(qp)


def _slc_attention(nq, kd, vd, selt, *, tq, tk):
    bsz, seq, _ = nq.shape
    g = NSA_KV_GROUPS
    gw = HEADS_PER_GROUP * HEAD_DIM
    nh = HEADS_PER_GROUP
    assert seq % tk == 0 and tk % tq == 0
    assert seq // SLC_BLOCK <= HEAD_DIM
    whole = lambda w: pl.BlockSpec((1, seq, w), lambda b, gi: (b, 0, gi))
    stat = pltpu.VMEM((nh, tq, LANES), F32)
    state = pltpu.VMEM((SLC_STATE_BUFFERS, nh, tq, LANES), F32)
    return pl.pallas_call(
        functools.partial(_slc_kernel, tq=tq, tk=tk),
        grid=(bsz, g),
        in_specs=[whole(gw), whole(LANES), whole(LANES),
                  pl.BlockSpec((1, 1, seq, LANES), lambda b, gi: (b, gi, 0, 0))],
        out_specs=whole(nh * LANES),
        out_shape=jax.ShapeDtypeStruct((bsz, seq, g * nh * LANES), BF16),
        scratch_shapes=([pltpu.VMEM((SLC_STATE_BUFFERS, nh * tq, LANES), BF16)] + [pltpu.VMEM((nh * tq, tk), F32)] * 2 +
                        [pltpu.VMEM((nh * tq, tk), BF16)] * 2 + [stat] * 3 + [state]),
        compiler_params=_params("parallel", "arbitrary"),
        name="slc_attn",
    )(nq, kd, vd, selt)


def _win_attention(nq, kd, vd, *, tq):
    bsz, seq, _ = nq.shape
    g = NSA_KV_GROUPS
    gw = HEADS_PER_GROUP * HEAD_DIM
    nh = HEADS_PER_GROUP
    assert WINDOW % tq == 0 and seq >= WINDOW + tq and seq // tq >= 2 and WIN_UNROLL % 2 == 0
    whole = lambda w: pl.BlockSpec((1, seq, w), lambda b, gi: (b, 0, gi))
    span = WINDOW + tq
    return pl.pallas_call(
        functools.partial(_win_kernel, tq=tq),
        grid=(bsz, g),
        in_specs=[whole(gw), whole(LANES), whole(LANES)],
        out_specs=whole(nh * LANES),
        out_shape=jax.ShapeDtypeStruct((bsz, seq, g * nh * LANES), BF16),
        scratch_shapes=([pltpu.VMEM((nh * tq, LANES), BF16)] * 2 + [pltpu.VMEM((nh * tq, span), F32)] * 2 +
                        [pltpu.VMEM((nh * tq, span), BF16)] * 2),
        compiler_params=_params("parallel", "arbitrary"),
        name="win_attn",
    )(nq, kd, vd)


def _out_kernel(x_ref, gate_ref, gpost_ref, ret_ref, oc_ref, os_ref, ow_ref, gl_ref, ngs_ref,
                wo_ref, o_ref):
    ts = x_ref.shape[1]
    gates = gl_ref[0]
    lo_off = 32
    assert 3 * NSA_HEADS <= lo_off
    g_hi = gates.astype(BF16).astype(F32)
    g_lo = (gates - g_hi).astype(BF16).astype(F32)
    lane = lax.broadcasted_iota(jnp.int32, (ts, LANES), 1)
    g_split = jnp.where(lane < lo_off, g_hi, pltpu.roll(g_lo, lo_off, 1)).astype(BF16)
    src = lax.broadcasted_iota(jnp.int32, (LANES, NSA_WIDTH), 0)
    head = lax.broadcasted_iota(jnp.int32, (LANES, NSA_WIDTH), 1) >> HEAD_SHIFT
    att = jnp.zeros((ts, NSA_WIDTH), F32)
    for i, branch in enumerate((oc_ref[0], _normalized_heads(os_ref), _normalized_heads(ow_ref))):
        expand = jnp.where((src == 3 * head + i) | (src == lo_off + 3 * head + i), 1.0, 0.0).astype(BF16)
        att = att + _dot(g_split, expand) * branch
    att = att * ngs_ref[0]
    y = _dot(ret_ref[0].astype(BF16), wo_ref[:RET_WIDTH, :]) + _dot(att.astype(BF16), wo_ref[RET_WIDTH:, :])
    ms = jnp.mean(y * y, axis=-1, keepdims=True)
    y = y * lax.rsqrt(ms + EPS) * gpost_ref[...]
    o_ref[0] = x_ref[0] + gate_ref[0] * y


def _out(x, gate, g_post, ret, o_cmp, o_slc, o_win, gl, ngs, w_out, ts):
    bsz, seq, d = x.shape
    row = lambda w: pl.BlockSpec((1, ts, w), lambda b, s: (b, s, 0))
    return pl.pallas_call(
        _out_kernel,
        grid=(bsz, seq // ts),
        in_specs=[row(d), pl.BlockSpec((1, 1, d), lambda b, s: (b, 0, 0)),
                  pl.BlockSpec((1, d), lambda b, s: (0, 0)),
                  row(RET_WIDTH), row(NSA_WIDTH), row(o_slc.shape[2]), row(o_win.shape[2]), row(LANES),
                  row(NSA_WIDTH), pl.BlockSpec(w_out.shape, lambda b, s: (0, 0))],
        out_specs=row(d),
        out_shape=jax.ShapeDtypeStruct((bsz, seq, d), x.dtype),
        compiler_params=_params("parallel", "arbitrary"),
        name="out",
    )(x, gate, g_post, ret, o_cmp, o_slc, o_win, gl, ngs, w_out)


def _retention_tables():
    h = RET_HEADS
    c = RET_CHUNK
    log_g = jnp.log1p(-jnp.power(2.0, -5.0 - jnp.arange(h, dtype=F32)))
    idx = jnp.arange(c, dtype=F32)
    diff = idx[:, None] - idx[None, :]
    dec = jnp.where(diff[None] >= 0, jnp.exp(jnp.maximum(diff, 0.0)[None] * log_g[:, None, None]), 0.0)
    zeta = jnp.exp((c - 1 - idx)[None, :] * log_g[:, None])
    xi = jnp.exp((idx + 1.0)[None, :] * log_g[:, None])
    cd = jnp.exp(c * log_g)
    to_lanes = lambda t: jnp.repeat(t.reshape(h // 2, 2, c).transpose(0, 2, 1), HEAD_DIM, axis=2)
    cd_l = jnp.repeat(cd.reshape(h // 2, 1, 2), HEAD_DIM, axis=2)
    return dec, to_lanes(zeta), to_lanes(xi), cd_l


def _overlap_t(seq):
    ncp = seq // CMP_STRIDE
    nc = (seq - CMP_BLOCK) // CMP_STRIDE + 1
    ns = seq // SLC_BLOCK
    cs = np.arange(ncp) * CMP_STRIDE
    ce = cs + CMP_BLOCK - 1
    ss = np.arange(ns) * SLC_BLOCK
    ov = (cs[None, :] <= ss[:, None] + SLC_BLOCK - 1) & (ce[None, :] >= ss[:, None]) & (np.arange(ncp) < nc)[None, :]
    return jnp.asarray(ov.astype(np.float32), dtype=BF16)


def _layer(x, c, positions, mod, g_pre, g_post, w_in, w_out,
           pe_k, w1_k, w2_k, pe_v, w1_v, w2_v):
    bsz, seq, d = x.shape
    shift, scale, gate = [t.reshape(bsz, 1, d) for t in jnp.split(mod, 3, axis=-1)]

    offs = np.cumsum((RET_WIDTH,) * 4 + (NSA_WIDTH, 6 * KV_WIDTH, 3 * NSA_HEADS, NSA_WIDTH))
    w_gl = w_in[:, offs[5]:offs[6]]
    w_all = jnp.concatenate(
        [w_in[:, :offs[5]], w_in[:, offs[6]:], w_gl,
         jnp.zeros((d, LANES - w_gl.shape[1]), w_in.dtype)], axis=1).astype(BF16)

    lanes = jnp.arange(LANES)
    half_r = HEAD_DIM // 2
    inv_r = jnp.power(RET_ROPE_BASE, -jnp.arange(half_r, dtype=F32) / half_r)
    half_n = ROPE_DIM // 2
    inv_n = jnp.power(NSA_ROPE_BASE, -jnp.arange(half_n, dtype=F32) / half_n)
    inv_ret_row = inv_r[lanes % half_r].reshape(1, LANES)
    inv_nsa_row = inv_n[lanes % half_n].reshape(1, LANES)
    pos3 = positions.reshape(bsz, seq, 1)

    (rq, rk, rv, rgs, nq, cmps, ksd, vsd, kwd, vwd, ngs, gl) = _proj(
        x, shift, scale, g_pre.reshape(1, d), pos3, inv_ret_row, inv_nsa_row, w_all, ts=min(PROJ_ROWS, seq))

    dec, zeta_l, xi_l, cd_l = _retention_tables()
    ret = _retention(rq, rk, rv, rgs, dec, zeta_l, xi_l, cd_l, rows=min(RET_ROWS, seq))

    ncp = seq // CMP_STRIDE
    nc = (seq - CMP_BLOCK) // CMP_STRIDE + 1
    pe2 = jnp.stack([pe_k, pe_v]).reshape(2, CMP_BLOCK // 2, LANES)
    w1p = jnp.stack([w1_k, w1_v]).reshape(2, CMP_BLOCK // 2, LANES, CMP_HIDDEN).astype(BF16)
    w2s = jnp.stack([w2_k, w2_v])
    w2d = jnp.concatenate([w2s, w2s], axis=-1).astype(BF16)
    cmp_end = np.arange(nc) * CMP_STRIDE + CMP_BLOCK - 1
    pos_cmp = jnp.pad(positions[:, cmp_end], ((0, 0), (0, ncp - nc))).reshape(bsz, ncp, 1)
    cmpkv = _compress(cmps, pe2, w1p, w2d, pos_cmp, inv_nsa_row)

    o_cmp, selt = _cmpsel(nq, cmpkv, _overlap_t(seq), tq=min(CMPSEL_TQ, seq))
    o_slc = _slc_attention(nq, ksd, vsd, selt, tq=SLC_TQ, tk=SLC_TK)
    o_win = _win_attention(nq, kwd, vwd, tq=WIN_TQ)

    return _out(x, gate, g_post.reshape(1, d), ret, o_cmp, o_slc, o_win, gl, ngs,
                w_out.astype(BF16), ts=min(OUT_ROWS, seq))


def kernel(x, c, positions, w_ada, b_ada, g_pre, g_post, w_in, w_out, cmp_pe_k, cmp_w1_k, cmp_w2_k, cmp_pe_v, cmp_w1_v, cmp_w2_v):
    for l in range(w_in.shape[0]):
        x = _layer(x, c, positions, _ada(c, w_ada, b_ada, l), g_pre[l], g_post[l], w_in[l], w_out[l],
                   cmp_pe_k[l], cmp_w1_k[l], cmp_w2_k[l], cmp_pe_v[l], cmp_w1_v[l], cmp_w2_v[l])
    return x
```

```python
import functools

import numpy as np
import jax
import jax.numpy as jnp
from jax import lax
from jax.experimental import pallas as pl
from jax.experimental.pallas import tpu as pltpu

F32 = jnp.float32
BF16 = jnp.bfloat16

LANES = 128
HEAD_DIM = 64
RET_HEADS = 8
NSA_HEADS = 8
NSA_KV_GROUPS = 2
HEADS_PER_GROUP = NSA_HEADS // NSA_KV_GROUPS
RET_WIDTH = RET_HEADS * HEAD_DIM
NSA_WIDTH = NSA_HEADS * HEAD_DIM
KV_WIDTH = NSA_KV_GROUPS * HEAD_DIM
RET_CHUNK = 128
RET_ROPE_BASE = 10000.0
NSA_ROPE_BASE = 500000.0
ROPE_DIM = HEAD_DIM // 4
CMP_BLOCK = 32
CMP_STRIDE = 16
CMP_HIDDEN = 256
SLC_BLOCK = 64
SLC_TOPK = 16
WINDOW = 512
NEG = -1e30
LOG2E = 1.4426950408889634
BIG = 1e9
EPS = 1e-6
GN_EPS = 1e-5
VMEM_LIMIT = 48 * 1024 * 1024

PROJ_ROWS = 1024
OUT_ROWS = 1024
RET_ROWS = 4096
CMPSEL_TQ = 256
SLC_TQ, SLC_TK = 256, 512
WIN_TQ = 128

NT_DIMS = (((1,), (1,)), ((), ()))
TN_DIMS = (((0,), (0,)), ((), ()))


def _dot(a, b):
    return jnp.dot(a, b, preferred_element_type=F32)


def _dot_nt(a, b):
    return lax.dot_general(a, b, NT_DIMS, preferred_element_type=F32)


def _dot_tn(a, b):
    return lax.dot_general(a, b, TN_DIMS, preferred_element_type=F32)


def _silu(z):
    return z * jax.nn.sigmoid(z)


def _params(*sem):
    return pltpu.CompilerParams(dimension_semantics=sem, vmem_limit_bytes=VMEM_LIMIT)


HEAD_SHIFT = 6
SLC_SHIFT = 6
assert 1 << HEAD_SHIFT == HEAD_DIM and 1 << SLC_SHIFT == SLC_BLOCK


def _lane_in_head(shape):
    return lax.broadcasted_iota(jnp.int32, shape, len(shape) - 1) & (HEAD_DIM - 1)


def _ada_kernel(c_ref, w_ref, b_ref, o_ref):
    a = _silu(c_ref[...])
    o_ref[...] = jnp.dot(a, w_ref[0], precision=lax.Precision.HIGHEST,
                         preferred_element_type=F32) + b_ref[0]


def _ada(c, w_layers, b_layers, layer):
    bsz, d = c.shape
    depth, _, n = w_layers.shape
    tn = d
    return pl.pallas_call(
        _ada_kernel,
        grid=(n // tn,),
        in_specs=[pl.BlockSpec((bsz, d), lambda j: (0, 0)),
                  pl.BlockSpec((1, d, tn), lambda j: (layer, 0, j)),
                  pl.BlockSpec((1, 1, tn), lambda j: (layer, 0, j))],
        out_specs=pl.BlockSpec((bsz, tn), lambda j: (0, j)),
        out_shape=jax.ShapeDtypeStruct((bsz, n), F32),
        compiler_params=_params("arbitrary"),
        name="ada",
    )(c, w_layers, b_layers.reshape(depth, 1, n))


_O_RQ, _O_RK, _O_RV, _O_RG, _O_NQ = 0, 512, 1024, 1536, 2048
_O_KV = 2560
_O_NG = 3328
_O_GL = 3840
_PROJ_COLS = 3968


def _rot_ret(z, cos, sin_signed, m):
    partner = jnp.where(m < HEAD_DIM // 2, pltpu.roll(z, LANES - HEAD_DIM // 2, 1),
                        pltpu.roll(z, HEAD_DIM // 2, 1))
    return z * cos + partner * sin_signed


def _rot_nsa(z, cos, sin_signed, m):
    half = ROPE_DIM // 2
    partner = jnp.where(m < half, pltpu.roll(z, LANES - half, 1), pltpu.roll(z, half, 1))
    return z * cos + partner * sin_signed


def _nsa_tables(posf, inv_row, m):
    half = ROPE_DIM // 2
    ang = posf * inv_row
    cos = jnp.where(m < ROPE_DIM, jnp.cos(ang), 1.0)
    s = jnp.sin(ang)
    sin = jnp.where(m < half, -s, jnp.where(m < ROPE_DIM, s, 0.0))
    return cos, sin


def _dup_groups(z, m_lane):
    r = pltpu.roll(z, HEAD_DIM, 1)
    lo = m_lane < HEAD_DIM
    return jnp.where(lo, z, r), jnp.where(lo, r, z)


def _proj_kernel(x_ref, shift_ref, scale_ref, gpre_ref, pos_ref, invr_ref, invn_ref, w_ref,
                 rq_ref, rk_ref, rv_ref, rgs_ref, nq_ref, cmps_ref, ksd_ref, vsd_ref,
                 kwd_ref, vwd_ref, ngs_ref, gl_ref,
                 cr_s, sr_s, cn_s, sn_s, z0_s, z1_s):
    ts = x_ref.shape[1]
    x = x_ref[0]
    ms = jnp.mean(x * x, axis=-1, keepdims=True)
    y = x * lax.rsqrt(ms + EPS) * gpre_ref[...]
    h = y * (1.0 + scale_ref[0]) + shift_ref[0]
    hb = h.astype(BF16)

    posf = pos_ref[0].astype(F32)
    lane = lax.broadcasted_iota(jnp.int32, (ts, LANES), 1)
    m = lane & (HEAD_DIM - 1)
    ang = posf * invr_ref[...]
    cr_s[...] = jnp.cos(ang)
    s = jnp.sin(ang)
    sr_s[...] = jnp.where(m < HEAD_DIM // 2, -s, s)
    cn, sn = _nsa_tables(posf, invn_ref[...], m)
    cn_s[...] = cn
    sn_s[...] = sn

    staging = [z0_s, z1_s]

    def mm(off, width):
        z_s = staging[0]
        staging.reverse()
        z_s[:, :width] = _dot(hb, w_ref[:, off:off + width])
        return lambda c: z_s[:, c * LANES:(c + 1) * LANES]

    scale_q = HEAD_DIM ** -0.5
    tiles = lambda width: [(c, slice(c * LANES, (c + 1) * LANES)) for c in range(width // LANES)]
    z = mm(_O_RQ, RET_WIDTH)
    for c, sl in tiles(RET_WIDTH):
        rq_ref[0, :, sl] = (_rot_ret(z(c), cr_s[...], sr_s[...], m) * scale_q).astype(BF16)
    z = mm(_O_RK, RET_WIDTH)
    for c, sl in tiles(RET_WIDTH):
        rk_ref[0, :, sl] = _rot_ret(z(c), cr_s[...], sr_s[...], m).astype(BF16)
    z = mm(_O_RV, RET_WIDTH)
    for c, sl in tiles(RET_WIDTH):
        rv_ref[0, :, sl] = z(c).astype(BF16)
    z = mm(_O_RG, RET_WIDTH)
    for c, sl in tiles(RET_WIDTH):
        rgs_ref[0, :, sl] = _silu(z(c)).astype(BF16)
    z = mm(_O_NQ, NSA_WIDTH)
    for c, sl in tiles(NSA_WIDTH):
        nq_ref[0, :, sl] = (_rot_nsa(z(c), cn_s[...], sn_s[...], m) * (scale_q * LOG2E)).astype(BF16)

    z = mm(_O_KV, 6 * KV_WIDTH)
    kc0, kc1 = _dup_groups(z(0), lane)
    vc0, vc1 = _dup_groups(z(1), lane)
    for i, t in enumerate((kc0, kc1, vc0, vc1)):
        cmps_ref[0, :, i * LANES:(i + 1) * LANES] = t
    lo = lane < HEAD_DIM
    tok = pl.program_id(1) * ts + lax.broadcasted_iota(jnp.int32, (ts, LANES), 0)
    onehot = jnp.where(lane - HEAD_DIM == (tok >> SLC_SHIFT), 1.0, 0.0)
    ones = jnp.ones((ts, LANES), F32)
    for c, ref, rotate, upper in ((2, ksd_ref, True, onehot), (3, vsd_ref, False, ones),
                                  (4, kwd_ref, True, None), (5, vwd_ref, False, ones)):
        t = _rot_nsa(z(c), cn_s[...], sn_s[...], m) if rotate else z(c)
        if upper is None:
            d0, d1 = _dup_groups(t, lane)
        else:
            d0, d1 = jnp.where(lo, t, upper), jnp.where(lo, pltpu.roll(t, HEAD_DIM, 1), upper)
        ref[0, :, :LANES] = d0.astype(BF16)
        ref[0, :, LANES:] = d1.astype(BF16)

    z = mm(_O_NG, NSA_WIDTH + LANES)
    for c, sl in tiles(NSA_WIDTH):
        ngs_ref[0, :, sl] = _silu(z(c)).astype(BF16)
    gl_ref[0] = jax.nn.sigmoid(z(NSA_WIDTH // LANES))


def _proj(x, shift, scale, g_pre, pos3, inv_ret_row, inv_nsa_row, w_all, ts):
    bsz, seq, d = x.shape
    row = lambda w: pl.BlockSpec((1, ts, w), lambda b, s: (b, s, 0))
    vec = pl.BlockSpec((1, 1, d), lambda b, s: (b, 0, 0))
    const2 = lambda shp: pl.BlockSpec(shp, lambda b, s: (0, 0))
    outs = [(RET_WIDTH, BF16), (RET_WIDTH, BF16), (RET_WIDTH, BF16), (RET_WIDTH, BF16),
            (NSA_WIDTH, BF16), (4 * LANES, F32), (2 * LANES, BF16), (2 * LANES, BF16),
            (2 * LANES, BF16), (2 * LANES, BF16), (NSA_WIDTH, BF16), (LANES, F32)]
    return pl.pallas_call(
        _proj_kernel,
        grid=(bsz, seq // ts),
        in_specs=[row(d), vec, vec, const2((1, d)), row(1), const2((1, LANES)), const2((1, LANES)),
                  const2((d, _PROJ_COLS))],
        out_specs=[row(w) for w, _ in outs],
        out_shape=[jax.ShapeDtypeStruct((bsz, seq, w), dt) for w, dt in outs],
        scratch_shapes=[pltpu.VMEM((ts, LANES), F32)] * 4 + [pltpu.VMEM((ts, 6 * KV_WIDTH), F32)] * 2,
        compiler_params=_params("parallel", "arbitrary"),
        name="proj",
    )(x, shift, scale, g_pre, pos3, inv_ret_row, inv_nsa_row, w_all)


def _ret_kernel(q_ref, k_ref, v_ref, g_ref, dec_ref, zeta_ref, xi_ref, cd_ref, o_ref, r_s):
    rows = q_ref.shape[1]
    cc = RET_CHUNK

    @pl.when(pl.program_id(2) == 0)
    def _():
        r_s[...] = jnp.zeros_like(r_s)

    lane = lax.broadcasted_iota(jnp.int32, (cc, LANES), 1)
    rowi = lax.broadcasted_iota(jnp.int32, (cc, LANES), 0)
    lo = lane < HEAD_DIM
    blockdiag = (rowi < HEAD_DIM) == lo
    inv_n = 1.0 / HEAD_DIM
    chunks = [slice(c * cc, (c + 1) * cc) for c in range(rows // cc)]
    s_ab, kvs = [], []
    for sl in chunks:
        q, k, v = q_ref[0, sl, :], k_ref[0, sl, :], v_ref[0, sl, :]
        zero = jnp.zeros_like(q)
        s_ab.append((_dot_nt(jnp.where(lo, q, zero), k), _dot_nt(jnp.where(lo, zero, q), k)))
        kz = (k.astype(F32) * zeta_ref[0]).astype(BF16)
        kvs.append(jnp.where(blockdiag, _dot_tn(kz, v), 0.0))
    inners = []
    for sl, (s_a, s_b) in zip(chunks, s_ab):
        v = v_ref[0, sl, :]
        inners.append(jnp.where(lo, _dot((s_a * dec_ref[0]).astype(BF16), v),
                                _dot((s_b * dec_ref[1]).astype(BF16), v)))
    r = r_s[...]
    outs = []
    for sl, inner, kv in zip(chunks, inners, kvs):
        outs.append(inner + _dot(q_ref[0, sl, :], r.astype(BF16)) * xi_ref[0])
        r = r * cd_ref[0] + kv
    r_s[...] = r
    head_ones = jnp.where(blockdiag, 1.0, 0.0).astype(BF16)

    def head_sums(z):
        hi = z.astype(BF16)
        lo_part = (z - hi.astype(F32)).astype(BF16)
        return _dot(hi, head_ones) + _dot(lo_part, head_ones)

    devs = [o - s * inv_n for o, s in zip(outs, [head_sums(o) for o in outs])]
    variances = [head_sums(d * d) * inv_n for d in devs]
    for sl, d, var in zip(chunks, devs, variances):
        o_ref[0, sl, :] = (d * lax.rsqrt(var + GN_EPS) * g_ref[0, sl, :]).astype(o_ref.dtype)


def _retention(rq, rk, rv, rgs, dec, zeta_l, xi_l, cd_l, rows):
    bsz, seq, _ = rq.shape
    npair = RET_HEADS // 2
    blk = pl.BlockSpec((1, rows, LANES), lambda b, p, s: (b, s, p))
    return pl.pallas_call(
        _ret_kernel,
        grid=(bsz, npair, seq // rows),
        in_specs=[blk, blk, blk, blk,
                  pl.BlockSpec((2, RET_CHUNK, RET_CHUNK), lambda b, p, s: (p, 0, 0)),
                  pl.BlockSpec((1, RET_CHUNK, LANES), lambda b, p, s: (p, 0, 0)),
                  pl.BlockSpec((1, RET_CHUNK, LANES), lambda b, p, s: (p, 0, 0)),
                  pl.BlockSpec((1, 1, LANES), lambda b, p, s: (p, 0, 0))],
        out_specs=blk,
        out_shape=jax.ShapeDtypeStruct((bsz, seq, RET_WIDTH), BF16),
        scratch_shapes=[pltpu.VMEM((LANES, LANES), F32)],
        compiler_params=_params("parallel", "parallel", "arbitrary"),
        name="retention",
    )(rq, rk, rv, rgs, dec, zeta_l, xi_l, cd_l)


def _cmp_kernel(src_ref, pe_ref, w1_ref, w2_ref, pos_ref, invn_ref, o_ref):
    ncp = o_ref.shape[3]
    pairs = CMP_BLOCK // 4
    lo = lax.broadcasted_iota(jnp.int32, (ncp, LANES), 1) < HEAD_DIM
    p_lo = jnp.zeros((ncp, CMP_HIDDEN), F32)
    p_hi = jnp.zeros((ncp, CMP_HIDDEN), F32)
    for m in range(pairs):
        t = jnp.where(lo, src_ref[0, pl.ds(2 * m, ncp, stride=CMP_STRIDE), :],
                      src_ref[0, pl.ds(2 * m + 1, ncp, stride=CMP_STRIDE), :])
        p_lo = p_lo + _dot((t + pe_ref[0, m:m + 1, :]).astype(BF16), w1_ref[0, m])
        p_hi = p_hi + _dot((t + pe_ref[0, pairs + m:pairs + m + 1, :]).astype(BF16), w1_ref[0, pairs + m])
    hidden = p_lo + pltpu.roll(p_hi, ncp - 1, 0)
    comp = _dot(_silu(hidden).astype(BF16), w2_ref[0])
    m = _lane_in_head(comp.shape)
    cos, sin = _nsa_tables(pos_ref[0].astype(F32), invn_ref[...], m)
    rotated = _rot_nsa(comp, cos, sin, m)
    is_key = pl.program_id(1) == 0
    o_ref[0, 0, 0] = jnp.where(is_key, rotated, comp).astype(BF16)


def _compress(cmps, pe2, w1p, w2d, pos_cmp, inv_nsa_row):
    bsz, seq, _ = cmps.shape
    ncp = seq // CMP_STRIDE
    g = NSA_KV_GROUPS
    return pl.pallas_call(
        _cmp_kernel,
        grid=(bsz, 2, g),
        in_specs=[pl.BlockSpec((1, seq, LANES), lambda b, j, gi: (b, 0, j * g + gi)),
                  pl.BlockSpec((1, CMP_BLOCK // 2, LANES), lambda b, j, gi: (j, 0, 0)),
                  pl.BlockSpec((1, CMP_BLOCK // 2, LANES, CMP_HIDDEN), lambda b, j, gi: (j, 0, 0, 0)),
                  pl.BlockSpec((1, CMP_HIDDEN, LANES), lambda b, j, gi: (j, 0, 0)),
                  pl.BlockSpec((1, ncp, 1), lambda b, j, gi: (b, 0, 0)),
                  pl.BlockSpec((1, LANES), lambda b, j, gi: (0, 0))],
        out_specs=pl.BlockSpec((1, 1, 1, ncp, LANES), lambda b, j, gi: (b, j, gi, 0, 0)),
        out_shape=jax.ShapeDtypeStruct((bsz, 2, g, ncp, LANES), BF16),
        compiler_params=_params("parallel", "arbitrary", "arbitrary"),
        name="compress",
    )(cmps, pe2, w1p, w2d, pos_cmp, inv_nsa_row)


def _cmpsel_kernel(q_ref, kc_ref, vc_ref, ovt_ref, o_ref, selt_ref,
                   qs0_s, qs1_s, s0_s, s1_s, p0_s, p1_s, hi0_s, hi1_s, lo0_s, lo1_s, score_s, *, tq):
    nh = HEADS_PER_GROUP
    sub = 8
    n_qt = q_ref.shape[1] // tq
    ncp = kc_ref.shape[3]
    nblk = ovt_ref.shape[0]
    kc = kc_ref[0, 0, 0]
    vc = vc_ref[0, 0, 0]
    qs_b, s_b, p_b, hi_b, lo_b = (qs0_s, qs1_s), (s0_s, s1_s), (p0_s, p1_s), (hi0_s, hi1_s), (lo0_s, lo1_s)

    def rows_of(i):
        return slice(i * tq, (i + 1) * tq)

    def last_block(i):
        return ((i + 1) * tq - 1) >> SLC_SHIFT

    def n_rows(i):
        return min(nblk, -(-(last_block(i) + 1) // sub) * sub)

    def stack(i, slot):
        _stack_head_queries(q_ref, qs_b[slot], tq, src=rows_of(i))

    def scores(i, slot):
        s_b[slot][...] = _dot_nt(qs_b[slot][...], kc)

    def softmax(i, slot):
        tok = i * tq + lax.broadcasted_iota(jnp.int32, (tq, LANES), 0)
        lane = lax.broadcasted_iota(jnp.int32, (tq, LANES), 1)
        biases = [jnp.where((lane + c * LANES) * CMP_STRIDE + (CMP_BLOCK - 1) <= tok, 0.0, NEG)
                  for c in range(ncp // LANES)]
        has_key = tok >= CMP_BLOCK - 1
        psum = None
        for h in range(nh):
            rows = slice(h * tq, (h + 1) * tq)
            tiles = [t + b for b, t in zip(biases, _lane_tiles(s_b[slot][rows, :]))]
            mx = _row_stat(tiles, jnp.maximum, jnp.max)
            es = [jnp.exp2(t - mx) for t in tiles]
            inv = jnp.where(has_key, 1.0 / _row_stat(es, jnp.add, jnp.sum), 0.0)
            ps = [e * inv for e in es]
            psum = ps if psum is None else [a + b for a, b in zip(psum, ps)]
            p_b[slot][rows, :] = jnp.concatenate(ps, axis=1).astype(BF16)
        psum = jnp.concatenate(psum, axis=1)
        hi = psum.astype(BF16)
        hi_b[slot][...] = hi
        lo_b[slot][...] = (psum - hi.astype(F32)).astype(BF16)

    def values(i, slot):
        pv = _dot(p_b[slot][...], vc)
        _store_heads(o_ref, [pv[h * tq:(h + 1) * tq, :] for h in range(nh)], tq, rows_of(i))
        ov = ovt_ref[0:n_rows(i), :]
        return _dot_nt(ov, hi_b[slot][...]) + _dot_nt(ov, lo_b[slot][...])

    def select(i, imp):
        nr = n_rows(i)
        blk = lax.broadcasted_iota(jnp.int32, (nr, tq), 0)
        cur = (i * tq + lax.broadcasted_iota(jnp.int32, (nr, tq), 1)) >> SLC_SHIFT
        forced = (blk == 0) | (blk == cur) | (blk == cur - 1)
        score_s[0:nr, :] = jnp.where(forced, BIG, jnp.where(blk <= cur, imp, NEG))
        row_in_group = lax.broadcasted_iota(jnp.int32, (sub, tq), 0)
        scs = [score_s[r * sub:(r + 1) * sub, :] for r in range(nr // sub)]
        cnts = [jnp.zeros((sub, tq), F32) for _ in scs]
        for src in range(last_block(i) + 1):
            ri = jnp.broadcast_to(score_s[src:src + 1, :], (sub, tq))
            for r, sc in enumerate(scs):
                if r * sub > src:
                    beats = ri >= sc
                elif (r + 1) * sub - 1 <= src:
                    beats = ri > sc
                else:
                    beats = (ri > sc) | ((ri == sc) & (row_in_group > src - r * sub))
                cnts[r] = cnts[r] + jnp.where(beats, 1.0, 0.0)
        sels = [jnp.where((cnt < float(min(SLC_TOPK, nblk))) & (sc > 0.5 * NEG), 1.0, 0.0)
                for cnt, sc in zip(cnts, scs)]
        pads = [jnp.zeros((HEAD_DIM, tq), F32)] + sels
        if nr < HEAD_DIM:
            pads.append(jnp.zeros((HEAD_DIM - nr, tq), F32))
        selt_ref[0, 0, rows_of(i), :] = jnp.concatenate(pads, axis=0).T.astype(selt_ref.dtype)

    stack(0, 0)
    scores(0, 0)
    if n_qt > 1:
        stack(1, 1)
    for i in range(n_qt):
        slot = i % 2
        imp = values(i - 1, 1 - slot) if i >= 1 else None
        if i + 1 < n_qt:
            scores(i + 1, 1 - slot)
        if i + 2 < n_qt:
            stack(i + 2, slot)
        softmax(i, slot)
        if i >= 1:
            select(i - 1, imp)
    select(n_qt - 1, values(n_qt - 1, (n_qt - 1) % 2))


def _cmpsel(nq, cmpkv, ovt, tq):
    bsz, seq, _ = nq.shape
    g = NSA_KV_GROUPS
    ncp = cmpkv.shape[3]
    nblk = ovt.shape[0]
    gw = HEADS_PER_GROUP * HEAD_DIM
    rows = HEADS_PER_GROUP * tq
    assert nblk <= HEAD_DIM and nblk % 8 == 0
    return pl.pallas_call(
        functools.partial(_cmpsel_kernel, tq=tq),
        grid=(bsz, g),
        in_specs=[pl.BlockSpec((1, seq, gw), lambda b, gi: (b, 0, gi)),
                  pl.BlockSpec((1, 1, 1, ncp, LANES), lambda b, gi: (b, 0, gi, 0, 0)),
                  pl.BlockSpec((1, 1, 1, ncp, LANES), lambda b, gi: (b, 1, gi, 0, 0)),
                  pl.BlockSpec((nblk, ncp), lambda b, gi: (0, 0))],
        out_specs=[pl.BlockSpec((1, seq, gw), lambda b, gi: (b, 0, gi)),
                   pl.BlockSpec((1, 1, seq, LANES), lambda b, gi: (b, gi, 0, 0))],
        out_shape=[jax.ShapeDtypeStruct((bsz, seq, NSA_WIDTH), BF16),
                   jax.ShapeDtypeStruct((bsz, g, seq, LANES), BF16)],
        scratch_shapes=([pltpu.VMEM((rows, LANES), BF16)] * 2 + [pltpu.VMEM((rows, ncp), F32)] * 2 +
                        [pltpu.VMEM((rows, ncp), BF16)] * 2 + [pltpu.VMEM((tq, ncp), BF16)] * 4 +
                        [pltpu.VMEM((nblk, tq), F32)]),
        compiler_params=_params("parallel", "arbitrary"),
        name="cmpsel",
    )(nq, cmpkv, cmpkv, ovt)


def _stack_head_queries(q_ref, qs_s, rows, src=None):
    src = slice(0, rows) if src is None else src
    lo = lax.broadcasted_iota(jnp.int32, (rows, LANES), 1) < HEAD_DIM
    for c in range(HEADS_PER_GROUP // 2):
        q = q_ref[0, src, c * LANES:(c + 1) * LANES]
        zero = jnp.zeros_like(q)
        qs_s[(2 * c) * rows:(2 * c + 1) * rows, :] = jnp.where(lo, q, zero)
        qs_s[(2 * c + 1) * rows:(2 * c + 2) * rows, :] = jnp.where(lo, zero, q)


def _store_heads(o_ref, outs, tq, rows=slice(None)):
    lo = lax.broadcasted_iota(jnp.int32, (tq, LANES), 1) < HEAD_DIM
    for c in range(HEADS_PER_GROUP // 2):
        o_ref[0, rows, c * LANES:(c + 1) * LANES] = jnp.where(lo, outs[2 * c], outs[2 * c + 1]).astype(o_ref.dtype)


def _store_head_sums(o_ref, accs, rows=slice(None)):
    for h, a in enumerate(accs):
        o_ref[0, rows, h * LANES:(h + 1) * LANES] = a.astype(o_ref.dtype)


def _normalized_heads(ref):
    rows, width = ref.shape[1], ref.shape[2]
    lo = lax.broadcasted_iota(jnp.int32, (rows, LANES), 1) < HEAD_DIM
    outs = []
    for c in range(width // (2 * LANES)):
        a = ref[0, :, (2 * c) * LANES:(2 * c + 1) * LANES].astype(F32)
        b = ref[0, :, (2 * c + 1) * LANES:(2 * c + 2) * LANES].astype(F32)
        num = jnp.where(lo, a, pltpu.roll(b, HEAD_DIM, 1))
        den = jnp.where(lo, pltpu.roll(a, HEAD_DIM, 1), b)
        outs.append(num * (1.0 / den))
    return jnp.concatenate(outs, axis=1)


def _lane_tiles(z):
    return [z[:, c * LANES:(c + 1) * LANES] for c in range(z.shape[1] // LANES)]


def _row_stat(tiles, combine, reduce):
    acc = functools.reduce(combine, tiles)
    return jnp.broadcast_to(reduce(acc, axis=-1, keepdims=True), acc.shape)


WIN_UNROLL = 8


def _win_kernel(q_ref, k_ref, v_ref, o_ref, qs0_s, qs1_s, s0_s, s1_s, p0_s, p1_s, *, tq):
    nh = HEADS_PER_GROUP
    span = WINDOW + tq
    n_qt = q_ref.shape[1] // tq
    qs_b, s_b, p_b = (qs0_s, qs1_s), (s0_s, s1_s), (p0_s, p1_s)

    def q_rows(i):
        return pl.ds(pl.multiple_of(i * tq, tq), tq)

    def k_rows(i):
        return pl.ds(pl.multiple_of(jnp.maximum(i * tq - WINDOW, 0), tq), span)

    def stack_queries(i, slot):
        _stack_head_queries(q_ref, qs_b[slot], tq, src=q_rows(jnp.minimum(i, n_qt - 1)))

    def scores(i, slot):
        i = jnp.minimum(i, n_qt - 1)
        s_b[slot][...] = _dot_nt(qs_b[slot][...], k_ref[0, k_rows(i), :])

    def values(i, slot):
        i = jnp.maximum(i, 0)
        pv = _dot(p_b[slot][...], v_ref[0, k_rows(i), :])
        _store_head_sums(o_ref, [pv[h * tq:(h + 1) * tq, :] for h in range(nh)], q_rows(i))

    def softmax(i, slot):
        tok = i * tq + lax.broadcasted_iota(jnp.int32, (tq, span), 0)
        key = jnp.maximum(i * tq - WINDOW, 0) + lax.broadcasted_iota(jnp.int32, (tq, span), 1)
        bias = jnp.where((key <= tok) & (tok - key < WINDOW), 0.0, NEG)
        for h in range(nh):
            rows = slice(h * tq, (h + 1) * tq)
            tiles = _lane_tiles(s_b[slot][rows, :] + bias)
            m = _row_stat(tiles, jnp.maximum, jnp.max)
            p_b[slot][rows, :] = jnp.concatenate([jnp.exp2(t - m) for t in tiles], axis=1).astype(BF16)

    def stages(first, count):
        for t in range(count):
            i, slot = first + t, t % 2
            values(i - 1, 1 - slot)
            scores(i + 1, 1 - slot)
            stack_queries(i + 2, slot)
            softmax(i, slot)

    p1_s[...] = jnp.ones_like(p1_s)
    stack_queries(0, 0)
    stack_queries(1, 1)
    scores(0, 0)

    def body(b, carry):
        stages(b * WIN_UNROLL, WIN_UNROLL)
        return carry

    lax.fori_loop(0, n_qt // WIN_UNROLL, body, 0)
    stages(n_qt - n_qt % WIN_UNROLL, n_qt % WIN_UNROLL)
    values(n_qt - 1, (n_qt - 1) % 2)


SLC_UNROLL = 8
SLC_STATE_BUFFERS = SLC_UNROLL + 1
assert SLC_UNROLL % 2 == 0


def _slc_kernel(q_ref, k_ref, v_ref, selt_ref, o_ref, qs_s, s0_s, s1_s, p0_s, p1_s, a0_s, a1_s,
                m_s, acc_s, *, tq, tk):
    nh = HEADS_PER_GROUP
    seq = q_ref.shape[1]
    n_qt = seq // tq
    s_b, p_b, a_b = (s0_s, s1_s), (p0_s, p1_s), (a0_s, a1_s)
    lo = lax.broadcasted_iota(jnp.int32, (tq, LANES), 1) < HEAD_DIM

    def n_tiles(qi):
        return (qi * tq + tq - 1) // tk + 1

    n_items = sum((qi * tq + tq - 1) // tk + 1 for qi in range(n_qt))

    def advance(qi, j):
        wrap = j + 1 >= n_tiles(qi)
        return jnp.where(wrap, jnp.minimum(qi + 1, n_qt - 1), qi), jnp.where(wrap, 0, j + 1)

    def q_rows(qi):
        return pl.ds(pl.multiple_of(qi * tq, tq), tq)

    def k_rows(j):
        return pl.ds(pl.multiple_of(j * tk, tk), tk)

    def stack_queries(qi):
        par = qi % SLC_STATE_BUFFERS
        unselected = jnp.where(selt_ref[0, 0, q_rows(qi), :].astype(F32) > 0.5, 0.0, NEG)
        for c in range(nh // 2):
            q = q_ref[0, q_rows(qi), c * LANES:(c + 1) * LANES].astype(F32)
            qs_s[par, (2 * c) * tq:(2 * c + 1) * tq, :] = jnp.where(lo, q, unselected).astype(BF16)
            qs_s[par, (2 * c + 1) * tq:(2 * c + 2) * tq, :] = jnp.where(
                lo, pltpu.roll(q, HEAD_DIM, 1), unselected).astype(BF16)

    def scores(qi, j, slot):
        s_b[slot][...] = _dot_nt(qs_s[qi % SLC_STATE_BUFFERS], k_ref[0, k_rows(j), :])

    def values(qi, j, slot):
        pv = _dot(p_b[slot][...], v_ref[0, k_rows(j), :])
        par = qi % SLC_STATE_BUFFERS
        for h in range(nh):
            acc_s[par, h] = a_b[slot][h] * acc_s[par, h] + pv[h * tq:(h + 1) * tq, :]

    def softmax(qi, j, slot):
        tok = qi * tq + lax.broadcasted_iota(jnp.int32, (tq, tk), 0)
        key = j * tk + lax.broadcasted_iota(jnp.int32, (tq, tk), 1)
        bias = jnp.where(key <= tok, 0.0, NEG)
        for h in range(nh):
            rows = slice(h * tq, (h + 1) * tq)
            tiles = _lane_tiles(s_b[slot][rows, :] + bias)
            m_prev = jnp.where(j == 0, NEG, m_s[h])
            m_new = jnp.maximum(m_prev, _row_stat(tiles, jnp.maximum, jnp.max))
            m_s[h] = m_new
            a_b[slot][h] = jnp.exp2(m_prev - m_new)
            p_b[slot][rows, :] = jnp.concatenate([jnp.exp2(t - m_new) for t in tiles], axis=1).astype(BF16)

    def finalize(qi):
        _store_head_sums(o_ref, [acc_s[qi % SLC_STATE_BUFFERS, h] for h in range(nh)], q_rows(qi))

    def is_last(qi, j):
        return (j + 1 >= n_tiles(qi)).astype(jnp.int32)

    def finalize_if(flag, qi):
        @pl.when(flag == 1)
        def _():
            finalize(qi)

    def block(carry, n_stages):
        fin, (qp, jp, p_last), (qc, jc), (qn, jn) = carry
        for flag, qi in fin:
            finalize_if(flag, qi)
        items = [(qp, jp, p_last), (qc, jc, is_last(qc, jc)), (qn, jn, is_last(qn, jn))]
        for _ in range(n_stages - 1):
            qx, jx = advance(items[-1][0], items[-1][1])
            items.append((qx, jx, is_last(qx, jx)))
        for qx, jx, _ in items[2:]:
            @pl.when(jx == 0)
            def _():
                stack_queries(qx)
        for t in range(n_stages):
            slot = t % 2
            values(items[t][0], items[t][1], 1 - slot)
            scores(items[t + 2][0], items[t + 2][1], 1 - slot)
            softmax(items[t + 1][0], items[t + 1][1], slot)
        fin = tuple((items[t][2], items[t][0]) for t in range(n_stages))
        nxt = advance(items[-1][0], items[-1][1])
        return fin, items[n_stages], items[n_stages + 1][:2], nxt

    m_s[...] = jnp.full_like(m_s, NEG)
    acc_s[...] = jnp.zeros_like(acc_s)
    p1_s[...] = jnp.zeros_like(p1_s)
    a1_s[...] = jnp.ones_like(a1_s)
    zero = jnp.int32(0)
    stack_queries(zero)
    scores(zero, zero, 0)
    unroll = SLC_UNROLL
    carry = (((zero, zero),) * unroll, (zero, zero, zero), (zero, zero), advance(zero, zero))
    carry = lax.fori_loop(0, n_items // unroll, lambda i, c: block(c, unroll), carry)
    if n_items % unroll:
        carry = block(carry, n_items % unroll)
    fin, (qp, jp, _), _, _ = carry
    for flag, qi in fin:
        finalize_if(flag, qi)
    values(qp, jp, (n_items - 1) % 2)
    finalize(qp)


def _slc_attention(nq, kd, vd, selt, *, tq, tk):
    bsz, seq, _ = nq.shape
    g = NSA_KV_GROUPS
    gw = HEADS_PER_GROUP * HEAD_DIM
    nh = HEADS_PER_GROUP
    assert seq % tk == 0 and tk % tq == 0
    assert seq // SLC_BLOCK <= HEAD_DIM
    whole = lambda w: pl.BlockSpec((1, seq, w), lambda b, gi: (b, 0, gi))
    stat = pltpu.VMEM((nh, tq, LANES), F32)
    state = pltpu.VMEM((SLC_STATE_BUFFERS, nh, tq, LANES), F32)
    return pl.pallas_call(
        functools.partial(_slc_kernel, tq=tq, tk=tk),
        grid=(bsz, g),
        in_specs=[whole(gw), whole(LANES), whole(LANES),
                  pl.BlockSpec((1, 1, seq, LANES), lambda b, gi: (b, gi, 0, 0))],
        out_specs=whole(nh * LANES),
        out_shape=jax.ShapeDtypeStruct((bsz, seq, g * nh * LANES), BF16),
        scratch_shapes=([pltpu.VMEM((SLC_STATE_BUFFERS, nh * tq, LANES), BF16)] + [pltpu.VMEM((nh * tq, tk), F32)] * 2 +
                        [pltpu.VMEM((nh * tq, tk), BF16)] * 2 + [stat] * 3 + [state]),
        compiler_params=_params("parallel", "arbitrary"),
        name="slc_attn",
    )(nq, kd, vd, selt)


def _win_attention(nq, kd, vd, *, tq):
    bsz, seq, _ = nq.shape
    g = NSA_KV_GROUPS
    gw = HEADS_PER_GROUP * HEAD_DIM
    nh = HEADS_PER_GROUP
    assert WINDOW % tq == 0 and seq >= WINDOW + tq and seq // tq >= 2 and WIN_UNROLL % 2 == 0
    whole = lambda w: pl.BlockSpec((1, seq, w), lambda b, gi: (b, 0, gi))
    span = WINDOW + tq
    return pl.pallas_call(
        functools.partial(_win_kernel, tq=tq),
        grid=(bsz, g),
        in_specs=[whole(gw), whole(LANES), whole(LANES)],
        out_specs=whole(nh * LANES),
        out_shape=jax.ShapeDtypeStruct((bsz, seq, g * nh * LANES), BF16),
        scratch_shapes=([pltpu.VMEM((nh * tq, LANES), BF16)] * 2 + [pltpu.VMEM((nh * tq, span), F32)] * 2 +
                        [pltpu.VMEM((nh * tq, span), BF16)] * 2),
        compiler_params=_params("parallel", "arbitrary"),
        name="win_attn",
    )(nq, kd, vd)


def _out_kernel(x_ref, gate_ref, gpost_ref, ret_ref, oc_ref, os_ref, ow_ref, gl_ref, ngs_ref,
                wo_ref, o_ref):
    ts = x_ref.shape[1]
    gates = gl_ref[0]
    lo_off = 32
    assert 3 * NSA_HEADS <= lo_off
    g_hi = gates.astype(BF16).astype(F32)
    g_lo = (gates - g_hi).astype(BF16).astype(F32)
    lane = lax.broadcasted_iota(jnp.int32, (ts, LANES), 1)
    g_split = jnp.where(lane < lo_off, g_hi, pltpu.roll(g_lo, lo_off, 1)).astype(BF16)
    src = lax.broadcasted_iota(jnp.int32, (LANES, NSA_WIDTH), 0)
    head = lax.broadcasted_iota(jnp.int32, (LANES, NSA_WIDTH), 1) >> HEAD_SHIFT
    att = jnp.zeros((ts, NSA_WIDTH), F32)
    for i, branch in enumerate((oc_ref[0], _normalized_heads(os_ref), _normalized_heads(ow_ref))):
        expand = jnp.where((src == 3 * head + i) | (src == lo_off + 3 * head + i), 1.0, 0.0).astype(BF16)
        att = att + _dot(g_split, expand) * branch
    att = att * ngs_ref[0]
    y = _dot(ret_ref[0].astype(BF16), wo_ref[:RET_WIDTH, :]) + _dot(att.astype(BF16), wo_ref[RET_WIDTH:, :])
    ms = jnp.mean(y * y, axis=-1, keepdims=True)
    y = y * lax.rsqrt(ms + EPS) * gpost_ref[...]
    o_ref[0] = x_ref[0] + gate_ref[0] * y


def _out(x, gate, g_post, ret, o_cmp, o_slc, o_win, gl, ngs, w_out, ts):
    bsz, seq, d = x.shape
    row = lambda w: pl.BlockSpec((1, ts, w), lambda b, s: (b, s, 0))
    return pl.pallas_call(
        _out_kernel,
        grid=(bsz, seq // ts),
        in_specs=[row(d), pl.BlockSpec((1, 1, d), lambda b, s: (b, 0, 0)),
                  pl.BlockSpec((1, d), lambda b, s: (0, 0)),
                  row(RET_WIDTH), row(NSA_WIDTH), row(o_slc.shape[2]), row(o_win.shape[2]), row(LANES),
                  row(NSA_WIDTH), pl.BlockSpec(w_out.shape, lambda b, s: (0, 0))],
        out_specs=row(d),
        out_shape=jax.ShapeDtypeStruct((bsz, seq, d), x.dtype),
        compiler_params=_params("parallel", "arbitrary"),
        name="out",
    )(x, gate, g_post, ret, o_cmp, o_slc, o_win, gl, ngs, w_out)


def _retention_tables():
    h = RET_HEADS
    c = RET_CHUNK
    log_g = jnp.log1p(-jnp.power(2.0, -5.0 - jnp.arange(h, dtype=F32)))
    idx = jnp.arange(c, dtype=F32)
    diff = idx[:, None] - idx[None, :]
    dec = jnp.where(diff[None] >= 0, jnp.exp(jnp.maximum(diff, 0.0)[None] * log_g[:, None, None]), 0.0)
    zeta = jnp.exp((c - 1 - idx)[None, :] * log_g[:, None])
    xi = jnp.exp((idx + 1.0)[None, :] * log_g[:, None])
    cd = jnp.exp(c * log_g)
    to_lanes = lambda t: jnp.repeat(t.reshape(h // 2, 2, c).transpose(0, 2, 1), HEAD_DIM, axis=2)
    cd_l = jnp.repeat(cd.reshape(h // 2, 1, 2), HEAD_DIM, axis=2)
    return dec, to_lanes(zeta), to_lanes(xi), cd_l


def _overlap_t(seq):
    ncp = seq // CMP_STRIDE
    nc = (seq - CMP_BLOCK) // CMP_STRIDE + 1
    ns = seq // SLC_BLOCK
    cs = np.arange(ncp) * CMP_STRIDE
    ce = cs + CMP_BLOCK - 1
    ss = np.arange(ns) * SLC_BLOCK
    ov = (cs[None, :] <= ss[:, None] + SLC_BLOCK - 1) & (ce[None, :] >= ss[:, None]) & (np.arange(ncp) < nc)[None, :]
    return jnp.asarray(ov.astype(np.float32), dtype=BF16)


def _layer(x, c, positions, mod, g_pre, g_post, w_in, w_out,
           pe_k, w1_k, w2_k, pe_v, w1_v, w2_v):
    bsz, seq, d = x.shape
    shift, scale, gate = [t.reshape(bsz, 1, d) for t in jnp.split(mod, 3, axis=-1)]

    offs = np.cumsum((RET_WIDTH,) * 4 + (NSA_WIDTH, 6 * KV_WIDTH, 3 * NSA_HEADS, NSA_WIDTH))
    w_gl = w_in[:, offs[5]:offs[6]]
    w_all = jnp.concatenate(
        [w_in[:, :offs[5]], w_in[:, offs[6]:], w_gl,
         jnp.zeros((d, LANES - w_gl.shape[1]), w_in.dtype)], axis=1).astype(BF16)

    lanes = jnp.arange(LANES)
    half_r = HEAD_DIM // 2
    inv_r = jnp.power(RET_ROPE_BASE, -jnp.arange(half_r, dtype=F32) / half_r)
    half_n = ROPE_DIM // 2
    inv_n = jnp.power(NSA_ROPE_BASE, -jnp.arange(half_n, dtype=F32) / half_n)
    inv_ret_row = inv_r[lanes % half_r].reshape(1, LANES)
    inv_nsa_row = inv_n[lanes % half_n].reshape(1, LANES)
    pos3 = positions.reshape(bsz, seq, 1)

    (rq, rk, rv, rgs, nq, cmps, ksd, vsd, kwd, vwd, ngs, gl) = _proj(
        x, shift, scale, g_pre.reshape(1, d), pos3, inv_ret_row, inv_nsa_row, w_all, ts=min(PROJ_ROWS, seq))

    dec, zeta_l, xi_l, cd_l = _retention_tables()
    ret = _retention(rq, rk, rv, rgs, dec, zeta_l, xi_l, cd_l, rows=min(RET_ROWS, seq))

    ncp = seq // CMP_STRIDE
    nc = (seq - CMP_BLOCK) // CMP_STRIDE + 1
    pe2 = jnp.stack([pe_k, pe_v]).reshape(2, CMP_BLOCK // 2, LANES)
    w1p = jnp.stack([w1_k, w1_v]).reshape(2, CMP_BLOCK // 2, LANES, CMP_HIDDEN).astype(BF16)
    w2s = jnp.stack([w2_k, w2_v])
    w2d = jnp.concatenate([w2s, w2s], axis=-1).astype(BF16)
    cmp_end = np.arange(nc) * CMP_STRIDE + CMP_BLOCK - 1
    pos_cmp = jnp.pad(positions[:, cmp_end], ((0, 0), (0, ncp - nc))).reshape(bsz, ncp, 1)
    cmpkv = _compress(cmps, pe2, w1p, w2d, pos_cmp, inv_nsa_row)

    o_cmp, selt = _cmpsel(nq, cmpkv, _overlap_t(seq), tq=min(CMPSEL_TQ, seq))
    o_slc = _slc_attention(nq, ksd, vsd, selt, tq=SLC_TQ, tk=SLC_TK)
    o_win = _win_attention(nq, kwd, vwd, tq=WIN_TQ)

    return _out(x, gate, g_post.reshape(1, d), ret, o_cmp, o_slc, o_win, gl, ngs,
                w_out.astype(BF16), ts=min(OUT_ROWS, seq))


def kernel(x, c, positions, w_ada, b_ada, g_pre, g_post, w_in, w_out, cmp_pe_k, cmp_w1_k, cmp_w2_k, cmp_pe_v, cmp_w1_v, cmp_w2_v):
    for l in range(w_in.shape[0]):
        x = _layer(x, c, positions, _ada(c, w_ada, b_ada, l), g_pre[l], g_post[l], w_in[l], w_out[l],
                   cmp_pe_k[l], cmp_w1_k[l], cmp_w2_k[l], cmp_pe_v[l], cmp_w1_v[l], cmp_w2_v[l])
    return x
```
